```python
import jax, jax.numpy as jnp
from jax import lax
import numpy as np

D_MODEL = 2048
BATCH = 8
SEQ = 4096
DEPTH = 1

N_META = 16
HEAD_DIM = 128
FOX_HEADS = 8
SB_HEADS = 8
FOX_WIDTH = FOX_HEADS * HEAD_DIM
SB_WIDTH = SB_HEADS * HEAD_DIM
D_FF = 5632
Q_BLOCK = 128
RMS_EPS = 1e-6
FFN_RESIDUAL_WEIGHT = 0.5
FORGET_BIAS_INIT = 3.0

IN_SIZES = [FOX_WIDTH, FOX_WIDTH, FOX_WIDTH, FOX_HEADS,
            SB_WIDTH, SB_WIDTH, SB_WIDTH,
            D_MODEL, D_MODEL]
IN_PROJ_WIDTH = sum(IN_SIZES)
IN_SPLIT_POINTS = [int(v) for v in np.cumsum(IN_SIZES)[:-1]]

kernel_name = "hybrid_fox_stickbreak_macaron"


def _rmsnorm(x, gain):
    x32 = x.astype(jnp.float32)
    y = x32 * lax.rsqrt(jnp.mean(x32 * x32, axis=-1, keepdims=True) + RMS_EPS)
    return (y * gain.astype(jnp.float32)).astype(x.dtype)


def _swiglu(x, w_gate, w_up, w_down):
    return (jax.nn.silu(x @ w_gate) * (x @ w_up)) @ w_down


def _query_blocks(total_len):
    blocks = [(0, N_META)]
    start = N_META
    while start < total_len:
        end = min(start + Q_BLOCK, total_len)
        blocks.append((start, end))
        start = end
    return blocks


def _fox_attention(q, k, v, log_f_cum):
    L = q.shape[1]
    scale = HEAD_DIM ** -0.5
    outs = []
    for q0, q1 in _query_blocks(L):
        qb, kb, vb = q[:, q0:q1], k[:, :q1], v[:, :q1]
        logits = jnp.einsum('bqhd,bkhd->bhqk', qb, kb).astype(jnp.float32) * scale
        decay = log_f_cum[:, :, q0:q1, None] - log_f_cum[:, :, None, :q1]
        t = jnp.arange(q0, q1)[:, None]
        s = jnp.arange(q1)[None, :]
        logits = jnp.where(s <= t, logits + decay, -jnp.inf)
        p = jax.nn.softmax(logits, axis=-1)
        outs.append(jnp.einsum('bhqk,bkhd->bqhd', p.astype(vb.dtype), vb))
    return jnp.concatenate(outs, axis=1)


def _stick_breaking_attention(q, k, v):
    L = q.shape[1]
    scale = HEAD_DIM ** -0.5
    outs = []
    for q0, q1 in _query_blocks(L):
        qb, kb, vb = q[:, q0:q1], k[:, :q1], v[:, :q1]
        z = jnp.einsum('bqhd,bkhd->bhqk', qb, kb).astype(jnp.float32) * scale
        t = jnp.arange(q0, q1)[:, None]
        s = jnp.arange(q1)[None, :]
        strict = s < t
        log_beta = jax.nn.log_sigmoid(z)
        log_one_minus = jnp.where(strict, log_beta - z, 0.0)
        later = lax.cumsum(log_one_minus, axis=3, reverse=True) - log_one_minus
        w = jnp.where(strict, jnp.exp(log_beta + later), 0.0)
        outs.append(jnp.einsum('bhqk,bkhd->bqhd', w.astype(vb.dtype), vb))
    return jnp.concatenate(outs, axis=1)


def _hybrid_mixer(xn, w_in, b_forget, fox_q_norm, fox_k_norm, w_branch_fox, w_branch_sb, w_out):
    B, L, _ = xn.shape
    proj = xn @ w_in
    fq, fk, fv, f_logit, sq, sk, sv, g_fox, g_sb = jnp.split(proj, IN_SPLIT_POINTS, axis=-1)
    heads = lambda a, n: a.reshape(B, L, n, HEAD_DIM)
    fq = _rmsnorm(heads(fq, FOX_HEADS), fox_q_norm)
    fk = _rmsnorm(heads(fk, FOX_HEADS), fox_k_norm)
    log_f = jax.nn.log_sigmoid((f_logit + b_forget).astype(jnp.float32))
    log_f_cum = jnp.transpose(lax.cumsum(log_f, axis=1), (0, 2, 1))
    o_fox = _fox_attention(fq, fk, heads(fv, FOX_HEADS), log_f_cum).reshape(B, L, FOX_WIDTH)
    o_sb = _stick_breaking_attention(heads(sq, SB_HEADS), heads(sk, SB_HEADS),
                                     heads(sv, SB_HEADS)).reshape(B, L, SB_WIDTH)
    merged = jax.nn.sigmoid(g_fox) * (o_fox @ w_branch_fox) + jax.nn.sigmoid(g_sb) * (o_sb @ w_branch_sb)
    return merged @ w_out


def _fwd_setup_inputs(seed: int = 0) -> dict:
    key = jax.random.key(seed)
    ks = jax.random.split(key, 20)
    f32 = jnp.float32
    nrm = lambda k, shape, fan_in: jax.random.normal(k, shape, f32) * (fan_in ** -0.5)
    gain = lambda k, shape: 1.0 + 0.02 * jax.random.normal(k, shape, f32)
    return {
        "x": jax.random.normal(ks[0], (BATCH, SEQ, D_MODEL), f32),
        "meta_tokens": jax.random.normal(ks[1], (N_META, D_MODEL), f32),
        "ffn1_norm": gain(ks[2], (DEPTH, D_MODEL)),
        "ffn1_w_gate": nrm(ks[3], (DEPTH, D_MODEL, D_FF), D_MODEL),
        "ffn1_w_up": nrm(ks[4], (DEPTH, D_MODEL, D_FF), D_MODEL),
        "ffn1_w_down": nrm(ks[5], (DEPTH, D_FF, D_MODEL), D_FF),
        "mix_norm": gain(ks[6], (DEPTH, D_MODEL)),
        "w_in": nrm(ks[7], (DEPTH, D_MODEL, IN_PROJ_WIDTH), D_MODEL),
        "b_forget": FORGET_BIAS_INIT + 0.1 * jax.random.normal(ks[8], (DEPTH, FOX_HEADS), f32),
        "fox_q_norm": gain(ks[9], (DEPTH, FOX_HEADS, HEAD_DIM)),
        "fox_k_norm": gain(ks[10], (DEPTH, FOX_HEADS, HEAD_DIM)),
        "w_branch_fox": nrm(ks[11], (DEPTH, FOX_WIDTH, D_MODEL), FOX_WIDTH),
        "w_branch_sb": nrm(ks[12], (DEPTH, SB_WIDTH, D_MODEL), SB_WIDTH),
        "w_out": nrm(ks[13], (DEPTH, D_MODEL, D_MODEL), D_MODEL),
        "ffn2_norm": gain(ks[14], (DEPTH, D_MODEL)),
        "ffn2_w_gate": nrm(ks[15], (DEPTH, D_MODEL, D_FF), D_MODEL),
        "ffn2_w_up": nrm(ks[16], (DEPTH, D_MODEL, D_FF), D_MODEL),
        "ffn2_w_down": nrm(ks[17], (DEPTH, D_FF, D_MODEL), D_FF),
    }


def _fwd_reference(x, meta_tokens, ffn1_norm, ffn1_w_gate, ffn1_w_up, ffn1_w_down, mix_norm, w_in, b_forget,
              fox_q_norm, fox_k_norm, w_branch_fox, w_branch_sb, w_out, ffn2_norm, ffn2_w_gate, ffn2_w_up,
              ffn2_w_down):
    B = x.shape[0]
    meta = jnp.broadcast_to(meta_tokens[None].astype(x.dtype), (B, N_META, D_MODEL))
    h = jnp.concatenate([meta, x], axis=1)
    for layer in range(DEPTH):
        h = h + FFN_RESIDUAL_WEIGHT * _swiglu(_rmsnorm(h, ffn1_norm[layer]), ffn1_w_gate[layer],
                                              ffn1_w_up[layer], ffn1_w_down[layer])
        h = h + _hybrid_mixer(_rmsnorm(h, mix_norm[layer]), w_in[layer], b_forget[layer],
                              fox_q_norm[layer], fox_k_norm[layer], w_branch_fox[layer],
                              w_branch_sb[layer], w_out[layer])
        h = h + FFN_RESIDUAL_WEIGHT * _swiglu(_rmsnorm(h, ffn2_norm[layer]), ffn2_w_gate[layer],
                                              ffn2_w_up[layer], ffn2_w_down[layer])
    return h[:, N_META:]


import jax as _jax
import jax.numpy as _jnp

TWIN_FORMAT = 'train_step'
FWD_PARAMS = ['x', 'meta_tokens', 'ffn1_norm', 'ffn1_w_gate', 'ffn1_w_up', 'ffn1_w_down', 'mix_norm', 'w_in', 'b_forget', 'fox_q_norm', 'fox_k_norm', 'w_branch_fox', 'w_branch_sb', 'w_out', 'ffn2_norm', 'ffn2_w_gate', 'ffn2_w_up', 'ffn2_w_down']
TWIN_WEIGHTS = ['meta_tokens', 'ffn1_norm', 'ffn1_w_gate', 'ffn1_w_up', 'ffn1_w_down', 'mix_norm', 'w_in', 'b_forget', 'fox_q_norm', 'fox_k_norm', 'w_branch_fox', 'w_branch_sb', 'w_out', 'ffn2_norm', 'ffn2_w_gate', 'ffn2_w_up', 'ffn2_w_down']
TWIN_DIFF_INPUT = 'x'
TWIN_INPUTS = ['x', 'meta_tokens', 'ffn1_norm', 'ffn1_w_gate', 'ffn1_w_up', 'ffn1_w_down', 'mix_norm', 'w_in', 'b_forget', 'fox_q_norm', 'fox_k_norm', 'w_branch_fox', 'w_branch_sb', 'w_out', 'ffn2_norm', 'ffn2_w_gate', 'ffn2_w_up', 'ffn2_w_down', 'loss_target', 'm_meta_tokens', 'm_ffn1_norm', 'm_ffn1_w_gate', 'm_ffn1_w_up', 'm_ffn1_w_down', 'm_mix_norm', 'm_w_in', 'm_b_forget', 'm_fox_q_norm', 'm_fox_k_norm', 'm_w_branch_fox', 'm_w_branch_sb', 'm_w_out', 'm_ffn2_norm', 'm_ffn2_w_gate', 'm_ffn2_w_up', 'm_ffn2_w_down', 'v_meta_tokens', 'v_ffn1_norm', 'v_ffn1_w_gate', 'v_ffn1_w_up', 'v_ffn1_w_down', 'v_mix_norm', 'v_w_in', 'v_b_forget', 'v_fox_q_norm', 'v_fox_k_norm', 'v_w_branch_fox', 'v_w_branch_sb', 'v_w_out', 'v_ffn2_norm', 'v_ffn2_w_gate', 'v_ffn2_w_up', 'v_ffn2_w_down']
TWIN_OUTPUTS = ['loss', 'grad_x', 'grad_meta_tokens', 'grad_ffn1_norm', 'grad_ffn1_w_gate', 'grad_ffn1_w_up', 'grad_ffn1_w_down', 'grad_mix_norm', 'grad_w_in', 'grad_b_forget', 'grad_fox_q_norm', 'grad_fox_k_norm', 'grad_w_branch_fox', 'grad_w_branch_sb', 'grad_w_out', 'grad_ffn2_norm', 'grad_ffn2_w_gate', 'grad_ffn2_w_up', 'grad_ffn2_w_down', 'delta_meta_tokens', 'delta_ffn1_norm', 'delta_ffn1_w_gate', 'delta_ffn1_w_up', 'delta_ffn1_w_down', 'delta_mix_norm', 'delta_w_in', 'delta_b_forget', 'delta_fox_q_norm', 'delta_fox_k_norm', 'delta_w_branch_fox', 'delta_w_branch_sb', 'delta_w_out', 'delta_ffn2_norm', 'delta_ffn2_w_gate', 'delta_ffn2_w_up', 'delta_ffn2_w_down', 'new_m_meta_tokens', 'new_m_ffn1_norm', 'new_m_ffn1_w_gate', 'new_m_ffn1_w_up', 'new_m_ffn1_w_down', 'new_m_mix_norm', 'new_m_w_in', 'new_m_b_forget', 'new_m_fox_q_norm', 'new_m_fox_k_norm', 'new_m_w_branch_fox', 'new_m_w_branch_sb', 'new_m_w_out', 'new_m_ffn2_norm', 'new_m_ffn2_w_gate', 'new_m_ffn2_w_up', 'new_m_ffn2_w_down', 'new_v_meta_tokens', 'new_v_ffn1_norm', 'new_v_ffn1_w_gate', 'new_v_ffn1_w_up', 'new_v_ffn1_w_down', 'new_v_mix_norm', 'new_v_w_in', 'new_v_b_forget', 'new_v_fox_q_norm', 'new_v_fox_k_norm', 'new_v_w_branch_fox', 'new_v_w_branch_sb', 'new_v_w_out', 'new_v_ffn2_norm', 'new_v_ffn2_w_gate', 'new_v_ffn2_w_up', 'new_v_ffn2_w_down']
TWIN_LEAF_KINDS = {'loss': 'loss', 'grad_x': 'grad_x', 'grad_meta_tokens': 'grad_w', 'grad_ffn1_norm': 'grad_w', 'grad_ffn1_w_gate': 'grad_w', 'grad_ffn1_w_up': 'grad_w', 'grad_ffn1_w_down': 'grad_w', 'grad_mix_norm': 'grad_w', 'grad_w_in': 'grad_w', 'grad_b_forget': 'grad_w', 'grad_fox_q_norm': 'grad_w', 'grad_fox_k_norm': 'grad_w', 'grad_w_branch_fox': 'grad_w', 'grad_w_branch_sb': 'grad_w', 'grad_w_out': 'grad_w', 'grad_ffn2_norm': 'grad_w', 'grad_ffn2_w_gate': 'grad_w', 'grad_ffn2_w_up': 'grad_w', 'grad_ffn2_w_down': 'grad_w', 'delta_meta_tokens': 'delta_w', 'delta_ffn1_norm': 'delta_w', 'delta_ffn1_w_gate': 'delta_w', 'delta_ffn1_w_up': 'delta_w', 'delta_ffn1_w_down': 'delta_w', 'delta_mix_norm': 'delta_w', 'delta_w_in': 'delta_w', 'delta_b_forget': 'delta_w', 'delta_fox_q_norm': 'delta_w', 'delta_fox_k_norm': 'delta_w', 'delta_w_branch_fox': 'delta_w', 'delta_w_branch_sb': 'delta_w', 'delta_w_out': 'delta_w', 'delta_ffn2_norm': 'delta_w', 'delta_ffn2_w_gate': 'delta_w', 'delta_ffn2_w_up': 'delta_w', 'delta_ffn2_w_down': 'delta_w', 'new_m_meta_tokens': 'new_m', 'new_m_ffn1_norm': 'new_m', 'new_m_ffn1_w_gate': 'new_m', 'new_m_ffn1_w_up': 'new_m', 'new_m_ffn1_w_down': 'new_m', 'new_m_mix_norm': 'new_m', 'new_m_w_in': 'new_m', 'new_m_b_forget': 'new_m', 'new_m_fox_q_norm': 'new_m', 'new_m_fox_k_norm': 'new_m', 'new_m_w_branch_fox': 'new_m', 'new_m_w_branch_sb': 'new_m', 'new_m_w_out': 'new_m', 'new_m_ffn2_norm': 'new_m', 'new_m_ffn2_w_gate': 'new_m', 'new_m_ffn2_w_up': 'new_m', 'new_m_ffn2_w_down': 'new_m', 'new_v_meta_tokens': 'new_v', 'new_v_ffn1_norm': 'new_v', 'new_v_ffn1_w_gate': 'new_v', 'new_v_ffn1_w_up': 'new_v', 'new_v_ffn1_w_down': 'new_v', 'new_v_mix_norm': 'new_v', 'new_v_w_in': 'new_v', 'new_v_b_forget': 'new_v', 'new_v_fox_q_norm': 'new_v', 'new_v_fox_k_norm': 'new_v', 'new_v_w_branch_fox': 'new_v', 'new_v_w_branch_sb': 'new_v', 'new_v_w_out': 'new_v', 'new_v_ffn2_norm': 'new_v', 'new_v_ffn2_w_gate': 'new_v', 'new_v_ffn2_w_up': 'new_v', 'new_v_ffn2_w_down': 'new_v'}


def _forward(args):
    return _fwd_reference(*[args[k] for k in FWD_PARAMS])


def _output_shape():
    def fwd():
        inp = _fwd_setup_inputs(0)
        return _fwd_reference(*[inp[k] for k in FWD_PARAMS])
    out = _jax.eval_shape(fwd)
    return out.shape, out.dtype

N_MICROBATCH = 1
ADAM_LR = 0.001
ADAM_B1 = 0.9
ADAM_B2 = 0.999
ADAM_EPS = 1e-08
ADAM_WD = 0.01
ADAM_STEP = 10
PER_EXAMPLE_BATCH_AXIS = {'x': 0, 'loss_target': 0}
SHARED_INPUTS = []
_WEIGHT_DTYPES = {'meta_tokens': _jnp.float32, 'ffn1_norm': _jnp.float32, 'ffn1_w_gate': _jnp.float32, 'ffn1_w_up': _jnp.float32, 'ffn1_w_down': _jnp.float32, 'mix_norm': _jnp.float32, 'w_in': _jnp.float32, 'b_forget': _jnp.float32, 'fox_q_norm': _jnp.float32, 'fox_k_norm': _jnp.float32, 'w_branch_fox': _jnp.float32, 'w_branch_sb': _jnp.float32, 'w_out': _jnp.float32, 'ffn2_norm': _jnp.float32, 'ffn2_w_gate': _jnp.float32, 'ffn2_w_up': _jnp.float32, 'ffn2_w_down': _jnp.float32}
MOMENT_SCALE = {'meta_tokens': 3.298794e-03, 'ffn1_norm': 3.090987e+00, 'ffn1_w_gate': 3.507631e-02, 'ffn1_w_up': 3.721023e-02, 'ffn1_w_down': 6.052969e-02, 'mix_norm': 3.302034e+00, 'w_in': 5.350524e-02, 'b_forget': 3.699731e+01, 'fox_q_norm': 4.334409e-01, 'fox_k_norm': 4.335377e-01, 'w_branch_fox': 2.934812e-02, 'w_branch_sb': 8.659090e-02, 'w_out': 8.118347e-02, 'ffn2_norm': 3.089253e+00, 'ffn2_w_gate': 2.803037e-02, 'ffn2_w_up': 3.188255e-02, 'ffn2_w_down': 5.131152e-02}


def _to_microbatches(a, axis):
    t = _jnp.moveaxis(a, axis, 0)
    t = t.reshape((N_MICROBATCH, t.shape[0] // N_MICROBATCH) + t.shape[1:])
    return _jnp.moveaxis(t, 1, axis + 1)


def setup_inputs(seed: int = 0) -> dict:
    inp = _fwd_setup_inputs(seed)
    key = _jax.random.fold_in(_jax.random.key(seed), 7919)
    shape, _ = _output_shape()
    out = dict(inp)
    out["loss_target"] = _jax.random.normal(_jax.random.fold_in(key, 0), shape, _jnp.float32)
    for i, name in enumerate(TWIN_WEIGHTS):
        w = inp[name].astype(_jnp.float32)
        if MOMENT_SCALE is None:
            s = _jnp.sqrt(_jnp.mean(_jnp.square(w)) + 1e-30)
        else:
            s = MOMENT_SCALE[name]
        km, kv = _jax.random.split(_jax.random.fold_in(key, i + 1))
        out[name] = w
        out["m_" + name] = s * _jax.random.normal(km, w.shape, _jnp.float32)
        out["v_" + name] = (s * s) * _jax.random.uniform(kv, w.shape, _jnp.float32, 0.5, 1.5)
    if N_MICROBATCH > 1:
        for name, axis in PER_EXAMPLE_BATCH_AXIS.items():
            out[name] = _to_microbatches(out[name], axis)
    return {'x': out['x'], 'meta_tokens': out['meta_tokens'], 'ffn1_norm': out['ffn1_norm'], 'ffn1_w_gate': out['ffn1_w_gate'], 'ffn1_w_up': out['ffn1_w_up'], 'ffn1_w_down': out['ffn1_w_down'], 'mix_norm': out['mix_norm'], 'w_in': out['w_in'], 'b_forget': out['b_forget'], 'fox_q_norm': out['fox_q_norm'], 'fox_k_norm': out['fox_k_norm'], 'w_branch_fox': out['w_branch_fox'], 'w_branch_sb': out['w_branch_sb'], 'w_out': out['w_out'], 'ffn2_norm': out['ffn2_norm'], 'ffn2_w_gate': out['ffn2_w_gate'], 'ffn2_w_up': out['ffn2_w_up'], 'ffn2_w_down': out['ffn2_w_down'], 'loss_target': out['loss_target'], 'm_meta_tokens': out['m_meta_tokens'], 'm_ffn1_norm': out['m_ffn1_norm'], 'm_ffn1_w_gate': out['m_ffn1_w_gate'], 'm_ffn1_w_up': out['m_ffn1_w_up'], 'm_ffn1_w_down': out['m_ffn1_w_down'], 'm_mix_norm': out['m_mix_norm'], 'm_w_in': out['m_w_in'], 'm_b_forget': out['m_b_forget'], 'm_fox_q_norm': out['m_fox_q_norm'], 'm_fox_k_norm': out['m_fox_k_norm'], 'm_w_branch_fox': out['m_w_branch_fox'], 'm_w_branch_sb': out['m_w_branch_sb'], 'm_w_out': out['m_w_out'], 'm_ffn2_norm': out['m_ffn2_norm'], 'm_ffn2_w_gate': out['m_ffn2_w_gate'], 'm_ffn2_w_up': out['m_ffn2_w_up'], 'm_ffn2_w_down': out['m_ffn2_w_down'], 'v_meta_tokens': out['v_meta_tokens'], 'v_ffn1_norm': out['v_ffn1_norm'], 'v_ffn1_w_gate': out['v_ffn1_w_gate'], 'v_ffn1_w_up': out['v_ffn1_w_up'], 'v_ffn1_w_down': out['v_ffn1_w_down'], 'v_mix_norm': out['v_mix_norm'], 'v_w_in': out['v_w_in'], 'v_b_forget': out['v_b_forget'], 'v_fox_q_norm': out['v_fox_q_norm'], 'v_fox_k_norm': out['v_fox_k_norm'], 'v_w_branch_fox': out['v_w_branch_fox'], 'v_w_branch_sb': out['v_w_branch_sb'], 'v_w_out': out['v_w_out'], 'v_ffn2_norm': out['v_ffn2_norm'], 'v_ffn2_w_gate': out['v_ffn2_w_gate'], 'v_ffn2_w_up': out['v_ffn2_w_up'], 'v_ffn2_w_down': out['v_ffn2_w_down']}


def _loss(weights, diff, rest, loss_target):
    with _jax.named_scope("forward"):
        args = {**rest, TWIN_DIFF_INPUT: diff, **{k: w.astype(_WEIGHT_DTYPES[k]) for k, w in weights.items()}}
        y = _forward(args)
    with _jax.named_scope("loss_head"):
        err = _jnp.square(y.astype(_jnp.float32) - loss_target)
        return 0.5 * _jnp.sum(_jnp.mean(err, axis=-1)) if err.ndim else 0.5 * err


def _adamw(w, g, m, v):
    m = ADAM_B1 * m + (1.0 - ADAM_B1) * g
    v = ADAM_B2 * v + (1.0 - ADAM_B2) * _jnp.square(g)
    m_hat = m / (1.0 - ADAM_B1 ** ADAM_STEP)
    v_hat = v / (1.0 - ADAM_B2 ** ADAM_STEP)
    delta = -ADAM_LR * (m_hat / (_jnp.sqrt(v_hat) + ADAM_EPS) + ADAM_WD * w)
    return delta, m, v


def reference(x, meta_tokens, ffn1_norm, ffn1_w_gate, ffn1_w_up, ffn1_w_down, mix_norm, w_in, b_forget, fox_q_norm, fox_k_norm, w_branch_fox, w_branch_sb, w_out, ffn2_norm, ffn2_w_gate, ffn2_w_up, ffn2_w_down, loss_target, m_meta_tokens, m_ffn1_norm, m_ffn1_w_gate, m_ffn1_w_up, m_ffn1_w_down, m_mix_norm, m_w_in, m_b_forget, m_fox_q_norm, m_fox_k_norm, m_w_branch_fox, m_w_branch_sb, m_w_out, m_ffn2_norm, m_ffn2_w_gate, m_ffn2_w_up, m_ffn2_w_down, v_meta_tokens, v_ffn1_norm, v_ffn1_w_gate, v_ffn1_w_up, v_ffn1_w_down, v_mix_norm, v_w_in, v_b_forget, v_fox_q_norm, v_fox_k_norm, v_w_branch_fox, v_w_branch_sb, v_w_out, v_ffn2_norm, v_ffn2_w_gate, v_ffn2_w_up, v_ffn2_w_down):
    given = dict(x=x, meta_tokens=meta_tokens, ffn1_norm=ffn1_norm, ffn1_w_gate=ffn1_w_gate, ffn1_w_up=ffn1_w_up, ffn1_w_down=ffn1_w_down, mix_norm=mix_norm, w_in=w_in, b_forget=b_forget, fox_q_norm=fox_q_norm, fox_k_norm=fox_k_norm, w_branch_fox=w_branch_fox, w_branch_sb=w_branch_sb, w_out=w_out, ffn2_norm=ffn2_norm, ffn2_w_gate=ffn2_w_gate, ffn2_w_up=ffn2_w_up, ffn2_w_down=ffn2_w_down, loss_target=loss_target, m_meta_tokens=m_meta_tokens, m_ffn1_norm=m_ffn1_norm, m_ffn1_w_gate=m_ffn1_w_gate, m_ffn1_w_up=m_ffn1_w_up, m_ffn1_w_down=m_ffn1_w_down, m_mix_norm=m_mix_norm, m_w_in=m_w_in, m_b_forget=m_b_forget, m_fox_q_norm=m_fox_q_norm, m_fox_k_norm=m_fox_k_norm, m_w_branch_fox=m_w_branch_fox, m_w_branch_sb=m_w_branch_sb, m_w_out=m_w_out, m_ffn2_norm=m_ffn2_norm, m_ffn2_w_gate=m_ffn2_w_gate, m_ffn2_w_up=m_ffn2_w_up, m_ffn2_w_down=m_ffn2_w_down, v_meta_tokens=v_meta_tokens, v_ffn1_norm=v_ffn1_norm, v_ffn1_w_gate=v_ffn1_w_gate, v_ffn1_w_up=v_ffn1_w_up, v_ffn1_w_down=v_ffn1_w_down, v_mix_norm=v_mix_norm, v_w_in=v_w_in, v_b_forget=v_b_forget, v_fox_q_norm=v_fox_q_norm, v_fox_k_norm=v_fox_k_norm, v_w_branch_fox=v_w_branch_fox, v_w_branch_sb=v_w_branch_sb, v_w_out=v_w_out, v_ffn2_norm=v_ffn2_norm, v_ffn2_w_gate=v_ffn2_w_gate, v_ffn2_w_up=v_ffn2_w_up, v_ffn2_w_down=v_ffn2_w_down)
    weights = {n: given[n] for n in TWIN_WEIGHTS}
    shared = {n: given[n] for n in SHARED_INPUTS}
    per_example = {n: given[n] for n in ['x']}
    grad_fn = _jax.value_and_grad(_loss, argnums=(0, 1))

    def one_microbatch(ex, loss_target):
        ex = dict(ex)
        diff = ex.pop(TWIN_DIFF_INPUT)
        return grad_fn(weights, diff, {**shared, **ex}, loss_target)

    if N_MICROBATCH == 1:
        loss, (grad_w, grad_x) = one_microbatch(per_example, given["loss_target"])
    else:
        def body(carry, xs):
            loss_sum, grad_sum = carry
            l_k, (gw_k, gx_k) = one_microbatch(xs[0], xs[1])
            with _jax.named_scope("update"):
                return (loss_sum + l_k, _jax.tree.map(_jnp.add, grad_sum, gw_k)), gx_k

        init = (_jnp.zeros((), _jnp.float32), _jax.tree.map(_jnp.zeros_like, weights))
        (loss, grad_w), grad_x = _jax.lax.scan(body, init, (per_example, given["loss_target"]))
    with _jax.named_scope("update"):
        delta_w, new_m, new_v = {}, {}, {}
        for n in TWIN_WEIGHTS:
            delta_w[n], new_m[n], new_v[n] = _adamw(weights[n], grad_w[n], given["m_" + n], given["v_" + n])
    return (loss, grad_x, *[grad_w[n] for n in TWIN_WEIGHTS], *[delta_w[n] for n in TWIN_WEIGHTS],
            *[new_m[n] for n in TWIN_WEIGHTS], *[new_v[n] for n in TWIN_WEIGHTS])
```

```python
import functools

import jax
import jax.numpy as jnp
from jax import lax
from jax.experimental import pallas as pl
from jax.experimental.pallas import tpu as pltpu

F32 = jnp.float32
BF = jnp.bfloat16
I32 = jnp.int32

HEAD_DIM = 128
RMS_EPS = 1e-6
FFN_RESIDUAL_WEIGHT = 0.5
ADAM_LR = 0.001
ADAM_B1 = 0.9
ADAM_B2 = 0.999
ADAM_EPS = 1e-08
ADAM_WD = 0.01
ADAM_STEP = 10

LANES = 128
SUBLANES = 8
ATT_TILE = 128
VMEM_CAP = 56 * 1024 * 1024
MESH_IDS = pl.DeviceIdType.MESH
N_CHIPS = 4
N_DEV = 8


def _pick(n, cands):
    for c in cands:
        if c <= n and n % c == 0:
            return c
    raise ValueError(f"no tile for {n} among {cands}")


def _round_up(n, m):
    return (n + m - 1) // m * m


def _tile_bytes(shape, dtype):
    item = jnp.dtype(dtype).itemsize
    dims = [d for d in shape if d is not None]
    if not dims:
        return 4 * LANES * SUBLANES
    last = _round_up(dims[-1], LANES)
    sub = _round_up(dims[-2], SUBLANES * (4 // item)) if len(dims) > 1 else 1
    lead = 1
    for d in dims[:-2]:
        lead *= d
    return lead * sub * last * item


def _vmem_limit(blocks, scratch=(), temps=0):
    need = 2 * sum(_tile_bytes(s, d) for s, d in blocks) + sum(_tile_bytes(s, d) for s, d in scratch) + temps
    return int(min(VMEM_CAP, max(need + (4 << 20), 16 << 20)))


def _call(body, *, name, out_shape, grid=(), in_specs=None, out_specs=None, scratch=(), sem=None, vmem=None,
          aliases=None, prefetch=0):
    params = pltpu.CompilerParams(dimension_semantics=sem, vmem_limit_bytes=vmem)
    if prefetch:
        grid_spec = pltpu.PrefetchScalarGridSpec(num_scalar_prefetch=prefetch, grid=grid, in_specs=in_specs,
                                                 out_specs=out_specs, scratch_shapes=scratch)
        return pl.pallas_call(body, out_shape=out_shape, grid_spec=grid_spec, name=name, compiler_params=params,
                              input_output_aliases=aliases or {})
    return pl.pallas_call(body, out_shape=out_shape, grid=grid, in_specs=in_specs, out_specs=out_specs,
                          scratch_shapes=scratch, name=name, compiler_params=params,
                          input_output_aliases=aliases or {})


def _dot(a, b, ca, cb):
    return lax.dot_general(a, b, (((ca,), (cb,)), ((), ())), preferred_element_type=F32)


def _sigmoid(x):
    return 1.0 / (1.0 + jnp.exp(-x))


def _log_sigmoid(x):
    return jnp.minimum(x, 0.0) - jnp.log1p(jnp.exp(-jnp.abs(x)))


def _split_dot(x, ones_bf, parts):
    out = None
    rem = x
    for p in range(parts):
        piece = rem.astype(BF)
        d = _dot(piece, ones_bf, 1, 0)
        out = d if out is None else out + d
        if p + 1 < parts:
            rem = rem - piece.astype(F32)
    return out


def _matmul(name, a_list, b_list, pairs, n_acc, epi, outs, *, M, N, K, tm, tn, tk, ta=False, tb=False,
            extras=(), n_outer=False, b_off=0):
    gi, gj, nk = M // tm, N // tn, K // tk
    assert gi * tm == M and gj * tn == N and nk * tk == K, (name, M, N, K, tm, tn, tk)
    n_a, n_b, n_e, n_o = len(a_list), len(b_list), len(extras), len(outs)

    def ij(g0, g1):
        return (g1, g0) if n_outer else (g0, g1)

    def a_map(g0, g1, k):
        i, _ = ij(g0, g1)
        return (k, i) if ta else (i, k)

    def b_map(g0, g1, k):
        _, j = ij(g0, g1)
        return (j + b_off, k) if tb else (k, j + b_off)

    def tile_map(fn):
        return lambda g0, g1, k: fn(*ij(g0, g1))

    a_block = (tk, tm) if ta else (tm, tk)
    b_block = (tn, tk) if tb else (tk, tn)
    in_specs = ([pl.BlockSpec(a_block, a_map)] * n_a + [pl.BlockSpec(b_block, b_map)] * n_b
                + [pl.BlockSpec(bs, tile_map(fn)) for _, bs, fn in extras])
    out_specs = [pl.BlockSpec(bs, tile_map(fn)) for _, _, bs, fn in outs]
    out_shape = [jax.ShapeDtypeStruct(s, d) for s, d, _, _ in outs]
    scratch = [pltpu.VMEM((tm, tn), F32) for _ in range(n_acc)] if nk > 1 else []

    def body(*refs):
        a_refs = refs[:n_a]
        b_refs = refs[n_a:n_a + n_b]
        e_refs = refs[n_a + n_b:n_a + n_b + n_e]
        o_refs = refs[n_a + n_b + n_e:n_a + n_b + n_e + n_o]
        acc_refs = refs[n_a + n_b + n_e + n_o:]

        def products():
            accs = [None] * n_acc
            for ai, bi, ci in pairs:
                a = a_refs[ai][...]
                b = b_refs[bi][...]
                d = _dot(a.astype(BF), b.astype(BF), 0 if ta else 1, 1 if tb else 0)
                accs[ci] = d if accs[ci] is None else accs[ci] + d
            return accs

        def finish(accs):
            res = epi(accs, [e[...] for e in e_refs])
            for o_ref, r in zip(o_refs, res):
                o_ref[...] = r.reshape(o_ref.shape).astype(o_ref.dtype)

        if nk == 1:
            finish(products())
        else:
            k = pl.program_id(2)

            @pl.when(k == 0)
            def _():
                for acc in acc_refs:
                    acc[...] = jnp.zeros_like(acc)

            for acc, d in zip(acc_refs, products()):
                acc[...] += d

            @pl.when(k == nk - 1)
            def _():
                finish([acc[...] for acc in acc_refs])

    blocks = ([(a_block, a.dtype) for a in a_list] + [(b_block, b.dtype) for b in b_list]
              + [(bs, e.dtype) for e, bs, _ in extras] + [(bs, d) for _, d, bs, _ in outs])
    vmem = _vmem_limit(blocks, [((tm, tn), F32)] * (n_acc if nk > 1 else 0), temps=6 * tm * tn * 4)
    grid = (gj, gi, nk) if n_outer else (gi, gj, nk)
    fn = _call(body, name=name, out_shape=out_shape, grid=grid, in_specs=in_specs, out_specs=out_specs,
               scratch=scratch, sem=("parallel", "parallel", "arbitrary"), vmem=vmem)
    return fn(*a_list, *b_list, *[e for e, _, _ in extras])


def _mn(tm, tn, col0=0):
    assert col0 % tn == 0
    off = col0 // tn
    return (tm, tn), (lambda i, j: (i, j + off))


def _rmsnorm_fwd(name, x, gain, tr):
    L, D = x.shape

    def body(x_ref, g_ref, o_ref):
        xv = x_ref[...]
        r = lax.rsqrt(jnp.mean(xv * xv, axis=-1, keepdims=True) + RMS_EPS)
        o_ref[...] = (xv * r * g_ref[...]).astype(BF)

    row = pl.BlockSpec((tr, D), lambda i: (i, 0))
    vec = pl.BlockSpec((1, D), lambda i: (0, 0))
    vmem = _vmem_limit([((tr, D), F32), ((tr, D), BF)], temps=3 * tr * D * 4)
    return _call(body, name=name, out_shape=jax.ShapeDtypeStruct((L, D), BF), grid=(L // tr,), in_specs=[row, vec],
                 out_specs=row, sem=("parallel",), vmem=vmem)(x, gain)


def _rmsnorm_bwd(name, dn, x, gain, dres, tr):
    L, D = x.shape
    steps = L // tr

    def body(dn_ref, x_ref, g_ref, dres_ref, dx_ref, dxb_ref, dg_ref):
        i = pl.program_id(0)
        xv = x_ref[...]
        r = lax.rsqrt(jnp.mean(xv * xv, axis=-1, keepdims=True) + RMS_EPS)
        xhat = xv * r
        dy = dn_ref[...]
        dxhat = dy * g_ref[...]
        dx = dres_ref[...] + r * (dxhat - xhat * jnp.mean(dxhat * xhat, axis=-1, keepdims=True))
        dx_ref[...] = dx
        dxb_ref[...] = dx.astype(BF)

        @pl.when(i == 0)
        def _():
            dg_ref[...] = jnp.zeros_like(dg_ref)

        dg_ref[...] += jnp.sum(dy * xhat, axis=0, keepdims=True)

    row = pl.BlockSpec((tr, D), lambda i: (i, 0))
    vec = pl.BlockSpec((1, D), lambda i: (0, 0))
    vmem = _vmem_limit([((tr, D), F32)] * 4 + [((tr, D), BF)], temps=4 * tr * D * 4)
    out_shape = [jax.ShapeDtypeStruct((L, D), F32), jax.ShapeDtypeStruct((L, D), BF),
                 jax.ShapeDtypeStruct((1, D), F32)]
    return _call(body, name=name, out_shape=out_shape, grid=(steps,), in_specs=[row, row, vec, row],
                 out_specs=[row, row, vec], sem=("arbitrary",), vmem=vmem)(dn, x, gain, dres)


def _loss_grad(name, h, target, n_meta, n_seq, tr):
    L, D = h.shape

    def body(h_ref, t_ref, dh_ref, dhb_ref, loss_ref):
        i = pl.program_id(0)
        rows = i * tr + lax.broadcasted_iota(I32, (tr, 1), 0)
        valid = (rows >= n_meta) & (rows < n_meta + n_seq)
        diff = jnp.where(valid, h_ref[...] - t_ref[...], 0.0)
        dh = diff * (1.0 / D)
        dh_ref[...] = dh
        dhb_ref[...] = dh.astype(BF)

        @pl.when(i == 0)
        def _():
            loss_ref[...] = jnp.zeros_like(loss_ref)

        loss_ref[...] += jnp.sum(diff * diff) * (0.5 / D)

    row = pl.BlockSpec((tr, D), lambda i: (i, 0))
    acc = pl.BlockSpec((1, LANES), lambda i: (0, 0))
    vmem = _vmem_limit([((tr, D), F32)] * 3 + [((tr, D), BF)], temps=3 * tr * D * 4)
    out_shape = [jax.ShapeDtypeStruct((L, D), F32), jax.ShapeDtypeStruct((L, D), BF),
                 jax.ShapeDtypeStruct((1, LANES), F32)]
    return _call(body, name=name, out_shape=out_shape, grid=(L // tr,), in_specs=[row, row],
                 out_specs=[row, row, acc], sem=("arbitrary",), vmem=vmem)(h, target)


def _qknorm_fwd(name, proj, gq, gk, heads, q_col, k_col, tr):
    L = proj.shape[0]

    def body(q_ref, k_ref, gq_ref, gk_ref, qn_ref, kn_ref):
        for x_ref, g_ref, o_ref in ((q_ref, gq_ref, qn_ref), (k_ref, gk_ref, kn_ref)):
            xv = x_ref[...].astype(F32)
            r = lax.rsqrt(jnp.mean(xv * xv, axis=-1, keepdims=True) + RMS_EPS)
            o_ref[...] = (xv * r * g_ref[...]).astype(BF)

    qb, kb = q_col // HEAD_DIM, k_col // HEAD_DIM
    in_specs = [pl.BlockSpec((tr, HEAD_DIM), lambda h, i: (i, qb + h)),
                pl.BlockSpec((tr, HEAD_DIM), lambda h, i: (i, kb + h)),
                pl.BlockSpec((None, 1, HEAD_DIM), lambda h, i: (h, 0, 0)),
                pl.BlockSpec((None, 1, HEAD_DIM), lambda h, i: (h, 0, 0))]
    out = pl.BlockSpec((tr, HEAD_DIM), lambda h, i: (i, h))
    out_shape = [jax.ShapeDtypeStruct((L, heads * HEAD_DIM), BF)] * 2
    return _call(body, name=name, out_shape=out_shape, grid=(heads, L // tr), in_specs=in_specs,
                 out_specs=[out, out], sem=("parallel", "parallel"), vmem=16 << 20)(proj, proj, gq, gk)


def _qknorm_bwd(name, proj, dqn, dkn, gq, gk, heads, q_col, k_col, tr):
    L = proj.shape[0]

    def body(q_ref, k_ref, dqn_ref, dkn_ref, gq_ref, gk_ref, dq_ref, dk_ref, dgq_ref, dgk_ref):
        i = pl.program_id(1)
        for x_ref, dy_ref, g_ref, dx_ref, dg_ref in ((q_ref, dqn_ref, gq_ref, dq_ref, dgq_ref),
                                                     (k_ref, dkn_ref, gk_ref, dk_ref, dgk_ref)):
            xv = x_ref[...].astype(F32)
            r = lax.rsqrt(jnp.mean(xv * xv, axis=-1, keepdims=True) + RMS_EPS)
            xhat = xv * r
            dy = dy_ref[...].astype(F32)
            dxhat = dy * g_ref[...]
            dx_ref[...] = (r * (dxhat - xhat * jnp.mean(dxhat * xhat, axis=-1, keepdims=True))).astype(BF)

            @pl.when(i == 0)
            def _():
                dg_ref[...] = jnp.zeros_like(dg_ref)

            dg_ref[...] += jnp.sum(dy * xhat, axis=0, keepdims=True)

    qb, kb = q_col // HEAD_DIM, k_col // HEAD_DIM
    tile = pl.BlockSpec((tr, HEAD_DIM), lambda h, i: (i, h))
    gain = pl.BlockSpec((None, 1, HEAD_DIM), lambda h, i: (h, 0, 0))
    in_specs = [pl.BlockSpec((tr, HEAD_DIM), lambda h, i: (i, qb + h)),
                pl.BlockSpec((tr, HEAD_DIM), lambda h, i: (i, kb + h)), tile, tile, gain, gain]
    out_shape = [jax.ShapeDtypeStruct((L, heads * HEAD_DIM), BF)] * 2 + [
        jax.ShapeDtypeStruct((heads, 1, HEAD_DIM), F32)] * 2
    return _call(body, name=name, out_shape=out_shape, grid=(heads, L // tr), in_specs=in_specs,
                 out_specs=[tile, tile, gain, gain], sem=("parallel", "arbitrary"),
                 vmem=16 << 20)(proj, proj, dqn, dkn, gq, gk)


def _tri(cmp):
    r = lax.broadcasted_iota(I32, (LANES, LANES), 0)
    c = lax.broadcasted_iota(I32, (LANES, LANES), 1)
    return jnp.where(cmp(r, c), 1.0, 0.0).astype(BF)


def _cum_fwd(name, fl, bias, n_rows):
    H, nbp, _ = fl.shape

    def body(fl_ref, b_ref, c_ref, tot_ref):
        lf = _log_sigmoid(fl_ref[...] + b_ref[...])
        c_ref[...] = _split_dot(lf, _tri(lambda r, c: r <= c), 3)
        tot_ref[...] = _split_dot(lf, jnp.ones((LANES, LANES), BF), 3)

        def step(r, carry):
            c_ref[pl.ds(r, 1), :] = c_ref[pl.ds(r, 1), :] + carry
            return carry + tot_ref[pl.ds(r, 1), :]

        lax.fori_loop(0, n_rows, step, jnp.zeros((1, LANES), F32))

    blk = pl.BlockSpec((None, nbp, LANES), lambda h: (h, 0, 0))
    vec = pl.BlockSpec((None, 1, LANES), lambda h: (h, 0, 0))
    return _call(body, name=name, out_shape=jax.ShapeDtypeStruct((H, nbp, LANES), F32), grid=(H,),
                 in_specs=[blk, vec], out_specs=blk, scratch=[pltpu.VMEM((nbp, LANES), F32)], sem=("parallel",),
                 vmem=16 << 20)(fl, bias)


def _cum_bwd(name, dcs, fl, bias, n_rows):
    H, nbp, _ = fl.shape

    def body(dcs_ref, fl_ref, b_ref, dfl_ref, db_ref, rin_ref, tot_ref):
        dc = -dcs_ref[...]
        rin_ref[...] = _split_dot(dc, _tri(lambda r, c: r >= c), 3)
        tot_ref[...] = _split_dot(dc, jnp.ones((LANES, LANES), BF), 3)
        dfl_ref[...] = jnp.zeros_like(dfl_ref)

        def step(t, carry):
            r = n_rows - 1 - t
            x = fl_ref[pl.ds(r, 1), :] + b_ref[...]
            dfl_ref[pl.ds(r, 1), :] = (rin_ref[pl.ds(r, 1), :] + carry) * _sigmoid(-x)
            return carry + tot_ref[pl.ds(r, 1), :]

        lax.fori_loop(0, n_rows, step, jnp.zeros((1, LANES), F32))
        db_ref[...] = jnp.zeros_like(db_ref) + jnp.sum(dfl_ref[...])

    blk = pl.BlockSpec((None, nbp, LANES), lambda h: (h, 0, 0))
    vec = pl.BlockSpec((None, 1, LANES), lambda h: (h, 0, 0))
    out_shape = [jax.ShapeDtypeStruct((H, nbp, LANES), F32), jax.ShapeDtypeStruct((H, 1, LANES), F32)]
    return _call(body, name=name, out_shape=out_shape, grid=(H,), in_specs=[blk, blk, vec], out_specs=[blk, vec],
                 scratch=[pltpu.VMEM((nbp, LANES), F32)] * 2, sem=("parallel",), vmem=16 << 20)(dcs, fl, bias)


def _att_specs(L, q_col, k_col, v_col):
    T = ATT_TILE
    qb, kb, vb = q_col // HEAD_DIM, k_col // HEAD_DIM, v_col // HEAD_DIM
    q_spec = pl.BlockSpec((T, HEAD_DIM), lambda h, i: (i, qb + h))
    k_spec = pl.BlockSpec((L, HEAD_DIM), lambda h, i: (0, kb + h))
    v_spec = pl.BlockSpec((L, HEAD_DIM), lambda h, i: (0, vb + h))
    return q_spec, k_spec, v_spec


def _tile_iotas():
    T = ATT_TILE
    return lax.broadcasted_iota(I32, (T, T), 0), lax.broadcasted_iota(I32, (T, T), 1)


def _rows(j):
    return pl.ds(pl.multiple_of(j * ATT_TILE, ATT_TILE), ATT_TILE)


def _fox_fwd(name, q_arr, k_arr, v_arr, c_row, c_col, heads, q_col, k_col, v_col):
    L = q_arr.shape[0]
    T = ATT_TILE
    scale = HEAD_DIM ** -0.5

    def body(q_ref, k_ref, v_ref, crow_ref, ccol_ref, o_ref, lse_ref):
        i = pl.program_id(1)
        q = q_ref[...]
        ct = ccol_ref[...]
        row, col = _tile_iotas()

        def tile(j, carry, masked):
            m, l, acc = carry
            ks = k_ref[_rows(j), :]
            vs = v_ref[_rows(j), :]
            s = _dot(q, ks, 1, 1) * scale + (ct - crow_ref[pl.ds(j, 1), :])
            if masked:
                s = jnp.where(col <= row, s, -jnp.inf)
            m_new = jnp.maximum(m, jnp.max(s, axis=1, keepdims=True))
            alpha = jnp.exp(m - m_new)
            p = jnp.exp(s - m_new)
            l = alpha * l + jnp.sum(p, axis=1, keepdims=True)
            acc = alpha * acc + _dot(p.astype(BF), vs, 1, 0)
            return m_new, l, acc

        init = (jnp.full((T, 1), -1e30, F32), jnp.zeros((T, 1), F32), jnp.zeros((T, HEAD_DIM), F32))
        carry = lax.fori_loop(0, i, lambda j, c: tile(j, c, False), init)
        m, l, acc = tile(i, carry, True)
        o_ref[...] = (acc / l).astype(o_ref.dtype)
        lse_ref[...] = m + jnp.log(l)

    nbp = c_row.shape[1]
    q_spec, k_spec, v_spec = _att_specs(L, q_col, k_col, v_col)
    crow_spec = pl.BlockSpec((None, nbp, LANES), lambda h, i: (h, 0, 0))
    col_spec = pl.BlockSpec((None, T, 1), lambda h, i: (h, i, 0))
    o_spec = pl.BlockSpec((T, HEAD_DIM), lambda h, i: (i, h))
    out_shape = [jax.ShapeDtypeStruct((L, heads * HEAD_DIM), BF), jax.ShapeDtypeStruct((heads, L, 1), F32)]
    vmem = _vmem_limit([((L, HEAD_DIM), BF)] * 2, temps=8 << 20)
    return _call(body, name=name, out_shape=out_shape, grid=(heads, L // T),
                 in_specs=[q_spec, k_spec, v_spec, crow_spec, col_spec], out_specs=[o_spec, col_spec],
                 sem=("parallel", "parallel"), vmem=vmem)(q_arr, k_arr, v_arr, c_row, c_col)


def _fox_bwd(name, q_arr, k_arr, v_arr, c_row, c_col, o, do, lse, heads, q_col, k_col, v_col):
    L = q_arr.shape[0]
    T = ATT_TILE
    nq = L // T
    scale = HEAD_DIM ** -0.5

    def body(q_ref, k_ref, v_ref, crow_ref, ccol_ref, o_ref, do_ref, lse_ref, dq_ref, dk_ref, dv_ref, dcs_ref,
             dk_acc, dv_acc):
        i = pl.program_id(1)

        @pl.when(i == 0)
        def _():
            dk_acc[...] = jnp.zeros_like(dk_acc)
            dv_acc[...] = jnp.zeros_like(dv_acc)
            dcs_ref[...] = jnp.zeros_like(dcs_ref)

        q = q_ref[...]
        do = do_ref[...]
        delta = jnp.sum(do.astype(F32) * o_ref[...].astype(F32), axis=1, keepdims=True)
        lse_t = lse_ref[...]
        ct = ccol_ref[...]
        row, col = _tile_iotas()

        def tile(j, dq, masked):
            ks = k_ref[_rows(j), :]
            vs = v_ref[_rows(j), :]
            s = _dot(q, ks, 1, 1) * scale + (ct - crow_ref[pl.ds(j, 1), :])
            if masked:
                s = jnp.where(col <= row, s, -jnp.inf)
            p = jnp.exp(s - lse_t)
            ds = p * (_dot(do, vs, 1, 1) - delta)
            dcs_ref[pl.ds(j, 1), :] += jnp.sum(ds, axis=0, keepdims=True)
            dsb = (ds * scale).astype(BF)
            dk_acc[_rows(j), :] += _dot(dsb, q, 0, 0)
            dv_acc[_rows(j), :] += _dot(p.astype(BF), do, 0, 0)
            return dq + _dot(dsb, ks, 1, 0)

        dq = lax.fori_loop(0, i, lambda j, c: tile(j, c, False), jnp.zeros((T, HEAD_DIM), F32))
        dq_ref[...] = tile(i, dq, True).astype(dq_ref.dtype)

        @pl.when(i == nq - 1)
        def _():
            dk_ref[...] = dk_acc[...].astype(dk_ref.dtype)
            dv_ref[...] = dv_acc[...].astype(dv_ref.dtype)

    nbp = c_row.shape[1]
    q_spec, k_spec, v_spec = _att_specs(L, q_col, k_col, v_col)
    crow_spec = pl.BlockSpec((None, nbp, LANES), lambda h, i: (h, 0, 0))
    col_spec = pl.BlockSpec((None, T, 1), lambda h, i: (h, i, 0))
    t_spec = pl.BlockSpec((T, HEAD_DIM), lambda h, i: (i, h))
    head_spec = pl.BlockSpec((L, HEAD_DIM), lambda h, i: (0, h))
    W = heads * HEAD_DIM
    out_shape = [jax.ShapeDtypeStruct((L, W), F32), jax.ShapeDtypeStruct((L, W), F32),
                 jax.ShapeDtypeStruct((L, W), BF), jax.ShapeDtypeStruct((heads, nbp, LANES), F32)]
    scratch = [pltpu.VMEM((L, HEAD_DIM), F32)] * 2
    vmem = _vmem_limit([((L, HEAD_DIM), BF)] * 3 + [((L, HEAD_DIM), F32)], [((L, HEAD_DIM), F32)] * 2,
                       temps=8 << 20)
    return _call(body, name=name, out_shape=out_shape, grid=(heads, nq),
                 in_specs=[q_spec, k_spec, v_spec, crow_spec, col_spec, t_spec, t_spec, col_spec],
                 out_specs=[t_spec, head_spec, head_spec, crow_spec], scratch=scratch,
                 sem=("parallel", "arbitrary"), vmem=vmem)(q_arr, k_arr, v_arr, c_row, c_col, o, do, lse)


def _sb_logits(q, ks, scale, valid):
    z = _dot(q, ks, 1, 1) * scale
    lb = jnp.minimum(z, 0.0) - jnp.log1p(jnp.exp(-jnp.abs(z)))
    lom = lb - z
    if valid is not None:
        lom = jnp.where(valid, lom, 0.0)
    return lb, lom


def _sb_fwd(name, proj, heads, q_col, k_col, v_col):
    L = proj.shape[0]
    T = ATT_TILE
    scale = HEAD_DIM ** -0.5

    def body(q_ref, k_ref, v_ref, o_ref):
        i = pl.program_id(1)
        q = q_ref[...]
        row, col = _tile_iotas()
        later_mat = jnp.where(row > col, 1.0, 0.0).astype(BF)

        def tile(j, carry, masked):
            run, acc = carry
            valid = (col < row) if masked else None
            lb, lom = _sb_logits(q, k_ref[_rows(j), :], scale, valid)
            w = jnp.exp(lb + _split_dot(lom, later_mat, 2) + run)
            if masked:
                w = jnp.where(valid, w, 0.0)
            acc = acc + _dot(w.astype(BF), v_ref[_rows(j), :], 1, 0)
            return run + jnp.sum(lom, axis=1, keepdims=True), acc

        carry = tile(i, (jnp.zeros((T, 1), F32), jnp.zeros((T, HEAD_DIM), F32)), True)
        _, acc = lax.fori_loop(0, i, lambda t, c: tile(i - 1 - t, c, False), carry)
        o_ref[...] = acc.astype(o_ref.dtype)

    q_spec, k_spec, v_spec = _att_specs(L, q_col, k_col, v_col)
    o_spec = pl.BlockSpec((T, HEAD_DIM), lambda h, i: (i, h))
    vmem = _vmem_limit([((L, HEAD_DIM), BF)] * 2, temps=8 << 20)
    return _call(body, name=name, out_shape=jax.ShapeDtypeStruct((L, heads * HEAD_DIM), BF), grid=(heads, L // T),
                 in_specs=[q_spec, k_spec, v_spec], out_specs=o_spec, sem=("parallel", "parallel"),
                 vmem=vmem)(proj, proj, proj)


def _sb_bwd(name, proj, do, heads, q_col, k_col, v_col):
    L = proj.shape[0]
    T = ATT_TILE
    nq = L // T
    scale = HEAD_DIM ** -0.5

    def body(q_ref, k_ref, v_ref, do_ref, dq_ref, dk_ref, dv_ref, da_buf, beta_buf, dk_acc, dv_acc):
        i = pl.program_id(1)

        @pl.when(i == 0)
        def _():
            dk_acc[...] = jnp.zeros_like(dk_acc)
            dv_acc[...] = jnp.zeros_like(dv_acc)

        q = q_ref[...]
        do = do_ref[...]
        row, col = _tile_iotas()
        later_mat = jnp.where(row > col, 1.0, 0.0).astype(BF)
        before_mat = jnp.where(row < col, 1.0, 0.0).astype(BF)

        def pass1(j, run, masked):
            valid = (col < row) if masked else None
            lb, lom = _sb_logits(q, k_ref[_rows(j), :], scale, valid)
            w = jnp.exp(lb + _split_dot(lom, later_mat, 2) + run)
            if masked:
                w = jnp.where(valid, w, 0.0)
            da_buf[j] = _dot(do, v_ref[_rows(j), :], 1, 1) * w
            beta_buf[j] = jnp.exp(lb)
            dv_acc[_rows(j), :] += _dot(w.astype(BF), do, 0, 0)
            return run + jnp.sum(lom, axis=1, keepdims=True)

        run = pass1(i, jnp.zeros((T, 1), F32), True)
        lax.fori_loop(0, i, lambda t, c: pass1(i - 1 - t, c, False), run)

        def pass2(j, carry, masked):
            g_run, dq = carry
            da = da_buf[j]
            beta = beta_buf[j]
            g = g_run + _split_dot(da, before_mat, 2)
            dz = da * (1.0 - beta) - g * beta
            if masked:
                dz = jnp.where(col < row, dz, 0.0)
            dzb = (dz * scale).astype(BF)
            dk_acc[_rows(j), :] += _dot(dzb, q, 0, 0)
            dq = dq + _dot(dzb, k_ref[_rows(j), :], 1, 0)
            return g_run + jnp.sum(da, axis=1, keepdims=True), dq

        init = (jnp.zeros((T, 1), F32), jnp.zeros((T, HEAD_DIM), F32))
        carry = lax.fori_loop(0, i, lambda j, c: pass2(j, c, False), init)
        _, dq = pass2(i, carry, True)
        dq_ref[...] = dq.astype(dq_ref.dtype)

        @pl.when(i == nq - 1)
        def _():
            dk_ref[...] = dk_acc[...].astype(dk_ref.dtype)
            dv_ref[...] = dv_acc[...].astype(dv_ref.dtype)

    q_spec, k_spec, v_spec = _att_specs(L, q_col, k_col, v_col)
    t_spec = pl.BlockSpec((T, HEAD_DIM), lambda h, i: (i, h))
    head_spec = pl.BlockSpec((L, HEAD_DIM), lambda h, i: (0, h))
    W = heads * HEAD_DIM
    out_shape = [jax.ShapeDtypeStruct((L, W), BF)] * 3
    scratch = [pltpu.VMEM((nq, T, T), F32)] * 2 + [pltpu.VMEM((L, HEAD_DIM), F32)] * 2
    vmem = _vmem_limit([((L, HEAD_DIM), BF)] * 4, [((nq, T, T), F32)] * 2 + [((L, HEAD_DIM), F32)] * 2,
                       temps=8 << 20)
    return _call(body, name=name, out_shape=out_shape, grid=(heads, nq), in_specs=[q_spec, k_spec, v_spec, t_spec],
                 out_specs=[t_spec, head_spec, head_spec], scratch=scratch, sem=("parallel", "arbitrary"),
                 vmem=vmem)(proj, proj, proj, do)


_ANY = pl.BlockSpec(memory_space=pl.ANY)


def _mesh_pos():
    return lax.axis_index("x"), lax.axis_index("y"), lax.axis_index("c")


def _other_chips(x, y):
    return [(1 - x, y), (x, 1 - y), (1 - x, 1 - y)]


def _shard_window(ref, kind, sidx, r0, nr, cs):
    if kind == "col":
        assert cs % LANES == 0
        return ref.at[pl.ds(r0, nr), pl.ds(pl.multiple_of(sidx * cs, LANES), cs)]
    return ref.at[sidx, pl.ds(r0, nr), :]


def _gather_shapes(shards, kinds):
    shapes = []
    for s, kind in zip(shards, kinds):
        R, C = s.shape
        shapes.append(jax.ShapeDtypeStruct((R, N_CHIPS * C) if kind == "col" else (N_CHIPS, R, C), s.dtype))
    return shapes


def _all_gather_ici(name, shards, kinds):
    n = len(shards)
    meta = [(kind, s.shape[0], s.shape[1]) for s, kind in zip(shards, kinds)]

    def body(*refs):
        srcs, dsts = refs[:n], refs[n:2 * n]
        send_sem, recv_sem, local_sem = refs[2 * n:]
        x, y, c = _mesh_pos()
        sidx = 2 * x + y
        chips = _other_chips(x, y)
        locals_, sends, recvs = [], [], []
        for t, (kind, R, cs) in enumerate(meta):
            half = R // 2
            r0 = pl.multiple_of(c * half, SUBLANES)
            cp = pltpu.make_async_copy(srcs[t], _shard_window(dsts[t], kind, sidx, 0, R, cs), local_sem.at[t])
            cp.start()
            locals_.append(cp)
            for p, (px, py) in enumerate(chips):
                k = 3 * t + p
                src = srcs[t].at[pl.ds(r0, half), :]
                cp = pltpu.make_async_remote_copy(
                    src_ref=src, dst_ref=_shard_window(dsts[t], kind, sidx, r0, half, cs),
                    send_sem=send_sem.at[k], recv_sem=recv_sem.at[k], device_id=(px, py, c),
                    device_id_type=MESH_IDS)
                cp.start()
                sends.append(cp)
                recvs.append(pltpu.make_async_remote_copy(
                    src_ref=src, dst_ref=_shard_window(dsts[t], kind, 2 * px + py, r0, half, cs),
                    send_sem=send_sem.at[k], recv_sem=recv_sem.at[k], device_id=(px, py, c),
                    device_id_type=MESH_IDS))
        for cp in recvs:
            cp.wait_recv()
        for cp in sends:
            cp.wait_send()
        for cp in locals_:
            cp.wait()

    scratch = [pltpu.SemaphoreType.DMA((3 * n,)), pltpu.SemaphoreType.DMA((3 * n,)), pltpu.SemaphoreType.DMA((n,))]
    return _call(body, name=name, out_shape=_gather_shapes(shards, kinds), in_specs=[_ANY] * n,
                 out_specs=[_ANY] * n, scratch=scratch)(*shards)


def _all_gather_d2d(name, gathered, shards, kinds):
    n = len(gathered)
    meta = [(kind, s.shape[0], s.shape[1]) for s, kind in zip(shards, kinds)]

    def body(*refs):
        bufs = refs[n:2 * n]
        send_sem, recv_sem = refs[2 * n:]
        x, y, c = _mesh_pos()
        sends, recvs = [], []
        for t, (kind, R, cs) in enumerate(meta):
            half = R // 2
            mine = pl.multiple_of(c * half, SUBLANES)
            theirs = pl.multiple_of((1 - c) * half, SUBLANES)
            for p, (px, py) in enumerate(_other_chips(x, y)):
                k = 3 * t + p
                win = _shard_window(bufs[t], kind, 2 * px + py, mine, half, cs)
                cp = pltpu.make_async_remote_copy(src_ref=win, dst_ref=win, send_sem=send_sem.at[k],
                                                  recv_sem=recv_sem.at[k], device_id=(x, y, 1 - c),
                                                  device_id_type=MESH_IDS)
                cp.start()
                sends.append(cp)
                got = _shard_window(bufs[t], kind, 2 * px + py, theirs, half, cs)
                recvs.append(pltpu.make_async_remote_copy(src_ref=win, dst_ref=got, send_sem=send_sem.at[k],
                                                          recv_sem=recv_sem.at[k], device_id=(x, y, 1 - c),
                                                          device_id_type=MESH_IDS))
        for cp in recvs:
            cp.wait_recv()
        for cp in sends:
            cp.wait_send()

    scratch = [pltpu.SemaphoreType.DMA((3 * n,)), pltpu.SemaphoreType.DMA((3 * n,))]
    out_shape = [jax.ShapeDtypeStruct(g.shape, g.dtype) for g in gathered]
    return _call(body, name=name, out_shape=out_shape, in_specs=[_ANY] * n, out_specs=[_ANY] * n, scratch=scratch,
                 aliases={t: t for t in range(n)})(*gathered)


def _reduce_to_sibling(name, grads):
    n = len(grads)

    def body(*refs):
        srcs, dsts = refs[:n], refs[n:2 * n]
        send_sem, recv_sem = refs[2 * n:]
        x, y, c = _mesh_pos()
        copies = []
        for t, g in enumerate(grads):
            half = g.shape[1] // 2
            theirs = pl.multiple_of((1 - c) * half, SUBLANES)
            cp = pltpu.make_async_remote_copy(src_ref=srcs[t].at[:, pl.ds(theirs, half), :], dst_ref=dsts[t],
                                              send_sem=send_sem.at[t], recv_sem=recv_sem.at[t],
                                              device_id=(x, y, 1 - c), device_id_type=MESH_IDS)
            cp.start()
            copies.append(cp)
        for cp in copies:
            cp.wait()

    out_shape = [jax.ShapeDtypeStruct((N_CHIPS, g.shape[1] // 2, g.shape[2]), g.dtype) for g in grads]
    scratch = [pltpu.SemaphoreType.DMA((n,)), pltpu.SemaphoreType.DMA((n,))]
    return _call(body, name=name, out_shape=out_shape, in_specs=[_ANY] * n, out_specs=[_ANY] * n,
                 scratch=scratch)(*grads)


def _exchange_chip_partials(name, partials):
    n = len(partials)

    def body(*refs):
        srcs, dsts = refs[:n], refs[n:2 * n]
        send_sem, recv_sem = refs[2 * n:]
        x, y, c = _mesh_pos()
        copies = []
        for t in range(n):
            for p, (px, py) in enumerate(_other_chips(x, y)):
                k = 3 * t + p
                cp = pltpu.make_async_remote_copy(src_ref=srcs[t].at[2 * px + py], dst_ref=dsts[t].at[p],
                                                  send_sem=send_sem.at[k], recv_sem=recv_sem.at[k],
                                                  device_id=(px, py, c), device_id_type=MESH_IDS)
                cp.start()
                copies.append(cp)
        for cp in copies:
            cp.wait()

    out_shape = [jax.ShapeDtypeStruct((3,) + p.shape[1:], p.dtype) for p in partials]
    scratch = [pltpu.SemaphoreType.DMA((3 * n,)), pltpu.SemaphoreType.DMA((3 * n,))]
    return _call(body, name=name, out_shape=out_shape, in_specs=[_ANY] * n, out_specs=[_ANY] * n,
                 scratch=scratch)(*partials)


def _share_halves(name, halves):
    n = len(halves)

    def body(*refs):
        srcs, dsts = refs[:n], refs[n:2 * n]
        send_sem, recv_sem, local_sem = refs[2 * n:]
        x, y, c = _mesh_pos()
        remote, local = [], []
        for t, h in enumerate(halves):
            half = h.shape[0]
            mine = dsts[t].at[pl.ds(pl.multiple_of(c * half, SUBLANES), half), :]
            cp = pltpu.make_async_copy(srcs[t], mine, local_sem.at[t])
            cp.start()
            local.append(cp)
            cp = pltpu.make_async_remote_copy(src_ref=srcs[t], dst_ref=mine, send_sem=send_sem.at[t],
                                              recv_sem=recv_sem.at[t], device_id=(x, y, 1 - c),
                                              device_id_type=MESH_IDS)
            cp.start()
            remote.append(cp)
        for t, h in enumerate(halves):
            half = h.shape[0]
            theirs = dsts[t].at[pl.ds(pl.multiple_of((1 - c) * half, SUBLANES), half), :]
            pltpu.make_async_remote_copy(src_ref=srcs[t], dst_ref=theirs, send_sem=send_sem.at[t],
                                         recv_sem=recv_sem.at[t], device_id=(x, y, 1 - c),
                                         device_id_type=MESH_IDS).wait_recv()
        for cp in remote:
            cp.wait_send()
        for cp in local:
            cp.wait()

    out_shape = [jax.ShapeDtypeStruct((2 * h.shape[0], h.shape[1]), h.dtype) for h in halves]
    scratch = [pltpu.SemaphoreType.DMA((n,)), pltpu.SemaphoreType.DMA((n,)), pltpu.SemaphoreType.DMA((n,))]
    return _call(body, name=name, out_shape=out_shape, in_specs=[_ANY] * n, out_specs=[_ANY] * n,
                 scratch=scratch)(*halves)


def _gather_small(name, v):
    def body(v_ref, out_ref, send_sem, recv_sem, local_sem):
        x, y, c = _mesh_pos()
        me = 4 * x + 2 * y + c
        local = pltpu.make_async_copy(v_ref, out_ref.at[me], local_sem)
        local.start()
        sends, recvs = [], []
        for k in range(1, N_DEV):
            px = 1 - x if k & 4 else x
            py = 1 - y if k & 2 else y
            pc = 1 - c if k & 1 else c
            cp = pltpu.make_async_remote_copy(src_ref=v_ref, dst_ref=out_ref.at[me], send_sem=send_sem.at[k],
                                              recv_sem=recv_sem.at[k], device_id=(px, py, pc),
                                              device_id_type=MESH_IDS)
            cp.start()
            sends.append(cp)
            recvs.append(pltpu.make_async_remote_copy(
                src_ref=v_ref, dst_ref=out_ref.at[4 * px + 2 * py + pc], send_sem=send_sem.at[k],
                recv_sem=recv_sem.at[k], device_id=(px, py, pc), device_id_type=MESH_IDS))
        for cp in recvs:
            cp.wait_recv()
        for cp in sends:
            cp.wait_send()
        local.wait()

    scratch = [pltpu.SemaphoreType.DMA((N_DEV,)), pltpu.SemaphoreType.DMA((N_DEV,)), pltpu.SemaphoreType.DMA(())]
    return _call(body, name=name, out_shape=jax.ShapeDtypeStruct((N_DEV,) + v.shape, v.dtype), in_specs=[_ANY],
                 out_specs=_ANY, scratch=scratch)(v)


def _row_tile(rows, cols, n_arrays, mult=SUBLANES):
    budget = (24 << 20) // (2 * n_arrays * _round_up(cols, LANES) * 4)
    for t in (512, 256, 128, 64, 32, 16, 8):
        if t <= max(budget, mult) and rows % t == 0 and t % mult == 0:
            return t
    raise ValueError(f"no row tile for {rows} x {cols}")


def _chip_partial(name, pos, own, recv):
    _, half, C = recv.shape
    tr = _row_tile(half, C, 3, 16)
    nh = half // tr

    def body(pos_ref, own_ref, recv_ref, out_ref):
        out_ref[...] = (own_ref[...] + recv_ref[...]).astype(BF)

    blk = pl.BlockSpec((None, tr, C), lambda s, r, pos_ref: (s, r, 0))
    own_blk = pl.BlockSpec((None, tr, C), lambda s, r, pos_ref: (s, pos_ref[0] * nh + r, 0))
    return _call(body, name=name, out_shape=jax.ShapeDtypeStruct(recv.shape, BF), grid=(N_CHIPS, nh),
                 in_specs=[own_blk, blk], out_specs=blk, sem=("parallel", "parallel"), vmem=40 << 20,
                 prefetch=1)(pos, own, recv)


def _final_half(name, pos, own, recv, others):
    _, half, C = recv.shape
    tr = _row_tile(half, C, 4, 16)
    nh = half // tr

    def body(pos_ref, own_ref, recv_ref, oth_ref, out_ref):
        acc = own_ref[...] + recv_ref[...]
        for p in range(3):
            acc = acc + oth_ref[p].astype(F32)
        out_ref[...] = acc

    own_blk = pl.BlockSpec((None, tr, C), lambda r, pos_ref: (pos_ref[1], pos_ref[0] * nh + r, 0))
    recv_blk = pl.BlockSpec((None, tr, C), lambda r, pos_ref: (pos_ref[1], r, 0))
    oth_blk = pl.BlockSpec((3, tr, C), lambda r, pos_ref: (0, r, 0))
    out_blk = pl.BlockSpec((tr, C), lambda r, pos_ref: (r, 0))
    return _call(body, name=name, out_shape=jax.ShapeDtypeStruct((half, C), F32), grid=(nh,),
                 in_specs=[own_blk, recv_blk, oth_blk], out_specs=out_blk, sem=("parallel",), vmem=40 << 20,
                 prefetch=1)(pos, own, recv, others)


def _adamw(name, w, g, m, v):
    R, C = w.shape
    tr = R if R < SUBLANES or R % SUBLANES else _row_tile(R, C, 7)
    c1 = 1.0 - ADAM_B1 ** ADAM_STEP
    c2 = 1.0 - ADAM_B2 ** ADAM_STEP

    def body(w_ref, g_ref, m_ref, v_ref, d_ref, nm_ref, nv_ref):
        gv = g_ref[...]
        nm = ADAM_B1 * m_ref[...] + (1.0 - ADAM_B1) * gv
        nv = ADAM_B2 * v_ref[...] + (1.0 - ADAM_B2) * (gv * gv)
        d_ref[...] = -ADAM_LR * ((nm / c1) / (jnp.sqrt(nv / c2) + ADAM_EPS) + ADAM_WD * w_ref[...])
        nm_ref[...] = nm
        nv_ref[...] = nv

    blk = pl.BlockSpec((tr, C), lambda r: (r, 0))
    out_shape = [jax.ShapeDtypeStruct((R, C), F32)] * 3
    return _call(body, name=name, out_shape=out_shape, grid=(R // tr,), in_specs=[blk] * 4, out_specs=[blk] * 3,
                 sem=("parallel",), vmem=40 << 20)(w, g, m, v)


def _sum_devices(name, gathered):
    _, R, _ = gathered.shape

    def body(g_ref, o_ref):
        acc = g_ref[0]
        for d in range(1, N_DEV):
            acc = acc + g_ref[d]
        o_ref[...] = acc

    return _call(body, name=name, out_shape=jax.ShapeDtypeStruct((R, LANES), F32), grid=(1,),
                 in_specs=[pl.BlockSpec((N_DEV, R, LANES), lambda i: (0, 0, 0))],
                 out_specs=pl.BlockSpec((R, LANES), lambda i: (0, 0)), sem=("arbitrary",), vmem=16 << 20)(gathered)


def _ffn_fwd(tag, h, gain, wg, wu, wd, tm):
    L, D = h.shape
    F = wg.shape[1]
    tn = _pick(F, (512, 256, 128))
    n = _rmsnorm_fwd(f"{tag}_norm", h, gain, tm)

    def gate_up(accs, _):
        a, u = accs
        return a, u, a * _sigmoid(a) * u

    mn = _mn(tm, tn)
    a, u, s = _matmul(f"{tag}_gate_up", [n], [wg, wu], [(0, 0, 0), (0, 1, 1)], 2, gate_up,
                      [((L, F), BF) + mn] * 3, M=L, N=F, K=D, tm=tm, tn=tn, tk=D, n_outer=True)
    td = _pick(D, (512, 256, 128))
    (h_out,) = _matmul(f"{tag}_down", [s], [wd], [(0, 0, 0)], 1,
                       lambda accs, ex: [ex[0] + FFN_RESIDUAL_WEIGHT * accs[0]], [((L, D), F32) + _mn(tm, td)],
                       M=L, N=D, K=F, tm=tm, tn=td, tk=_pick(F, (1408, 512, 256, 128)), extras=[(h,) + _mn(tm, td)],
                       n_outer=True)
    return h_out, (h, n, a, u, s)


def _ffn_bwd(tag, dh, dh_bf, saved, gain, wg, wu, wd, tm, cs_ff):
    h, n, a, u, s = saved
    L, D = h.shape
    F = wg.shape[1]
    tn = _pick(F, (512, 256, 128))
    tkl = _pick(L, (1408, 384, 256, 128))

    def act_grad(accs, ex):
        ds = FFN_RESIDUAL_WEIGHT * accs[0]
        av, uv = ex[0].astype(F32), ex[1].astype(F32)
        sg = _sigmoid(av)
        return ds * uv * sg * (1.0 + av * (1.0 - sg)), ds * av * sg

    mn = _mn(tm, tn)
    da, du = _matmul(f"{tag}_dact", [dh_bf], [wd], [(0, 0, 0)], 1, act_grad, [((L, F), BF) + mn] * 2, M=L, N=F,
                     K=D, tm=tm, tn=tn, tk=D, tb=True, extras=[(a,) + mn, (u,) + mn], n_outer=True)
    td = _pick(D, (512, 256, 128))
    (dwd,) = _matmul(f"{tag}_dwd", [s], [dh_bf], [(0, 0, 0)], 1, lambda accs, _: [FFN_RESIDUAL_WEIGHT * accs[0]],
                     [((N_CHIPS, cs_ff, D), F32, (None, cs_ff, td), lambda i, j: (i, 0, j))], M=F, N=D, K=L,
                     tm=cs_ff, tn=td, tk=tkl, ta=True)
    tdn = _pick(D, (512, 256, 128))
    (dn,) = _matmul(f"{tag}_dn", [da, du], [wg, wu], [(0, 0, 0), (1, 1, 0)], 1, lambda accs, _: accs,
                    [((L, D), F32) + _mn(tm, tdn)], M=L, N=D, K=F, tm=tm, tn=tdn,
                    tk=_pick(F, (1408, 512, 256, 128)), tb=True)
    tmw = _pick(D, (1024, 512, 256, 128))
    shard_out = ((N_CHIPS, D, cs_ff), F32, (None, tmw, cs_ff), lambda i, j: (j, i, 0))
    dwg, dwu = _matmul(f"{tag}_dwgu", [n], [da, du], [(0, 0, 0), (0, 1, 1)], 2, lambda accs, _: accs,
                       [shard_out] * 2, M=D, N=F, K=L, tm=tmw, tn=cs_ff, tk=_pick(L, (384, 256, 128)), ta=True)
    dh_in, dh_in_bf, dgain = _rmsnorm_bwd(f"{tag}_dnorm", dn, h, gain, dh, tm)
    return dh_in, dh_in_bf, dgain, dwg, dwu, dwd


def kernel(x, meta_tokens, ffn1_norm, ffn1_w_gate, ffn1_w_up, ffn1_w_down, mix_norm, w_in, b_forget, fox_q_norm, fox_k_norm, w_branch_fox, w_branch_sb, w_out, ffn2_norm, ffn2_w_gate, ffn2_w_up, ffn2_w_down, loss_target, m_meta_tokens, m_ffn1_norm, m_ffn1_w_gate, m_ffn1_w_up, m_ffn1_w_down, m_mix_norm, m_w_in, m_b_forget, m_fox_q_norm, m_fox_k_norm, m_w_branch_fox, m_w_branch_sb, m_w_out, m_ffn2_norm, m_ffn2_w_gate, m_ffn2_w_up, m_ffn2_w_down, v_meta_tokens, v_ffn1_norm, v_ffn1_w_gate, v_ffn1_w_up, v_ffn1_w_down, v_mix_norm, v_w_in, v_b_forget, v_fox_q_norm, v_fox_k_norm, v_w_branch_fox, v_w_branch_sb, v_w_out, v_ffn2_norm, v_ffn2_w_gate, v_ffn2_w_up, v_ffn2_w_down):
    weights = dict(meta_tokens=meta_tokens, ffn1_norm=ffn1_norm, ffn1_w_gate=ffn1_w_gate, ffn1_w_up=ffn1_w_up,
                   ffn1_w_down=ffn1_w_down, mix_norm=mix_norm, w_in=w_in, b_forget=b_forget, fox_q_norm=fox_q_norm,
                   fox_k_norm=fox_k_norm, w_branch_fox=w_branch_fox, w_branch_sb=w_branch_sb, w_out=w_out,
                   ffn2_norm=ffn2_norm, ffn2_w_gate=ffn2_w_gate, ffn2_w_up=ffn2_w_up, ffn2_w_down=ffn2_w_down)
    moments_m = dict(meta_tokens=m_meta_tokens, ffn1_norm=m_ffn1_norm, ffn1_w_gate=m_ffn1_w_gate,
                     ffn1_w_up=m_ffn1_w_up, ffn1_w_down=m_ffn1_w_down, mix_norm=m_mix_norm, w_in=m_w_in,
                     b_forget=m_b_forget, fox_q_norm=m_fox_q_norm, fox_k_norm=m_fox_k_norm,
                     w_branch_fox=m_w_branch_fox, w_branch_sb=m_w_branch_sb, w_out=m_w_out, ffn2_norm=m_ffn2_norm,
                     ffn2_w_gate=m_ffn2_w_gate, ffn2_w_up=m_ffn2_w_up, ffn2_w_down=m_ffn2_w_down)
    moments_v = dict(meta_tokens=v_meta_tokens, ffn1_norm=v_ffn1_norm, ffn1_w_gate=v_ffn1_w_gate,
                     ffn1_w_up=v_ffn1_w_up, ffn1_w_down=v_ffn1_w_down, mix_norm=v_mix_norm, w_in=v_w_in,
                     b_forget=v_b_forget, fox_q_norm=v_fox_q_norm, fox_k_norm=v_fox_k_norm,
                     w_branch_fox=v_w_branch_fox, w_branch_sb=v_w_branch_sb, w_out=v_w_out, ffn2_norm=v_ffn2_norm,
                     ffn2_w_gate=v_ffn2_w_gate, ffn2_w_up=v_ffn2_w_up, ffn2_w_down=v_ffn2_w_down)
    names = list(weights)

    _, S, D = x.shape
    NM = meta_tokens.shape[0]
    L_real = NM + S
    L = _round_up(L_real, ATT_TILE)
    nblk = L // ATT_TILE
    nbp = _round_up(nblk, SUBLANES)
    cs_ff = ffn1_w_gate.shape[2]
    F = N_CHIPS * cs_ff
    H = b_forget.shape[1]
    FW = w_branch_fox.shape[1]
    SW = w_branch_sb.shape[1]
    HS = SW // HEAD_DIM
    cs_in = w_in.shape[2]
    W_IN = N_CHIPS * cs_in
    assert FW == H * HEAD_DIM and W_IN == 3 * FW + H + 3 * SW + 2 * D
    cs_d = D // N_CHIPS
    tm = _pick(L, (384, 256, 128))

    x_pos, y_pos, c_pos = _mesh_pos()
    pos = jnp.stack([c_pos, 2 * x_pos + y_pos]).astype(I32)

    shard_of = {
        "ffn1_w_gate": (ffn1_w_gate[0], "col"), "ffn1_w_up": (ffn1_w_up[0], "col"),
        "ffn1_w_down": (ffn1_w_down[0], "maj"), "w_in": (w_in[0], "maj"),
        "w_branch_fox": (w_branch_fox[0], "col"), "w_branch_sb": (w_branch_sb[0], "col"),
        "w_out": (w_out[0], "maj"), "ffn2_w_gate": (ffn2_w_gate[0], "col"), "ffn2_w_up": (ffn2_w_up[0], "col"),
        "ffn2_w_down": (ffn2_w_down[0], "maj"),
    }
    g_names = list(shard_of) + ["meta_tokens"]
    shards = [shard_of[k][0].astype(BF) for k in shard_of] + [meta_tokens]
    kinds = [shard_of[k][1] for k in shard_of] + ["col"]
    gathered = _all_gather_ici("gather_weights_ici", shards, kinds)
    gathered = _all_gather_d2d("gather_weights_d2d", gathered, shards, kinds)
    full = dict(zip(g_names, gathered))
    wg1, wu1, wg2, wu2 = full["ffn1_w_gate"], full["ffn1_w_up"], full["ffn2_w_gate"], full["ffn2_w_up"]
    wd1 = full["ffn1_w_down"].reshape(F, D)
    wd2 = full["ffn2_w_down"].reshape(F, D)
    wbf, wbs = full["w_branch_fox"], full["w_branch_sb"]
    wo = full["w_out"].reshape(D, D)
    w_in_full = jnp.transpose(full["w_in"], (1, 0, 2)).reshape(D, W_IN)
    c_f = 3 * FW
    QKV_S, GATES, FCOL = 3 * FW, 3 * FW + 3 * SW, 3 * FW + 3 * SW + 2 * D
    W_PROJ = FCOL + LANES
    w_proj = jnp.concatenate([w_in_full[:, :c_f], w_in_full[:, c_f + H:],
                              jnp.pad(w_in_full[:, c_f:c_f + H], ((0, 0), (0, LANES - H)))], axis=1)

    h0 = jnp.concatenate([full["meta_tokens"], x[0], jnp.zeros((L - L_real, D), F32)], axis=0)
    target = jnp.concatenate([jnp.zeros((NM, D), F32), loss_target[0], jnp.zeros((L - L_real, D), F32)], axis=0)

    h1, saved1 = _ffn_fwd("ffn1", h0, ffn1_norm, wg1, wu1, wd1, tm)

    n2 = _rmsnorm_fwd("mix_norm", h1, mix_norm, tm)
    tp = _pick(FCOL, (512, 256, 128))
    (proj,) = _matmul("in_proj", [n2], [w_proj], [(0, 0, 0)], 1, lambda accs, _: accs,
                      [((L, FCOL), BF) + _mn(tm, tp)], M=L, N=FCOL, K=D, tm=tm, tn=tp, tk=D, n_outer=True)
    (f_logit,) = _matmul("forget_proj", [n2], [w_proj], [(0, 0, 0)], 1, lambda accs, _: accs,
                         [((L, LANES), F32) + _mn(tm, LANES)], M=L, N=LANES, K=D, tm=tm, tn=LANES, tk=D,
                         b_off=FCOL // LANES)
    fl = jnp.pad(jnp.transpose(f_logit[:, :H]).reshape(H, nblk, LANES), ((0, 0), (0, nbp - nblk), (0, 0)))
    bias = jnp.broadcast_to(b_forget[0][:, None, None], (H, 1, LANES))
    c_row = _cum_fwd("forget_cumsum", fl, bias, nblk)
    c_col = c_row[:, :nblk].reshape(H, L, 1)
    gq, gk = fox_q_norm[0][:, None, :], fox_k_norm[0][:, None, :]
    qn, kn = _qknorm_fwd("fox_qk_norm", proj, gq, gk, H, 0, FW, tm)
    o_fox, lse = _fox_fwd("fox_attention", qn, kn, proj, c_row, c_col, H, 0, 0, 2 * FW)
    o_sb = _sb_fwd("sb_attention", proj, HS, QKV_S, QKV_S + SW, QKV_S + 2 * SW)

    td = _pick(D, (512, 256, 128))

    def merge(accs, ex):
        bf_, bs_ = accs
        return _sigmoid(ex[0].astype(F32)) * bf_ + _sigmoid(ex[1].astype(F32)) * bs_, bf_, bs_

    merged, br_f, br_s = _matmul("branch_merge", [o_fox, o_sb], [wbf, wbs], [(0, 0, 0), (1, 1, 1)], 2, merge,
                                 [((L, D), BF) + _mn(tm, td)] * 3, M=L, N=D, K=FW, tm=tm, tn=td, tk=FW,
                                 extras=[(proj,) + _mn(tm, td, GATES), (proj,) + _mn(tm, td, GATES + D)],
                                 n_outer=True)
    (h2,) = _matmul("out_proj", [merged], [wo], [(0, 0, 0)], 1, lambda accs, ex: [ex[0] + accs[0]],
                    [((L, D), F32) + _mn(tm, td)], M=L, N=D, K=D, tm=tm, tn=td, tk=D, extras=[(h1,) + _mn(tm, td)],
                    n_outer=True)

    h3, saved2 = _ffn_fwd("ffn2", h2, ffn2_norm, wg2, wu2, wd2, tm)
    dh3, dh3_bf, loss_part = _loss_grad("loss", h3, target, NM, S, tm)

    dh2, dh2_bf, dg_ffn2, dwg2, dwu2, dwd2 = _ffn_bwd("ffn2", dh3, dh3_bf, saved2, ffn2_norm, wg2, wu2, wd2, tm,
                                                      cs_ff)

    def gate_grad(accs, ex):
        dm = accs[0]
        gf, gs, bf_, bs_ = [e.astype(F32) for e in ex]
        sf, ss = _sigmoid(gf), _sigmoid(gs)
        return dm * bf_ * sf * (1.0 - sf), dm * bs_ * ss * (1.0 - ss), dm * sf, dm * ss

    mn_d = _mn(tm, td)
    dgf, dgs, dbr_f, dbr_s = _matmul(
        "d_merged", [dh2_bf], [wo], [(0, 0, 0)], 1, gate_grad, [((L, D), BF) + mn_d] * 4, M=L, N=D, K=D, tm=tm,
        tn=td, tk=D, tb=True, extras=[(proj,) + _mn(tm, td, GATES), (proj,) + _mn(tm, td, GATES + D),
                                      (br_f,) + mn_d, (br_s,) + mn_d], n_outer=True)
    tkl = _pick(L, (1408, 384, 256, 128))
    (dwo,) = _matmul("d_w_out", [merged], [dh2_bf], [(0, 0, 0)], 1, lambda accs, _: accs,
                     [((N_CHIPS, cs_d, D), F32, (None, cs_d, td), lambda i, j: (i, 0, j))], M=D, N=D, K=L, tm=cs_d,
                     tn=td, tk=tkl, ta=True)
    tw = _pick(FW, (512, 256, 128))
    do_fox, do_sb = _matmul("d_branch_in", [dbr_f, dbr_s], [wbf, wbs], [(0, 0, 0), (1, 1, 1)], 2,
                            lambda accs, _: accs, [((L, FW), BF) + _mn(tm, tw)] * 2, M=L, N=FW, K=D, tm=tm, tn=tw,
                            tk=D, tb=True)
    tmb = _pick(FW, (1024, 512, 256, 128))
    dwbf, dwbs = _matmul("d_w_branch", [o_fox, o_sb], [dbr_f, dbr_s], [(0, 0, 0), (1, 1, 1)], 2,
                         lambda accs, _: accs,
                         [((N_CHIPS, FW, cs_d), F32, (None, tmb, cs_d), lambda i, j: (j, i, 0))] * 2, M=FW, N=D, K=L,
                         tm=tmb, tn=cs_d, tk=_pick(L, (384, 256, 128)), ta=True)

    dqn, dkn, dfv, dcs = _fox_bwd("fox_attention_bwd", qn, kn, proj, c_row, c_col, o_fox, do_fox, lse, H, 0, 0,
                                  2 * FW)
    dsq, dsk, dsv = _sb_bwd("sb_attention_bwd", proj, do_sb, HS, QKV_S, QKV_S + SW, QKV_S + 2 * SW)
    dfq, dfk, dgq, dgk = _qknorm_bwd("fox_qk_norm_bwd", proj, dqn, dkn, gq, gk, H, 0, FW, tm)
    dfl, dbias = _cum_bwd("forget_cumsum_bwd", dcs, fl, bias, nblk)
    dfl_cols = jnp.pad(jnp.transpose(dfl[:, :nblk].reshape(H, L)), ((0, 0), (0, LANES - H))).astype(BF)
    dproj = jnp.concatenate([dfq, dfk, dfv, dsq, dsk, dsv, dgf, dgs, dfl_cols], axis=1)

    tkp = _pick(W_PROJ, (1152, 640, 512, 384, 256, 128))
    (dn2,) = _matmul("d_mix_norm_in", [dproj], [w_proj], [(0, 0, 0)], 1, lambda accs, _: accs,
                     [((L, D), F32) + _mn(tm, td)], M=L, N=D, K=W_PROJ, tm=tm, tn=td, tk=tkp, tb=True)
    tmw = _pick(D, (1024, 512, 256, 128))
    tnp = _pick(W_PROJ, (1152, 640, 512, 384, 256, 128))
    (dw_proj,) = _matmul("d_w_in", [n2], [dproj], [(0, 0, 0)], 1, lambda accs, _: accs,
                         [((D, W_PROJ), F32) + _mn(tmw, tnp)], M=D, N=W_PROJ, K=L, tm=tmw, tn=tnp,
                         tk=_pick(L, (384, 256, 128)), ta=True)
    dh1, dh1_bf, dg_mix = _rmsnorm_bwd("mix_dnorm", dn2, h1, mix_norm, dh2, tm)

    dh0, _, dg_ffn1, dwg1, dwu1, dwd1 = _ffn_bwd("ffn1", dh1, dh1_bf, saved1, ffn1_norm, wg1, wu1, wd1, tm, cs_ff)
    grad_x = dh0[NM:L_real][None]

    dw_in_ref = jnp.concatenate([dw_proj[:, :c_f], dw_proj[:, FCOL:FCOL + H], dw_proj[:, c_f:FCOL]], axis=1)
    dw_in = jnp.transpose(dw_in_ref.reshape(D, N_CHIPS, cs_in), (1, 0, 2))
    big = ["ffn1_w_gate", "ffn1_w_up", "ffn1_w_down", "w_in", "w_branch_fox", "w_branch_sb", "w_out",
           "ffn2_w_gate", "ffn2_w_up", "ffn2_w_down"]
    local = [dwg1, dwu1, dwd1, dw_in, dwbf, dwbs, dwo, dwg2, dwu2, dwd2]
    from_sibling = _reduce_to_sibling("grads_to_sibling", local)
    partials = [_chip_partial(f"chip_sum_{k}", pos, g, r) for k, g, r in zip(big, local, from_sibling)]
    from_chips = _exchange_chip_partials("grads_to_owner", partials)
    halves = [_final_half(f"total_{k}", pos, g, r, o) for k, g, r, o in zip(big, local, from_sibling, from_chips)]
    grads = dict(zip(big, _share_halves("grads_to_core_pair", halves)))

    small = [loss_part[:, :1].reshape(1), dh0[:NM].reshape(-1), dg_ffn1.reshape(-1), dg_mix.reshape(-1),
             dg_ffn2.reshape(-1), dbias[:, 0, 0], dgq.reshape(-1), dgk.reshape(-1)]
    sizes = [s.shape[0] for s in small]
    flat = jnp.concatenate(small)
    rows = _round_up(-(-flat.shape[0] // LANES), SUBLANES)
    packed = jnp.pad(flat, (0, rows * LANES - flat.shape[0])).reshape(rows, LANES)
    total = _sum_devices("sum_small", _gather_small("gather_small", packed)).reshape(-1)
    pieces, off = [], 0
    for n_el in sizes:
        pieces.append(total[off:off + n_el])
        off += n_el
    loss = pieces[0][0]
    d_meta = lax.dynamic_slice_in_dim(pieces[1].reshape(NM, D), pos[1] * cs_d, cs_d, axis=1)
    grads.update(meta_tokens=d_meta, ffn1_norm=pieces[2].reshape(1, D), mix_norm=pieces[3].reshape(1, D),
                 ffn2_norm=pieces[4].reshape(1, D), b_forget=pieces[5].reshape(1, H),
                 fox_q_norm=pieces[6].reshape(1, H, HEAD_DIM), fox_k_norm=pieces[7].reshape(1, H, HEAD_DIM))

    out_g, out_d, out_m, out_v = [], [], [], []
    for k in names:
        w = weights[k]
        shape2 = (1, w.size) if w.size < LANES * SUBLANES else (w.size // w.shape[-1], w.shape[-1])
        g2 = grads[k].reshape(shape2)
        d, nm, nv = _adamw(f"adamw_{k}", w.reshape(shape2), g2, moments_m[k].reshape(shape2),
                           moments_v[k].reshape(shape2))
        out_g.append(g2.reshape(w.shape))
        out_d.append(d.reshape(w.shape))
        out_m.append(nm.reshape(w.shape))
        out_v.append(nv.reshape(w.shape))
    return (loss, grad_x, *out_g, *out_d, *out_m, *out_v)
```

```python
import functools

import jax
import jax.numpy as jnp
from jax import lax
from jax.experimental import pallas as pl
from jax.experimental.pallas import tpu as pltpu

F32 = jnp.float32
BF = jnp.bfloat16
I32 = jnp.int32

HEAD_DIM = 128
RMS_EPS = 1e-6
FFN_RESIDUAL_WEIGHT = 0.5
ADAM_LR = 0.001
ADAM_B1 = 0.9
ADAM_B2 = 0.999
ADAM_EPS = 1e-08
ADAM_WD = 0.01
ADAM_STEP = 10

LANES = 128
SUBLANES = 8
ATT_TILE = 128
ATT_HEADS = 4
VMEM_CAP = 56 * 1024 * 1024
MESH_IDS = pl.DeviceIdType.MESH
N_CHIPS = 4
N_DEV = 8


def _pick(n, cands):
    for c in cands:
        if c <= n and n % c == 0:
            return c
    raise ValueError(f"no tile for {n} among {cands}")


def _round_up(n, m):
    return (n + m - 1) // m * m


def _tile_bytes(shape, dtype):
    item = jnp.dtype(dtype).itemsize
    dims = [d for d in shape if d is not None]
    if not dims:
        return 4 * LANES * SUBLANES
    last = _round_up(dims[-1], LANES)
    sub = _round_up(dims[-2], SUBLANES * (4 // item)) if len(dims) > 1 else 1
    lead = 1
    for d in dims[:-2]:
        lead *= d
    return lead * sub * last * item


def _vmem_limit(blocks, scratch=(), temps=0):
    need = 2 * sum(_tile_bytes(s, d) for s, d in blocks) + sum(_tile_bytes(s, d) for s, d in scratch) + temps
    return int(min(VMEM_CAP, max(need + (4 << 20), 16 << 20)))


def _call(body, *, name, out_shape, grid=(), in_specs=None, out_specs=None, scratch=(), sem=None, vmem=None,
          aliases=None, prefetch=0):
    params = pltpu.CompilerParams(dimension_semantics=sem, vmem_limit_bytes=vmem)
    if prefetch:
        grid_spec = pltpu.PrefetchScalarGridSpec(num_scalar_prefetch=prefetch, grid=grid, in_specs=in_specs,
                                                 out_specs=out_specs, scratch_shapes=scratch)
        return pl.pallas_call(body, out_shape=out_shape, grid_spec=grid_spec, name=name, compiler_params=params,
                              input_output_aliases=aliases or {})
    return pl.pallas_call(body, out_shape=out_shape, grid=grid, in_specs=in_specs, out_specs=out_specs,
                          scratch_shapes=scratch, name=name, compiler_params=params,
                          input_output_aliases=aliases or {})


def _dot(a, b, ca, cb):
    return lax.dot_general(a, b, (((ca,), (cb,)), ((), ())), preferred_element_type=F32)


def _sigmoid(x):
    return 1.0 / (1.0 + jnp.exp(-x))


def _log_sigmoid(x):
    return jnp.minimum(x, 0.0) - jnp.log1p(jnp.exp(-jnp.abs(x)))


def _split(x, parts):
    pieces = []
    rem = x
    for p in range(parts):
        piece = rem.astype(BF)
        pieces.append(piece)
        if p + 1 < parts:
            rem = rem - piece.astype(F32)
    return pieces


def _pieces_dot(pieces, ones_bf):
    out = None
    for piece in pieces:
        d = _dot(piece, ones_bf, 1, 0)
        out = d if out is None else out + d
    return out


def _split_dot(x, ones_bf, parts):
    return _pieces_dot(_split(x, parts), ones_bf)


def _matmul(name, a_list, b_list, pairs, n_acc, epi, outs, *, M, N, K, tm, tn, tk, ta=False, tb=False,
            extras=(), n_outer=False, b_off=0):
    gi, gj, nk = M // tm, N // tn, K // tk
    assert gi * tm == M and gj * tn == N and nk * tk == K, (name, M, N, K, tm, tn, tk)
    n_a, n_b, n_e, n_o = len(a_list), len(b_list), len(extras), len(outs)

    def ij(g0, g1):
        return (g1, g0) if n_outer else (g0, g1)

    def a_map(g0, g1, k):
        i, _ = ij(g0, g1)
        return (k, i) if ta else (i, k)

    def b_map(g0, g1, k):
        _, j = ij(g0, g1)
        return (j + b_off, k) if tb else (k, j + b_off)

    def tile_map(fn):
        return lambda g0, g1, k: fn(*ij(g0, g1))

    a_block = (tk, tm) if ta else (tm, tk)
    b_block = (tn, tk) if tb else (tk, tn)
    in_specs = ([pl.BlockSpec(a_block, a_map)] * n_a + [pl.BlockSpec(b_block, b_map)] * n_b
                + [pl.BlockSpec(bs, tile_map(fn)) for _, bs, fn in extras])
    out_specs = [pl.BlockSpec(bs, tile_map(fn)) for _, _, bs, fn in outs]
    out_shape = [jax.ShapeDtypeStruct(s, d) for s, d, _, _ in outs]
    scratch = [pltpu.VMEM((tm, tn), F32) for _ in range(n_acc)] if nk > 1 else []

    def body(*refs):
        a_refs = refs[:n_a]
        b_refs = refs[n_a:n_a + n_b]
        e_refs = refs[n_a + n_b:n_a + n_b + n_e]
        o_refs = refs[n_a + n_b + n_e:n_a + n_b + n_e + n_o]
        acc_refs = refs[n_a + n_b + n_e + n_o:]

        def products():
            accs = [None] * n_acc
            for ai, bi, ci in pairs:
                a = a_refs[ai][...]
                b = b_refs[bi][...]
                d = _dot(a.astype(BF), b.astype(BF), 0 if ta else 1, 1 if tb else 0)
                accs[ci] = d if accs[ci] is None else accs[ci] + d
            return accs

        def finish(accs):
            res = epi(accs, [e[...] for e in e_refs])
            for o_ref, r in zip(o_refs, res):
                o_ref[...] = r.reshape(o_ref.shape).astype(o_ref.dtype)

        if nk == 1:
            finish(products())
        else:
            k = pl.program_id(2)

            @pl.when(k == 0)
            def _():
                for acc in acc_refs:
                    acc[...] = jnp.zeros_like(acc)

            for acc, d in zip(acc_refs, products()):
                acc[...] += d

            @pl.when(k == nk - 1)
            def _():
                finish([acc[...] for acc in acc_refs])

    blocks = ([(a_block, a.dtype) for a in a_list] + [(b_block, b.dtype) for b in b_list]
              + [(bs, e.dtype) for e, bs, _ in extras] + [(bs, d) for _, d, bs, _ in outs])
    vmem = _vmem_limit(blocks, [((tm, tn), F32)] * (n_acc if nk > 1 else 0), temps=6 * tm * tn * 4)
    grid = (gj, gi, nk) if n_outer else (gi, gj, nk)
    fn = _call(body, name=name, out_shape=out_shape, grid=grid, in_specs=in_specs, out_specs=out_specs,
               scratch=scratch, sem=("parallel", "parallel", "arbitrary"), vmem=vmem)
    return fn(*a_list, *b_list, *[e for e, _, _ in extras])


def _mn(tm, tn, col0=0):
    assert col0 % tn == 0
    off = col0 // tn
    return (tm, tn), (lambda i, j: (i, j + off))


def _rmsnorm_fwd(name, x, gain, tr):
    L, D = x.shape

    def body(x_ref, g_ref, o_ref):
        xv = x_ref[...]
        r = lax.rsqrt(jnp.mean(xv * xv, axis=-1, keepdims=True) + RMS_EPS)
        o_ref[...] = (xv * r * g_ref[...]).astype(BF)

    row = pl.BlockSpec((tr, D), lambda i: (i, 0))
    vec = pl.BlockSpec((1, D), lambda i: (0, 0))
    vmem = _vmem_limit([((tr, D), F32), ((tr, D), BF)], temps=3 * tr * D * 4)
    return _call(body, name=name, out_shape=jax.ShapeDtypeStruct((L, D), BF), grid=(L // tr,), in_specs=[row, vec],
                 out_specs=row, sem=("parallel",), vmem=vmem)(x, gain)


def _rmsnorm_bwd(name, dn, x, gain, dres, tr):
    L, D = x.shape
    steps = L // tr

    def body(dn_ref, x_ref, g_ref, dres_ref, dx_ref, dxb_ref, dg_ref):
        i = pl.program_id(0)
        xv = x_ref[...]
        r = lax.rsqrt(jnp.mean(xv * xv, axis=-1, keepdims=True) + RMS_EPS)
        xhat = xv * r
        dy = dn_ref[...]
        dxhat = dy * g_ref[...]
        dx = dres_ref[...] + r * (dxhat - xhat * jnp.mean(dxhat * xhat, axis=-1, keepdims=True))
        dx_ref[...] = dx
        dxb_ref[...] = dx.astype(BF)

        @pl.when(i == 0)
        def _():
            dg_ref[...] = jnp.zeros_like(dg_ref)

        dg_ref[...] += jnp.sum(dy * xhat, axis=0, keepdims=True)

    row = pl.BlockSpec((tr, D), lambda i: (i, 0))
    vec = pl.BlockSpec((1, D), lambda i: (0, 0))
    vmem = _vmem_limit([((tr, D), F32)] * 4 + [((tr, D), BF)], temps=4 * tr * D * 4)
    out_shape = [jax.ShapeDtypeStruct((L, D), F32), jax.ShapeDtypeStruct((L, D), BF),
                 jax.ShapeDtypeStruct((1, D), F32)]
    return _call(body, name=name, out_shape=out_shape, grid=(steps,), in_specs=[row, row, vec, row],
                 out_specs=[row, row, vec], sem=("arbitrary",), vmem=vmem)(dn, x, gain, dres)


def _loss_grad(name, h, target, n_meta, n_seq, tr):
    L, D = h.shape

    def body(h_ref, t_ref, dh_ref, dhb_ref, loss_ref):
        i = pl.program_id(0)
        rows = i * tr + lax.broadcasted_iota(I32, (tr, 1), 0)
        valid = (rows >= n_meta) & (rows < n_meta + n_seq)
        diff = jnp.where(valid, h_ref[...] - t_ref[...], 0.0)
        dh = diff * (1.0 / D)
        dh_ref[...] = dh
        dhb_ref[...] = dh.astype(BF)

        @pl.when(i == 0)
        def _():
            loss_ref[...] = jnp.zeros_like(loss_ref)

        loss_ref[...] += jnp.sum(diff * diff) * (0.5 / D)

    row = pl.BlockSpec((tr, D), lambda i: (i, 0))
    acc = pl.BlockSpec((1, LANES), lambda i: (0, 0))
    vmem = _vmem_limit([((tr, D), F32)] * 3 + [((tr, D), BF)], temps=3 * tr * D * 4)
    out_shape = [jax.ShapeDtypeStruct((L, D), F32), jax.ShapeDtypeStruct((L, D), BF),
                 jax.ShapeDtypeStruct((1, LANES), F32)]
    return _call(body, name=name, out_shape=out_shape, grid=(L // tr,), in_specs=[row, row],
                 out_specs=[row, row, acc], sem=("arbitrary",), vmem=vmem)(h, target)


def _qknorm_fwd(name, proj, gq, gk, heads, q_col, k_col, tr):
    L = proj.shape[0]

    def body(q_ref, k_ref, gq_ref, gk_ref, qn_ref, kn_ref):
        for x_ref, g_ref, o_ref in ((q_ref, gq_ref, qn_ref), (k_ref, gk_ref, kn_ref)):
            xv = x_ref[...].astype(F32)
            r = lax.rsqrt(jnp.mean(xv * xv, axis=-1, keepdims=True) + RMS_EPS)
            o_ref[...] = (xv * r * g_ref[...]).astype(BF)

    qb, kb = q_col // HEAD_DIM, k_col // HEAD_DIM
    in_specs = [pl.BlockSpec((tr, HEAD_DIM), lambda h, i: (i, qb + h)),
                pl.BlockSpec((tr, HEAD_DIM), lambda h, i: (i, kb + h)),
                pl.BlockSpec((None, 1, HEAD_DIM), lambda h, i: (h, 0, 0)),
                pl.BlockSpec((None, 1, HEAD_DIM), lambda h, i: (h, 0, 0))]
    out = pl.BlockSpec((tr, HEAD_DIM), lambda h, i: (i, h))
    out_shape = [jax.ShapeDtypeStruct((L, heads * HEAD_DIM), BF)] * 2
    return _call(body, name=name, out_shape=out_shape, grid=(heads, L // tr), in_specs=in_specs,
                 out_specs=[out, out], sem=("parallel", "parallel"), vmem=16 << 20)(proj, proj, gq, gk)


def _qknorm_bwd(name, proj, dqn, dkn, gq, gk, heads, q_col, k_col, tr):
    L = proj.shape[0]

    def body(q_ref, k_ref, dqn_ref, dkn_ref, gq_ref, gk_ref, dq_ref, dk_ref, dgq_ref, dgk_ref):
        i = pl.program_id(1)
        for x_ref, dy_ref, g_ref, dx_ref, dg_ref in ((q_ref, dqn_ref, gq_ref, dq_ref, dgq_ref),
                                                     (k_ref, dkn_ref, gk_ref, dk_ref, dgk_ref)):
            xv = x_ref[...].astype(F32)
            r = lax.rsqrt(jnp.mean(xv * xv, axis=-1, keepdims=True) + RMS_EPS)
            xhat = xv * r
            dy = dy_ref[...].astype(F32)
            dxhat = dy * g_ref[...]
            dx_ref[...] = (r * (dxhat - xhat * jnp.mean(dxhat * xhat, axis=-1, keepdims=True))).astype(BF)

            @pl.when(i == 0)
            def _():
                dg_ref[...] = jnp.zeros_like(dg_ref)

            dg_ref[...] += jnp.sum(dy * xhat, axis=0, keepdims=True)

    qb, kb = q_col // HEAD_DIM, k_col // HEAD_DIM
    tile = pl.BlockSpec((tr, HEAD_DIM), lambda h, i: (i, h))
    gain = pl.BlockSpec((None, 1, HEAD_DIM), lambda h, i: (h, 0, 0))
    in_specs = [pl.BlockSpec((tr, HEAD_DIM), lambda h, i: (i, qb + h)),
                pl.BlockSpec((tr, HEAD_DIM), lambda h, i: (i, kb + h)), tile, tile, gain, gain]
    out_shape = [jax.ShapeDtypeStruct((L, heads * HEAD_DIM), BF)] * 2 + [
        jax.ShapeDtypeStruct((heads, 1, HEAD_DIM), F32)] * 2
    return _call(body, name=name, out_shape=out_shape, grid=(heads, L // tr), in_specs=in_specs,
                 out_specs=[tile, tile, gain, gain], sem=("parallel", "arbitrary"),
                 vmem=16 << 20)(proj, proj, dqn, dkn, gq, gk)


def _tri(cmp):
    r = lax.broadcasted_iota(I32, (LANES, LANES), 0)
    c = lax.broadcasted_iota(I32, (LANES, LANES), 1)
    return jnp.where(cmp(r, c), 1.0, 0.0).astype(BF)


def _cum_fwd(name, fl, bias, n_rows):
    H, nbp, _ = fl.shape

    def body(fl_ref, b_ref, c_ref, tot_ref):
        lf = _log_sigmoid(fl_ref[...] + b_ref[...])
        c_ref[...] = _split_dot(lf, _tri(lambda r, c: r <= c), 3)
        tot_ref[...] = _split_dot(lf, jnp.ones((LANES, LANES), BF), 3)

        def step(r, carry):
            c_ref[pl.ds(r, 1), :] = c_ref[pl.ds(r, 1), :] + carry
            return carry + tot_ref[pl.ds(r, 1), :]

        lax.fori_loop(0, n_rows, step, jnp.zeros((1, LANES), F32))

    blk = pl.BlockSpec((None, nbp, LANES), lambda h: (h, 0, 0))
    vec = pl.BlockSpec((None, 1, LANES), lambda h: (h, 0, 0))
    return _call(body, name=name, out_shape=jax.ShapeDtypeStruct((H, nbp, LANES), F32), grid=(H,),
                 in_specs=[blk, vec], out_specs=blk, scratch=[pltpu.VMEM((nbp, LANES), F32)], sem=("parallel",),
                 vmem=16 << 20)(fl, bias)


def _cum_bwd(name, drs, dcs, fl, bias, n_rows):
    H, nbp, _ = fl.shape

    def body(drs_ref, dcs_ref, fl_ref, b_ref, dfl_ref, db_ref, rin_ref, tot_ref):
        dc = drs_ref[...] - dcs_ref[...]
        rin_ref[...] = _split_dot(dc, _tri(lambda r, c: r >= c), 3)
        tot_ref[...] = _split_dot(dc, jnp.ones((LANES, LANES), BF), 3)
        dfl_ref[...] = jnp.zeros_like(dfl_ref)

        def step(t, carry):
            r = n_rows - 1 - t
            x = fl_ref[pl.ds(r, 1), :] + b_ref[...]
            dfl_ref[pl.ds(r, 1), :] = (rin_ref[pl.ds(r, 1), :] + carry) * _sigmoid(-x)
            return carry + tot_ref[pl.ds(r, 1), :]

        lax.fori_loop(0, n_rows, step, jnp.zeros((1, LANES), F32))
        db_ref[...] = jnp.zeros_like(db_ref) + jnp.sum(dfl_ref[...])

    blk = pl.BlockSpec((None, nbp, LANES), lambda h: (h, 0, 0))
    vec = pl.BlockSpec((None, 1, LANES), lambda h: (h, 0, 0))
    out_shape = [jax.ShapeDtypeStruct((H, nbp, LANES), F32), jax.ShapeDtypeStruct((H, 1, LANES), F32)]
    return _call(body, name=name, out_shape=out_shape, grid=(H,), in_specs=[blk, blk, blk, vec],
                 out_specs=[blk, vec], scratch=[pltpu.VMEM((nbp, LANES), F32)] * 2, sem=("parallel",),
                 vmem=16 << 20)(drs, dcs, fl, bias)


def _att_specs(L, G, q_col, k_col, v_col):
    T = ATT_TILE
    W = G * HEAD_DIM
    assert q_col % W == 0 and k_col % W == 0 and v_col % W == 0
    qb, kb, vb = q_col // W, k_col // W, v_col // W
    q_spec = pl.BlockSpec((T, W), lambda h, i: (i, qb + h))
    k_spec = pl.BlockSpec((L, W), lambda h, i: (0, kb + h), pipeline_mode=pl.Buffered(1))
    v_spec = pl.BlockSpec((L, W), lambda h, i: (0, vb + h), pipeline_mode=pl.Buffered(1))
    return q_spec, k_spec, v_spec


def _head_lanes(G):
    return [slice(g * HEAD_DIM, (g + 1) * HEAD_DIM) for g in range(G)]


def _tile_iotas():
    T = ATT_TILE
    return lax.broadcasted_iota(I32, (T, T), 0), lax.broadcasted_iota(I32, (T, T), 1)


def _rows(j):
    return pl.ds(pl.multiple_of(j * ATT_TILE, ATT_TILE), ATT_TILE)


def _fox_fwd(name, q_arr, k_arr, v_arr, c_row, c_col, heads, G, q_col, k_col, v_col):
    L = q_arr.shape[0]
    T = ATT_TILE
    scale = HEAD_DIM ** -0.5
    lanes = _head_lanes(G)

    def body(q_ref, k_ref, v_ref, crow_ref, ccol_ref, o_ref, lse_ref):
        i = pl.program_id(1)
        qs = [q_ref[:, hl] for hl in lanes]
        cts = [ccol_ref[g] for g in range(G)]
        row, col = _tile_iotas()

        def tile(j, carry, masked):
            qk = [_dot(qs[g], k_ref[_rows(j), hl], 1, 1) for g, hl in enumerate(lanes)]
            stats = []
            for g in range(G):
                m, l, _ = carry[g]
                s = qk[g] * scale + (cts[g] - crow_ref[g, pl.ds(j, 1), :])
                if masked:
                    s = jnp.where(col <= row, s, -jnp.inf)
                m_new = jnp.maximum(m, jnp.max(s, axis=1, keepdims=True))
                alpha = jnp.exp(m - m_new)
                p = jnp.exp(s - m_new)
                stats.append((m_new, alpha, alpha * l + jnp.sum(p, axis=1, keepdims=True), p.astype(BF)))
            pv = [_dot(stats[g][3], v_ref[_rows(j), hl], 1, 0) for g, hl in enumerate(lanes)]
            return tuple((stats[g][0], stats[g][2], stats[g][1] * carry[g][2] + pv[g]) for g in range(G))

        init = tuple((jnp.full((T, 1), -1e30, F32), jnp.zeros((T, 1), F32), jnp.zeros((T, HEAD_DIM), F32))
                     for _ in range(G))
        carry = lax.fori_loop(0, i, lambda j, c: tile(j, c, False), init)
        for g, (m, l, acc) in enumerate(tile(i, carry, True)):
            o_ref[:, lanes[g]] = (acc / l).astype(o_ref.dtype)
            lse_ref[g] = m + jnp.log(l)

    nbp = c_row.shape[1]
    W = G * HEAD_DIM
    q_spec, k_spec, v_spec = _att_specs(L, G, q_col, k_col, v_col)
    crow_spec = pl.BlockSpec((G, nbp, LANES), lambda h, i: (h, 0, 0))
    col_spec = pl.BlockSpec((G, T, 1), lambda h, i: (h, i, 0))
    o_spec = pl.BlockSpec((T, W), lambda h, i: (i, h))
    out_shape = [jax.ShapeDtypeStruct((L, heads * HEAD_DIM), BF), jax.ShapeDtypeStruct((heads, L, 1), F32)]
    vmem = _vmem_limit([((L, W), BF)] * 2, temps=8 << 20)
    return _call(body, name=name, out_shape=out_shape, grid=(heads // G, L // T),
                 in_specs=[q_spec, k_spec, v_spec, crow_spec, col_spec], out_specs=[o_spec, col_spec],
                 sem=("parallel", "parallel"), vmem=vmem)(q_arr, k_arr, v_arr, c_row, c_col)


def _fox_bwd(name, q_arr, k_arr, v_arr, c_row, c_col, o, do, lse, heads, G, q_col, k_col, v_col):
    L = q_arr.shape[0]
    T = ATT_TILE
    nq = L // T
    scale = HEAD_DIM ** -0.5
    lanes = _head_lanes(G)

    def body(q_ref, k_ref, v_ref, crow_ref, ccol_ref, o_ref, do_ref, lse_ref, dq_ref, dk_acc, dv_acc, dcs_ref,
             drs_ref):
        i = pl.program_id(1)

        @pl.when(i == 0)
        def _():
            dk_acc[...] = jnp.zeros_like(dk_acc)
            dv_acc[...] = jnp.zeros_like(dv_acc)
            dcs_ref[...] = jnp.zeros_like(dcs_ref)

        qs = [q_ref[:, hl] for hl in lanes]
        dos = [do_ref[:, hl] for hl in lanes]
        deltas = [jnp.sum(dos[g].astype(F32) * o_ref[:, hl].astype(F32), axis=1, keepdims=True)
                  for g, hl in enumerate(lanes)]
        lses = [lse_ref[g] for g in range(G)]
        cts = [ccol_ref[g] for g in range(G)]
        row, col = _tile_iotas()

        def tile(j, carry, masked):
            ks = [k_ref[_rows(j), hl] for hl in lanes]
            qk = [_dot(qs[g], ks[g], 1, 1) for g in range(G)]
            dp = [_dot(dos[g], v_ref[_rows(j), hl], 1, 1) for g, hl in enumerate(lanes)]
            pbs, dsbs, row_sums = [], [], []
            for g in range(G):
                s = qk[g] * scale + (cts[g] - crow_ref[g, pl.ds(j, 1), :])
                if masked:
                    s = jnp.where(col <= row, s, -jnp.inf)
                p = jnp.exp(s - lses[g])
                ds = p * (dp[g] - deltas[g])
                dcs_ref[g, pl.ds(j, 1), :] += jnp.sum(ds, axis=0, keepdims=True)
                row_sums.append(carry[g][1] + jnp.sum(ds, axis=1, keepdims=True))
                pbs.append(p.astype(BF))
                dsbs.append((ds * scale).astype(BF))
            for g, hl in enumerate(lanes):
                dk_acc[_rows(j), hl] += _dot(dsbs[g], qs[g], 0, 0)
            for g, hl in enumerate(lanes):
                dv_acc[_rows(j), hl] += _dot(pbs[g], dos[g], 0, 0)
            return tuple((carry[g][0] + _dot(dsbs[g], ks[g], 1, 0), row_sums[g]) for g in range(G))

        init = tuple((jnp.zeros((T, HEAD_DIM), F32), jnp.zeros((T, 1), F32)) for _ in range(G))
        carry = lax.fori_loop(0, i, lambda j, c: tile(j, c, False), init)
        for g, (dq, row_sum) in enumerate(tile(i, carry, True)):
            dq_ref[:, lanes[g]] = dq.astype(dq_ref.dtype)
            drs_ref[g] = row_sum

    nbp = c_row.shape[1]
    WG = G * HEAD_DIM
    q_spec, k_spec, v_spec = _att_specs(L, G, q_col, k_col, v_col)
    crow_spec = pl.BlockSpec((G, nbp, LANES), lambda h, i: (h, 0, 0))
    col_spec = pl.BlockSpec((G, T, 1), lambda h, i: (h, i, 0))
    t_spec = pl.BlockSpec((T, WG), lambda h, i: (i, h))
    head_spec = pl.BlockSpec((L, WG), lambda h, i: (0, h), pipeline_mode=pl.Buffered(1))
    W = heads * HEAD_DIM
    out_shape = [jax.ShapeDtypeStruct((L, W), F32)] * 3 + [jax.ShapeDtypeStruct((heads, nbp, LANES), F32),
                                                           jax.ShapeDtypeStruct((heads, L, 1), F32)]
    vmem = _vmem_limit([], [((L, WG), BF)] * 2 + [((L, WG), F32)] * 2, temps=10 << 20)
    return _call(body, name=name, out_shape=out_shape, grid=(heads // G, nq),
                 in_specs=[q_spec, k_spec, v_spec, crow_spec, col_spec, t_spec, t_spec, col_spec],
                 out_specs=[t_spec, head_spec, head_spec, crow_spec, col_spec],
                 sem=("parallel", "arbitrary"), vmem=vmem)(q_arr, k_arr, v_arr, c_row, c_col, o, do, lse)


def _sb_logits(qk, scale, valid):
    z = qk * scale
    lb = jnp.minimum(z, 0.0) - jnp.log1p(jnp.exp(-jnp.abs(z)))
    lom = lb - z
    if valid is not None:
        lom = jnp.where(valid, lom, 0.0)
    return lb, lom


def _sb_fwd(name, proj, heads, G, q_col, k_col, v_col):
    L = proj.shape[0]
    T = ATT_TILE
    scale = HEAD_DIM ** -0.5
    lanes = _head_lanes(G)

    def body(q_ref, k_ref, v_ref, o_ref):
        i = pl.program_id(1)
        qs = [q_ref[:, hl] for hl in lanes]
        row, col = _tile_iotas()
        later_mat = jnp.where(row > col, 1.0, 0.0).astype(BF)

        def tile(j, carry, masked):
            valid = (col < row) if masked else None
            qk = [_dot(qs[g], k_ref[_rows(j), hl], 1, 1) for g, hl in enumerate(lanes)]
            logits = [_sb_logits(qk[g], scale, valid) for g in range(G)]
            pieces = [_split(lom, 2) for _, lom in logits]
            later = [_pieces_dot(pieces[g], later_mat) for g in range(G)]
            ws = []
            for g in range(G):
                w = jnp.exp(logits[g][0] + later[g] + carry[g][0])
                if masked:
                    w = jnp.where(valid, w, 0.0)
                ws.append(w.astype(BF))
            wv = [_dot(ws[g], v_ref[_rows(j), hl], 1, 0) for g, hl in enumerate(lanes)]
            return tuple((carry[g][0] + jnp.sum(logits[g][1], axis=1, keepdims=True), carry[g][1] + wv[g])
                         for g in range(G))

        init = tuple((jnp.zeros((T, 1), F32), jnp.zeros((T, HEAD_DIM), F32)) for _ in range(G))
        carry = tile(i, init, True)
        carry = lax.fori_loop(0, i, lambda t, c: tile(i - 1 - t, c, False), carry)
        for g, (_, acc) in enumerate(carry):
            o_ref[:, lanes[g]] = acc.astype(o_ref.dtype)

    W = G * HEAD_DIM
    q_spec, k_spec, v_spec = _att_specs(L, G, q_col, k_col, v_col)
    o_spec = pl.BlockSpec((T, W), lambda h, i: (i, h))
    vmem = _vmem_limit([((L, W), BF)] * 2, temps=8 << 20)
    return _call(body, name=name, out_shape=jax.ShapeDtypeStruct((L, heads * HEAD_DIM), BF),
                 grid=(heads // G, L // T), in_specs=[q_spec, k_spec, v_spec], out_specs=o_spec,
                 sem=("parallel", "parallel"), vmem=vmem)(proj, proj, proj)


def _sb_bwd(name, proj, do, heads, G, q_col, k_col, v_col):
    L = proj.shape[0]
    T = ATT_TILE
    nq = L // T
    scale = HEAD_DIM ** -0.5
    lanes = _head_lanes(G)

    def body(q_ref, k_ref, v_ref, do_ref, dq_ref, dk_acc, dv_acc, da_buf, beta_buf):
        i = pl.program_id(1)

        @pl.when(i == 0)
        def _():
            dk_acc[...] = jnp.zeros_like(dk_acc)
            dv_acc[...] = jnp.zeros_like(dv_acc)

        qs = [q_ref[:, hl] for hl in lanes]
        dos = [do_ref[:, hl] for hl in lanes]
        row, col = _tile_iotas()
        later_mat = jnp.where(row > col, 1.0, 0.0).astype(BF)
        before_mat = jnp.where(row < col, 1.0, 0.0).astype(BF)

        def pass1(j, runs, masked):
            valid = (col < row) if masked else None
            qk = [_dot(qs[g], k_ref[_rows(j), hl], 1, 1) for g, hl in enumerate(lanes)]
            dw = [_dot(dos[g], v_ref[_rows(j), hl], 1, 1) for g, hl in enumerate(lanes)]
            logits = [_sb_logits(qk[g], scale, valid) for g in range(G)]
            pieces = [_split(lom, 2) for _, lom in logits]
            later = [_pieces_dot(pieces[g], later_mat) for g in range(G)]
            ws = []
            for g in range(G):
                w = jnp.exp(logits[g][0] + later[g] + runs[g])
                if masked:
                    w = jnp.where(valid, w, 0.0)
                da_buf[g * nq + j] = dw[g] * w
                beta_buf[g * nq + j] = jnp.exp(logits[g][0])
                ws.append(w.astype(BF))
            for g, hl in enumerate(lanes):
                dv_acc[_rows(j), hl] += _dot(ws[g], dos[g], 0, 0)
            return tuple(runs[g] + jnp.sum(logits[g][1], axis=1, keepdims=True) for g in range(G))

        runs = pass1(i, tuple(jnp.zeros((T, 1), F32) for _ in range(G)), True)
        lax.fori_loop(0, i, lambda t, c: pass1(i - 1 - t, c, False), runs)

        def pass2(j, carry, masked):
            das = [da_buf[g * nq + j] for g in range(G)]
            pieces = [_split(da, 2) for da in das]
            before = [_pieces_dot(pieces[g], before_mat) for g in range(G)]
            dzbs = []
            for g in range(G):
                beta = beta_buf[g * nq + j]
                dz = das[g] * (1.0 - beta) - (carry[g][0] + before[g]) * beta
                if masked:
                    dz = jnp.where(col < row, dz, 0.0)
                dzbs.append((dz * scale).astype(BF))
            for g, hl in enumerate(lanes):
                dk_acc[_rows(j), hl] += _dot(dzbs[g], qs[g], 0, 0)
            dq = [_dot(dzbs[g], k_ref[_rows(j), hl], 1, 0) for g, hl in enumerate(lanes)]
            return tuple((carry[g][0] + jnp.sum(das[g], axis=1, keepdims=True), carry[g][1] + dq[g])
                         for g in range(G))

        init = tuple((jnp.zeros((T, 1), F32), jnp.zeros((T, HEAD_DIM), F32)) for _ in range(G))
        carry = lax.fori_loop(0, i, lambda j, c: pass2(j, c, False), init)
        for g, (_, dq) in enumerate(pass2(i, carry, True)):
            dq_ref[:, lanes[g]] = dq.astype(dq_ref.dtype)

    WG = G * HEAD_DIM
    q_spec, k_spec, v_spec = _att_specs(L, G, q_col, k_col, v_col)
    t_spec = pl.BlockSpec((T, WG), lambda h, i: (i, h))
    head_spec = pl.BlockSpec((L, WG), lambda h, i: (0, h), pipeline_mode=pl.Buffered(1))
    W = heads * HEAD_DIM
    out_shape = [jax.ShapeDtypeStruct((L, W), BF)] + [jax.ShapeDtypeStruct((L, W), F32)] * 2
    scratch = [pltpu.VMEM((G * nq, T, T), F32)] * 2
    vmem = _vmem_limit([], [((L, WG), BF)] * 2 + [((L, WG), F32)] * 2 + [((G * nq, T, T), F32)] * 2,
                       temps=6 << 20)
    return _call(body, name=name, out_shape=out_shape, grid=(heads // G, nq),
                 in_specs=[q_spec, k_spec, v_spec, t_spec], out_specs=[t_spec, head_spec, head_spec],
                 scratch=scratch, sem=("parallel", "arbitrary"), vmem=vmem)(proj, proj, proj, do)


_ANY = pl.BlockSpec(memory_space=pl.ANY)


def _mesh_pos():
    return lax.axis_index("x"), lax.axis_index("y"), lax.axis_index("c")


def _other_chips(x, y):
    return [(1 - x, y), (x, 1 - y), (1 - x, 1 - y)]


def _shard_window(ref, kind, sidx, r0, nr, cs):
    if kind == "col":
        assert cs % LANES == 0
        return ref.at[pl.ds(r0, nr), pl.ds(pl.multiple_of(sidx * cs, LANES), cs)]
    return ref.at[sidx, pl.ds(r0, nr), :]


def _place_shard(name, pos, shard, kind, dtype):
    R, C = shard.shape
    tr = _row_tile(R, C, 2, 16)

    def body(pos_ref, s_ref, o_ref):
        o_ref[...] = s_ref[...].astype(o_ref.dtype)

    if kind == "col":
        assert C % LANES == 0
        out_shape = jax.ShapeDtypeStruct((R, N_CHIPS * C), dtype)
        out_spec = pl.BlockSpec((tr, C), lambda r, pos_ref: (r, pos_ref[1]))
    else:
        out_shape = jax.ShapeDtypeStruct((N_CHIPS, R, C), dtype)
        out_spec = pl.BlockSpec((None, tr, C), lambda r, pos_ref: (pos_ref[1], r, 0))
    return _call(body, name=name, out_shape=out_shape, grid=(R // tr,),
                 in_specs=[pl.BlockSpec((tr, C), lambda r, pos_ref: (r, 0))], out_specs=out_spec, sem=("parallel",),
                 vmem=32 << 20, prefetch=1)(pos, shard)


def _all_gather_ici(name, gathered, shards, kinds):
    n = len(gathered)
    meta = [(kind, s.shape[0], s.shape[1]) for s, kind in zip(shards, kinds)]

    def body(*refs):
        bufs = refs[n:2 * n]
        send_sem, recv_sem = refs[2 * n:]
        x, y, c = _mesh_pos()
        sidx = 2 * x + y
        sends, recvs = [], []
        for t, (kind, R, cs) in enumerate(meta):
            half = R // 2
            r0 = pl.multiple_of(c * half, SUBLANES)
            mine = _shard_window(bufs[t], kind, sidx, r0, half, cs)
            for p, (px, py) in enumerate(_other_chips(x, y)):
                k = 3 * t + p
                cp = pltpu.make_async_remote_copy(src_ref=mine, dst_ref=mine, send_sem=send_sem.at[k],
                                                  recv_sem=recv_sem.at[k], device_id=(px, py, c),
                                                  device_id_type=MESH_IDS)
                cp.start()
                sends.append(cp)
                recvs.append(pltpu.make_async_remote_copy(
                    src_ref=mine, dst_ref=_shard_window(bufs[t], kind, 2 * px + py, r0, half, cs),
                    send_sem=send_sem.at[k], recv_sem=recv_sem.at[k], device_id=(px, py, c),
                    device_id_type=MESH_IDS))
        for cp in recvs:
            cp.wait_recv()
        for cp in sends:
            cp.wait_send()

    scratch = [pltpu.SemaphoreType.DMA((3 * n,)), pltpu.SemaphoreType.DMA((3 * n,))]
    out_shape = [jax.ShapeDtypeStruct(g.shape, g.dtype) for g in gathered]
    return _call(body, name=name, out_shape=out_shape, in_specs=[_ANY] * n, out_specs=[_ANY] * n, scratch=scratch,
                 aliases={t: t for t in range(n)})(*gathered)


def _all_gather_d2d(name, gathered, shards, kinds):
    n = len(gathered)
    meta = [(kind, s.shape[0], s.shape[1]) for s, kind in zip(shards, kinds)]

    def body(*refs):
        bufs = refs[n:2 * n]
        send_sem, recv_sem = refs[2 * n:]
        x, y, c = _mesh_pos()
        sends, recvs = [], []
        for t, (kind, R, cs) in enumerate(meta):
            half = R // 2
            mine = pl.multiple_of(c * half, SUBLANES)
            theirs = pl.multiple_of((1 - c) * half, SUBLANES)
            for p, (px, py) in enumerate(_other_chips(x, y)):
                k = 3 * t + p
                win = _shard_window(bufs[t], kind, 2 * px + py, mine, half, cs)
                cp = pltpu.make_async_remote_copy(src_ref=win, dst_ref=win, send_sem=send_sem.at[k],
                                                  recv_sem=recv_sem.at[k], device_id=(x, y, 1 - c),
                                                  device_id_type=MESH_IDS)
                cp.start()
                sends.append(cp)
                got = _shard_window(bufs[t], kind, 2 * px + py, theirs, half, cs)
                recvs.append(pltpu.make_async_remote_copy(src_ref=win, dst_ref=got, send_sem=send_sem.at[k],
                                                          recv_sem=recv_sem.at[k], device_id=(x, y, 1 - c),
                                                          device_id_type=MESH_IDS))
        for cp in recvs:
            cp.wait_recv()
        for cp in sends:
            cp.wait_send()

    scratch = [pltpu.SemaphoreType.DMA((3 * n,)), pltpu.SemaphoreType.DMA((3 * n,))]
    out_shape = [jax.ShapeDtypeStruct(g.shape, g.dtype) for g in gathered]
    return _call(body, name=name, out_shape=out_shape, in_specs=[_ANY] * n, out_specs=[_ANY] * n, scratch=scratch,
                 aliases={t: t for t in range(n)})(*gathered)


def _reduce_to_sibling(name, grads):
    n = len(grads)

    def body(*refs):
        srcs, dsts = refs[:n], refs[n:2 * n]
        send_sem, recv_sem = refs[2 * n:]
        x, y, c = _mesh_pos()
        copies = []
        for t, g in enumerate(grads):
            half = g.shape[1] // 2
            theirs = pl.multiple_of((1 - c) * half, SUBLANES)
            cp = pltpu.make_async_remote_copy(src_ref=srcs[t].at[:, pl.ds(theirs, half), :], dst_ref=dsts[t],
                                              send_sem=send_sem.at[t], recv_sem=recv_sem.at[t],
                                              device_id=(x, y, 1 - c), device_id_type=MESH_IDS)
            cp.start()
            copies.append(cp)
        for cp in copies:
            cp.wait()

    out_shape = [jax.ShapeDtypeStruct((N_CHIPS, g.shape[1] // 2, g.shape[2]), g.dtype) for g in grads]
    scratch = [pltpu.SemaphoreType.DMA((n,)), pltpu.SemaphoreType.DMA((n,))]
    return _call(body, name=name, out_shape=out_shape, in_specs=[_ANY] * n, out_specs=[_ANY] * n,
                 scratch=scratch)(*grads)


def _exchange_chip_partials(name, partials):
    n = len(partials)

    def body(*refs):
        srcs, dsts = refs[:n], refs[n:2 * n]
        send_sem, recv_sem = refs[2 * n:]
        x, y, c = _mesh_pos()
        copies = []
        for t in range(n):
            for p, (px, py) in enumerate(_other_chips(x, y)):
                k = 3 * t + p
                cp = pltpu.make_async_remote_copy(src_ref=srcs[t].at[2 * px + py], dst_ref=dsts[t].at[p],
                                                  send_sem=send_sem.at[k], recv_sem=recv_sem.at[k],
                                                  device_id=(px, py, c), device_id_type=MESH_IDS)
                cp.start()
                copies.append(cp)
        for cp in copies:
            cp.wait()

    out_shape = [jax.ShapeDtypeStruct((3,) + p.shape[1:], p.dtype) for p in partials]
    scratch = [pltpu.SemaphoreType.DMA((3 * n,)), pltpu.SemaphoreType.DMA((3 * n,))]
    return _call(body, name=name, out_shape=out_shape, in_specs=[_ANY] * n, out_specs=[_ANY] * n,
                 scratch=scratch)(*partials)


def _share_halves(name, totals):
    n = len(totals)

    def body(*refs):
        bufs = refs[n:2 * n]
        send_sem, recv_sem = refs[2 * n:]
        x, y, c = _mesh_pos()
        sends, recvs = [], []
        for t, g in enumerate(totals):
            half = g.shape[0] // 2
            mine = bufs[t].at[pl.ds(pl.multiple_of(c * half, SUBLANES), half), :]
            theirs = bufs[t].at[pl.ds(pl.multiple_of((1 - c) * half, SUBLANES), half), :]
            cp = pltpu.make_async_remote_copy(src_ref=mine, dst_ref=mine, send_sem=send_sem.at[t],
                                              recv_sem=recv_sem.at[t], device_id=(x, y, 1 - c),
                                              device_id_type=MESH_IDS)
            cp.start()
            sends.append(cp)
            recvs.append(pltpu.make_async_remote_copy(src_ref=mine, dst_ref=theirs, send_sem=send_sem.at[t],
                                                      recv_sem=recv_sem.at[t], device_id=(x, y, 1 - c),
                                                      device_id_type=MESH_IDS))
        for cp in recvs:
            cp.wait_recv()
        for cp in sends:
            cp.wait_send()

    out_shape = [jax.ShapeDtypeStruct(g.shape, g.dtype) for g in totals]
    scratch = [pltpu.SemaphoreType.DMA((n,)), pltpu.SemaphoreType.DMA((n,))]
    return _call(body, name=name, out_shape=out_shape, in_specs=[_ANY] * n, out_specs=[_ANY] * n, scratch=scratch,
                 aliases={t: t for t in range(n)})(*totals)


def _gather_small(name, v):
    def body(v_ref, out_ref, send_sem, recv_sem, local_sem):
        x, y, c = _mesh_pos()
        me = 4 * x + 2 * y + c
        local = pltpu.make_async_copy(v_ref, out_ref.at[me], local_sem)
        local.start()
        sends, recvs = [], []
        for k in range(1, N_DEV):
            px = 1 - x if k & 4 else x
            py = 1 - y if k & 2 else y
            pc = 1 - c if k & 1 else c
            cp = pltpu.make_async_remote_copy(src_ref=v_ref, dst_ref=out_ref.at[me], send_sem=send_sem.at[k],
                                              recv_sem=recv_sem.at[k], device_id=(px, py, pc),
                                              device_id_type=MESH_IDS)
            cp.start()
            sends.append(cp)
            recvs.append(pltpu.make_async_remote_copy(
                src_ref=v_ref, dst_ref=out_ref.at[4 * px + 2 * py + pc], send_sem=send_sem.at[k],
                recv_sem=recv_sem.at[k], device_id=(px, py, pc), device_id_type=MESH_IDS))
        for cp in recvs:
            cp.wait_recv()
        for cp in sends:
            cp.wait_send()
        local.wait()

    scratch = [pltpu.SemaphoreType.DMA((N_DEV,)), pltpu.SemaphoreType.DMA((N_DEV,)), pltpu.SemaphoreType.DMA(())]
    return _call(body, name=name, out_shape=jax.ShapeDtypeStruct((N_DEV,) + v.shape, v.dtype), in_specs=[_ANY],
                 out_specs=_ANY, scratch=scratch)(v)


def _row_tile(rows, cols, n_arrays, mult=SUBLANES):
    budget = (24 << 20) // (2 * n_arrays * _round_up(cols, LANES) * 4)
    for t in (512, 256, 128, 64, 32, 16, 8):
        if t <= max(budget, mult) and rows % t == 0 and t % mult == 0:
            return t
    raise ValueError(f"no row tile for {rows} x {cols}")


def _chip_partial(name, pos, own, recv):
    _, half, C = recv.shape
    tr = _row_tile(half, C, 3, 16)
    nh = half // tr

    def body(pos_ref, own_ref, recv_ref, out_ref):
        out_ref[...] = (own_ref[...] + recv_ref[...]).astype(BF)

    blk = pl.BlockSpec((None, tr, C), lambda s, r, pos_ref: (s, r, 0))
    own_blk = pl.BlockSpec((None, tr, C), lambda s, r, pos_ref: (s, pos_ref[0] * nh + r, 0))
    return _call(body, name=name, out_shape=jax.ShapeDtypeStruct(recv.shape, BF), grid=(N_CHIPS, nh),
                 in_specs=[own_blk, blk], out_specs=blk, sem=("parallel", "parallel"), vmem=40 << 20,
                 prefetch=1)(pos, own, recv)


def _final_half(name, pos, own, recv, others):
    _, half, C = recv.shape
    tr = _row_tile(half, C, 4, 16)
    nh = half // tr

    def body(pos_ref, own_ref, recv_ref, oth_ref, out_ref):
        acc = own_ref[...] + recv_ref[...]
        for p in range(3):
            acc = acc + oth_ref[p].astype(F32)
        out_ref[...] = acc

    own_blk = pl.BlockSpec((None, tr, C), lambda r, pos_ref: (pos_ref[1], pos_ref[0] * nh + r, 0))
    recv_blk = pl.BlockSpec((None, tr, C), lambda r, pos_ref: (pos_ref[1], r, 0))
    oth_blk = pl.BlockSpec((3, tr, C), lambda r, pos_ref: (0, r, 0))
    out_blk = pl.BlockSpec((tr, C), lambda r, pos_ref: (pos_ref[0] * nh + r, 0))
    return _call(body, name=name, out_shape=jax.ShapeDtypeStruct((2 * half, C), F32), grid=(nh,),
                 in_specs=[own_blk, recv_blk, oth_blk], out_specs=out_blk, sem=("parallel",), vmem=40 << 20,
                 prefetch=1)(pos, own, recv, others)


def _adamw(name, w, g, m, v):
    R, C = w.shape
    tr = R if R < SUBLANES or R % SUBLANES else _row_tile(R, C, 7)
    c1 = 1.0 - ADAM_B1 ** ADAM_STEP
    c2 = 1.0 - ADAM_B2 ** ADAM_STEP

    def body(w_ref, g_ref, m_ref, v_ref, d_ref, nm_ref, nv_ref):
        gv = g_ref[...]
        nm = ADAM_B1 * m_ref[...] + (1.0 - ADAM_B1) * gv
        nv = ADAM_B2 * v_ref[...] + (1.0 - ADAM_B2) * (gv * gv)
        d_ref[...] = -ADAM_LR * ((nm / c1) / (jnp.sqrt(nv / c2) + ADAM_EPS) + ADAM_WD * w_ref[...])
        nm_ref[...] = nm
        nv_ref[...] = nv

    blk = pl.BlockSpec((tr, C), lambda r: (r, 0))
    out_shape = [jax.ShapeDtypeStruct((R, C), F32)] * 3
    return _call(body, name=name, out_shape=out_shape, grid=(R // tr,), in_specs=[blk] * 4, out_specs=[blk] * 3,
                 sem=("parallel",), vmem=40 << 20)(w, g, m, v)


def _sum_devices(name, gathered):
    _, R, _ = gathered.shape

    def body(g_ref, o_ref):
        acc = g_ref[0]
        for d in range(1, N_DEV):
            acc = acc + g_ref[d]
        o_ref[...] = acc

    return _call(body, name=name, out_shape=jax.ShapeDtypeStruct((R, LANES), F32), grid=(1,),
                 in_specs=[pl.BlockSpec((N_DEV, R, LANES), lambda i: (0, 0, 0))],
                 out_specs=pl.BlockSpec((R, LANES), lambda i: (0, 0)), sem=("arbitrary",), vmem=16 << 20)(gathered)


def _ffn_fwd(tag, h, gain, wg, wu, wd, tm):
    L, D = h.shape
    F = wg.shape[1]
    tn = _pick(F, (512, 256, 128))
    n = _rmsnorm_fwd(f"{tag}_norm", h, gain, tm)

    def gate_up(accs, _):
        a, u = accs
        return a, u, a * _sigmoid(a) * u

    mn = _mn(tm, tn)
    a, u, s = _matmul(f"{tag}_gate_up", [n], [wg, wu], [(0, 0, 0), (0, 1, 1)], 2, gate_up,
                      [((L, F), BF) + mn] * 3, M=L, N=F, K=D, tm=tm, tn=tn, tk=D, n_outer=True)
    td = _pick(D, (512, 256, 128))
    (h_out,) = _matmul(f"{tag}_down", [s], [wd], [(0, 0, 0)], 1,
                       lambda accs, ex: [ex[0] + FFN_RESIDUAL_WEIGHT * accs[0]], [((L, D), F32) + _mn(tm, td)],
                       M=L, N=D, K=F, tm=tm, tn=td, tk=_pick(F, (1408, 512, 256, 128)), extras=[(h,) + _mn(tm, td)],
                       n_outer=True)
    return h_out, (h, n, a, u, s)


def _ffn_bwd(tag, dh, dh_bf, saved, gain, wg, wu, wd, tm, cs_ff):
    h, n, a, u, s = saved
    L, D = h.shape
    F = wg.shape[1]
    tn = _pick(F, (512, 256, 128))
    tkl = _pick(L, (1408, 384, 256, 128))

    def act_grad(accs, ex):
        ds = FFN_RESIDUAL_WEIGHT * accs[0]
        av, uv = ex[0].astype(F32), ex[1].astype(F32)
        sg = _sigmoid(av)
        return ds * uv * sg * (1.0 + av * (1.0 - sg)), ds * av * sg

    mn = _mn(tm, tn)
    da, du = _matmul(f"{tag}_dact", [dh_bf], [wd], [(0, 0, 0)], 1, act_grad, [((L, F), BF) + mn] * 2, M=L, N=F,
                     K=D, tm=tm, tn=tn, tk=D, tb=True, extras=[(a,) + mn, (u,) + mn], n_outer=True)
    td = _pick(D, (512, 256, 128))
    (dwd,) = _matmul(f"{tag}_dwd", [s], [dh_bf], [(0, 0, 0)], 1, lambda accs, _: [FFN_RESIDUAL_WEIGHT * accs[0]],
                     [((N_CHIPS, cs_ff, D), F32, (None, cs_ff, td), lambda i, j: (i, 0, j))], M=F, N=D, K=L,
                     tm=cs_ff, tn=td, tk=tkl, ta=True)
    tdn = _pick(D, (512, 256, 128))
    (dn,) = _matmul(f"{tag}_dn", [da, du], [wg, wu], [(0, 0, 0), (1, 1, 0)], 1, lambda accs, _: accs,
                    [((L, D), F32) + _mn(tm, tdn)], M=L, N=D, K=F, tm=tm, tn=tdn,
                    tk=_pick(F, (1408, 512, 256, 128)), tb=True)
    tmw = _pick(D, (1024, 512, 256, 128))
    shard_out = ((N_CHIPS, D, cs_ff), F32, (None, tmw, cs_ff), lambda i, j: (j, i, 0))
    dwg, dwu = _matmul(f"{tag}_dwgu", [n], [da, du], [(0, 0, 0), (0, 1, 1)], 2, lambda accs, _: accs,
                       [shard_out] * 2, M=D, N=F, K=L, tm=tmw, tn=cs_ff, tk=_pick(L, (384, 256, 128)), ta=True)
    dh_in, dh_in_bf, dgain = _rmsnorm_bwd(f"{tag}_dnorm", dn, h, gain, dh, tm)
    return dh_in, dh_in_bf, dgain, dwg, dwu, dwd


def kernel(x, meta_tokens, ffn1_norm, ffn1_w_gate, ffn1_w_up, ffn1_w_down, mix_norm, w_in, b_forget, fox_q_norm, fox_k_norm, w_branch_fox, w_branch_sb, w_out, ffn2_norm, ffn2_w_gate, ffn2_w_up, ffn2_w_down, loss_target, m_meta_tokens, m_ffn1_norm, m_ffn1_w_gate, m_ffn1_w_up, m_ffn1_w_down, m_mix_norm, m_w_in, m_b_forget, m_fox_q_norm, m_fox_k_norm, m_w_branch_fox, m_w_branch_sb, m_w_out, m_ffn2_norm, m_ffn2_w_gate, m_ffn2_w_up, m_ffn2_w_down, v_meta_tokens, v_ffn1_norm, v_ffn1_w_gate, v_ffn1_w_up, v_ffn1_w_down, v_mix_norm, v_w_in, v_b_forget, v_fox_q_norm, v_fox_k_norm, v_w_branch_fox, v_w_branch_sb, v_w_out, v_ffn2_norm, v_ffn2_w_gate, v_ffn2_w_up, v_ffn2_w_down):
    weights = dict(meta_tokens=meta_tokens, ffn1_norm=ffn1_norm, ffn1_w_gate=ffn1_w_gate, ffn1_w_up=ffn1_w_up,
                   ffn1_w_down=ffn1_w_down, mix_norm=mix_norm, w_in=w_in, b_forget=b_forget, fox_q_norm=fox_q_norm,
                   fox_k_norm=fox_k_norm, w_branch_fox=w_branch_fox, w_branch_sb=w_branch_sb, w_out=w_out,
                   ffn2_norm=ffn2_norm, ffn2_w_gate=ffn2_w_gate, ffn2_w_up=ffn2_w_up, ffn2_w_down=ffn2_w_down)
    moments_m = dict(meta_tokens=m_meta_tokens, ffn1_norm=m_ffn1_norm, ffn1_w_gate=m_ffn1_w_gate,
                     ffn1_w_up=m_ffn1_w_up, ffn1_w_down=m_ffn1_w_down, mix_norm=m_mix_norm, w_in=m_w_in,
                     b_forget=m_b_forget, fox_q_norm=m_fox_q_norm, fox_k_norm=m_fox_k_norm,
                     w_branch_fox=m_w_branch_fox, w_branch_sb=m_w_branch_sb, w_out=m_w_out, ffn2_norm=m_ffn2_norm,
                     ffn2_w_gate=m_ffn2_w_gate, ffn2_w_up=m_ffn2_w_up, ffn2_w_down=m_ffn2_w_down)
    moments_v = dict(meta_tokens=v_meta_tokens, ffn1_norm=v_ffn1_norm, ffn1_w_gate=v_ffn1_w_gate,
                     ffn1_w_up=v_ffn1_w_up, ffn1_w_down=v_ffn1_w_down, mix_norm=v_mix_norm, w_in=v_w_in,
                     b_forget=v_b_forget, fox_q_norm=v_fox_q_norm, fox_k_norm=v_fox_k_norm,
                     w_branch_fox=v_w_branch_fox, w_branch_sb=v_w_branch_sb, w_out=v_w_out, ffn2_norm=v_ffn2_norm,
                     ffn2_w_gate=v_ffn2_w_gate, ffn2_w_up=v_ffn2_w_up, ffn2_w_down=v_ffn2_w_down)
    names = list(weights)

    _, S, D = x.shape
    NM = meta_tokens.shape[0]
    L_real = NM + S
    L = _round_up(L_real, ATT_TILE)
    nblk = L // ATT_TILE
    nbp = _round_up(nblk, SUBLANES)
    cs_ff = ffn1_w_gate.shape[2]
    F = N_CHIPS * cs_ff
    H = b_forget.shape[1]
    FW = w_branch_fox.shape[1]
    SW = w_branch_sb.shape[1]
    HS = SW // HEAD_DIM
    cs_in = w_in.shape[2]
    W_IN = N_CHIPS * cs_in
    assert FW == H * HEAD_DIM and W_IN == 3 * FW + H + 3 * SW + 2 * D
    cs_d = D // N_CHIPS
    tm = _pick(L, (384, 256, 128))

    x_pos, y_pos, c_pos = _mesh_pos()
    pos = jnp.stack([c_pos, 2 * x_pos + y_pos]).astype(I32)

    shard_of = {
        "ffn1_w_gate": (ffn1_w_gate[0], "col"), "ffn1_w_up": (ffn1_w_up[0], "col"),
        "ffn1_w_down": (ffn1_w_down[0], "maj"), "w_in": (w_in[0], "maj"),
        "w_branch_fox": (w_branch_fox[0], "col"), "w_branch_sb": (w_branch_sb[0], "col"),
        "w_out": (w_out[0], "maj"), "ffn2_w_gate": (ffn2_w_gate[0], "col"), "ffn2_w_up": (ffn2_w_up[0], "col"),
        "ffn2_w_down": (ffn2_w_down[0], "maj"),
    }
    g_names = list(shard_of) + ["meta_tokens"]
    shards = [shard_of[k][0] for k in shard_of] + [meta_tokens]
    kinds = [shard_of[k][1] for k in shard_of] + ["col"]
    dtypes = [BF] * len(shard_of) + [F32]
    gathered = [_place_shard(f"place_{k}", pos, s, kind, dt)
                for k, s, kind, dt in zip(g_names, shards, kinds, dtypes)]
    gathered = _all_gather_ici("gather_weights_ici", gathered, shards, kinds)
    gathered = _all_gather_d2d("gather_weights_d2d", gathered, shards, kinds)
    full = dict(zip(g_names, gathered))
    wg1, wu1, wg2, wu2 = full["ffn1_w_gate"], full["ffn1_w_up"], full["ffn2_w_gate"], full["ffn2_w_up"]
    wd1 = full["ffn1_w_down"].reshape(F, D)
    wd2 = full["ffn2_w_down"].reshape(F, D)
    wbf, wbs = full["w_branch_fox"], full["w_branch_sb"]
    wo = full["w_out"].reshape(D, D)
    w_in_full = jnp.transpose(full["w_in"], (1, 0, 2)).reshape(D, W_IN)
    c_f = 3 * FW
    QKV_S, GATES, FCOL = 3 * FW, 3 * FW + 3 * SW, 3 * FW + 3 * SW + 2 * D
    W_PROJ = FCOL + LANES
    w_proj = jnp.concatenate([w_in_full[:, :c_f], w_in_full[:, c_f + H:],
                              jnp.pad(w_in_full[:, c_f:c_f + H], ((0, 0), (0, LANES - H)))], axis=1)

    h0 = jnp.concatenate([full["meta_tokens"], x[0], jnp.zeros((L - L_real, D), F32)], axis=0)
    target = jnp.concatenate([jnp.zeros((NM, D), F32), loss_target[0], jnp.zeros((L - L_real, D), F32)], axis=0)

    h1, saved1 = _ffn_fwd("ffn1", h0, ffn1_norm, wg1, wu1, wd1, tm)

    n2 = _rmsnorm_fwd("mix_norm", h1, mix_norm, tm)
    tp = _pick(FCOL, (512, 256, 128))
    (proj,) = _matmul("in_proj", [n2], [w_proj], [(0, 0, 0)], 1, lambda accs, _: accs,
                      [((L, FCOL), BF) + _mn(tm, tp)], M=L, N=FCOL, K=D, tm=tm, tn=tp, tk=D, n_outer=True)
    (f_logit,) = _matmul("forget_proj", [n2], [w_proj], [(0, 0, 0)], 1, lambda accs, _: accs,
                         [((L, LANES), F32) + _mn(tm, LANES)], M=L, N=LANES, K=D, tm=tm, tn=LANES, tk=D,
                         b_off=FCOL // LANES)
    fl = jnp.pad(jnp.transpose(f_logit[:, :H]).reshape(H, nblk, LANES), ((0, 0), (0, nbp - nblk), (0, 0)))
    bias = jnp.broadcast_to(b_forget[0][:, None, None], (H, 1, LANES))
    c_row = _cum_fwd("forget_cumsum", fl, bias, nblk)
    c_col = c_row[:, :nblk].reshape(H, L, 1)
    gq, gk = fox_q_norm[0][:, None, :], fox_k_norm[0][:, None, :]
    qn, kn = _qknorm_fwd("fox_qk_norm", proj, gq, gk, H, 0, FW, tm)
    o_fox, lse = _fox_fwd("fox_attention", qn, kn, proj, c_row, c_col, H, ATT_HEADS, 0, 0, 2 * FW)
    o_sb = _sb_fwd("sb_attention", proj, HS, ATT_HEADS, QKV_S, QKV_S + SW, QKV_S + 2 * SW)

    td = _pick(D, (512, 256, 128))

    def merge(accs, ex):
        bf_, bs_ = accs
        return _sigmoid(ex[0].astype(F32)) * bf_ + _sigmoid(ex[1].astype(F32)) * bs_, bf_, bs_

    merged, br_f, br_s = _matmul("branch_merge", [o_fox, o_sb], [wbf, wbs], [(0, 0, 0), (1, 1, 1)], 2, merge,
                                 [((L, D), BF) + _mn(tm, td)] * 3, M=L, N=D, K=FW, tm=tm, tn=td, tk=FW,
                                 extras=[(proj,) + _mn(tm, td, GATES), (proj,) + _mn(tm, td, GATES + D)],
                                 n_outer=True)
    (h2,) = _matmul("out_proj", [merged], [wo], [(0, 0, 0)], 1, lambda accs, ex: [ex[0] + accs[0]],
                    [((L, D), F32) + _mn(tm, td)], M=L, N=D, K=D, tm=tm, tn=td, tk=D, extras=[(h1,) + _mn(tm, td)],
                    n_outer=True)

    h3, saved2 = _ffn_fwd("ffn2", h2, ffn2_norm, wg2, wu2, wd2, tm)
    dh3, dh3_bf, loss_part = _loss_grad("loss", h3, target, NM, S, tm)

    dh2, dh2_bf, dg_ffn2, dwg2, dwu2, dwd2 = _ffn_bwd("ffn2", dh3, dh3_bf, saved2, ffn2_norm, wg2, wu2, wd2, tm,
                                                      cs_ff)

    def gate_grad(accs, ex):
        dm = accs[0]
        gf, gs, bf_, bs_ = [e.astype(F32) for e in ex]
        sf, ss = _sigmoid(gf), _sigmoid(gs)
        return dm * bf_ * sf * (1.0 - sf), dm * bs_ * ss * (1.0 - ss), dm * sf, dm * ss

    mn_d = _mn(tm, td)
    dgf, dgs, dbr_f, dbr_s = _matmul(
        "d_merged", [dh2_bf], [wo], [(0, 0, 0)], 1, gate_grad, [((L, D), BF) + mn_d] * 4, M=L, N=D, K=D, tm=tm,
        tn=td, tk=D, tb=True, extras=[(proj,) + _mn(tm, td, GATES), (proj,) + _mn(tm, td, GATES + D),
                                      (br_f,) + mn_d, (br_s,) + mn_d], n_outer=True)
    tkl = _pick(L, (1408, 384, 256, 128))
    (dwo,) = _matmul("d_w_out", [merged], [dh2_bf], [(0, 0, 0)], 1, lambda accs, _: accs,
                     [((N_CHIPS, cs_d, D), F32, (None, cs_d, td), lambda i, j: (i, 0, j))], M=D, N=D, K=L, tm=cs_d,
                     tn=td, tk=tkl, ta=True)
    tw = _pick(FW, (512, 256, 128))
    do_fox, do_sb = _matmul("d_branch_in", [dbr_f, dbr_s], [wbf, wbs], [(0, 0, 0), (1, 1, 1)], 2,
                            lambda accs, _: accs, [((L, FW), BF) + _mn(tm, tw)] * 2, M=L, N=FW, K=D, tm=tm, tn=tw,
                            tk=D, tb=True)
    tmb = _pick(FW, (1024, 512, 256, 128))
    dwbf, dwbs = _matmul("d_w_branch", [o_fox, o_sb], [dbr_f, dbr_s], [(0, 0, 0), (1, 1, 1)], 2,
                         lambda accs, _: accs,
                         [((N_CHIPS, FW, cs_d), F32, (None, tmb, cs_d), lambda i, j: (j, i, 0))] * 2, M=FW, N=D, K=L,
                         tm=tmb, tn=cs_d, tk=_pick(L, (384, 256, 128)), ta=True)

    dqn, dkn, dfv, dcs, drs = _fox_bwd("fox_attention_bwd", qn, kn, proj, c_row, c_col, o_fox, do_fox, lse, H,
                                       ATT_HEADS, 0, 0, 2 * FW)
    dsq, dsk, dsv = _sb_bwd("sb_attention_bwd", proj, do_sb, HS, ATT_HEADS, QKV_S, QKV_S + SW,
                            QKV_S + 2 * SW)
    dfq, dfk, dgq, dgk = _qknorm_bwd("fox_qk_norm_bwd", proj, dqn, dkn, gq, gk, H, 0, FW, tm)
    drs_row = jnp.pad(drs.reshape(H, nblk, LANES), ((0, 0), (0, nbp - nblk), (0, 0)))
    dfl, dbias = _cum_bwd("forget_cumsum_bwd", drs_row, dcs, fl, bias, nblk)
    dfl_cols = jnp.pad(jnp.transpose(dfl[:, :nblk].reshape(H, L)), ((0, 0), (0, LANES - H))).astype(BF)
    dproj = jnp.concatenate([dfq, dfk, dfv.astype(BF), dsq, dsk.astype(BF), dsv.astype(BF), dgf, dgs, dfl_cols],
                            axis=1)

    tkp = _pick(W_PROJ, (1152, 640, 512, 384, 256, 128))
    (dn2,) = _matmul("d_mix_norm_in", [dproj], [w_proj], [(0, 0, 0)], 1, lambda accs, _: accs,
                     [((L, D), F32) + _mn(tm, td)], M=L, N=D, K=W_PROJ, tm=tm, tn=td, tk=tkp, tb=True)
    tmw = _pick(D, (1024, 512, 256, 128))
    tnp = _pick(W_PROJ, (1152, 640, 512, 384, 256, 128))
    (dw_proj,) = _matmul("d_w_in", [n2], [dproj], [(0, 0, 0)], 1, lambda accs, _: accs,
                         [((D, W_PROJ), F32) + _mn(tmw, tnp)], M=D, N=W_PROJ, K=L, tm=tmw, tn=tnp,
                         tk=_pick(L, (384, 256, 128)), ta=True)
    dh1, dh1_bf, dg_mix = _rmsnorm_bwd("mix_dnorm", dn2, h1, mix_norm, dh2, tm)

    dh0, _, dg_ffn1, dwg1, dwu1, dwd1 = _ffn_bwd("ffn1", dh1, dh1_bf, saved1, ffn1_norm, wg1, wu1, wd1, tm, cs_ff)
    grad_x = dh0[NM:L_real][None]

    dw_in_ref = jnp.concatenate([dw_proj[:, :c_f], dw_proj[:, FCOL:FCOL + H], dw_proj[:, c_f:FCOL]], axis=1)
    dw_in = jnp.transpose(dw_in_ref.reshape(D, N_CHIPS, cs_in), (1, 0, 2))
    big = ["ffn1_w_gate", "ffn1_w_up", "ffn1_w_down", "w_in", "w_branch_fox", "w_branch_sb", "w_out",
           "ffn2_w_gate", "ffn2_w_up", "ffn2_w_down"]
    local = [dwg1, dwu1, dwd1, dw_in, dwbf, dwbs, dwo, dwg2, dwu2, dwd2]
    from_sibling = _reduce_to_sibling("grads_to_sibling", local)
    partials = [_chip_partial(f"chip_sum_{k}", pos, g, r) for k, g, r in zip(big, local, from_sibling)]
    from_chips = _exchange_chip_partials("grads_to_owner", partials)
    halves = [_final_half(f"total_{k}", pos, g, r, o) for k, g, r, o in zip(big, local, from_sibling, from_chips)]
    grads = dict(zip(big, _share_halves("grads_to_core_pair", halves)))

    small = [loss_part[:, :1].reshape(1), dh0[:NM].reshape(-1), dg_ffn1.reshape(-1), dg_mix.reshape(-1),
             dg_ffn2.reshape(-1), dbias[:, 0, 0], dgq.reshape(-1), dgk.reshape(-1)]
    sizes = [s.shape[0] for s in small]
    flat = jnp.concatenate(small)
    rows = _round_up(-(-flat.shape[0] // LANES), SUBLANES)
    packed = jnp.pad(flat, (0, rows * LANES - flat.shape[0])).reshape(rows, LANES)
    total = _sum_devices("sum_small", _gather_small("gather_small", packed)).reshape(-1)
    pieces, off = [], 0
    for n_el in sizes:
        pieces.append(total[off:off + n_el])
        off += n_el
    loss = pieces[0][0]
    d_meta = lax.dynamic_slice_in_dim(pieces[1].reshape(NM, D), pos[1] * cs_d, cs_d, axis=1)
    grads.update(meta_tokens=d_meta, ffn1_norm=pieces[2].reshape(1, D), mix_norm=pieces[3].reshape(1, D),
                 ffn2_norm=pieces[4].reshape(1, D), b_forget=pieces[5].reshape(1, H),
                 fox_q_norm=pieces[6].reshape(1, H, HEAD_DIM), fox_k_norm=pieces[7].reshape(1, H, HEAD_DIM))

    out_g, out_d, out_m, out_v = [], [], [], []
    for k in names:
        w = weights[k]
        shape2 = (1, w.size) if w.size < LANES * SUBLANES else (w.size // w.shape[-1], w.shape[-1])
        g2 = grads[k].reshape(shape2)
        d, nm, nv = _adamw(f"adamw_{k}", w.reshape(shape2), g2, moments_m[k].reshape(shape2),
                           moments_v[k].reshape(shape2))
        out_g.append(g2.reshape(w.shape))
        out_d.append(d.reshape(w.shape))
        out_m.append(nm.reshape(w.shape))
        out_v.append(nv.reshape(w.shape))
    return (loss, grad_x, *out_g, *out_d, *out_m, *out_v)
```

```python
import functools

import jax
import jax.numpy as jnp
from jax import lax
from jax.experimental import pallas as pl
from jax.experimental.pallas import tpu as pltpu

F32 = jnp.float32
BF = jnp.bfloat16
I32 = jnp.int32

HEAD_DIM = 128
RMS_EPS = 1e-6
FFN_RESIDUAL_WEIGHT = 0.5
ADAM_LR = 0.001
ADAM_B1 = 0.9
ADAM_B2 = 0.999
ADAM_EPS = 1e-08
ADAM_WD = 0.01
ADAM_STEP = 10

LANES = 128
SUBLANES = 8
ATT_TILE = 128
ATT_HEADS = 4
ATT_HEADS_SB_FWD = 8
VMEM_CAP = 56 * 1024 * 1024
MESH_IDS = pl.DeviceIdType.MESH
N_CHIPS = 4
N_DEV = 8


def _pick(n, cands):
    for c in cands:
        if c <= n and n % c == 0:
            return c
    raise ValueError(f"no tile for {n} among {cands}")


def _round_up(n, m):
    return (n + m - 1) // m * m


def _tile_bytes(shape, dtype):
    item = jnp.dtype(dtype).itemsize
    dims = [d for d in shape if d is not None]
    if not dims:
        return 4 * LANES * SUBLANES
    last = _round_up(dims[-1], LANES)
    sub = _round_up(dims[-2], SUBLANES * (4 // item)) if len(dims) > 1 else 1
    lead = 1
    for d in dims[:-2]:
        lead *= d
    return lead * sub * last * item


def _vmem_limit(blocks, scratch=(), temps=0):
    need = 2 * sum(_tile_bytes(s, d) for s, d in blocks) + sum(_tile_bytes(s, d) for s, d in scratch) + temps
    return int(min(VMEM_CAP, max(need + (4 << 20), 16 << 20)))


def _call(body, *, name, out_shape, grid=(), in_specs=None, out_specs=None, scratch=(), sem=None, vmem=None,
          aliases=None, prefetch=0):
    params = pltpu.CompilerParams(dimension_semantics=sem, vmem_limit_bytes=vmem)
    if prefetch:
        grid_spec = pltpu.PrefetchScalarGridSpec(num_scalar_prefetch=prefetch, grid=grid, in_specs=in_specs,
                                                 out_specs=out_specs, scratch_shapes=scratch)
        return pl.pallas_call(body, out_shape=out_shape, grid_spec=grid_spec, name=name, compiler_params=params,
                              input_output_aliases=aliases or {})
    return pl.pallas_call(body, out_shape=out_shape, grid=grid, in_specs=in_specs, out_specs=out_specs,
                          scratch_shapes=scratch, name=name, compiler_params=params,
                          input_output_aliases=aliases or {})


def _dot(a, b, ca, cb):
    return lax.dot_general(a, b, (((ca,), (cb,)), ((), ())), preferred_element_type=F32)


def _sigmoid(x):
    return 1.0 / (1.0 + jnp.exp(-x))


def _log_sigmoid(x):
    return jnp.minimum(x, 0.0) - jnp.log1p(jnp.exp(-jnp.abs(x)))


def _split(x, parts):
    pieces = []
    rem = x
    for p in range(parts):
        piece = rem.astype(BF)
        pieces.append(piece)
        if p + 1 < parts:
            rem = rem - piece.astype(F32)
    return pieces


def _pieces_dot(pieces, ones_bf):
    out = None
    for piece in pieces:
        d = _dot(piece, ones_bf, 1, 0)
        out = d if out is None else out + d
    return out


def _split_dot(x, ones_bf, parts):
    return _pieces_dot(_split(x, parts), ones_bf)


def _matmul(name, a_list, b_list, pairs, n_acc, epi, outs, *, M, N, K, tm, tn, tk, ta=False, tb=False,
            extras=(), n_outer=False, b_off=0):
    gi, gj, nk = M // tm, N // tn, K // tk
    assert gi * tm == M and gj * tn == N and nk * tk == K, (name, M, N, K, tm, tn, tk)
    n_a, n_b, n_e, n_o = len(a_list), len(b_list), len(extras), len(outs)

    def ij(g0, g1):
        return (g1, g0) if n_outer else (g0, g1)

    def a_map(g0, g1, k):
        i, _ = ij(g0, g1)
        return (k, i) if ta else (i, k)

    def b_map(g0, g1, k):
        _, j = ij(g0, g1)
        return (j + b_off, k) if tb else (k, j + b_off)

    def tile_map(fn):
        return lambda g0, g1, k: fn(*ij(g0, g1))

    a_block = (tk, tm) if ta else (tm, tk)
    b_block = (tn, tk) if tb else (tk, tn)
    in_specs = ([pl.BlockSpec(a_block, a_map)] * n_a + [pl.BlockSpec(b_block, b_map)] * n_b
                + [pl.BlockSpec(bs, tile_map(fn)) for _, bs, fn in extras])
    out_specs = [pl.BlockSpec(bs, tile_map(fn)) for _, _, bs, fn in outs]
    out_shape = [jax.ShapeDtypeStruct(s, d) for s, d, _, _ in outs]
    scratch = [pltpu.VMEM((tm, tn), F32) for _ in range(n_acc)] if nk > 1 else []

    def body(*refs):
        a_refs = refs[:n_a]
        b_refs = refs[n_a:n_a + n_b]
        e_refs = refs[n_a + n_b:n_a + n_b + n_e]
        o_refs = refs[n_a + n_b + n_e:n_a + n_b + n_e + n_o]
        acc_refs = refs[n_a + n_b + n_e + n_o:]

        def products():
            accs = [None] * n_acc
            for ai, bi, ci in pairs:
                a = a_refs[ai][...]
                b = b_refs[bi][...]
                d = _dot(a.astype(BF), b.astype(BF), 0 if ta else 1, 1 if tb else 0)
                accs[ci] = d if accs[ci] is None else accs[ci] + d
            return accs

        def finish(accs):
            res = epi(accs, [e[...] for e in e_refs])
            for o_ref, r in zip(o_refs, res):
                o_ref[...] = r.reshape(o_ref.shape).astype(o_ref.dtype)

        if nk == 1:
            finish(products())
        else:
            k = pl.program_id(2)

            @pl.when(k == 0)
            def _():
                for acc in acc_refs:
                    acc[...] = jnp.zeros_like(acc)

            for acc, d in zip(acc_refs, products()):
                acc[...] += d

            @pl.when(k == nk - 1)
            def _():
                finish([acc[...] for acc in acc_refs])

    blocks = ([(a_block, a.dtype) for a in a_list] + [(b_block, b.dtype) for b in b_list]
              + [(bs, e.dtype) for e, bs, _ in extras] + [(bs, d) for _, d, bs, _ in outs])
    vmem = _vmem_limit(blocks, [((tm, tn), F32)] * (n_acc if nk > 1 else 0), temps=6 * tm * tn * 4)
    grid = (gj, gi, nk) if n_outer else (gi, gj, nk)
    fn = _call(body, name=name, out_shape=out_shape, grid=grid, in_specs=in_specs, out_specs=out_specs,
               scratch=scratch, sem=("parallel", "parallel", "arbitrary"), vmem=vmem)
    return fn(*a_list, *b_list, *[e for e, _, _ in extras])


def _mn(tm, tn, col0=0):
    assert col0 % tn == 0
    off = col0 // tn
    return (tm, tn), (lambda i, j: (i, j + off))


def _rmsnorm_fwd(name, x, gain, tr):
    L, D = x.shape

    def body(x_ref, g_ref, o_ref):
        xv = x_ref[...]
        r = lax.rsqrt(jnp.mean(xv * xv, axis=-1, keepdims=True) + RMS_EPS)
        o_ref[...] = (xv * r * g_ref[...]).astype(BF)

    row = pl.BlockSpec((tr, D), lambda i: (i, 0))
    vec = pl.BlockSpec((1, D), lambda i: (0, 0))
    vmem = _vmem_limit([((tr, D), F32), ((tr, D), BF)], temps=3 * tr * D * 4)
    return _call(body, name=name, out_shape=jax.ShapeDtypeStruct((L, D), BF), grid=(L // tr,), in_specs=[row, vec],
                 out_specs=row, sem=("parallel",), vmem=vmem)(x, gain)


def _rmsnorm_bwd(name, dn, x, gain, dres, tr):
    L, D = x.shape
    steps = L // tr

    def body(dn_ref, x_ref, g_ref, dres_ref, dx_ref, dxb_ref, dg_ref):
        i = pl.program_id(0)
        xv = x_ref[...]
        r = lax.rsqrt(jnp.mean(xv * xv, axis=-1, keepdims=True) + RMS_EPS)
        xhat = xv * r
        dy = dn_ref[...]
        dxhat = dy * g_ref[...]
        dx = dres_ref[...] + r * (dxhat - xhat * jnp.mean(dxhat * xhat, axis=-1, keepdims=True))
        dx_ref[...] = dx
        dxb_ref[...] = dx.astype(BF)

        @pl.when(i == 0)
        def _():
            dg_ref[...] = jnp.zeros_like(dg_ref)

        dg_ref[...] += jnp.sum(dy * xhat, axis=0, keepdims=True)

    row = pl.BlockSpec((tr, D), lambda i: (i, 0))
    vec = pl.BlockSpec((1, D), lambda i: (0, 0))
    vmem = _vmem_limit([((tr, D), F32)] * 4 + [((tr, D), BF)], temps=4 * tr * D * 4)
    out_shape = [jax.ShapeDtypeStruct((L, D), F32), jax.ShapeDtypeStruct((L, D), BF),
                 jax.ShapeDtypeStruct((1, D), F32)]
    return _call(body, name=name, out_shape=out_shape, grid=(steps,), in_specs=[row, row, vec, row],
                 out_specs=[row, row, vec], sem=("arbitrary",), vmem=vmem)(dn, x, gain, dres)


def _loss_grad(name, h, target, n_meta, n_seq, tr):
    L, D = h.shape

    def body(h_ref, t_ref, dh_ref, dhb_ref, loss_ref):
        i = pl.program_id(0)
        rows = i * tr + lax.broadcasted_iota(I32, (tr, 1), 0)
        valid = (rows >= n_meta) & (rows < n_meta + n_seq)
        diff = jnp.where(valid, h_ref[...] - t_ref[...], 0.0)
        dh = diff * (1.0 / D)
        dh_ref[...] = dh
        dhb_ref[...] = dh.astype(BF)

        @pl.when(i == 0)
        def _():
            loss_ref[...] = jnp.zeros_like(loss_ref)

        loss_ref[...] += jnp.sum(diff * diff) * (0.5 / D)

    row = pl.BlockSpec((tr, D), lambda i: (i, 0))
    acc = pl.BlockSpec((1, LANES), lambda i: (0, 0))
    vmem = _vmem_limit([((tr, D), F32)] * 3 + [((tr, D), BF)], temps=3 * tr * D * 4)
    out_shape = [jax.ShapeDtypeStruct((L, D), F32), jax.ShapeDtypeStruct((L, D), BF),
                 jax.ShapeDtypeStruct((1, LANES), F32)]
    return _call(body, name=name, out_shape=out_shape, grid=(L // tr,), in_specs=[row, row],
                 out_specs=[row, row, acc], sem=("arbitrary",), vmem=vmem)(h, target)


def _qknorm_fwd(name, proj, gq, gk, heads, q_col, k_col, tr):
    L = proj.shape[0]

    def body(q_ref, k_ref, gq_ref, gk_ref, qn_ref, kn_ref):
        for x_ref, g_ref, o_ref in ((q_ref, gq_ref, qn_ref), (k_ref, gk_ref, kn_ref)):
            xv = x_ref[...].astype(F32)
            r = lax.rsqrt(jnp.mean(xv * xv, axis=-1, keepdims=True) + RMS_EPS)
            o_ref[...] = (xv * r * g_ref[...]).astype(BF)

    qb, kb = q_col // HEAD_DIM, k_col // HEAD_DIM
    in_specs = [pl.BlockSpec((tr, HEAD_DIM), lambda h, i: (i, qb + h)),
                pl.BlockSpec((tr, HEAD_DIM), lambda h, i: (i, kb + h)),
                pl.BlockSpec((None, 1, HEAD_DIM), lambda h, i: (h, 0, 0)),
                pl.BlockSpec((None, 1, HEAD_DIM), lambda h, i: (h, 0, 0))]
    out = pl.BlockSpec((tr, HEAD_DIM), lambda h, i: (i, h))
    out_shape = [jax.ShapeDtypeStruct((L, heads * HEAD_DIM), BF)] * 2
    return _call(body, name=name, out_shape=out_shape, grid=(heads, L // tr), in_specs=in_specs,
                 out_specs=[out, out], sem=("parallel", "parallel"), vmem=16 << 20)(proj, proj, gq, gk)


def _qknorm_bwd(name, proj, dqn, dkn, gq, gk, heads, q_col, k_col, tr):
    L = proj.shape[0]

    def body(q_ref, k_ref, dqn_ref, dkn_ref, gq_ref, gk_ref, dq_ref, dk_ref, dgq_ref, dgk_ref):
        i = pl.program_id(1)
        for x_ref, dy_ref, g_ref, dx_ref, dg_ref in ((q_ref, dqn_ref, gq_ref, dq_ref, dgq_ref),
                                                     (k_ref, dkn_ref, gk_ref, dk_ref, dgk_ref)):
            xv = x_ref[...].astype(F32)
            r = lax.rsqrt(jnp.mean(xv * xv, axis=-1, keepdims=True) + RMS_EPS)
            xhat = xv * r
            dy = dy_ref[...].astype(F32)
            dxhat = dy * g_ref[...]
            dx_ref[...] = (r * (dxhat - xhat * jnp.mean(dxhat * xhat, axis=-1, keepdims=True))).astype(BF)

            @pl.when(i == 0)
            def _():
                dg_ref[...] = jnp.zeros_like(dg_ref)

            dg_ref[...] += jnp.sum(dy * xhat, axis=0, keepdims=True)

    qb, kb = q_col // HEAD_DIM, k_col // HEAD_DIM
    tile = pl.BlockSpec((tr, HEAD_DIM), lambda h, i: (i, h))
    gain = pl.BlockSpec((None, 1, HEAD_DIM), lambda h, i: (h, 0, 0))
    in_specs = [pl.BlockSpec((tr, HEAD_DIM), lambda h, i: (i, qb + h)),
                pl.BlockSpec((tr, HEAD_DIM), lambda h, i: (i, kb + h)), tile, tile, gain, gain]
    out_shape = [jax.ShapeDtypeStruct((L, heads * HEAD_DIM), BF)] * 2 + [
        jax.ShapeDtypeStruct((heads, 1, HEAD_DIM), F32)] * 2
    return _call(body, name=name, out_shape=out_shape, grid=(heads, L // tr), in_specs=in_specs,
                 out_specs=[tile, tile, gain, gain], sem=("parallel", "arbitrary"),
                 vmem=16 << 20)(proj, proj, dqn, dkn, gq, gk)


def _tri(cmp):
    r = lax.broadcasted_iota(I32, (LANES, LANES), 0)
    c = lax.broadcasted_iota(I32, (LANES, LANES), 1)
    return jnp.where(cmp(r, c), 1.0, 0.0).astype(BF)


def _cum_fwd(name, fl, bias, n_rows):
    H, nbp, _ = fl.shape

    def body(fl_ref, b_ref, c_ref, tot_ref):
        lf = _log_sigmoid(fl_ref[...] + b_ref[...])
        c_ref[...] = _split_dot(lf, _tri(lambda r, c: r <= c), 3)
        tot_ref[...] = _split_dot(lf, jnp.ones((LANES, LANES), BF), 3)

        def step(r, carry):
            c_ref[pl.ds(r, 1), :] = c_ref[pl.ds(r, 1), :] + carry
            return carry + tot_ref[pl.ds(r, 1), :]

        lax.fori_loop(0, n_rows, step, jnp.zeros((1, LANES), F32))

    blk = pl.BlockSpec((None, nbp, LANES), lambda h: (h, 0, 0))
    vec = pl.BlockSpec((None, 1, LANES), lambda h: (h, 0, 0))
    return _call(body, name=name, out_shape=jax.ShapeDtypeStruct((H, nbp, LANES), F32), grid=(H,),
                 in_specs=[blk, vec], out_specs=blk, scratch=[pltpu.VMEM((nbp, LANES), F32)], sem=("parallel",),
                 vmem=16 << 20)(fl, bias)


def _cum_bwd(name, drs, dcs, fl, bias, n_rows):
    H, nbp, _ = fl.shape

    def body(drs_ref, dcs_ref, fl_ref, b_ref, dfl_ref, db_ref, rin_ref, tot_ref):
        dc = drs_ref[...] - dcs_ref[...]
        rin_ref[...] = _split_dot(dc, _tri(lambda r, c: r >= c), 3)
        tot_ref[...] = _split_dot(dc, jnp.ones((LANES, LANES), BF), 3)
        dfl_ref[...] = jnp.zeros_like(dfl_ref)

        def step(t, carry):
            r = n_rows - 1 - t
            x = fl_ref[pl.ds(r, 1), :] + b_ref[...]
            dfl_ref[pl.ds(r, 1), :] = (rin_ref[pl.ds(r, 1), :] + carry) * _sigmoid(-x)
            return carry + tot_ref[pl.ds(r, 1), :]

        lax.fori_loop(0, n_rows, step, jnp.zeros((1, LANES), F32))
        db_ref[...] = jnp.zeros_like(db_ref) + jnp.sum(dfl_ref[...])

    blk = pl.BlockSpec((None, nbp, LANES), lambda h: (h, 0, 0))
    vec = pl.BlockSpec((None, 1, LANES), lambda h: (h, 0, 0))
    out_shape = [jax.ShapeDtypeStruct((H, nbp, LANES), F32), jax.ShapeDtypeStruct((H, 1, LANES), F32)]
    return _call(body, name=name, out_shape=out_shape, grid=(H,), in_specs=[blk, blk, blk, vec],
                 out_specs=[blk, vec], scratch=[pltpu.VMEM((nbp, LANES), F32)] * 2, sem=("parallel",),
                 vmem=16 << 20)(drs, dcs, fl, bias)


def _att_specs(L, G, q_col, k_col, v_col):
    T = ATT_TILE
    W = G * HEAD_DIM
    assert q_col % W == 0 and k_col % W == 0 and v_col % W == 0
    qb, kb, vb = q_col // W, k_col // W, v_col // W
    q_spec = pl.BlockSpec((T, W), lambda h, i: (i, qb + h))
    k_spec = pl.BlockSpec((L, W), lambda h, i: (0, kb + h), pipeline_mode=pl.Buffered(1))
    v_spec = pl.BlockSpec((L, W), lambda h, i: (0, vb + h), pipeline_mode=pl.Buffered(1))
    return q_spec, k_spec, v_spec


def _head_lanes(G):
    return [slice(g * HEAD_DIM, (g + 1) * HEAD_DIM) for g in range(G)]


def _tile_iotas():
    T = ATT_TILE
    return lax.broadcasted_iota(I32, (T, T), 0), lax.broadcasted_iota(I32, (T, T), 1)


def _rows(j):
    return pl.ds(pl.multiple_of(j * ATT_TILE, ATT_TILE), ATT_TILE)


def _fox_fwd(name, q_arr, k_arr, v_arr, c_row, c_col, heads, G, q_col, k_col, v_col):
    L = q_arr.shape[0]
    T = ATT_TILE
    scale = HEAD_DIM ** -0.5
    lanes = _head_lanes(G)

    def body(q_ref, k_ref, v_ref, crow_ref, ccol_ref, o_ref, lse_ref):
        i = pl.program_id(1)
        qs = [q_ref[:, hl] for hl in lanes]
        cts = [ccol_ref[g] for g in range(G)]
        row, col = _tile_iotas()

        def tile(j, carry, masked):
            qk = [_dot(qs[g], k_ref[_rows(j), hl], 1, 1) for g, hl in enumerate(lanes)]
            stats = []
            for g in range(G):
                m, l, _ = carry[g]
                s = qk[g] * scale + (cts[g] - crow_ref[g, pl.ds(j, 1), :])
                if masked:
                    s = jnp.where(col <= row, s, -jnp.inf)
                m_new = jnp.maximum(m, jnp.max(s, axis=1, keepdims=True))
                alpha = jnp.exp(m - m_new)
                p = jnp.exp(s - m_new)
                stats.append((m_new, alpha, alpha * l + jnp.sum(p, axis=1, keepdims=True), p.astype(BF)))
            pv = [_dot(stats[g][3], v_ref[_rows(j), hl], 1, 0) for g, hl in enumerate(lanes)]
            return tuple((stats[g][0], stats[g][2], stats[g][1] * carry[g][2] + pv[g]) for g in range(G))

        init = tuple((jnp.full((T, 1), -1e30, F32), jnp.zeros((T, 1), F32), jnp.zeros((T, HEAD_DIM), F32))
                     for _ in range(G))
        carry = lax.fori_loop(0, i, lambda j, c: tile(j, c, False), init)
        for g, (m, l, acc) in enumerate(tile(i, carry, True)):
            o_ref[:, lanes[g]] = (acc / l).astype(o_ref.dtype)
            lse_ref[g] = m + jnp.log(l)

    nbp = c_row.shape[1]
    W = G * HEAD_DIM
    q_spec, k_spec, v_spec = _att_specs(L, G, q_col, k_col, v_col)
    crow_spec = pl.BlockSpec((G, nbp, LANES), lambda h, i: (h, 0, 0))
    col_spec = pl.BlockSpec((G, T, 1), lambda h, i: (h, i, 0))
    o_spec = pl.BlockSpec((T, W), lambda h, i: (i, h))
    out_shape = [jax.ShapeDtypeStruct((L, heads * HEAD_DIM), BF), jax.ShapeDtypeStruct((heads, L, 1), F32)]
    vmem = _vmem_limit([((L, W), BF)] * 2, temps=8 << 20)
    return _call(body, name=name, out_shape=out_shape, grid=(heads // G, L // T),
                 in_specs=[q_spec, k_spec, v_spec, crow_spec, col_spec], out_specs=[o_spec, col_spec],
                 sem=("parallel", "parallel"), vmem=vmem)(q_arr, k_arr, v_arr, c_row, c_col)


def _fox_bwd(name, q_arr, k_arr, v_arr, c_row, c_col, o, do, lse, heads, G, q_col, k_col, v_col):
    L = q_arr.shape[0]
    T = ATT_TILE
    nq = L // T
    scale = HEAD_DIM ** -0.5
    lanes = _head_lanes(G)

    def body(q_ref, k_ref, v_ref, crow_ref, ccol_ref, o_ref, do_ref, lse_ref, dq_ref, dk_acc, dv_acc, dcs_ref,
             drs_ref):
        i = pl.program_id(1)

        @pl.when(i == 0)
        def _():
            dk_acc[...] = jnp.zeros_like(dk_acc)
            dv_acc[...] = jnp.zeros_like(dv_acc)
            dcs_ref[...] = jnp.zeros_like(dcs_ref)

        qs = [q_ref[:, hl] for hl in lanes]
        dos = [do_ref[:, hl] for hl in lanes]
        deltas = [jnp.sum(dos[g].astype(F32) * o_ref[:, hl].astype(F32), axis=1, keepdims=True)
                  for g, hl in enumerate(lanes)]
        lses = [lse_ref[g] for g in range(G)]
        cts = [ccol_ref[g] for g in range(G)]
        row, col = _tile_iotas()

        def tile(j, carry, masked):
            ks = [k_ref[_rows(j), hl] for hl in lanes]
            qk = [_dot(qs[g], ks[g], 1, 1) for g in range(G)]
            dp = [_dot(dos[g], v_ref[_rows(j), hl], 1, 1) for g, hl in enumerate(lanes)]
            pbs, dsbs, row_sums = [], [], []
            for g in range(G):
                s = qk[g] * scale + (cts[g] - crow_ref[g, pl.ds(j, 1), :])
                if masked:
                    s = jnp.where(col <= row, s, -jnp.inf)
                p = jnp.exp(s - lses[g])
                ds = p * (dp[g] - deltas[g])
                dcs_ref[g, pl.ds(j, 1), :] += jnp.sum(ds, axis=0, keepdims=True)
                row_sums.append(carry[g][1] + jnp.sum(ds, axis=1, keepdims=True))
                pbs.append(p.astype(BF))
                dsbs.append((ds * scale).astype(BF))
            for g, hl in enumerate(lanes):
                dk_acc[_rows(j), hl] += _dot(dsbs[g], qs[g], 0, 0)
            for g, hl in enumerate(lanes):
                dv_acc[_rows(j), hl] += _dot(pbs[g], dos[g], 0, 0)
            return tuple((carry[g][0] + _dot(dsbs[g], ks[g], 1, 0), row_sums[g]) for g in range(G))

        init = tuple((jnp.zeros((T, HEAD_DIM), F32), jnp.zeros((T, 1), F32)) for _ in range(G))
        carry = lax.fori_loop(0, i, lambda j, c: tile(j, c, False), init)
        for g, (dq, row_sum) in enumerate(tile(i, carry, True)):
            dq_ref[:, lanes[g]] = dq.astype(dq_ref.dtype)
            drs_ref[g] = row_sum

    nbp = c_row.shape[1]
    WG = G * HEAD_DIM
    q_spec, k_spec, v_spec = _att_specs(L, G, q_col, k_col, v_col)
    crow_spec = pl.BlockSpec((G, nbp, LANES), lambda h, i: (h, 0, 0))
    col_spec = pl.BlockSpec((G, T, 1), lambda h, i: (h, i, 0))
    t_spec = pl.BlockSpec((T, WG), lambda h, i: (i, h))
    head_spec = pl.BlockSpec((L, WG), lambda h, i: (0, h), pipeline_mode=pl.Buffered(1))
    W = heads * HEAD_DIM
    out_shape = [jax.ShapeDtypeStruct((L, W), F32)] * 3 + [jax.ShapeDtypeStruct((heads, nbp, LANES), F32),
                                                           jax.ShapeDtypeStruct((heads, L, 1), F32)]
    vmem = _vmem_limit([], [((L, WG), BF)] * 2 + [((L, WG), F32)] * 2, temps=10 << 20)
    return _call(body, name=name, out_shape=out_shape, grid=(heads // G, nq),
                 in_specs=[q_spec, k_spec, v_spec, crow_spec, col_spec, t_spec, t_spec, col_spec],
                 out_specs=[t_spec, head_spec, head_spec, crow_spec, col_spec],
                 sem=("parallel", "arbitrary"), vmem=vmem)(q_arr, k_arr, v_arr, c_row, c_col, o, do, lse)


def _sb_logits(qk, scale, valid):
    z = qk * scale
    lb = jnp.minimum(z, 0.0) - jnp.log1p(jnp.exp(-jnp.abs(z)))
    lom = lb - z
    if valid is not None:
        lom = jnp.where(valid, lom, 0.0)
    return lb, lom


def _sb_fwd(name, proj, heads, G, q_col, k_col, v_col):
    L = proj.shape[0]
    T = ATT_TILE
    scale = HEAD_DIM ** -0.5
    lanes = _head_lanes(G)

    def body(q_ref, k_ref, v_ref, o_ref):
        i = pl.program_id(1)
        qs = [q_ref[:, hl] for hl in lanes]
        row, col = _tile_iotas()
        later_mat = jnp.where(row > col, 1.0, 0.0).astype(BF)

        def tile(j, carry, masked):
            valid = (col < row) if masked else None
            qk = [_dot(qs[g], k_ref[_rows(j), hl], 1, 1) for g, hl in enumerate(lanes)]
            logits = [_sb_logits(qk[g], scale, valid) for g in range(G)]
            pieces = [_split(lom, 2) for _, lom in logits]
            later = [_pieces_dot(pieces[g], later_mat) for g in range(G)]
            ws = []
            for g in range(G):
                w = jnp.exp(logits[g][0] + later[g] + carry[g][0])
                if masked:
                    w = jnp.where(valid, w, 0.0)
                ws.append(w.astype(BF))
            wv = [_dot(ws[g], v_ref[_rows(j), hl], 1, 0) for g, hl in enumerate(lanes)]
            return tuple((carry[g][0] + jnp.sum(logits[g][1], axis=1, keepdims=True), carry[g][1] + wv[g])
                         for g in range(G))

        init = tuple((jnp.zeros((T, 1), F32), jnp.zeros((T, HEAD_DIM), F32)) for _ in range(G))
        carry = tile(i, init, True)
        carry = lax.fori_loop(0, i, lambda t, c: tile(i - 1 - t, c, False), carry)
        for g, (_, acc) in enumerate(carry):
            o_ref[:, lanes[g]] = acc.astype(o_ref.dtype)

    W = G * HEAD_DIM
    q_spec, k_spec, v_spec = _att_specs(L, G, q_col, k_col, v_col)
    o_spec = pl.BlockSpec((T, W), lambda h, i: (i, h))
    vmem = _vmem_limit([((L, W), BF)] * 2, temps=8 << 20)
    return _call(body, name=name, out_shape=jax.ShapeDtypeStruct((L, heads * HEAD_DIM), BF),
                 grid=(heads // G, L // T), in_specs=[q_spec, k_spec, v_spec], out_specs=o_spec,
                 sem=("parallel", "parallel"), vmem=vmem)(proj, proj, proj)


def _sb_bwd(name, proj, do, heads, G, q_col, k_col, v_col):
    L = proj.shape[0]
    T = ATT_TILE
    nq = L // T
    scale = HEAD_DIM ** -0.5
    lanes = _head_lanes(G)

    def body(q_ref, k_ref, v_ref, do_ref, dq_ref, dk_acc, dv_acc, da_buf, beta_buf):
        i = pl.program_id(1)

        @pl.when(i == 0)
        def _():
            dk_acc[...] = jnp.zeros_like(dk_acc)
            dv_acc[...] = jnp.zeros_like(dv_acc)

        qs = [q_ref[:, hl] for hl in lanes]
        dos = [do_ref[:, hl] for hl in lanes]
        row, col = _tile_iotas()
        later_mat = jnp.where(row > col, 1.0, 0.0).astype(BF)
        before_mat = jnp.where(row < col, 1.0, 0.0).astype(BF)

        def pass1(j, runs, masked):
            valid = (col < row) if masked else None
            qk = [_dot(qs[g], k_ref[_rows(j), hl], 1, 1) for g, hl in enumerate(lanes)]
            dw = [_dot(dos[g], v_ref[_rows(j), hl], 1, 1) for g, hl in enumerate(lanes)]
            logits = [_sb_logits(qk[g], scale, valid) for g in range(G)]
            pieces = [_split(lom, 2) for _, lom in logits]
            later = [_pieces_dot(pieces[g], later_mat) for g in range(G)]
            ws = []
            for g in range(G):
                w = jnp.exp(logits[g][0] + later[g] + runs[g])
                if masked:
                    w = jnp.where(valid, w, 0.0)
                da_buf[g * nq + j] = dw[g] * w
                beta_buf[g * nq + j] = jnp.exp(logits[g][0])
                ws.append(w.astype(BF))
            for g, hl in enumerate(lanes):
                dv_acc[_rows(j), hl] += _dot(ws[g], dos[g], 0, 0)
            return tuple(runs[g] + jnp.sum(logits[g][1], axis=1, keepdims=True) for g in range(G))

        runs = pass1(i, tuple(jnp.zeros((T, 1), F32) for _ in range(G)), True)
        lax.fori_loop(0, i, lambda t, c: pass1(i - 1 - t, c, False), runs)

        def pass2(j, carry, masked):
            das = [da_buf[g * nq + j] for g in range(G)]
            pieces = [_split(da, 2) for da in das]
            before = [_pieces_dot(pieces[g], before_mat) for g in range(G)]
            dzbs = []
            for g in range(G):
                beta = beta_buf[g * nq + j]
                dz = das[g] * (1.0 - beta) - (carry[g][0] + before[g]) * beta
                if masked:
                    dz = jnp.where(col < row, dz, 0.0)
                dzbs.append((dz * scale).astype(BF))
            for g, hl in enumerate(lanes):
                dk_acc[_rows(j), hl] += _dot(dzbs[g], qs[g], 0, 0)
            dq = [_dot(dzbs[g], k_ref[_rows(j), hl], 1, 0) for g, hl in enumerate(lanes)]
            return tuple((carry[g][0] + jnp.sum(das[g], axis=1, keepdims=True), carry[g][1] + dq[g])
                         for g in range(G))

        init = tuple((jnp.zeros((T, 1), F32), jnp.zeros((T, HEAD_DIM), F32)) for _ in range(G))
        carry = lax.fori_loop(0, i, lambda j, c: pass2(j, c, False), init)
        for g, (_, dq) in enumerate(pass2(i, carry, True)):
            dq_ref[:, lanes[g]] = dq.astype(dq_ref.dtype)

    WG = G * HEAD_DIM
    q_spec, k_spec, v_spec = _att_specs(L, G, q_col, k_col, v_col)
    t_spec = pl.BlockSpec((T, WG), lambda h, i: (i, h))
    head_spec = pl.BlockSpec((L, WG), lambda h, i: (0, h), pipeline_mode=pl.Buffered(1))
    W = heads * HEAD_DIM
    out_shape = [jax.ShapeDtypeStruct((L, W), BF)] + [jax.ShapeDtypeStruct((L, W), F32)] * 2
    scratch = [pltpu.VMEM((G * nq, T, T), F32)] * 2
    vmem = _vmem_limit([], [((L, WG), BF)] * 2 + [((L, WG), F32)] * 2 + [((G * nq, T, T), F32)] * 2,
                       temps=6 << 20)
    return _call(body, name=name, out_shape=out_shape, grid=(heads // G, nq),
                 in_specs=[q_spec, k_spec, v_spec, t_spec], out_specs=[t_spec, head_spec, head_spec],
                 scratch=scratch, sem=("parallel", "arbitrary"), vmem=vmem)(proj, proj, proj, do)


_ANY = pl.BlockSpec(memory_space=pl.ANY)


def _mesh_pos():
    return lax.axis_index("x"), lax.axis_index("y"), lax.axis_index("c")


def _other_chips(x, y):
    return [(1 - x, y), (x, 1 - y), (1 - x, 1 - y)]


def _shard_window(ref, kind, sidx, r0, nr, cs):
    if kind == "col":
        assert cs % LANES == 0
        return ref.at[pl.ds(r0, nr), pl.ds(pl.multiple_of(sidx * cs, LANES), cs)]
    return ref.at[sidx, pl.ds(r0, nr), :]


def _place_shard(name, pos, shard, kind, dtype):
    R, C = shard.shape
    tr = _row_tile(R, C, 2, 16)

    def body(pos_ref, s_ref, o_ref):
        o_ref[...] = s_ref[...].astype(o_ref.dtype)

    if kind == "col":
        assert C % LANES == 0
        out_shape = jax.ShapeDtypeStruct((R, N_CHIPS * C), dtype)
        out_spec = pl.BlockSpec((tr, C), lambda r, pos_ref: (r, pos_ref[1]))
    else:
        out_shape = jax.ShapeDtypeStruct((N_CHIPS, R, C), dtype)
        out_spec = pl.BlockSpec((None, tr, C), lambda r, pos_ref: (pos_ref[1], r, 0))
    return _call(body, name=name, out_shape=out_shape, grid=(R // tr,),
                 in_specs=[pl.BlockSpec((tr, C), lambda r, pos_ref: (r, 0))], out_specs=out_spec, sem=("parallel",),
                 vmem=32 << 20, prefetch=1)(pos, shard)


def _all_gather_ici(name, gathered, shards, kinds):
    n = len(gathered)
    meta = [(kind, s.shape[0], s.shape[1]) for s, kind in zip(shards, kinds)]

    def body(*refs):
        bufs = refs[n:2 * n]
        send_sem, recv_sem = refs[2 * n:]
        x, y, c = _mesh_pos()
        sidx = 2 * x + y
        sends, recvs = [], []
        for t, (kind, R, cs) in enumerate(meta):
            half = R // 2
            r0 = pl.multiple_of(c * half, SUBLANES)
            mine = _shard_window(bufs[t], kind, sidx, r0, half, cs)
            for p, (px, py) in enumerate(_other_chips(x, y)):
                k = 3 * t + p
                cp = pltpu.make_async_remote_copy(src_ref=mine, dst_ref=mine, send_sem=send_sem.at[k],
                                                  recv_sem=recv_sem.at[k], device_id=(px, py, c),
                                                  device_id_type=MESH_IDS)
                cp.start()
                sends.append(cp)
                recvs.append(pltpu.make_async_remote_copy(
                    src_ref=mine, dst_ref=_shard_window(bufs[t], kind, 2 * px + py, r0, half, cs),
                    send_sem=send_sem.at[k], recv_sem=recv_sem.at[k], device_id=(px, py, c),
                    device_id_type=MESH_IDS))
        for cp in recvs:
            cp.wait_recv()
        for cp in sends:
            cp.wait_send()

    scratch = [pltpu.SemaphoreType.DMA((3 * n,)), pltpu.SemaphoreType.DMA((3 * n,))]
    out_shape = [jax.ShapeDtypeStruct(g.shape, g.dtype) for g in gathered]
    return _call(body, name=name, out_shape=out_shape, in_specs=[_ANY] * n, out_specs=[_ANY] * n, scratch=scratch,
                 aliases={t: t for t in range(n)})(*gathered)


def _all_gather_d2d(name, gathered, shards, kinds):
    n = len(gathered)
    meta = [(kind, s.shape[0], s.shape[1]) for s, kind in zip(shards, kinds)]

    def body(*refs):
        bufs = refs[n:2 * n]
        send_sem, recv_sem = refs[2 * n:]
        x, y, c = _mesh_pos()
        sends, recvs = [], []
        for t, (kind, R, cs) in enumerate(meta):
            half = R // 2
            mine = pl.multiple_of(c * half, SUBLANES)
            theirs = pl.multiple_of((1 - c) * half, SUBLANES)
            for p, (px, py) in enumerate(_other_chips(x, y)):
                k = 3 * t + p
                win = _shard_window(bufs[t], kind, 2 * px + py, mine, half, cs)
                cp = pltpu.make_async_remote_copy(src_ref=win, dst_ref=win, send_sem=send_sem.at[k],
                                                  recv_sem=recv_sem.at[k], device_id=(x, y, 1 - c),
                                                  device_id_type=MESH_IDS)
                cp.start()
                sends.append(cp)
                got = _shard_window(bufs[t], kind, 2 * px + py, theirs, half, cs)
                recvs.append(pltpu.make_async_remote_copy(src_ref=win, dst_ref=got, send_sem=send_sem.at[k],
                                                          recv_sem=recv_sem.at[k], device_id=(x, y, 1 - c),
                                                          device_id_type=MESH_IDS))
        for cp in recvs:
            cp.wait_recv()
        for cp in sends:
            cp.wait_send()

    scratch = [pltpu.SemaphoreType.DMA((3 * n,)), pltpu.SemaphoreType.DMA((3 * n,))]
    out_shape = [jax.ShapeDtypeStruct(g.shape, g.dtype) for g in gathered]
    return _call(body, name=name, out_shape=out_shape, in_specs=[_ANY] * n, out_specs=[_ANY] * n, scratch=scratch,
                 aliases={t: t for t in range(n)})(*gathered)


def _reduce_to_sibling(name, grads):
    n = len(grads)

    def body(*refs):
        srcs, dsts = refs[:n], refs[n:2 * n]
        send_sem, recv_sem = refs[2 * n:]
        x, y, c = _mesh_pos()
        copies = []
        for t, g in enumerate(grads):
            half = g.shape[1] // 2
            theirs = pl.multiple_of((1 - c) * half, SUBLANES)
            cp = pltpu.make_async_remote_copy(src_ref=srcs[t].at[:, pl.ds(theirs, half), :], dst_ref=dsts[t],
                                              send_sem=send_sem.at[t], recv_sem=recv_sem.at[t],
                                              device_id=(x, y, 1 - c), device_id_type=MESH_IDS)
            cp.start()
            copies.append(cp)
        for cp in copies:
            cp.wait()

    out_shape = [jax.ShapeDtypeStruct((N_CHIPS, g.shape[1] // 2, g.shape[2]), g.dtype) for g in grads]
    scratch = [pltpu.SemaphoreType.DMA((n,)), pltpu.SemaphoreType.DMA((n,))]
    return _call(body, name=name, out_shape=out_shape, in_specs=[_ANY] * n, out_specs=[_ANY] * n,
                 scratch=scratch)(*grads)


def _exchange_chip_partials(name, partials):
    n = len(partials)

    def body(*refs):
        srcs, dsts = refs[:n], refs[n:2 * n]
        send_sem, recv_sem = refs[2 * n:]
        x, y, c = _mesh_pos()
        copies = []
        for t in range(n):
            for p, (px, py) in enumerate(_other_chips(x, y)):
                k = 3 * t + p
                cp = pltpu.make_async_remote_copy(src_ref=srcs[t].at[2 * px + py], dst_ref=dsts[t].at[p],
                                                  send_sem=send_sem.at[k], recv_sem=recv_sem.at[k],
                                                  device_id=(px, py, c), device_id_type=MESH_IDS)
                cp.start()
                copies.append(cp)
        for cp in copies:
            cp.wait()

    out_shape = [jax.ShapeDtypeStruct((3,) + p.shape[1:], p.dtype) for p in partials]
    scratch = [pltpu.SemaphoreType.DMA((3 * n,)), pltpu.SemaphoreType.DMA((3 * n,))]
    return _call(body, name=name, out_shape=out_shape, in_specs=[_ANY] * n, out_specs=[_ANY] * n,
                 scratch=scratch)(*partials)


def _share_halves(name, totals):
    n = len(totals)

    def body(*refs):
        bufs = refs[n:2 * n]
        send_sem, recv_sem = refs[2 * n:]
        x, y, c = _mesh_pos()
        sends, recvs = [], []
        for t, g in enumerate(totals):
            half = g.shape[0] // 2
            mine = bufs[t].at[pl.ds(pl.multiple_of(c * half, SUBLANES), half), :]
            theirs = bufs[t].at[pl.ds(pl.multiple_of((1 - c) * half, SUBLANES), half), :]
            cp = pltpu.make_async_remote_copy(src_ref=mine, dst_ref=mine, send_sem=send_sem.at[t],
                                              recv_sem=recv_sem.at[t], device_id=(x, y, 1 - c),
                                              device_id_type=MESH_IDS)
            cp.start()
            sends.append(cp)
            recvs.append(pltpu.make_async_remote_copy(src_ref=mine, dst_ref=theirs, send_sem=send_sem.at[t],
                                                      recv_sem=recv_sem.at[t], device_id=(x, y, 1 - c),
                                                      device_id_type=MESH_IDS))
        for cp in recvs:
            cp.wait_recv()
        for cp in sends:
            cp.wait_send()

    out_shape = [jax.ShapeDtypeStruct(g.shape, g.dtype) for g in totals]
    scratch = [pltpu.SemaphoreType.DMA((n,)), pltpu.SemaphoreType.DMA((n,))]
    return _call(body, name=name, out_shape=out_shape, in_specs=[_ANY] * n, out_specs=[_ANY] * n, scratch=scratch,
                 aliases={t: t for t in range(n)})(*totals)


def _gather_small(name, v):
    def body(v_ref, out_ref, send_sem, recv_sem, local_sem):
        x, y, c = _mesh_pos()
        me = 4 * x + 2 * y + c
        local = pltpu.make_async_copy(v_ref, out_ref.at[me], local_sem)
        local.start()
        sends, recvs = [], []
        for k in range(1, N_DEV):
            px = 1 - x if k & 4 else x
            py = 1 - y if k & 2 else y
            pc = 1 - c if k & 1 else c
            cp = pltpu.make_async_remote_copy(src_ref=v_ref, dst_ref=out_ref.at[me], send_sem=send_sem.at[k],
                                              recv_sem=recv_sem.at[k], device_id=(px, py, pc),
                                              device_id_type=MESH_IDS)
            cp.start()
            sends.append(cp)
            recvs.append(pltpu.make_async_remote_copy(
                src_ref=v_ref, dst_ref=out_ref.at[4 * px + 2 * py + pc], send_sem=send_sem.at[k],
                recv_sem=recv_sem.at[k], device_id=(px, py, pc), device_id_type=MESH_IDS))
        for cp in recvs:
            cp.wait_recv()
        for cp in sends:
            cp.wait_send()
        local.wait()

    scratch = [pltpu.SemaphoreType.DMA((N_DEV,)), pltpu.SemaphoreType.DMA((N_DEV,)), pltpu.SemaphoreType.DMA(())]
    return _call(body, name=name, out_shape=jax.ShapeDtypeStruct((N_DEV,) + v.shape, v.dtype), in_specs=[_ANY],
                 out_specs=_ANY, scratch=scratch)(v)


def _row_tile(rows, cols, n_arrays, mult=SUBLANES):
    budget = (24 << 20) // (2 * n_arrays * _round_up(cols, LANES) * 4)
    for t in (512, 256, 128, 64, 32, 16, 8):
        if t <= max(budget, mult) and rows % t == 0 and t % mult == 0:
            return t
    raise ValueError(f"no row tile for {rows} x {cols}")


def _chip_partial(name, pos, own, recv):
    _, half, C = recv.shape
    tr = _row_tile(half, C, 3, 16)
    nh = half // tr

    def body(pos_ref, own_ref, recv_ref, out_ref):
        out_ref[...] = (own_ref[...] + recv_ref[...]).astype(BF)

    blk = pl.BlockSpec((None, tr, C), lambda s, r, pos_ref: (s, r, 0))
    own_blk = pl.BlockSpec((None, tr, C), lambda s, r, pos_ref: (s, pos_ref[0] * nh + r, 0))
    return _call(body, name=name, out_shape=jax.ShapeDtypeStruct(recv.shape, BF), grid=(N_CHIPS, nh),
                 in_specs=[own_blk, blk], out_specs=blk, sem=("parallel", "parallel"), vmem=40 << 20,
                 prefetch=1)(pos, own, recv)


def _final_half(name, pos, own, recv, others):
    _, half, C = recv.shape
    tr = _row_tile(half, C, 4, 16)
    nh = half // tr

    def body(pos_ref, own_ref, recv_ref, oth_ref, out_ref):
        acc = own_ref[...] + recv_ref[...]
        for p in range(3):
            acc = acc + oth_ref[p].astype(F32)
        out_ref[...] = acc

    own_blk = pl.BlockSpec((None, tr, C), lambda r, pos_ref: (pos_ref[1], pos_ref[0] * nh + r, 0))
    recv_blk = pl.BlockSpec((None, tr, C), lambda r, pos_ref: (pos_ref[1], r, 0))
    oth_blk = pl.BlockSpec((3, tr, C), lambda r, pos_ref: (0, r, 0))
    out_blk = pl.BlockSpec((tr, C), lambda r, pos_ref: (pos_ref[0] * nh + r, 0))
    return _call(body, name=name, out_shape=jax.ShapeDtypeStruct((2 * half, C), F32), grid=(nh,),
                 in_specs=[own_blk, recv_blk, oth_blk], out_specs=out_blk, sem=("parallel",), vmem=40 << 20,
                 prefetch=1)(pos, own, recv, others)


def _adamw(name, w, g, m, v):
    R, C = w.shape
    tr = R if R < SUBLANES or R % SUBLANES else _row_tile(R, C, 7)
    c1 = 1.0 - ADAM_B1 ** ADAM_STEP
    c2 = 1.0 - ADAM_B2 ** ADAM_STEP

    def body(w_ref, g_ref, m_ref, v_ref, d_ref, nm_ref, nv_ref):
        gv = g_ref[...]
        nm = ADAM_B1 * m_ref[...] + (1.0 - ADAM_B1) * gv
        nv = ADAM_B2 * v_ref[...] + (1.0 - ADAM_B2) * (gv * gv)
        d_ref[...] = -ADAM_LR * ((nm / c1) / (jnp.sqrt(nv / c2) + ADAM_EPS) + ADAM_WD * w_ref[...])
        nm_ref[...] = nm
        nv_ref[...] = nv

    blk = pl.BlockSpec((tr, C), lambda r: (r, 0))
    out_shape = [jax.ShapeDtypeStruct((R, C), F32)] * 3
    return _call(body, name=name, out_shape=out_shape, grid=(R // tr,), in_specs=[blk] * 4, out_specs=[blk] * 3,
                 sem=("parallel",), vmem=40 << 20)(w, g, m, v)


def _sum_devices(name, gathered):
    _, R, _ = gathered.shape

    def body(g_ref, o_ref):
        acc = g_ref[0]
        for d in range(1, N_DEV):
            acc = acc + g_ref[d]
        o_ref[...] = acc

    return _call(body, name=name, out_shape=jax.ShapeDtypeStruct((R, LANES), F32), grid=(1,),
                 in_specs=[pl.BlockSpec((N_DEV, R, LANES), lambda i: (0, 0, 0))],
                 out_specs=pl.BlockSpec((R, LANES), lambda i: (0, 0)), sem=("arbitrary",), vmem=16 << 20)(gathered)


def _ffn_fwd(tag, h, gain, wg, wu, wd, tm):
    L, D = h.shape
    F = wg.shape[1]
    tn = _pick(F, (512, 256, 128))
    n = _rmsnorm_fwd(f"{tag}_norm", h, gain, tm)

    def gate_up(accs, _):
        a, u = accs
        return a, u, a * _sigmoid(a) * u

    mn = _mn(tm, tn)
    a, u, s = _matmul(f"{tag}_gate_up", [n], [wg, wu], [(0, 0, 0), (0, 1, 1)], 2, gate_up,
                      [((L, F), BF) + mn] * 3, M=L, N=F, K=D, tm=tm, tn=tn, tk=D, n_outer=True)
    td = _pick(D, (512, 256, 128))
    (h_out,) = _matmul(f"{tag}_down", [s], [wd], [(0, 0, 0)], 1,
                       lambda accs, ex: [ex[0] + FFN_RESIDUAL_WEIGHT * accs[0]], [((L, D), F32) + _mn(tm, td)],
                       M=L, N=D, K=F, tm=tm, tn=td, tk=F, extras=[(h,) + _mn(tm, td)], n_outer=True)
    return h_out, (h, n, a, u, s)


def _ffn_bwd(tag, dh, dh_bf, saved, gain, wg, wu, wd, tm, cs_ff):
    h, n, a, u, s = saved
    L, D = h.shape
    F = wg.shape[1]
    tn = _pick(F, (512, 256, 128))
    tkl = _pick(L, (1408, 384, 256, 128))

    def act_grad(accs, ex):
        ds = FFN_RESIDUAL_WEIGHT * accs[0]
        av, uv = ex[0].astype(F32), ex[1].astype(F32)
        sg = _sigmoid(av)
        return ds * uv * sg * (1.0 + av * (1.0 - sg)), ds * av * sg

    mn = _mn(tm, tn)
    da, du = _matmul(f"{tag}_dact", [dh_bf], [wd], [(0, 0, 0)], 1, act_grad, [((L, F), BF) + mn] * 2, M=L, N=F,
                     K=D, tm=tm, tn=tn, tk=D, tb=True, extras=[(a,) + mn, (u,) + mn], n_outer=True)
    td = _pick(D, (512, 256, 128))
    (dwd,) = _matmul(f"{tag}_dwd", [s], [dh_bf], [(0, 0, 0)], 1, lambda accs, _: [FFN_RESIDUAL_WEIGHT * accs[0]],
                     [((N_CHIPS, cs_ff, D), F32, (None, cs_ff, td), lambda i, j: (i, 0, j))], M=F, N=D, K=L,
                     tm=cs_ff, tn=td, tk=tkl, ta=True)
    tdn = _pick(D, (256, 128))
    (dn,) = _matmul(f"{tag}_dn", [da, du], [wg, wu], [(0, 0, 0), (1, 1, 0)], 1, lambda accs, _: accs,
                    [((L, D), F32) + _mn(tm, tdn)], M=L, N=D, K=F, tm=tm, tn=tdn, tk=F, tb=True)
    tmw = _pick(D, (512, 256, 128))
    shard_out = ((N_CHIPS, D, cs_ff), F32, (None, tmw, cs_ff), lambda i, j: (j, i, 0))
    dwg, dwu = _matmul(f"{tag}_dwgu", [n], [da, du], [(0, 0, 0), (0, 1, 1)], 2, lambda accs, _: accs,
                       [shard_out] * 2, M=D, N=F, K=L, tm=tmw, tn=cs_ff, tk=tkl, ta=True)
    dh_in, dh_in_bf, dgain = _rmsnorm_bwd(f"{tag}_dnorm", dn, h, gain, dh, tm)
    return dh_in, dh_in_bf, dgain, dwg, dwu, dwd


def kernel(x, meta_tokens, ffn1_norm, ffn1_w_gate, ffn1_w_up, ffn1_w_down, mix_norm, w_in, b_forget, fox_q_norm, fox_k_norm, w_branch_fox, w_branch_sb, w_out, ffn2_norm, ffn2_w_gate, ffn2_w_up, ffn2_w_down, loss_target, m_meta_tokens, m_ffn1_norm, m_ffn1_w_gate, m_ffn1_w_up, m_ffn1_w_down, m_mix_norm, m_w_in, m_b_forget, m_fox_q_norm, m_fox_k_norm, m_w_branch_fox, m_w_branch_sb, m_w_out, m_ffn2_norm, m_ffn2_w_gate, m_ffn2_w_up, m_ffn2_w_down, v_meta_tokens, v_ffn1_norm, v_ffn1_w_gate, v_ffn1_w_up, v_ffn1_w_down, v_mix_norm, v_w_in, v_b_forget, v_fox_q_norm, v_fox_k_norm, v_w_branch_fox, v_w_branch_sb, v_w_out, v_ffn2_norm, v_ffn2_w_gate, v_ffn2_w_up, v_ffn2_w_down):
    weights = dict(meta_tokens=meta_tokens, ffn1_norm=ffn1_norm, ffn1_w_gate=ffn1_w_gate, ffn1_w_up=ffn1_w_up,
                   ffn1_w_down=ffn1_w_down, mix_norm=mix_norm, w_in=w_in, b_forget=b_forget, fox_q_norm=fox_q_norm,
                   fox_k_norm=fox_k_norm, w_branch_fox=w_branch_fox, w_branch_sb=w_branch_sb, w_out=w_out,
                   ffn2_norm=ffn2_norm, ffn2_w_gate=ffn2_w_gate, ffn2_w_up=ffn2_w_up, ffn2_w_down=ffn2_w_down)
    moments_m = dict(meta_tokens=m_meta_tokens, ffn1_norm=m_ffn1_norm, ffn1_w_gate=m_ffn1_w_gate,
                     ffn1_w_up=m_ffn1_w_up, ffn1_w_down=m_ffn1_w_down, mix_norm=m_mix_norm, w_in=m_w_in,
                     b_forget=m_b_forget, fox_q_norm=m_fox_q_norm, fox_k_norm=m_fox_k_norm,
                     w_branch_fox=m_w_branch_fox, w_branch_sb=m_w_branch_sb, w_out=m_w_out, ffn2_norm=m_ffn2_norm,
                     ffn2_w_gate=m_ffn2_w_gate, ffn2_w_up=m_ffn2_w_up, ffn2_w_down=m_ffn2_w_down)
    moments_v = dict(meta_tokens=v_meta_tokens, ffn1_norm=v_ffn1_norm, ffn1_w_gate=v_ffn1_w_gate,
                     ffn1_w_up=v_ffn1_w_up, ffn1_w_down=v_ffn1_w_down, mix_norm=v_mix_norm, w_in=v_w_in,
                     b_forget=v_b_forget, fox_q_norm=v_fox_q_norm, fox_k_norm=v_fox_k_norm,
                     w_branch_fox=v_w_branch_fox, w_branch_sb=v_w_branch_sb, w_out=v_w_out, ffn2_norm=v_ffn2_norm,
                     ffn2_w_gate=v_ffn2_w_gate, ffn2_w_up=v_ffn2_w_up, ffn2_w_down=v_ffn2_w_down)
    names = list(weights)

    _, S, D = x.shape
    NM = meta_tokens.shape[0]
    L_real = NM + S
    L = _round_up(L_real, ATT_TILE)
    nblk = L // ATT_TILE
    nbp = _round_up(nblk, SUBLANES)
    cs_ff = ffn1_w_gate.shape[2]
    F = N_CHIPS * cs_ff
    H = b_forget.shape[1]
    FW = w_branch_fox.shape[1]
    SW = w_branch_sb.shape[1]
    HS = SW // HEAD_DIM
    cs_in = w_in.shape[2]
    W_IN = N_CHIPS * cs_in
    assert FW == H * HEAD_DIM and W_IN == 3 * FW + H + 3 * SW + 2 * D
    cs_d = D // N_CHIPS
    tm = _pick(L, (384, 256, 128))

    x_pos, y_pos, c_pos = _mesh_pos()
    pos = jnp.stack([c_pos, 2 * x_pos + y_pos]).astype(I32)

    shard_of = {
        "ffn1_w_gate": (ffn1_w_gate[0], "col"), "ffn1_w_up": (ffn1_w_up[0], "col"),
        "ffn1_w_down": (ffn1_w_down[0], "maj"), "w_in": (w_in[0], "maj"),
        "w_branch_fox": (w_branch_fox[0], "col"), "w_branch_sb": (w_branch_sb[0], "col"),
        "w_out": (w_out[0], "maj"), "ffn2_w_gate": (ffn2_w_gate[0], "col"), "ffn2_w_up": (ffn2_w_up[0], "col"),
        "ffn2_w_down": (ffn2_w_down[0], "maj"),
    }
    g_names = list(shard_of) + ["meta_tokens"]
    shards = [shard_of[k][0] for k in shard_of] + [meta_tokens]
    kinds = [shard_of[k][1] for k in shard_of] + ["col"]
    dtypes = [BF] * len(shard_of) + [F32]
    gathered = [_place_shard(f"place_{k}", pos, s, kind, dt)
                for k, s, kind, dt in zip(g_names, shards, kinds, dtypes)]
    gathered = _all_gather_ici("gather_weights_ici", gathered, shards, kinds)
    gathered = _all_gather_d2d("gather_weights_d2d", gathered, shards, kinds)
    full = dict(zip(g_names, gathered))
    wg1, wu1, wg2, wu2 = full["ffn1_w_gate"], full["ffn1_w_up"], full["ffn2_w_gate"], full["ffn2_w_up"]
    wd1 = full["ffn1_w_down"].reshape(F, D)
    wd2 = full["ffn2_w_down"].reshape(F, D)
    wbf, wbs = full["w_branch_fox"], full["w_branch_sb"]
    wo = full["w_out"].reshape(D, D)
    w_in_full = jnp.transpose(full["w_in"], (1, 0, 2)).reshape(D, W_IN)
    c_f = 3 * FW
    QKV_S, GATES, FCOL = 3 * FW, 3 * FW + 3 * SW, 3 * FW + 3 * SW + 2 * D
    W_PROJ = FCOL + LANES
    w_proj = jnp.concatenate([w_in_full[:, :c_f], w_in_full[:, c_f + H:],
                              jnp.pad(w_in_full[:, c_f:c_f + H], ((0, 0), (0, LANES - H)))], axis=1)

    h0 = jnp.concatenate([full["meta_tokens"], x[0], jnp.zeros((L - L_real, D), F32)], axis=0)
    target = jnp.concatenate([jnp.zeros((NM, D), F32), loss_target[0], jnp.zeros((L - L_real, D), F32)], axis=0)

    h1, saved1 = _ffn_fwd("ffn1", h0, ffn1_norm, wg1, wu1, wd1, tm)

    n2 = _rmsnorm_fwd("mix_norm", h1, mix_norm, tm)
    tp = _pick(FCOL, (512, 256, 128))
    (proj,) = _matmul("in_proj", [n2], [w_proj], [(0, 0, 0)], 1, lambda accs, _: accs,
                      [((L, FCOL), BF) + _mn(tm, tp)], M=L, N=FCOL, K=D, tm=tm, tn=tp, tk=D, n_outer=True)
    (f_logit,) = _matmul("forget_proj", [n2], [w_proj], [(0, 0, 0)], 1, lambda accs, _: accs,
                         [((L, LANES), F32) + _mn(tm, LANES)], M=L, N=LANES, K=D, tm=tm, tn=LANES, tk=D,
                         b_off=FCOL // LANES)
    fl = jnp.pad(jnp.transpose(f_logit[:, :H]).reshape(H, nblk, LANES), ((0, 0), (0, nbp - nblk), (0, 0)))
    bias = jnp.broadcast_to(b_forget[0][:, None, None], (H, 1, LANES))
    c_row = _cum_fwd("forget_cumsum", fl, bias, nblk)
    c_col = c_row[:, :nblk].reshape(H, L, 1)
    gq, gk = fox_q_norm[0][:, None, :], fox_k_norm[0][:, None, :]
    qn, kn = _qknorm_fwd("fox_qk_norm", proj, gq, gk, H, 0, FW, tm)
    o_fox, lse = _fox_fwd("fox_attention", qn, kn, proj, c_row, c_col, H, ATT_HEADS, 0, 0, 2 * FW)
    o_sb = _sb_fwd("sb_attention", proj, HS, min(HS, ATT_HEADS_SB_FWD), QKV_S, QKV_S + SW, QKV_S + 2 * SW)

    td = _pick(D, (512, 256, 128))

    def merge(accs, ex):
        bf_, bs_ = accs
        return _sigmoid(ex[0].astype(F32)) * bf_ + _sigmoid(ex[1].astype(F32)) * bs_, bf_, bs_

    merged, br_f, br_s = _matmul("branch_merge", [o_fox, o_sb], [wbf, wbs], [(0, 0, 0), (1, 1, 1)], 2, merge,
                                 [((L, D), BF) + _mn(tm, td)] * 3, M=L, N=D, K=FW, tm=tm, tn=td, tk=FW,
                                 extras=[(proj,) + _mn(tm, td, GATES), (proj,) + _mn(tm, td, GATES + D)],
                                 n_outer=True)
    (h2,) = _matmul("out_proj", [merged], [wo], [(0, 0, 0)], 1, lambda accs, ex: [ex[0] + accs[0]],
                    [((L, D), F32) + _mn(tm, td)], M=L, N=D, K=D, tm=tm, tn=td, tk=D, extras=[(h1,) + _mn(tm, td)],
                    n_outer=True)

    h3, saved2 = _ffn_fwd("ffn2", h2, ffn2_norm, wg2, wu2, wd2, tm)
    dh3, dh3_bf, loss_part = _loss_grad("loss", h3, target, NM, S, tm)

    dh2, dh2_bf, dg_ffn2, dwg2, dwu2, dwd2 = _ffn_bwd("ffn2", dh3, dh3_bf, saved2, ffn2_norm, wg2, wu2, wd2, tm,
                                                      cs_ff)

    def gate_grad(accs, ex):
        dm = accs[0]
        gf, gs, bf_, bs_ = [e.astype(F32) for e in ex]
        sf, ss = _sigmoid(gf), _sigmoid(gs)
        return dm * bf_ * sf * (1.0 - sf), dm * bs_ * ss * (1.0 - ss), dm * sf, dm * ss

    mn_d = _mn(tm, td)
    dgf, dgs, dbr_f, dbr_s = _matmul(
        "d_merged", [dh2_bf], [wo], [(0, 0, 0)], 1, gate_grad, [((L, D), BF) + mn_d] * 4, M=L, N=D, K=D, tm=tm,
        tn=td, tk=D, tb=True, extras=[(proj,) + _mn(tm, td, GATES), (proj,) + _mn(tm, td, GATES + D),
                                      (br_f,) + mn_d, (br_s,) + mn_d], n_outer=True)
    tkl = _pick(L, (1408, 384, 256, 128))
    (dwo,) = _matmul("d_w_out", [merged], [dh2_bf], [(0, 0, 0)], 1, lambda accs, _: accs,
                     [((N_CHIPS, cs_d, D), F32, (None, cs_d, td), lambda i, j: (i, 0, j))], M=D, N=D, K=L, tm=cs_d,
                     tn=td, tk=tkl, ta=True)
    tw = _pick(FW, (512, 256, 128))
    do_fox, do_sb = _matmul("d_branch_in", [dbr_f, dbr_s], [wbf, wbs], [(0, 0, 0), (1, 1, 1)], 2,
                            lambda accs, _: accs, [((L, FW), BF) + _mn(tm, tw)] * 2, M=L, N=FW, K=D, tm=tm, tn=tw,
                            tk=D, tb=True)
    tmb = _pick(FW, (1024, 512, 256, 128))
    dwbf, dwbs = _matmul("d_w_branch", [o_fox, o_sb], [dbr_f, dbr_s], [(0, 0, 0), (1, 1, 1)], 2,
                         lambda accs, _: accs,
                         [((N_CHIPS, FW, cs_d), F32, (None, tmb, cs_d), lambda i, j: (j, i, 0))] * 2, M=FW, N=D, K=L,
                         tm=tmb, tn=cs_d, tk=tkl, ta=True)

    dqn, dkn, dfv, dcs, drs = _fox_bwd("fox_attention_bwd", qn, kn, proj, c_row, c_col, o_fox, do_fox, lse, H,
                                       ATT_HEADS, 0, 0, 2 * FW)
    dsq, dsk, dsv = _sb_bwd("sb_attention_bwd", proj, do_sb, HS, ATT_HEADS, QKV_S, QKV_S + SW,
                            QKV_S + 2 * SW)
    dfq, dfk, dgq, dgk = _qknorm_bwd("fox_qk_norm_bwd", proj, dqn, dkn, gq, gk, H, 0, FW, tm)
    drs_row = jnp.pad(drs.reshape(H, nblk, LANES), ((0, 0), (0, nbp - nblk), (0, 0)))
    dfl, dbias = _cum_bwd("forget_cumsum_bwd", drs_row, dcs, fl, bias, nblk)
    dfl_cols = jnp.pad(jnp.transpose(dfl[:, :nblk].reshape(H, L)), ((0, 0), (0, LANES - H))).astype(BF)
    dproj = jnp.concatenate([dfq, dfk, dfv.astype(BF), dsq, dsk.astype(BF), dsv.astype(BF), dgf, dgs, dfl_cols],
                            axis=1)

    tdn = _pick(D, (256, 128))
    (dn2,) = _matmul("d_mix_norm_in", [dproj], [w_proj], [(0, 0, 0)], 1, lambda accs, _: accs,
                     [((L, D), F32) + _mn(tm, tdn)], M=L, N=D, K=W_PROJ, tm=tm, tn=tdn, tk=W_PROJ, tb=True)
    tmw = _pick(D, (1024, 512, 256, 128))
    tnp = _pick(W_PROJ, (1152, 640, 512, 384, 256, 128))
    (dw_proj,) = _matmul("d_w_in", [n2], [dproj], [(0, 0, 0)], 1, lambda accs, _: accs,
                         [((D, W_PROJ), F32) + _mn(tmw, tnp)], M=D, N=W_PROJ, K=L, tm=tmw, tn=tnp, tk=tkl,
                         ta=True)
    dh1, dh1_bf, dg_mix = _rmsnorm_bwd("mix_dnorm", dn2, h1, mix_norm, dh2, tm)

    dh0, _, dg_ffn1, dwg1, dwu1, dwd1 = _ffn_bwd("ffn1", dh1, dh1_bf, saved1, ffn1_norm, wg1, wu1, wd1, tm, cs_ff)
    grad_x = dh0[NM:L_real][None]

    dw_in_ref = jnp.concatenate([dw_proj[:, :c_f], dw_proj[:, FCOL:FCOL + H], dw_proj[:, c_f:FCOL]], axis=1)
    dw_in = jnp.transpose(dw_in_ref.reshape(D, N_CHIPS, cs_in), (1, 0, 2))
    big = ["ffn1_w_gate", "ffn1_w_up", "ffn1_w_down", "w_in", "w_branch_fox", "w_branch_sb", "w_out",
           "ffn2_w_gate", "ffn2_w_up", "ffn2_w_down"]
    local = [dwg1, dwu1, dwd1, dw_in, dwbf, dwbs, dwo, dwg2, dwu2, dwd2]
    from_sibling = _reduce_to_sibling("grads_to_sibling", local)
    partials = [_chip_partial(f"chip_sum_{k}", pos, g, r) for k, g, r in zip(big, local, from_sibling)]
    from_chips = _exchange_chip_partials("grads_to_owner", partials)
    halves = [_final_half(f"total_{k}", pos, g, r, o) for k, g, r, o in zip(big, local, from_sibling, from_chips)]
    grads = dict(zip(big, _share_halves("grads_to_core_pair", halves)))

    small = [loss_part[:, :1].reshape(1), dh0[:NM].reshape(-1), dg_ffn1.reshape(-1), dg_mix.reshape(-1),
             dg_ffn2.reshape(-1), dbias[:, 0, 0], dgq.reshape(-1), dgk.reshape(-1)]
    sizes = [s.shape[0] for s in small]
    flat = jnp.concatenate(small)
    rows = _round_up(-(-flat.shape[0] // LANES), SUBLANES)
    packed = jnp.pad(flat, (0, rows * LANES - flat.shape[0])).reshape(rows, LANES)
    total = _sum_devices("sum_small", _gather_small("gather_small", packed)).reshape(-1)
    pieces, off = [], 0
    for n_el in sizes:
        pieces.append(total[off:off + n_el])
        off += n_el
    loss = pieces[0][0]
    d_meta = lax.dynamic_slice_in_dim(pieces[1].reshape(NM, D), pos[1] * cs_d, cs_d, axis=1)
    grads.update(meta_tokens=d_meta, ffn1_norm=pieces[2].reshape(1, D), mix_norm=pieces[3].reshape(1, D),
                 ffn2_norm=pieces[4].reshape(1, D), b_forget=pieces[5].reshape(1, H),
                 fox_q_norm=pieces[6].reshape(1, H, HEAD_DIM), fox_k_norm=pieces[7].reshape(1, H, HEAD_DIM))

    out_g, out_d, out_m, out_v = [], [], [], []
    for k in names:
        w = weights[k]
        shape2 = (1, w.size) if w.size < LANES * SUBLANES else (w.size // w.shape[-1], w.shape[-1])
        g2 = grads[k].reshape(shape2)
        d, nm, nv = _adamw(f"adamw_{k}", w.reshape(shape2), g2, moments_m[k].reshape(shape2),
                           moments_v[k].reshape(shape2))
        out_g.append(g2.reshape(w.shape))
        out_d.append(d.reshape(w.shape))
        out_m.append(nm.reshape(w.shape))
        out_v.append(nv.reshape(w.shape))
    return (loss, grad_x, *out_g, *out_d, *out_m, *out_v)
```

```python
import functools

import jax
import jax.numpy as jnp
from jax import lax
from jax.experimental import pallas as pl
from jax.experimental.pallas import tpu as pltpu

F32 = jnp.float32
BF = jnp.bfloat16
I32 = jnp.int32

HEAD_DIM = 128
RMS_EPS = 1e-6
FFN_RESIDUAL_WEIGHT = 0.5
ADAM_LR = 0.001
ADAM_B1 = 0.9
ADAM_B2 = 0.999
ADAM_EPS = 1e-08
ADAM_WD = 0.01
ADAM_STEP = 10

LANES = 128
SUBLANES = 8
ATT_TILE = 128
ATT_HEADS = 4
ATT_HEADS_SB_FWD = 8
VMEM_CAP = 56 * 1024 * 1024
MESH_IDS = pl.DeviceIdType.MESH
N_CHIPS = 4
N_DEV = 8


def _pick(n, cands):
    for c in cands:
        if c <= n and n % c == 0:
            return c
    raise ValueError(f"no tile for {n} among {cands}")


def _round_up(n, m):
    return (n + m - 1) // m * m


def _tile_bytes(shape, dtype):
    item = jnp.dtype(dtype).itemsize
    dims = [d for d in shape if d is not None]
    if not dims:
        return 4 * LANES * SUBLANES
    last = _round_up(dims[-1], LANES)
    sub = _round_up(dims[-2], SUBLANES * (4 // item)) if len(dims) > 1 else 1
    lead = 1
    for d in dims[:-2]:
        lead *= d
    return lead * sub * last * item


def _vmem_limit(blocks, scratch=(), temps=0):
    need = 2 * sum(_tile_bytes(s, d) for s, d in blocks) + sum(_tile_bytes(s, d) for s, d in scratch) + temps
    return int(min(VMEM_CAP, max(need + (4 << 20), 16 << 20)))


def _call(body, *, name, out_shape, grid=(), in_specs=None, out_specs=None, scratch=(), sem=None, vmem=None,
          aliases=None, prefetch=0):
    params = pltpu.CompilerParams(dimension_semantics=sem, vmem_limit_bytes=vmem)
    if prefetch:
        grid_spec = pltpu.PrefetchScalarGridSpec(num_scalar_prefetch=prefetch, grid=grid, in_specs=in_specs,
                                                 out_specs=out_specs, scratch_shapes=scratch)
        return pl.pallas_call(body, out_shape=out_shape, grid_spec=grid_spec, name=name, compiler_params=params,
                              input_output_aliases=aliases or {})
    return pl.pallas_call(body, out_shape=out_shape, grid=grid, in_specs=in_specs, out_specs=out_specs,
                          scratch_shapes=scratch, name=name, compiler_params=params,
                          input_output_aliases=aliases or {})


def _dot(a, b, ca, cb):
    return lax.dot_general(a, b, (((ca,), (cb,)), ((), ())), preferred_element_type=F32)


def _sigmoid(x):
    return 1.0 / (1.0 + jnp.exp(-x))


def _log_sigmoid(x):
    return jnp.minimum(x, 0.0) - jnp.log1p(jnp.exp(-jnp.abs(x)))


def _split(x, parts):
    pieces = []
    rem = x
    for p in range(parts):
        piece = rem.astype(BF)
        pieces.append(piece)
        if p + 1 < parts:
            rem = rem - piece.astype(F32)
    return pieces


def _pieces_dot(pieces, ones_bf):
    out = None
    for piece in pieces:
        d = _dot(piece, ones_bf, 1, 0)
        out = d if out is None else out + d
    return out


def _split_dot(x, ones_bf, parts):
    return _pieces_dot(_split(x, parts), ones_bf)


def _matmul(name, a_list, b_list, pairs, n_acc, epi, outs, *, M, N, K, tm, tn, tk, ta=False, tb=False,
            extras=(), n_outer=False, b_off=0):
    gi, gj, nk = M // tm, N // tn, K // tk
    assert gi * tm == M and gj * tn == N and nk * tk == K, (name, M, N, K, tm, tn, tk)
    n_a, n_b, n_e, n_o = len(a_list), len(b_list), len(extras), len(outs)

    def ij(g0, g1):
        return (g1, g0) if n_outer else (g0, g1)

    def a_map(g0, g1, k):
        i, _ = ij(g0, g1)
        return (k, i) if ta else (i, k)

    def b_map(g0, g1, k):
        _, j = ij(g0, g1)
        return (j + b_off, k) if tb else (k, j + b_off)

    def tile_map(fn):
        return lambda g0, g1, k: fn(*ij(g0, g1))

    a_block = (tk, tm) if ta else (tm, tk)
    b_block = (tn, tk) if tb else (tk, tn)
    in_specs = ([pl.BlockSpec(a_block, a_map)] * n_a + [pl.BlockSpec(b_block, b_map)] * n_b
                + [pl.BlockSpec(bs, tile_map(fn)) for _, bs, fn in extras])
    out_specs = [pl.BlockSpec(bs, tile_map(fn)) for _, _, bs, fn in outs]
    out_shape = [jax.ShapeDtypeStruct(s, d) for s, d, _, _ in outs]
    scratch = [pltpu.VMEM((tm, tn), F32) for _ in range(n_acc)] if nk > 1 else []

    def body(*refs):
        a_refs = refs[:n_a]
        b_refs = refs[n_a:n_a + n_b]
        e_refs = refs[n_a + n_b:n_a + n_b + n_e]
        o_refs = refs[n_a + n_b + n_e:n_a + n_b + n_e + n_o]
        acc_refs = refs[n_a + n_b + n_e + n_o:]

        def products():
            accs = [None] * n_acc
            for ai, bi, ci in pairs:
                a = a_refs[ai][...]
                b = b_refs[bi][...]
                d = _dot(a.astype(BF), b.astype(BF), 0 if ta else 1, 1 if tb else 0)
                accs[ci] = d if accs[ci] is None else accs[ci] + d
            return accs

        def finish(accs):
            res = epi(accs, [e[...] for e in e_refs])
            for o_ref, r in zip(o_refs, res):
                o_ref[...] = r.reshape(o_ref.shape).astype(o_ref.dtype)

        if nk == 1:
            finish(products())
        else:
            k = pl.program_id(2)

            @pl.when(k == 0)
            def _():
                for acc in acc_refs:
                    acc[...] = jnp.zeros_like(acc)

            for acc, d in zip(acc_refs, products()):
                acc[...] += d

            @pl.when(k == nk - 1)
            def _():
                finish([acc[...] for acc in acc_refs])

    blocks = ([(a_block, a.dtype) for a in a_list] + [(b_block, b.dtype) for b in b_list]
              + [(bs, e.dtype) for e, bs, _ in extras] + [(bs, d) for _, d, bs, _ in outs])
    vmem = _vmem_limit(blocks, [((tm, tn), F32)] * (n_acc if nk > 1 else 0), temps=6 * tm * tn * 4)
    grid = (gj, gi, nk) if n_outer else (gi, gj, nk)
    fn = _call(body, name=name, out_shape=out_shape, grid=grid, in_specs=in_specs, out_specs=out_specs,
               scratch=scratch, sem=("parallel", "parallel", "arbitrary"), vmem=vmem)
    return fn(*a_list, *b_list, *[e for e, _, _ in extras])


def _mn(tm, tn, col0=0):
    assert col0 % tn == 0
    off = col0 // tn
    return (tm, tn), (lambda i, j: (i, j + off))


def _rmsnorm_fwd(name, x, gain, tr):
    L, D = x.shape

    def body(x_ref, g_ref, o_ref):
        xv = x_ref[...]
        r = lax.rsqrt(jnp.mean(xv * xv, axis=-1, keepdims=True) + RMS_EPS)
        o_ref[...] = (xv * r * g_ref[...]).astype(BF)

    row = pl.BlockSpec((tr, D), lambda i: (i, 0))
    vec = pl.BlockSpec((1, D), lambda i: (0, 0))
    vmem = _vmem_limit([((tr, D), F32), ((tr, D), BF)], temps=3 * tr * D * 4)
    return _call(body, name=name, out_shape=jax.ShapeDtypeStruct((L, D), BF), grid=(L // tr,), in_specs=[row, vec],
                 out_specs=row, sem=("parallel",), vmem=vmem)(x, gain)


def _rmsnorm_bwd(name, dn, x, gain, dres, tr):
    L, D = x.shape
    steps = L // tr

    def body(dn_ref, x_ref, g_ref, dres_ref, dx_ref, dxb_ref, dg_ref):
        i = pl.program_id(0)
        xv = x_ref[...]
        r = lax.rsqrt(jnp.mean(xv * xv, axis=-1, keepdims=True) + RMS_EPS)
        xhat = xv * r
        dy = dn_ref[...]
        dxhat = dy * g_ref[...]
        dx = dres_ref[...] + r * (dxhat - xhat * jnp.mean(dxhat * xhat, axis=-1, keepdims=True))
        dx_ref[...] = dx
        dxb_ref[...] = dx.astype(BF)

        @pl.when(i == 0)
        def _():
            dg_ref[...] = jnp.zeros_like(dg_ref)

        dg_ref[...] += jnp.sum(dy * xhat, axis=0, keepdims=True)

    row = pl.BlockSpec((tr, D), lambda i: (i, 0))
    vec = pl.BlockSpec((1, D), lambda i: (0, 0))
    vmem = _vmem_limit([((tr, D), F32)] * 4 + [((tr, D), BF)], temps=4 * tr * D * 4)
    out_shape = [jax.ShapeDtypeStruct((L, D), F32), jax.ShapeDtypeStruct((L, D), BF),
                 jax.ShapeDtypeStruct((1, D), F32)]
    return _call(body, name=name, out_shape=out_shape, grid=(steps,), in_specs=[row, row, vec, row],
                 out_specs=[row, row, vec], sem=("arbitrary",), vmem=vmem)(dn, x, gain, dres)


def _loss_grad(name, h, target, n_meta, n_seq, tr):
    L, D = h.shape

    def body(h_ref, t_ref, dh_ref, dhb_ref, loss_ref):
        i = pl.program_id(0)
        rows = i * tr + lax.broadcasted_iota(I32, (tr, 1), 0)
        valid = (rows >= n_meta) & (rows < n_meta + n_seq)
        diff = jnp.where(valid, h_ref[...] - t_ref[...], 0.0)
        dh = diff * (1.0 / D)
        dh_ref[...] = dh
        dhb_ref[...] = dh.astype(BF)

        @pl.when(i == 0)
        def _():
            loss_ref[...] = jnp.zeros_like(loss_ref)

        loss_ref[...] += jnp.sum(diff * diff) * (0.5 / D)

    row = pl.BlockSpec((tr, D), lambda i: (i, 0))
    acc = pl.BlockSpec((1, LANES), lambda i: (0, 0))
    vmem = _vmem_limit([((tr, D), F32)] * 3 + [((tr, D), BF)], temps=3 * tr * D * 4)
    out_shape = [jax.ShapeDtypeStruct((L, D), F32), jax.ShapeDtypeStruct((L, D), BF),
                 jax.ShapeDtypeStruct((1, LANES), F32)]
    return _call(body, name=name, out_shape=out_shape, grid=(L // tr,), in_specs=[row, row],
                 out_specs=[row, row, acc], sem=("arbitrary",), vmem=vmem)(h, target)


def _qknorm_fwd(name, proj, gq, gk, heads, q_col, k_col, tr):
    L = proj.shape[0]

    def body(q_ref, k_ref, gq_ref, gk_ref, qn_ref, kn_ref):
        for x_ref, g_ref, o_ref in ((q_ref, gq_ref, qn_ref), (k_ref, gk_ref, kn_ref)):
            xv = x_ref[...].astype(F32)
            r = lax.rsqrt(jnp.mean(xv * xv, axis=-1, keepdims=True) + RMS_EPS)
            o_ref[...] = (xv * r * g_ref[...]).astype(BF)

    qb, kb = q_col // HEAD_DIM, k_col // HEAD_DIM
    in_specs = [pl.BlockSpec((tr, HEAD_DIM), lambda h, i: (i, qb + h)),
                pl.BlockSpec((tr, HEAD_DIM), lambda h, i: (i, kb + h)),
                pl.BlockSpec((None, 1, HEAD_DIM), lambda h, i: (h, 0, 0)),
                pl.BlockSpec((None, 1, HEAD_DIM), lambda h, i: (h, 0, 0))]
    out = pl.BlockSpec((tr, HEAD_DIM), lambda h, i: (i, h))
    out_shape = [jax.ShapeDtypeStruct((L, heads * HEAD_DIM), BF)] * 2
    return _call(body, name=name, out_shape=out_shape, grid=(heads, L // tr), in_specs=in_specs,
                 out_specs=[out, out], sem=("parallel", "parallel"), vmem=16 << 20)(proj, proj, gq, gk)


def _qknorm_bwd(name, proj, dqn, dkn, gq, gk, heads, q_col, k_col, tr):
    L = proj.shape[0]

    def body(q_ref, k_ref, dqn_ref, dkn_ref, gq_ref, gk_ref, dq_ref, dk_ref, dgq_ref, dgk_ref):
        i = pl.program_id(1)
        for x_ref, dy_ref, g_ref, dx_ref, dg_ref in ((q_ref, dqn_ref, gq_ref, dq_ref, dgq_ref),
                                                     (k_ref, dkn_ref, gk_ref, dk_ref, dgk_ref)):
            xv = x_ref[...].astype(F32)
            r = lax.rsqrt(jnp.mean(xv * xv, axis=-1, keepdims=True) + RMS_EPS)
            xhat = xv * r
            dy = dy_ref[...].astype(F32)
            dxhat = dy * g_ref[...]
            dx_ref[...] = (r * (dxhat - xhat * jnp.mean(dxhat * xhat, axis=-1, keepdims=True))).astype(BF)

            @pl.when(i == 0)
            def _():
                dg_ref[...] = jnp.zeros_like(dg_ref)

            dg_ref[...] += jnp.sum(dy * xhat, axis=0, keepdims=True)

    qb, kb = q_col // HEAD_DIM, k_col // HEAD_DIM
    tile = pl.BlockSpec((tr, HEAD_DIM), lambda h, i: (i, h))
    gain = pl.BlockSpec((None, 1, HEAD_DIM), lambda h, i: (h, 0, 0))
    in_specs = [pl.BlockSpec((tr, HEAD_DIM), lambda h, i: (i, qb + h)),
                pl.BlockSpec((tr, HEAD_DIM), lambda h, i: (i, kb + h)), tile, tile, gain, gain]
    out_shape = [jax.ShapeDtypeStruct((L, heads * HEAD_DIM), BF)] * 2 + [
        jax.ShapeDtypeStruct((heads, 1, HEAD_DIM), F32)] * 2
    return _call(body, name=name, out_shape=out_shape, grid=(heads, L // tr), in_specs=in_specs,
                 out_specs=[tile, tile, gain, gain], sem=("parallel", "arbitrary"),
                 vmem=16 << 20)(proj, proj, dqn, dkn, gq, gk)


def _tri(cmp):
    r = lax.broadcasted_iota(I32, (LANES, LANES), 0)
    c = lax.broadcasted_iota(I32, (LANES, LANES), 1)
    return jnp.where(cmp(r, c), 1.0, 0.0).astype(BF)


def _cum_fwd(name, fl, bias, n_rows):
    H, nbp, _ = fl.shape

    def body(fl_ref, b_ref, c_ref, tot_ref):
        lf = _log_sigmoid(fl_ref[...] + b_ref[...])
        c_ref[...] = _split_dot(lf, _tri(lambda r, c: r <= c), 3)
        tot_ref[...] = _split_dot(lf, jnp.ones((LANES, LANES), BF), 3)

        def step(r, carry):
            c_ref[pl.ds(r, 1), :] = c_ref[pl.ds(r, 1), :] + carry
            return carry + tot_ref[pl.ds(r, 1), :]

        lax.fori_loop(0, n_rows, step, jnp.zeros((1, LANES), F32))

    blk = pl.BlockSpec((None, nbp, LANES), lambda h: (h, 0, 0))
    vec = pl.BlockSpec((None, 1, LANES), lambda h: (h, 0, 0))
    return _call(body, name=name, out_shape=jax.ShapeDtypeStruct((H, nbp, LANES), F32), grid=(H,),
                 in_specs=[blk, vec], out_specs=blk, scratch=[pltpu.VMEM((nbp, LANES), F32)], sem=("parallel",),
                 vmem=16 << 20)(fl, bias)


def _cum_bwd(name, drs, dcs, fl, bias, n_rows):
    H, nbp, _ = fl.shape

    def body(drs_ref, dcs_ref, fl_ref, b_ref, dfl_ref, db_ref, rin_ref, tot_ref):
        dc = drs_ref[...] - dcs_ref[...]
        rin_ref[...] = _split_dot(dc, _tri(lambda r, c: r >= c), 3)
        tot_ref[...] = _split_dot(dc, jnp.ones((LANES, LANES), BF), 3)
        dfl_ref[...] = jnp.zeros_like(dfl_ref)

        def step(t, carry):
            r = n_rows - 1 - t
            x = fl_ref[pl.ds(r, 1), :] + b_ref[...]
            dfl_ref[pl.ds(r, 1), :] = (rin_ref[pl.ds(r, 1), :] + carry) * _sigmoid(-x)
            return carry + tot_ref[pl.ds(r, 1), :]

        lax.fori_loop(0, n_rows, step, jnp.zeros((1, LANES), F32))
        db_ref[...] = jnp.zeros_like(db_ref) + jnp.sum(dfl_ref[...])

    blk = pl.BlockSpec((None, nbp, LANES), lambda h: (h, 0, 0))
    vec = pl.BlockSpec((None, 1, LANES), lambda h: (h, 0, 0))
    out_shape = [jax.ShapeDtypeStruct((H, nbp, LANES), F32), jax.ShapeDtypeStruct((H, 1, LANES), F32)]
    return _call(body, name=name, out_shape=out_shape, grid=(H,), in_specs=[blk, blk, blk, vec],
                 out_specs=[blk, vec], scratch=[pltpu.VMEM((nbp, LANES), F32)] * 2, sem=("parallel",),
                 vmem=16 << 20)(drs, dcs, fl, bias)


def _att_specs(L, G, q_col, k_col, v_col):
    T = ATT_TILE
    W = G * HEAD_DIM
    assert q_col % W == 0 and k_col % W == 0 and v_col % W == 0
    qb, kb, vb = q_col // W, k_col // W, v_col // W
    q_spec = pl.BlockSpec((T, W), lambda h, i: (i, qb + h))
    k_spec = pl.BlockSpec((L, W), lambda h, i: (0, kb + h), pipeline_mode=pl.Buffered(1))
    v_spec = pl.BlockSpec((L, W), lambda h, i: (0, vb + h), pipeline_mode=pl.Buffered(1))
    return q_spec, k_spec, v_spec


def _head_lanes(G):
    return [slice(g * HEAD_DIM, (g + 1) * HEAD_DIM) for g in range(G)]


def _tile_iotas():
    T = ATT_TILE
    return lax.broadcasted_iota(I32, (T, T), 0), lax.broadcasted_iota(I32, (T, T), 1)


def _rows(j):
    return pl.ds(pl.multiple_of(j * ATT_TILE, ATT_TILE), ATT_TILE)


def _fox_fwd(name, q_arr, k_arr, v_arr, c_row, c_col, heads, G, q_col, k_col, v_col):
    L = q_arr.shape[0]
    T = ATT_TILE
    scale = HEAD_DIM ** -0.5
    lanes = _head_lanes(G)

    def body(q_ref, k_ref, v_ref, crow_ref, ccol_ref, o_ref, lse_ref):
        i = pl.program_id(1)
        qs = [q_ref[:, hl] for hl in lanes]
        cts = [ccol_ref[g] for g in range(G)]
        row, col = _tile_iotas()

        def tile(j, carry, masked):
            qk = [_dot(qs[g], k_ref[_rows(j), hl], 1, 1) for g, hl in enumerate(lanes)]
            stats = []
            for g in range(G):
                m, l, _ = carry[g]
                s = qk[g] * scale + (cts[g] - crow_ref[g, pl.ds(j, 1), :])
                if masked:
                    s = jnp.where(col <= row, s, -jnp.inf)
                m_new = jnp.maximum(m, jnp.max(s, axis=1, keepdims=True))
                alpha = jnp.exp(m - m_new)
                p = jnp.exp(s - m_new)
                stats.append((m_new, alpha, alpha * l + jnp.sum(p, axis=1, keepdims=True), p.astype(BF)))
            pv = [_dot(stats[g][3], v_ref[_rows(j), hl], 1, 0) for g, hl in enumerate(lanes)]
            return tuple((stats[g][0], stats[g][2], stats[g][1] * carry[g][2] + pv[g]) for g in range(G))

        init = tuple((jnp.full((T, 1), -1e30, F32), jnp.zeros((T, 1), F32), jnp.zeros((T, HEAD_DIM), F32))
                     for _ in range(G))
        carry = lax.fori_loop(0, i, lambda j, c: tile(j, c, False), init)
        for g, (m, l, acc) in enumerate(tile(i, carry, True)):
            o_ref[:, lanes[g]] = (acc / l).astype(o_ref.dtype)
            lse_ref[g] = m + jnp.log(l)

    nbp = c_row.shape[1]
    W = G * HEAD_DIM
    q_spec, k_spec, v_spec = _att_specs(L, G, q_col, k_col, v_col)
    crow_spec = pl.BlockSpec((G, nbp, LANES), lambda h, i: (h, 0, 0))
    col_spec = pl.BlockSpec((G, T, 1), lambda h, i: (h, i, 0))
    o_spec = pl.BlockSpec((T, W), lambda h, i: (i, h))
    out_shape = [jax.ShapeDtypeStruct((L, heads * HEAD_DIM), BF), jax.ShapeDtypeStruct((heads, L, 1), F32)]
    vmem = _vmem_limit([((L, W), BF)] * 2, temps=8 << 20)
    return _call(body, name=name, out_shape=out_shape, grid=(heads // G, L // T),
                 in_specs=[q_spec, k_spec, v_spec, crow_spec, col_spec], out_specs=[o_spec, col_spec],
                 sem=("parallel", "parallel"), vmem=vmem)(q_arr, k_arr, v_arr, c_row, c_col)


def _fox_bwd(name, q_arr, k_arr, v_arr, c_row, c_col, o, do, lse, heads, G, q_col, k_col, v_col):
    L = q_arr.shape[0]
    T = ATT_TILE
    nq = L // T
    scale = HEAD_DIM ** -0.5
    lanes = _head_lanes(G)

    def body(q_ref, k_ref, v_ref, crow_ref, ccol_ref, o_ref, do_ref, lse_ref, dq_ref, dk_acc, dv_acc, dcs_ref,
             drs_ref):
        i = pl.program_id(1)

        @pl.when(i == 0)
        def _():
            dk_acc[...] = jnp.zeros_like(dk_acc)
            dv_acc[...] = jnp.zeros_like(dv_acc)
            dcs_ref[...] = jnp.zeros_like(dcs_ref)

        qs = [q_ref[:, hl] for hl in lanes]
        dos = [do_ref[:, hl] for hl in lanes]
        deltas = [jnp.sum(dos[g].astype(F32) * o_ref[:, hl].astype(F32), axis=1, keepdims=True)
                  for g, hl in enumerate(lanes)]
        lses = [lse_ref[g] for g in range(G)]
        cts = [ccol_ref[g] for g in range(G)]
        row, col = _tile_iotas()

        def tile(j, carry, masked):
            ks = [k_ref[_rows(j), hl] for hl in lanes]
            qk = [_dot(qs[g], ks[g], 1, 1) for g in range(G)]
            dp = [_dot(dos[g], v_ref[_rows(j), hl], 1, 1) for g, hl in enumerate(lanes)]
            pbs, dsbs, row_sums = [], [], []
            for g in range(G):
                s = qk[g] * scale + (cts[g] - crow_ref[g, pl.ds(j, 1), :])
                if masked:
                    s = jnp.where(col <= row, s, -jnp.inf)
                p = jnp.exp(s - lses[g])
                ds = p * (dp[g] - deltas[g])
                dcs_ref[g, pl.ds(j, 1), :] += jnp.sum(ds, axis=0, keepdims=True)
                row_sums.append(carry[g][1] + jnp.sum(ds, axis=1, keepdims=True))
                pbs.append(p.astype(BF))
                dsbs.append((ds * scale).astype(BF))
            for g, hl in enumerate(lanes):
                dk_acc[_rows(j), hl] += _dot(dsbs[g], qs[g], 0, 0)
            for g, hl in enumerate(lanes):
                dv_acc[_rows(j), hl] += _dot(pbs[g], dos[g], 0, 0)
            return tuple((carry[g][0] + _dot(dsbs[g], ks[g], 1, 0), row_sums[g]) for g in range(G))

        init = tuple((jnp.zeros((T, HEAD_DIM), F32), jnp.zeros((T, 1), F32)) for _ in range(G))
        carry = lax.fori_loop(0, i, lambda j, c: tile(j, c, False), init)
        for g, (dq, row_sum) in enumerate(tile(i, carry, True)):
            dq_ref[:, lanes[g]] = dq.astype(dq_ref.dtype)
            drs_ref[g] = row_sum

    nbp = c_row.shape[1]
    WG = G * HEAD_DIM
    q_spec, k_spec, v_spec = _att_specs(L, G, q_col, k_col, v_col)
    crow_spec = pl.BlockSpec((G, nbp, LANES), lambda h, i: (h, 0, 0))
    col_spec = pl.BlockSpec((G, T, 1), lambda h, i: (h, i, 0))
    t_spec = pl.BlockSpec((T, WG), lambda h, i: (i, h))
    head_spec = pl.BlockSpec((L, WG), lambda h, i: (0, h), pipeline_mode=pl.Buffered(1))
    W = heads * HEAD_DIM
    out_shape = [jax.ShapeDtypeStruct((L, W), F32)] * 3 + [jax.ShapeDtypeStruct((heads, nbp, LANES), F32),
                                                           jax.ShapeDtypeStruct((heads, L, 1), F32)]
    vmem = _vmem_limit([], [((L, WG), BF)] * 2 + [((L, WG), F32)] * 2, temps=10 << 20)
    return _call(body, name=name, out_shape=out_shape, grid=(heads // G, nq),
                 in_specs=[q_spec, k_spec, v_spec, crow_spec, col_spec, t_spec, t_spec, col_spec],
                 out_specs=[t_spec, head_spec, head_spec, crow_spec, col_spec],
                 sem=("parallel", "arbitrary"), vmem=vmem)(q_arr, k_arr, v_arr, c_row, c_col, o, do, lse)


def _sb_logits(qk, scale, valid):
    z = qk * scale
    lb = jnp.minimum(z, 0.0) - jnp.log1p(jnp.exp(-jnp.abs(z)))
    lom = lb - z
    if valid is not None:
        lom = jnp.where(valid, lom, 0.0)
    return lb, lom


def _sb_fwd(name, proj, heads, G, q_col, k_col, v_col):
    L = proj.shape[0]
    T = ATT_TILE
    scale = HEAD_DIM ** -0.5
    lanes = _head_lanes(G)

    def body(q_ref, k_ref, v_ref, o_ref):
        i = pl.program_id(1)
        qs = [q_ref[:, hl] for hl in lanes]
        row, col = _tile_iotas()
        later_mat = jnp.where(row > col, 1.0, 0.0).astype(BF)

        def tile(j, carry, masked):
            valid = (col < row) if masked else None
            qk = [_dot(qs[g], k_ref[_rows(j), hl], 1, 1) for g, hl in enumerate(lanes)]
            logits = [_sb_logits(qk[g], scale, valid) for g in range(G)]
            pieces = [_split(lom, 2) for _, lom in logits]
            later = [_pieces_dot(pieces[g], later_mat) for g in range(G)]
            ws = []
            for g in range(G):
                w = jnp.exp(logits[g][0] + later[g] + carry[g][0])
                if masked:
                    w = jnp.where(valid, w, 0.0)
                ws.append(w.astype(BF))
            wv = [_dot(ws[g], v_ref[_rows(j), hl], 1, 0) for g, hl in enumerate(lanes)]
            return tuple((carry[g][0] + jnp.sum(logits[g][1], axis=1, keepdims=True), carry[g][1] + wv[g])
                         for g in range(G))

        init = tuple((jnp.zeros((T, 1), F32), jnp.zeros((T, HEAD_DIM), F32)) for _ in range(G))
        carry = tile(i, init, True)
        carry = lax.fori_loop(0, i, lambda t, c: tile(i - 1 - t, c, False), carry)
        for g, (_, acc) in enumerate(carry):
            o_ref[:, lanes[g]] = acc.astype(o_ref.dtype)

    W = G * HEAD_DIM
    q_spec, k_spec, v_spec = _att_specs(L, G, q_col, k_col, v_col)
    o_spec = pl.BlockSpec((T, W), lambda h, i: (i, h))
    vmem = _vmem_limit([((L, W), BF)] * 2, temps=8 << 20)
    return _call(body, name=name, out_shape=jax.ShapeDtypeStruct((L, heads * HEAD_DIM), BF),
                 grid=(heads // G, L // T), in_specs=[q_spec, k_spec, v_spec], out_specs=o_spec,
                 sem=("parallel", "parallel"), vmem=vmem)(proj, proj, proj)


def _sb_bwd(name, proj, do, heads, G, q_col, k_col, v_col):
    L = proj.shape[0]
    T = ATT_TILE
    nq = L // T
    scale = HEAD_DIM ** -0.5
    lanes = _head_lanes(G)

    def body(q_ref, k_ref, v_ref, do_ref, dq_ref, dk_acc, dv_acc, da_buf, beta_buf):
        i = pl.program_id(1)

        @pl.when(i == 0)
        def _():
            dk_acc[...] = jnp.zeros_like(dk_acc)
            dv_acc[...] = jnp.zeros_like(dv_acc)

        qs = [q_ref[:, hl] for hl in lanes]
        dos = [do_ref[:, hl] for hl in lanes]
        row, col = _tile_iotas()
        later_mat = jnp.where(row > col, 1.0, 0.0).astype(BF)
        before_mat = jnp.where(row < col, 1.0, 0.0).astype(BF)

        def pass1(j, runs, masked):
            valid = (col < row) if masked else None
            qk = [_dot(qs[g], k_ref[_rows(j), hl], 1, 1) for g, hl in enumerate(lanes)]
            dw = [_dot(dos[g], v_ref[_rows(j), hl], 1, 1) for g, hl in enumerate(lanes)]
            logits = [_sb_logits(qk[g], scale, valid) for g in range(G)]
            pieces = [_split(lom, 2) for _, lom in logits]
            later = [_pieces_dot(pieces[g], later_mat) for g in range(G)]
            ws = []
            for g in range(G):
                w = jnp.exp(logits[g][0] + later[g] + runs[g])
                if masked:
                    w = jnp.where(valid, w, 0.0)
                da_buf[g * nq + j] = dw[g] * w
                beta_buf[g * nq + j] = jnp.exp(logits[g][0])
                ws.append(w.astype(BF))
            for g, hl in enumerate(lanes):
                dv_acc[_rows(j), hl] += _dot(ws[g], dos[g], 0, 0)
            return tuple(runs[g] + jnp.sum(logits[g][1], axis=1, keepdims=True) for g in range(G))

        runs = pass1(i, tuple(jnp.zeros((T, 1), F32) for _ in range(G)), True)
        lax.fori_loop(0, i, lambda t, c: pass1(i - 1 - t, c, False), runs)

        def pass2(j, carry, masked):
            das = [da_buf[g * nq + j] for g in range(G)]
            pieces = [_split(da, 2) for da in das]
            before = [_pieces_dot(pieces[g], before_mat) for g in range(G)]
            dzbs = []
            for g in range(G):
                beta = beta_buf[g * nq + j]
                dz = das[g] * (1.0 - beta) - (carry[g][0] + before[g]) * beta
                if masked:
                    dz = jnp.where(col < row, dz, 0.0)
                dzbs.append((dz * scale).astype(BF))
            for g, hl in enumerate(lanes):
                dk_acc[_rows(j), hl] += _dot(dzbs[g], qs[g], 0, 0)
            dq = [_dot(dzbs[g], k_ref[_rows(j), hl], 1, 0) for g, hl in enumerate(lanes)]
            return tuple((carry[g][0] + jnp.sum(das[g], axis=1, keepdims=True), carry[g][1] + dq[g])
                         for g in range(G))

        init = tuple((jnp.zeros((T, 1), F32), jnp.zeros((T, HEAD_DIM), F32)) for _ in range(G))
        carry = lax.fori_loop(0, i, lambda j, c: pass2(j, c, False), init)
        for g, (_, dq) in enumerate(pass2(i, carry, True)):
            dq_ref[:, lanes[g]] = dq.astype(dq_ref.dtype)

    WG = G * HEAD_DIM
    q_spec, k_spec, v_spec = _att_specs(L, G, q_col, k_col, v_col)
    t_spec = pl.BlockSpec((T, WG), lambda h, i: (i, h))
    head_spec = pl.BlockSpec((L, WG), lambda h, i: (0, h), pipeline_mode=pl.Buffered(1))
    W = heads * HEAD_DIM
    out_shape = [jax.ShapeDtypeStruct((L, W), BF)] + [jax.ShapeDtypeStruct((L, W), F32)] * 2
    scratch = [pltpu.VMEM((G * nq, T, T), F32)] * 2
    vmem = _vmem_limit([], [((L, WG), BF)] * 2 + [((L, WG), F32)] * 2 + [((G * nq, T, T), F32)] * 2,
                       temps=6 << 20)
    return _call(body, name=name, out_shape=out_shape, grid=(heads // G, nq),
                 in_specs=[q_spec, k_spec, v_spec, t_spec], out_specs=[t_spec, head_spec, head_spec],
                 scratch=scratch, sem=("parallel", "arbitrary"), vmem=vmem)(proj, proj, proj, do)


_ANY = pl.BlockSpec(memory_space=pl.ANY)


def _mesh_pos():
    return lax.axis_index("x"), lax.axis_index("y"), lax.axis_index("c")


def _other_chips(x, y):
    return [(1 - x, y), (x, 1 - y), (1 - x, 1 - y)]


def _shard_window(ref, kind, sidx, r0, nr, cs):
    if kind == "col":
        assert cs % LANES == 0
        return ref.at[pl.ds(r0, nr), pl.ds(pl.multiple_of(sidx * cs, LANES), cs)]
    return ref.at[sidx, pl.ds(r0, nr), :]


def _place_shard(name, pos, shard, kind, dtype):
    R, C = shard.shape
    tr = _row_tile(R, C, 2, 16)

    def body(pos_ref, s_ref, o_ref):
        o_ref[...] = s_ref[...].astype(o_ref.dtype)

    if kind == "col":
        assert C % LANES == 0
        out_shape = jax.ShapeDtypeStruct((R, N_CHIPS * C), dtype)
        out_spec = pl.BlockSpec((tr, C), lambda r, pos_ref: (r, pos_ref[1]))
    else:
        out_shape = jax.ShapeDtypeStruct((N_CHIPS, R, C), dtype)
        out_spec = pl.BlockSpec((None, tr, C), lambda r, pos_ref: (pos_ref[1], r, 0))
    return _call(body, name=name, out_shape=out_shape, grid=(R // tr,),
                 in_specs=[pl.BlockSpec((tr, C), lambda r, pos_ref: (r, 0))], out_specs=out_spec, sem=("parallel",),
                 vmem=32 << 20, prefetch=1)(pos, shard)


_HBM = pl.BlockSpec(memory_space=pltpu.HBM)
_SEM = pl.BlockSpec(memory_space=pltpu.SEMAPHORE)
_KEEP_ORDER = pltpu.SideEffectType.DATAFLOW_SIDE_EFFECTING


def _in_hbm(x):
    return pltpu.with_memory_space_constraint(x, pltpu.HBM)


def _gather_copies(bufs, meta, send_sem, recv_sem):
    x, y, c = _mesh_pos()
    out = []
    for t, (kind, R, cs) in enumerate(meta):
        half = R // 2
        r0 = pl.multiple_of(c * half, SUBLANES)
        mine = _shard_window(bufs[t], kind, 2 * x + y, r0, half, cs)
        for p, (px, py) in enumerate(_other_chips(x, y)):
            k = 3 * t + p
            args = dict(send_sem=send_sem.at[k], recv_sem=recv_sem.at[k], device_id=(px, py, c),
                        device_id_type=MESH_IDS)
            out.append((pltpu.make_async_remote_copy(src_ref=mine, dst_ref=mine, **args),
                        pltpu.make_async_remote_copy(
                            src_ref=mine, dst_ref=_shard_window(bufs[t], kind, 2 * px + py, r0, half, cs), **args)))
    return out


def _all_gather_ici_start(name, gathered, shards, kinds):
    n = len(gathered)
    meta = [(kind, s.shape[0], s.shape[1]) for s, kind in zip(shards, kinds)]

    def body(*refs):
        send_sem, recv_sem = refs[n], refs[n + 1]
        bufs = refs[n + 2:2 * n + 2]
        token = refs[2 * n + 2]
        for send, _ in _gather_copies(bufs, meta, send_sem, recv_sem):
            send.start()
        token[...] = jnp.zeros_like(token)

    out_shape = (pltpu.SemaphoreType.DMA((3 * n,)), pltpu.SemaphoreType.DMA((3 * n,)),
                 *[pltpu.HBM(g.shape, g.dtype) for g in gathered], jax.ShapeDtypeStruct((SUBLANES, LANES), F32))
    out_specs = (_SEM, _SEM, *[_HBM] * n, pl.BlockSpec(memory_space=pltpu.VMEM))
    res = pl.pallas_call(body, out_shape=out_shape, in_specs=[_HBM] * n, out_specs=out_specs,
                         input_output_aliases={t: 2 + t for t in range(n)}, name=name,
                         compiler_params=pltpu.CompilerParams(has_side_effects=_KEEP_ORDER))(
        *[_in_hbm(g) for g in gathered])
    return res[0], res[1], list(res[2:2 + n]), res[2 + n]


def _all_gather_ici_wait(name, bufs, send_sem, recv_sem, after, shards, kinds):
    n = len(bufs)
    meta = [(kind, s.shape[0], s.shape[1]) for s, kind in zip(shards, kinds)]

    def body(*refs):
        for send, recv in _gather_copies(refs[:n], meta, refs[n], refs[n + 1]):
            send.wait_send()
            recv.wait_recv()

    out_shape = tuple(pltpu.HBM(b.shape, b.dtype) for b in bufs)
    res = pl.pallas_call(body, out_shape=out_shape, in_specs=[_HBM] * n + [_SEM, _SEM, _ANY],
                         out_specs=tuple([_HBM] * n), input_output_aliases={t: t for t in range(n)}, name=name,
                         compiler_params=pltpu.CompilerParams(has_side_effects=_KEEP_ORDER))(
        *bufs, send_sem, recv_sem, after)
    return list(res)


def _all_gather_d2d(name, gathered, shards, kinds):
    n = len(gathered)
    meta = [(kind, s.shape[0], s.shape[1]) for s, kind in zip(shards, kinds)]

    def body(*refs):
        bufs = refs[n:2 * n]
        send_sem, recv_sem = refs[2 * n:]
        x, y, c = _mesh_pos()
        sends, recvs = [], []
        for t, (kind, R, cs) in enumerate(meta):
            half = R // 2
            mine = pl.multiple_of(c * half, SUBLANES)
            theirs = pl.multiple_of((1 - c) * half, SUBLANES)
            for p, (px, py) in enumerate(_other_chips(x, y)):
                k = 3 * t + p
                win = _shard_window(bufs[t], kind, 2 * px + py, mine, half, cs)
                cp = pltpu.make_async_remote_copy(src_ref=win, dst_ref=win, send_sem=send_sem.at[k],
                                                  recv_sem=recv_sem.at[k], device_id=(x, y, 1 - c),
                                                  device_id_type=MESH_IDS)
                cp.start()
                sends.append(cp)
                got = _shard_window(bufs[t], kind, 2 * px + py, theirs, half, cs)
                recvs.append(pltpu.make_async_remote_copy(src_ref=win, dst_ref=got, send_sem=send_sem.at[k],
                                                          recv_sem=recv_sem.at[k], device_id=(x, y, 1 - c),
                                                          device_id_type=MESH_IDS))
        for cp in recvs:
            cp.wait_recv()
        for cp in sends:
            cp.wait_send()

    scratch = [pltpu.SemaphoreType.DMA((3 * n,)), pltpu.SemaphoreType.DMA((3 * n,))]
    out_shape = [jax.ShapeDtypeStruct(g.shape, g.dtype) for g in gathered]
    return _call(body, name=name, out_shape=out_shape, in_specs=[_ANY] * n, out_specs=[_ANY] * n, scratch=scratch,
                 aliases={t: t for t in range(n)})(*gathered)


def _reduce_to_sibling(name, grads):
    n = len(grads)

    def body(*refs):
        srcs, dsts = refs[:n], refs[n:2 * n]
        send_sem, recv_sem = refs[2 * n:]
        x, y, c = _mesh_pos()
        copies = []
        for t, g in enumerate(grads):
            half = g.shape[1] // 2
            theirs = pl.multiple_of((1 - c) * half, SUBLANES)
            cp = pltpu.make_async_remote_copy(src_ref=srcs[t].at[:, pl.ds(theirs, half), :], dst_ref=dsts[t],
                                              send_sem=send_sem.at[t], recv_sem=recv_sem.at[t],
                                              device_id=(x, y, 1 - c), device_id_type=MESH_IDS)
            cp.start()
            copies.append(cp)
        for cp in copies:
            cp.wait()

    out_shape = [jax.ShapeDtypeStruct((N_CHIPS, g.shape[1] // 2, g.shape[2]), g.dtype) for g in grads]
    scratch = [pltpu.SemaphoreType.DMA((n,)), pltpu.SemaphoreType.DMA((n,))]
    return _call(body, name=name, out_shape=out_shape, in_specs=[_ANY] * n, out_specs=[_ANY] * n,
                 scratch=scratch)(*grads)


def _exchange_copies(parts, lands, send_sem, recv_sem):
    x, y, c = _mesh_pos()
    out = []
    for t in range(len(parts)):
        for p, (px, py) in enumerate(_other_chips(x, y)):
            k = 3 * t + p
            out.append(pltpu.make_async_remote_copy(src_ref=parts[t].at[2 * px + py], dst_ref=lands[t].at[p],
                                                    send_sem=send_sem.at[k], recv_sem=recv_sem.at[k],
                                                    device_id=(px, py, c), device_id_type=MESH_IDS))
    return out


def _exchange_start(name, partials):
    n = len(partials)
    lands = [lax.empty((3,) + p.shape[1:], p.dtype) for p in partials]

    def body(*refs):
        send_sem, recv_sem = refs[2 * n], refs[2 * n + 1]
        parts, zones = refs[2 * n + 2:3 * n + 2], refs[3 * n + 2:4 * n + 2]
        token = refs[4 * n + 2]
        for cp in _exchange_copies(parts, zones, send_sem, recv_sem):
            cp.start()
        token[...] = jnp.zeros_like(token)

    out_shape = (pltpu.SemaphoreType.DMA((3 * n,)), pltpu.SemaphoreType.DMA((3 * n,)),
                 *[pltpu.HBM(a.shape, a.dtype) for a in partials + lands],
                 jax.ShapeDtypeStruct((SUBLANES, LANES), F32))
    out_specs = (_SEM, _SEM, *[_HBM] * (2 * n), pl.BlockSpec(memory_space=pltpu.VMEM))
    res = pl.pallas_call(body, out_shape=out_shape, in_specs=[_HBM] * (2 * n), out_specs=out_specs,
                         input_output_aliases={t: 2 + t for t in range(2 * n)}, name=name,
                         compiler_params=pltpu.CompilerParams(has_side_effects=_KEEP_ORDER))(
        *[_in_hbm(a) for a in partials + lands])
    return res[0], res[1], list(res[2:2 + n]), list(res[2 + n:2 + 2 * n]), res[2 + 2 * n]


def _exchange_wait(name, parts, lands, send_sem, recv_sem, after):
    n = len(parts)

    def body(*refs):
        for cp in _exchange_copies(refs[:n], refs[n:2 * n], refs[2 * n], refs[2 * n + 1]):
            cp.wait_send()
            cp.wait_recv()

    out_shape = tuple(pltpu.HBM(a.shape, a.dtype) for a in parts + lands)
    res = pl.pallas_call(body, out_shape=out_shape, in_specs=[_HBM] * (2 * n) + [_SEM, _SEM, _ANY],
                         out_specs=tuple([_HBM] * (2 * n)), input_output_aliases={t: t for t in range(2 * n)},
                         name=name, compiler_params=pltpu.CompilerParams(has_side_effects=_KEEP_ORDER))(
        *parts, *lands, send_sem, recv_sem, after)
    return list(res[n:])


def _share_halves(name, totals):
    n = len(totals)

    def body(*refs):
        bufs = refs[n:2 * n]
        send_sem, recv_sem = refs[2 * n:]
        x, y, c = _mesh_pos()
        sends, recvs = [], []
        for t, g in enumerate(totals):
            half = g.shape[0] // 2
            mine = bufs[t].at[pl.ds(pl.multiple_of(c * half, SUBLANES), half), :]
            theirs = bufs[t].at[pl.ds(pl.multiple_of((1 - c) * half, SUBLANES), half), :]
            cp = pltpu.make_async_remote_copy(src_ref=mine, dst_ref=mine, send_sem=send_sem.at[t],
                                              recv_sem=recv_sem.at[t], device_id=(x, y, 1 - c),
                                              device_id_type=MESH_IDS)
            cp.start()
            sends.append(cp)
            recvs.append(pltpu.make_async_remote_copy(src_ref=mine, dst_ref=theirs, send_sem=send_sem.at[t],
                                                      recv_sem=recv_sem.at[t], device_id=(x, y, 1 - c),
                                                      device_id_type=MESH_IDS))
        for cp in recvs:
            cp.wait_recv()
        for cp in sends:
            cp.wait_send()

    out_shape = [jax.ShapeDtypeStruct(g.shape, g.dtype) for g in totals]
    scratch = [pltpu.SemaphoreType.DMA((n,)), pltpu.SemaphoreType.DMA((n,))]
    return _call(body, name=name, out_shape=out_shape, in_specs=[_ANY] * n, out_specs=[_ANY] * n, scratch=scratch,
                 aliases={t: t for t in range(n)})(*totals)


def _gather_small(name, v):
    def body(v_ref, out_ref, send_sem, recv_sem, local_sem):
        x, y, c = _mesh_pos()
        me = 4 * x + 2 * y + c
        local = pltpu.make_async_copy(v_ref, out_ref.at[me], local_sem)
        local.start()
        sends, recvs = [], []
        for k in range(1, N_DEV):
            px = 1 - x if k & 4 else x
            py = 1 - y if k & 2 else y
            pc = 1 - c if k & 1 else c
            cp = pltpu.make_async_remote_copy(src_ref=v_ref, dst_ref=out_ref.at[me], send_sem=send_sem.at[k],
                                              recv_sem=recv_sem.at[k], device_id=(px, py, pc),
                                              device_id_type=MESH_IDS)
            cp.start()
            sends.append(cp)
            recvs.append(pltpu.make_async_remote_copy(
                src_ref=v_ref, dst_ref=out_ref.at[4 * px + 2 * py + pc], send_sem=send_sem.at[k],
                recv_sem=recv_sem.at[k], device_id=(px, py, pc), device_id_type=MESH_IDS))
        for cp in recvs:
            cp.wait_recv()
        for cp in sends:
            cp.wait_send()
        local.wait()

    scratch = [pltpu.SemaphoreType.DMA((N_DEV,)), pltpu.SemaphoreType.DMA((N_DEV,)), pltpu.SemaphoreType.DMA(())]
    return _call(body, name=name, out_shape=jax.ShapeDtypeStruct((N_DEV,) + v.shape, v.dtype), in_specs=[_ANY],
                 out_specs=_ANY, scratch=scratch)(v)


def _row_tile(rows, cols, n_arrays, mult=SUBLANES):
    budget = (24 << 20) // (2 * n_arrays * _round_up(cols, LANES) * 4)
    for t in (512, 256, 128, 64, 32, 16, 8):
        if t <= max(budget, mult) and rows % t == 0 and t % mult == 0:
            return t
    raise ValueError(f"no row tile for {rows} x {cols}")


def _chip_partial(name, pos, own, recv):
    _, half, C = recv.shape
    tr = _row_tile(half, C, 3, 16)
    nh = half // tr

    def body(pos_ref, own_ref, recv_ref, out_ref):
        out_ref[...] = (own_ref[...] + recv_ref[...]).astype(BF)

    blk = pl.BlockSpec((None, tr, C), lambda s, r, pos_ref: (s, r, 0))
    own_blk = pl.BlockSpec((None, tr, C), lambda s, r, pos_ref: (s, pos_ref[0] * nh + r, 0))
    return _call(body, name=name, out_shape=jax.ShapeDtypeStruct(recv.shape, BF), grid=(N_CHIPS, nh),
                 in_specs=[own_blk, blk], out_specs=blk, sem=("parallel", "parallel"), vmem=40 << 20,
                 prefetch=1)(pos, own, recv)


def _final_half(name, pos, own, recv, others):
    _, half, C = recv.shape
    tr = _row_tile(half, C, 4, 16)
    nh = half // tr

    def body(pos_ref, own_ref, recv_ref, oth_ref, out_ref):
        acc = own_ref[...] + recv_ref[...]
        for p in range(3):
            acc = acc + oth_ref[p].astype(F32)
        out_ref[...] = acc

    own_blk = pl.BlockSpec((None, tr, C), lambda r, pos_ref: (pos_ref[1], pos_ref[0] * nh + r, 0))
    recv_blk = pl.BlockSpec((None, tr, C), lambda r, pos_ref: (pos_ref[1], r, 0))
    oth_blk = pl.BlockSpec((3, tr, C), lambda r, pos_ref: (0, r, 0))
    out_blk = pl.BlockSpec((tr, C), lambda r, pos_ref: (pos_ref[0] * nh + r, 0))
    return _call(body, name=name, out_shape=jax.ShapeDtypeStruct((2 * half, C), F32), grid=(nh,),
                 in_specs=[own_blk, recv_blk, oth_blk], out_specs=out_blk, sem=("parallel",), vmem=40 << 20,
                 prefetch=1)(pos, own, recv, others)


def _adamw(name, w, g, m, v):
    R, C = w.shape
    tr = R if R < SUBLANES or R % SUBLANES else _row_tile(R, C, 7)
    c1 = 1.0 - ADAM_B1 ** ADAM_STEP
    c2 = 1.0 - ADAM_B2 ** ADAM_STEP

    def body(w_ref, g_ref, m_ref, v_ref, d_ref, nm_ref, nv_ref):
        gv = g_ref[...]
        nm = ADAM_B1 * m_ref[...] + (1.0 - ADAM_B1) * gv
        nv = ADAM_B2 * v_ref[...] + (1.0 - ADAM_B2) * (gv * gv)
        d_ref[...] = -ADAM_LR * ((nm / c1) / (jnp.sqrt(nv / c2) + ADAM_EPS) + ADAM_WD * w_ref[...])
        nm_ref[...] = nm
        nv_ref[...] = nv

    blk = pl.BlockSpec((tr, C), lambda r: (r, 0))
    out_shape = [jax.ShapeDtypeStruct((R, C), F32)] * 3
    return _call(body, name=name, out_shape=out_shape, grid=(R // tr,), in_specs=[blk] * 4, out_specs=[blk] * 3,
                 sem=("parallel",), vmem=40 << 20)(w, g, m, v)


def _sum_devices(name, gathered):
    _, R, _ = gathered.shape

    def body(g_ref, o_ref):
        acc = g_ref[0]
        for d in range(1, N_DEV):
            acc = acc + g_ref[d]
        o_ref[...] = acc

    return _call(body, name=name, out_shape=jax.ShapeDtypeStruct((R, LANES), F32), grid=(1,),
                 in_specs=[pl.BlockSpec((N_DEV, R, LANES), lambda i: (0, 0, 0))],
                 out_specs=pl.BlockSpec((R, LANES), lambda i: (0, 0)), sem=("arbitrary",), vmem=16 << 20)(gathered)


def _ffn_fwd(tag, h, gain, wg, wu, wd, tm):
    L, D = h.shape
    F = wg.shape[1]
    tn = _pick(F, (512, 256, 128))
    n = _rmsnorm_fwd(f"{tag}_norm", h, gain, tm)

    def gate_up(accs, _):
        a, u = accs
        return a, u, a * _sigmoid(a) * u

    mn = _mn(tm, tn)
    a, u, s = _matmul(f"{tag}_gate_up", [n], [wg, wu], [(0, 0, 0), (0, 1, 1)], 2, gate_up,
                      [((L, F), BF) + mn] * 3, M=L, N=F, K=D, tm=tm, tn=tn, tk=D, n_outer=True)
    if callable(wd):
        wd = wd(s)
    td = _pick(D, (512, 256, 128))
    (h_out,) = _matmul(f"{tag}_down", [s], [wd], [(0, 0, 0)], 1,
                       lambda accs, ex: [ex[0] + FFN_RESIDUAL_WEIGHT * accs[0]], [((L, D), F32) + _mn(tm, td)],
                       M=L, N=D, K=F, tm=tm, tn=td, tk=F, extras=[(h,) + _mn(tm, td)], n_outer=True)
    return h_out, (h, n, a, u, s), wd


def _ffn_bwd(tag, dh, dh_bf, saved, gain, wg, wu, wd, tm, cs_ff):
    h, n, a, u, s = saved
    L, D = h.shape
    F = wg.shape[1]
    tn = _pick(F, (512, 256, 128))
    tkl = _pick(L, (1408, 384, 256, 128))

    def act_grad(accs, ex):
        ds = FFN_RESIDUAL_WEIGHT * accs[0]
        av, uv = ex[0].astype(F32), ex[1].astype(F32)
        sg = _sigmoid(av)
        return ds * uv * sg * (1.0 + av * (1.0 - sg)), ds * av * sg

    mn = _mn(tm, tn)
    da, du = _matmul(f"{tag}_dact", [dh_bf], [wd], [(0, 0, 0)], 1, act_grad, [((L, F), BF) + mn] * 2, M=L, N=F,
                     K=D, tm=tm, tn=tn, tk=D, tb=True, extras=[(a,) + mn, (u,) + mn], n_outer=True)
    td = _pick(D, (512, 256, 128))
    (dwd,) = _matmul(f"{tag}_dwd", [s], [dh_bf], [(0, 0, 0)], 1, lambda accs, _: [FFN_RESIDUAL_WEIGHT * accs[0]],
                     [((N_CHIPS, cs_ff, D), F32, (None, cs_ff, td), lambda i, j: (i, 0, j))], M=F, N=D, K=L,
                     tm=cs_ff, tn=td, tk=tkl, ta=True)
    tdn = _pick(D, (256, 128))
    (dn,) = _matmul(f"{tag}_dn", [da, du], [wg, wu], [(0, 0, 0), (1, 1, 0)], 1, lambda accs, _: accs,
                    [((L, D), F32) + _mn(tm, tdn)], M=L, N=D, K=F, tm=tm, tn=tdn, tk=F, tb=True)
    tmw = _pick(D, (512, 256, 128))
    shard_out = ((N_CHIPS, D, cs_ff), F32, (None, tmw, cs_ff), lambda i, j: (j, i, 0))
    dwg, dwu = _matmul(f"{tag}_dwgu", [n], [da, du], [(0, 0, 0), (0, 1, 1)], 2, lambda accs, _: accs,
                       [shard_out] * 2, M=D, N=F, K=L, tm=tmw, tn=cs_ff, tk=tkl, ta=True)
    dh_in, dh_in_bf, dgain = _rmsnorm_bwd(f"{tag}_dnorm", dn, h, gain, dh, tm)
    return dh_in, dh_in_bf, dgain, dwg, dwu, dwd


def kernel(x, meta_tokens, ffn1_norm, ffn1_w_gate, ffn1_w_up, ffn1_w_down, mix_norm, w_in, b_forget, fox_q_norm, fox_k_norm, w_branch_fox, w_branch_sb, w_out, ffn2_norm, ffn2_w_gate, ffn2_w_up, ffn2_w_down, loss_target, m_meta_tokens, m_ffn1_norm, m_ffn1_w_gate, m_ffn1_w_up, m_ffn1_w_down, m_mix_norm, m_w_in, m_b_forget, m_fox_q_norm, m_fox_k_norm, m_w_branch_fox, m_w_branch_sb, m_w_out, m_ffn2_norm, m_ffn2_w_gate, m_ffn2_w_up, m_ffn2_w_down, v_meta_tokens, v_ffn1_norm, v_ffn1_w_gate, v_ffn1_w_up, v_ffn1_w_down, v_mix_norm, v_w_in, v_b_forget, v_fox_q_norm, v_fox_k_norm, v_w_branch_fox, v_w_branch_sb, v_w_out, v_ffn2_norm, v_ffn2_w_gate, v_ffn2_w_up, v_ffn2_w_down):
    weights = dict(meta_tokens=meta_tokens, ffn1_norm=ffn1_norm, ffn1_w_gate=ffn1_w_gate, ffn1_w_up=ffn1_w_up,
                   ffn1_w_down=ffn1_w_down, mix_norm=mix_norm, w_in=w_in, b_forget=b_forget, fox_q_norm=fox_q_norm,
                   fox_k_norm=fox_k_norm, w_branch_fox=w_branch_fox, w_branch_sb=w_branch_sb, w_out=w_out,
                   ffn2_norm=ffn2_norm, ffn2_w_gate=ffn2_w_gate, ffn2_w_up=ffn2_w_up, ffn2_w_down=ffn2_w_down)
    moments_m = dict(meta_tokens=m_meta_tokens, ffn1_norm=m_ffn1_norm, ffn1_w_gate=m_ffn1_w_gate,
                     ffn1_w_up=m_ffn1_w_up, ffn1_w_down=m_ffn1_w_down, mix_norm=m_mix_norm, w_in=m_w_in,
                     b_forget=m_b_forget, fox_q_norm=m_fox_q_norm, fox_k_norm=m_fox_k_norm,
                     w_branch_fox=m_w_branch_fox, w_branch_sb=m_w_branch_sb, w_out=m_w_out, ffn2_norm=m_ffn2_norm,
                     ffn2_w_gate=m_ffn2_w_gate, ffn2_w_up=m_ffn2_w_up, ffn2_w_down=m_ffn2_w_down)
    moments_v = dict(meta_tokens=v_meta_tokens, ffn1_norm=v_ffn1_norm, ffn1_w_gate=v_ffn1_w_gate,
                     ffn1_w_up=v_ffn1_w_up, ffn1_w_down=v_ffn1_w_down, mix_norm=v_mix_norm, w_in=v_w_in,
                     b_forget=v_b_forget, fox_q_norm=v_fox_q_norm, fox_k_norm=v_fox_k_norm,
                     w_branch_fox=v_w_branch_fox, w_branch_sb=v_w_branch_sb, w_out=v_w_out, ffn2_norm=v_ffn2_norm,
                     ffn2_w_gate=v_ffn2_w_gate, ffn2_w_up=v_ffn2_w_up, ffn2_w_down=v_ffn2_w_down)
    names = list(weights)

    _, S, D = x.shape
    NM = meta_tokens.shape[0]
    L_real = NM + S
    L = _round_up(L_real, ATT_TILE)
    nblk = L // ATT_TILE
    nbp = _round_up(nblk, SUBLANES)
    cs_ff = ffn1_w_gate.shape[2]
    F = N_CHIPS * cs_ff
    H = b_forget.shape[1]
    FW = w_branch_fox.shape[1]
    SW = w_branch_sb.shape[1]
    HS = SW // HEAD_DIM
    cs_in = w_in.shape[2]
    W_IN = N_CHIPS * cs_in
    assert FW == H * HEAD_DIM and W_IN == 3 * FW + H + 3 * SW + 2 * D
    cs_d = D // N_CHIPS
    tm = _pick(L, (384, 256, 128))

    x_pos, y_pos, c_pos = _mesh_pos()
    pos = jnp.stack([c_pos, 2 * x_pos + y_pos]).astype(I32)

    shard_of = {
        "ffn1_w_gate": (ffn1_w_gate[0], "col"), "ffn1_w_up": (ffn1_w_up[0], "col"),
        "ffn1_w_down": (ffn1_w_down[0], "maj"), "w_in": (w_in[0], "maj"),
        "w_branch_fox": (w_branch_fox[0], "col"), "w_branch_sb": (w_branch_sb[0], "col"),
        "w_out": (w_out[0], "maj"), "ffn2_w_gate": (ffn2_w_gate[0], "col"), "ffn2_w_up": (ffn2_w_up[0], "col"),
        "ffn2_w_down": (ffn2_w_down[0], "maj"),
    }
    g_names = list(shard_of) + ["meta_tokens"]
    shards = [shard_of[k][0] for k in shard_of] + [meta_tokens]
    kinds = [shard_of[k][1] for k in shard_of] + ["col"]
    dtypes = [BF] * len(shard_of) + [F32]
    info = {k: (s, kind) for k, s, kind in zip(g_names, shards, kinds)}
    placed = {k: _place_shard(f"place_{k}", pos, s, kind, dt)
              for k, s, kind, dt in zip(g_names, shards, kinds, dtypes)}
    groups = [["meta_tokens", "ffn1_w_gate", "ffn1_w_up"], ["ffn1_w_down"], ["w_in"],
              ["w_branch_fox", "w_branch_sb", "w_out", "ffn2_w_gate", "ffn2_w_up", "ffn2_w_down"]]
    in_flight = []
    for gi, grp in enumerate(groups):
        g_shards, g_kinds = [info[k][0] for k in grp], [info[k][1] for k in grp]
        in_flight.append(_all_gather_ici_start(f"gather_start_{gi}", [placed[k] for k in grp], g_shards, g_kinds))
    all_started = sum(f[3] for f in in_flight)
    full = {}

    def arrive(gi, after):
        grp = groups[gi]
        g_shards, g_kinds = [info[k][0] for k in grp], [info[k][1] for k in grp]
        send_sem, recv_sem, bufs, _ = in_flight[gi]
        bufs = _all_gather_ici_wait(f"gather_wait_{gi}", bufs, send_sem, recv_sem, after, g_shards, g_kinds)
        full.update(zip(grp, _all_gather_d2d(f"gather_d2d_{gi}", bufs, g_shards, g_kinds)))

    arrive(0, all_started)
    wg1, wu1 = full["ffn1_w_gate"], full["ffn1_w_up"]
    c_f = 3 * FW
    QKV_S, GATES, FCOL = 3 * FW, 3 * FW + 3 * SW, 3 * FW + 3 * SW + 2 * D
    W_PROJ = FCOL + LANES

    h0 = jnp.concatenate([full["meta_tokens"], x[0], jnp.zeros((L - L_real, D), F32)], axis=0)
    target = jnp.concatenate([jnp.zeros((NM, D), F32), loss_target[0], jnp.zeros((L - L_real, D), F32)], axis=0)

    def late_wd1(s):
        arrive(1, s)
        return full["ffn1_w_down"].reshape(F, D)

    h1, saved1, wd1 = _ffn_fwd("ffn1", h0, ffn1_norm, wg1, wu1, late_wd1, tm)

    arrive(2, h1)
    w_in_full = jnp.transpose(full["w_in"], (1, 0, 2)).reshape(D, W_IN)
    w_proj = jnp.concatenate([w_in_full[:, :c_f], w_in_full[:, c_f + H:],
                              jnp.pad(w_in_full[:, c_f:c_f + H], ((0, 0), (0, LANES - H)))], axis=1)
    n2 = _rmsnorm_fwd("mix_norm", h1, mix_norm, tm)
    tp = _pick(FCOL, (512, 256, 128))
    (proj,) = _matmul("in_proj", [n2], [w_proj], [(0, 0, 0)], 1, lambda accs, _: accs,
                      [((L, FCOL), BF) + _mn(tm, tp)], M=L, N=FCOL, K=D, tm=tm, tn=tp, tk=D, n_outer=True)
    (f_logit,) = _matmul("forget_proj", [n2], [w_proj], [(0, 0, 0)], 1, lambda accs, _: accs,
                         [((L, LANES), F32) + _mn(tm, LANES)], M=L, N=LANES, K=D, tm=tm, tn=LANES, tk=D,
                         b_off=FCOL // LANES)
    fl = jnp.pad(jnp.transpose(f_logit[:, :H]).reshape(H, nblk, LANES), ((0, 0), (0, nbp - nblk), (0, 0)))
    bias = jnp.broadcast_to(b_forget[0][:, None, None], (H, 1, LANES))
    c_row = _cum_fwd("forget_cumsum", fl, bias, nblk)
    c_col = c_row[:, :nblk].reshape(H, L, 1)
    gq, gk = fox_q_norm[0][:, None, :], fox_k_norm[0][:, None, :]
    qn, kn = _qknorm_fwd("fox_qk_norm", proj, gq, gk, H, 0, FW, tm)
    o_fox, lse = _fox_fwd("fox_attention", qn, kn, proj, c_row, c_col, H, ATT_HEADS, 0, 0, 2 * FW)
    o_sb = _sb_fwd("sb_attention", proj, HS, min(HS, ATT_HEADS_SB_FWD), QKV_S, QKV_S + SW, QKV_S + 2 * SW)

    arrive(3, o_sb)
    wg2, wu2 = full["ffn2_w_gate"], full["ffn2_w_up"]
    wd2 = full["ffn2_w_down"].reshape(F, D)
    wbf, wbs = full["w_branch_fox"], full["w_branch_sb"]
    wo = full["w_out"].reshape(D, D)
    td = _pick(D, (512, 256, 128))

    def merge(accs, ex):
        bf_, bs_ = accs
        return _sigmoid(ex[0].astype(F32)) * bf_ + _sigmoid(ex[1].astype(F32)) * bs_, bf_, bs_

    merged, br_f, br_s = _matmul("branch_merge", [o_fox, o_sb], [wbf, wbs], [(0, 0, 0), (1, 1, 1)], 2, merge,
                                 [((L, D), BF) + _mn(tm, td)] * 3, M=L, N=D, K=FW, tm=tm, tn=td, tk=FW,
                                 extras=[(proj,) + _mn(tm, td, GATES), (proj,) + _mn(tm, td, GATES + D)],
                                 n_outer=True)
    (h2,) = _matmul("out_proj", [merged], [wo], [(0, 0, 0)], 1, lambda accs, ex: [ex[0] + accs[0]],
                    [((L, D), F32) + _mn(tm, td)], M=L, N=D, K=D, tm=tm, tn=td, tk=D, extras=[(h1,) + _mn(tm, td)],
                    n_outer=True)

    h3, saved2, _ = _ffn_fwd("ffn2", h2, ffn2_norm, wg2, wu2, wd2, tm)
    dh3, dh3_bf, loss_part = _loss_grad("loss", h3, target, NM, S, tm)

    dh2, dh2_bf, dg_ffn2, dwg2, dwu2, dwd2 = _ffn_bwd("ffn2", dh3, dh3_bf, saved2, ffn2_norm, wg2, wu2, wd2, tm,
                                                      cs_ff)

    def scatter_begin(tag, keys, local):
        from_sibling = _reduce_to_sibling(f"grads_to_sibling_{tag}", local)
        partials = [_chip_partial(f"chip_sum_{k}", pos, g, r) for k, g, r in zip(keys, local, from_sibling)]
        send_sem, recv_sem, parts, lands, token = _exchange_start(f"grads_to_owner_start_{tag}", partials)
        return tag, keys, local, from_sibling, send_sem, recv_sem, parts, lands, token

    def scatter_end(state, after):
        tag, keys, local, from_sibling, send_sem, recv_sem, parts, lands, _ = state
        from_chips = _exchange_wait(f"grads_to_owner_wait_{tag}", parts, lands, send_sem, recv_sem, after)
        return {k: _final_half(f"total_{k}", pos, g, r, o)
                for k, g, r, o in zip(keys, local, from_sibling, from_chips)}

    totals = {}
    scatter_ffn2 = scatter_begin("ffn2", ["ffn2_w_gate", "ffn2_w_up", "ffn2_w_down"], [dwg2, dwu2, dwd2])
    c_row_bwd = c_row + scatter_ffn2[-1][0, 0]

    def gate_grad(accs, ex):
        dm = accs[0]
        gf, gs, bf_, bs_ = [e.astype(F32) for e in ex]
        sf, ss = _sigmoid(gf), _sigmoid(gs)
        return dm * bf_ * sf * (1.0 - sf), dm * bs_ * ss * (1.0 - ss), dm * sf, dm * ss

    mn_d = _mn(tm, td)
    dgf, dgs, dbr_f, dbr_s = _matmul(
        "d_merged", [dh2_bf], [wo], [(0, 0, 0)], 1, gate_grad, [((L, D), BF) + mn_d] * 4, M=L, N=D, K=D, tm=tm,
        tn=td, tk=D, tb=True, extras=[(proj,) + _mn(tm, td, GATES), (proj,) + _mn(tm, td, GATES + D),
                                      (br_f,) + mn_d, (br_s,) + mn_d], n_outer=True)
    tkl = _pick(L, (1408, 384, 256, 128))
    (dwo,) = _matmul("d_w_out", [merged], [dh2_bf], [(0, 0, 0)], 1, lambda accs, _: accs,
                     [((N_CHIPS, cs_d, D), F32, (None, cs_d, td), lambda i, j: (i, 0, j))], M=D, N=D, K=L, tm=cs_d,
                     tn=td, tk=tkl, ta=True)
    tw = _pick(FW, (512, 256, 128))
    do_fox, do_sb = _matmul("d_branch_in", [dbr_f, dbr_s], [wbf, wbs], [(0, 0, 0), (1, 1, 1)], 2,
                            lambda accs, _: accs, [((L, FW), BF) + _mn(tm, tw)] * 2, M=L, N=FW, K=D, tm=tm, tn=tw,
                            tk=D, tb=True)
    tmb = _pick(FW, (1024, 512, 256, 128))
    dwbf, dwbs = _matmul("d_w_branch", [o_fox, o_sb], [dbr_f, dbr_s], [(0, 0, 0), (1, 1, 1)], 2,
                         lambda accs, _: accs,
                         [((N_CHIPS, FW, cs_d), F32, (None, tmb, cs_d), lambda i, j: (j, i, 0))] * 2, M=FW, N=D, K=L,
                         tm=tmb, tn=cs_d, tk=tkl, ta=True)

    dqn, dkn, dfv, dcs, drs = _fox_bwd("fox_attention_bwd", qn, kn, proj, c_row_bwd, c_col, o_fox, do_fox, lse, H,
                                       ATT_HEADS, 0, 0, 2 * FW)
    dsq, dsk, dsv = _sb_bwd("sb_attention_bwd", proj, do_sb, HS, ATT_HEADS, QKV_S, QKV_S + SW,
                            QKV_S + 2 * SW)
    dfq, dfk, dgq, dgk = _qknorm_bwd("fox_qk_norm_bwd", proj, dqn, dkn, gq, gk, H, 0, FW, tm)
    drs_row = jnp.pad(drs.reshape(H, nblk, LANES), ((0, 0), (0, nbp - nblk), (0, 0)))
    dfl, dbias = _cum_bwd("forget_cumsum_bwd", drs_row, dcs, fl, bias, nblk)
    dfl_cols = jnp.pad(jnp.transpose(dfl[:, :nblk].reshape(H, L)), ((0, 0), (0, LANES - H))).astype(BF)
    dproj = jnp.concatenate([dfq, dfk, dfv.astype(BF), dsq, dsk.astype(BF), dsv.astype(BF), dgf, dgs, dfl_cols],
                            axis=1)

    tdn = _pick(D, (256, 128))
    (dn2,) = _matmul("d_mix_norm_in", [dproj], [w_proj], [(0, 0, 0)], 1, lambda accs, _: accs,
                     [((L, D), F32) + _mn(tm, tdn)], M=L, N=D, K=W_PROJ, tm=tm, tn=tdn, tk=W_PROJ, tb=True)
    tmw = _pick(D, (1024, 512, 256, 128))
    tnp = _pick(W_PROJ, (1152, 640, 512, 384, 256, 128))
    (dw_proj,) = _matmul("d_w_in", [n2], [dproj], [(0, 0, 0)], 1, lambda accs, _: accs,
                         [((D, W_PROJ), F32) + _mn(tmw, tnp)], M=D, N=W_PROJ, K=L, tm=tmw, tn=tnp, tk=tkl,
                         ta=True)
    dh1, dh1_bf, dg_mix = _rmsnorm_bwd("mix_dnorm", dn2, h1, mix_norm, dh2, tm)
    dw_in_ref = jnp.concatenate([dw_proj[:, :c_f], dw_proj[:, FCOL:FCOL + H], dw_proj[:, c_f:FCOL]], axis=1)
    dw_in = jnp.transpose(dw_in_ref.reshape(D, N_CHIPS, cs_in), (1, 0, 2))
    totals.update(scatter_end(scatter_ffn2, dh1))
    scatter_mix = scatter_begin("mix", ["w_in", "w_branch_fox", "w_branch_sb", "w_out"], [dw_in, dwbf, dwbs, dwo])

    dh0, _, dg_ffn1, dwg1, dwu1, dwd1 = _ffn_bwd("ffn1", dh1, dh1_bf, saved1, ffn1_norm, wg1, wu1, wd1, tm, cs_ff)
    grad_x = dh0[NM:L_real][None]
    totals.update(scatter_end(scatter_mix, dh0))
    scatter_ffn1 = scatter_begin("ffn1", ["ffn1_w_gate", "ffn1_w_up", "ffn1_w_down"], [dwg1, dwu1, dwd1])
    totals.update(scatter_end(scatter_ffn1, scatter_ffn1[-1]))
    big = list(totals)
    grads = dict(zip(big, _share_halves("grads_to_core_pair", [totals[k] for k in big])))

    small = [loss_part[:, :1].reshape(1), dh0[:NM].reshape(-1), dg_ffn1.reshape(-1), dg_mix.reshape(-1),
             dg_ffn2.reshape(-1), dbias[:, 0, 0], dgq.reshape(-1), dgk.reshape(-1)]
    sizes = [s.shape[0] for s in small]
    flat = jnp.concatenate(small)
    rows = _round_up(-(-flat.shape[0] // LANES), SUBLANES)
    packed = jnp.pad(flat, (0, rows * LANES - flat.shape[0])).reshape(rows, LANES)
    total = _sum_devices("sum_small", _gather_small("gather_small", packed)).reshape(-1)
    pieces, off = [], 0
    for n_el in sizes:
        pieces.append(total[off:off + n_el])
        off += n_el
    loss = pieces[0][0]
    d_meta = lax.dynamic_slice_in_dim(pieces[1].reshape(NM, D), pos[1] * cs_d, cs_d, axis=1)
    grads.update(meta_tokens=d_meta, ffn1_norm=pieces[2].reshape(1, D), mix_norm=pieces[3].reshape(1, D),
                 ffn2_norm=pieces[4].reshape(1, D), b_forget=pieces[5].reshape(1, H),
                 fox_q_norm=pieces[6].reshape(1, H, HEAD_DIM), fox_k_norm=pieces[7].reshape(1, H, HEAD_DIM))

    out_g, out_d, out_m, out_v = [], [], [], []
    for k in names:
        w = weights[k]
        shape2 = (1, w.size) if w.size < LANES * SUBLANES else (w.size // w.shape[-1], w.shape[-1])
        g2 = grads[k].reshape(shape2)
        d, nm, nv = _adamw(f"adamw_{k}", w.reshape(shape2), g2, moments_m[k].reshape(shape2),
                           moments_v[k].reshape(shape2))
        out_g.append(g2.reshape(w.shape))
        out_d.append(d.reshape(w.shape))
        out_m.append(nm.reshape(w.shape))
        out_v.append(nv.reshape(w.shape))
    return (loss, grad_x, *out_g, *out_d, *out_m, *out_v)
```

```python
import functools

import jax
import jax.numpy as jnp
from jax import lax
from jax.experimental import pallas as pl
from jax.experimental.pallas import tpu as pltpu

F32 = jnp.float32
BF = jnp.bfloat16
I32 = jnp.int32

HEAD_DIM = 128
RMS_EPS = 1e-6
FFN_RESIDUAL_WEIGHT = 0.5
ADAM_LR = 0.001
ADAM_B1 = 0.9
ADAM_B2 = 0.999
ADAM_EPS = 1e-08
ADAM_WD = 0.01
ADAM_STEP = 10

LANES = 128
SUBLANES = 8
ATT_TILE = 128
ATT_HEADS = 4
ATT_HEADS_SB_FWD = 8
VMEM_CAP = 56 * 1024 * 1024
MESH_IDS = pl.DeviceIdType.MESH
N_CHIPS = 4
N_DEV = 8


def _pick(n, cands):
    for c in cands:
        if c <= n and n % c == 0:
            return c
    raise ValueError(f"no tile for {n} among {cands}")


def _round_up(n, m):
    return (n + m - 1) // m * m


def _tile_bytes(shape, dtype):
    item = jnp.dtype(dtype).itemsize
    dims = [d for d in shape if d is not None]
    if not dims:
        return 4 * LANES * SUBLANES
    last = _round_up(dims[-1], LANES)
    sub = _round_up(dims[-2], SUBLANES * (4 // item)) if len(dims) > 1 else 1
    lead = 1
    for d in dims[:-2]:
        lead *= d
    return lead * sub * last * item


def _vmem_limit(blocks, scratch=(), temps=0):
    need = 2 * sum(_tile_bytes(s, d) for s, d in blocks) + sum(_tile_bytes(s, d) for s, d in scratch) + temps
    return int(min(VMEM_CAP, max(need + (4 << 20), 16 << 20)))


def _call(body, *, name, out_shape, grid=(), in_specs=None, out_specs=None, scratch=(), sem=None, vmem=None,
          aliases=None, prefetch=0):
    params = pltpu.CompilerParams(dimension_semantics=sem, vmem_limit_bytes=vmem)
    if prefetch:
        grid_spec = pltpu.PrefetchScalarGridSpec(num_scalar_prefetch=prefetch, grid=grid, in_specs=in_specs,
                                                 out_specs=out_specs, scratch_shapes=scratch)
        return pl.pallas_call(body, out_shape=out_shape, grid_spec=grid_spec, name=name, compiler_params=params,
                              input_output_aliases=aliases or {})
    return pl.pallas_call(body, out_shape=out_shape, grid=grid, in_specs=in_specs, out_specs=out_specs,
                          scratch_shapes=scratch, name=name, compiler_params=params,
                          input_output_aliases=aliases or {})


def _dot(a, b, ca, cb):
    return lax.dot_general(a, b, (((ca,), (cb,)), ((), ())), preferred_element_type=F32)


def _sigmoid(x):
    return 1.0 / (1.0 + jnp.exp(-x))


def _log_sigmoid(x):
    return jnp.minimum(x, 0.0) - jnp.log1p(jnp.exp(-jnp.abs(x)))


def _split(x, parts):
    pieces = []
    rem = x
    for p in range(parts):
        piece = rem.astype(BF)
        pieces.append(piece)
        if p + 1 < parts:
            rem = rem - piece.astype(F32)
    return pieces


def _pieces_dot(pieces, ones_bf):
    out = None
    for piece in pieces:
        d = _dot(piece, ones_bf, 1, 0)
        out = d if out is None else out + d
    return out


def _split_dot(x, ones_bf, parts):
    return _pieces_dot(_split(x, parts), ones_bf)


def _matmul(name, a_list, b_list, pairs, n_acc, epi, outs, *, M, N, K, tm, tn, tk, ta=False, tb=False,
            extras=(), n_outer=False, b_off=0, after=None):
    if after is not None:
        user_epi = epi
        extras = list(extras) + [(after, (SUBLANES, LANES), lambda i, j: (0, 0))]
        epi = lambda accs, ex: user_epi(accs, ex[:-1])
    gi, gj, nk = M // tm, N // tn, K // tk
    assert gi * tm == M and gj * tn == N and nk * tk == K, (name, M, N, K, tm, tn, tk)
    n_a, n_b, n_e, n_o = len(a_list), len(b_list), len(extras), len(outs)

    def ij(g0, g1):
        return (g1, g0) if n_outer else (g0, g1)

    def a_map(g0, g1, k):
        i, _ = ij(g0, g1)
        return (k, i) if ta else (i, k)

    def b_map(g0, g1, k):
        _, j = ij(g0, g1)
        return (j + b_off, k) if tb else (k, j + b_off)

    def tile_map(fn):
        return lambda g0, g1, k: fn(*ij(g0, g1))

    a_block = (tk, tm) if ta else (tm, tk)
    b_block = (tn, tk) if tb else (tk, tn)
    in_specs = ([pl.BlockSpec(a_block, a_map)] * n_a + [pl.BlockSpec(b_block, b_map)] * n_b
                + [pl.BlockSpec(bs, tile_map(fn)) for _, bs, fn in extras])
    out_specs = [pl.BlockSpec(bs, tile_map(fn)) for _, _, bs, fn in outs]
    out_shape = [jax.ShapeDtypeStruct(s, d) for s, d, _, _ in outs]
    scratch = [pltpu.VMEM((tm, tn), F32) for _ in range(n_acc)] if nk > 1 else []

    def body(*refs):
        a_refs = refs[:n_a]
        b_refs = refs[n_a:n_a + n_b]
        e_refs = refs[n_a + n_b:n_a + n_b + n_e]
        o_refs = refs[n_a + n_b + n_e:n_a + n_b + n_e + n_o]
        acc_refs = refs[n_a + n_b + n_e + n_o:]

        def products():
            accs = [None] * n_acc
            for ai, bi, ci in pairs:
                a = a_refs[ai][...]
                b = b_refs[bi][...]
                d = _dot(a.astype(BF), b.astype(BF), 0 if ta else 1, 1 if tb else 0)
                accs[ci] = d if accs[ci] is None else accs[ci] + d
            return accs

        def finish(accs):
            res = epi(accs, [e[...] for e in e_refs])
            for o_ref, r in zip(o_refs, res):
                o_ref[...] = r.reshape(o_ref.shape).astype(o_ref.dtype)

        if nk == 1:
            finish(products())
        else:
            k = pl.program_id(2)

            @pl.when(k == 0)
            def _():
                for acc in acc_refs:
                    acc[...] = jnp.zeros_like(acc)

            for acc, d in zip(acc_refs, products()):
                acc[...] += d

            @pl.when(k == nk - 1)
            def _():
                finish([acc[...] for acc in acc_refs])

    blocks = ([(a_block, a.dtype) for a in a_list] + [(b_block, b.dtype) for b in b_list]
              + [(bs, e.dtype) for e, bs, _ in extras] + [(bs, d) for _, d, bs, _ in outs])
    vmem = _vmem_limit(blocks, [((tm, tn), F32)] * (n_acc if nk > 1 else 0), temps=6 * tm * tn * 4)
    grid = (gj, gi, nk) if n_outer else (gi, gj, nk)
    fn = _call(body, name=name, out_shape=out_shape, grid=grid, in_specs=in_specs, out_specs=out_specs,
               scratch=scratch, sem=("parallel", "parallel", "arbitrary"), vmem=vmem)
    return fn(*a_list, *b_list, *[e for e, _, _ in extras])


def _mn(tm, tn, col0=0):
    assert col0 % tn == 0
    off = col0 // tn
    return (tm, tn), (lambda i, j: (i, j + off))


def _rmsnorm_fwd(name, x, gain, tr):
    L, D = x.shape

    def body(x_ref, g_ref, o_ref):
        xv = x_ref[...]
        r = lax.rsqrt(jnp.mean(xv * xv, axis=-1, keepdims=True) + RMS_EPS)
        o_ref[...] = (xv * r * g_ref[...]).astype(BF)

    row = pl.BlockSpec((tr, D), lambda i: (i, 0))
    vec = pl.BlockSpec((1, D), lambda i: (0, 0))
    vmem = _vmem_limit([((tr, D), F32), ((tr, D), BF)], temps=3 * tr * D * 4)
    return _call(body, name=name, out_shape=jax.ShapeDtypeStruct((L, D), BF), grid=(L // tr,), in_specs=[row, vec],
                 out_specs=row, sem=("parallel",), vmem=vmem)(x, gain)


def _rmsnorm_bwd(name, dn, x, gain, dres, tr):
    L, D = x.shape
    steps = L // tr

    def body(dn_ref, x_ref, g_ref, dres_ref, dx_ref, dxb_ref, dg_ref):
        i = pl.program_id(0)
        xv = x_ref[...]
        r = lax.rsqrt(jnp.mean(xv * xv, axis=-1, keepdims=True) + RMS_EPS)
        xhat = xv * r
        dy = dn_ref[...]
        dxhat = dy * g_ref[...]
        dx = dres_ref[...] + r * (dxhat - xhat * jnp.mean(dxhat * xhat, axis=-1, keepdims=True))
        dx_ref[...] = dx
        dxb_ref[...] = dx.astype(BF)

        @pl.when(i == 0)
        def _():
            dg_ref[...] = jnp.zeros_like(dg_ref)

        dg_ref[...] += jnp.sum(dy * xhat, axis=0, keepdims=True)

    row = pl.BlockSpec((tr, D), lambda i: (i, 0))
    vec = pl.BlockSpec((1, D), lambda i: (0, 0))
    vmem = _vmem_limit([((tr, D), F32)] * 4 + [((tr, D), BF)], temps=4 * tr * D * 4)
    out_shape = [jax.ShapeDtypeStruct((L, D), F32), jax.ShapeDtypeStruct((L, D), BF),
                 jax.ShapeDtypeStruct((1, D), F32)]
    return _call(body, name=name, out_shape=out_shape, grid=(steps,), in_specs=[row, row, vec, row],
                 out_specs=[row, row, vec], sem=("arbitrary",), vmem=vmem)(dn, x, gain, dres)


def _loss_grad(name, h, target, n_meta, n_seq, tr):
    L, D = h.shape

    def body(h_ref, t_ref, dh_ref, dhb_ref, loss_ref):
        i = pl.program_id(0)
        rows = i * tr + lax.broadcasted_iota(I32, (tr, 1), 0)
        valid = (rows >= n_meta) & (rows < n_meta + n_seq)
        diff = jnp.where(valid, h_ref[...] - t_ref[...], 0.0)
        dh = diff * (1.0 / D)
        dh_ref[...] = dh
        dhb_ref[...] = dh.astype(BF)

        @pl.when(i == 0)
        def _():
            loss_ref[...] = jnp.zeros_like(loss_ref)

        loss_ref[...] += jnp.sum(diff * diff) * (0.5 / D)

    row = pl.BlockSpec((tr, D), lambda i: (i, 0))
    acc = pl.BlockSpec((1, LANES), lambda i: (0, 0))
    vmem = _vmem_limit([((tr, D), F32)] * 3 + [((tr, D), BF)], temps=3 * tr * D * 4)
    out_shape = [jax.ShapeDtypeStruct((L, D), F32), jax.ShapeDtypeStruct((L, D), BF),
                 jax.ShapeDtypeStruct((1, LANES), F32)]
    return _call(body, name=name, out_shape=out_shape, grid=(L // tr,), in_specs=[row, row],
                 out_specs=[row, row, acc], sem=("arbitrary",), vmem=vmem)(h, target)


def _qknorm_fwd(name, proj, gq, gk, heads, q_col, k_col, tr):
    L = proj.shape[0]

    def body(q_ref, k_ref, gq_ref, gk_ref, qn_ref, kn_ref):
        for x_ref, g_ref, o_ref in ((q_ref, gq_ref, qn_ref), (k_ref, gk_ref, kn_ref)):
            xv = x_ref[...].astype(F32)
            r = lax.rsqrt(jnp.mean(xv * xv, axis=-1, keepdims=True) + RMS_EPS)
            o_ref[...] = (xv * r * g_ref[...]).astype(BF)

    qb, kb = q_col // HEAD_DIM, k_col // HEAD_DIM
    in_specs = [pl.BlockSpec((tr, HEAD_DIM), lambda h, i: (i, qb + h)),
                pl.BlockSpec((tr, HEAD_DIM), lambda h, i: (i, kb + h)),
                pl.BlockSpec((None, 1, HEAD_DIM), lambda h, i: (h, 0, 0)),
                pl.BlockSpec((None, 1, HEAD_DIM), lambda h, i: (h, 0, 0))]
    out = pl.BlockSpec((tr, HEAD_DIM), lambda h, i: (i, h))
    out_shape = [jax.ShapeDtypeStruct((L, heads * HEAD_DIM), BF)] * 2
    return _call(body, name=name, out_shape=out_shape, grid=(heads, L // tr), in_specs=in_specs,
                 out_specs=[out, out], sem=("parallel", "parallel"), vmem=16 << 20)(proj, proj, gq, gk)


def _qknorm_bwd(name, proj, dqn, dkn, gq, gk, heads, q_col, k_col, tr):
    L = proj.shape[0]

    def body(q_ref, k_ref, dqn_ref, dkn_ref, gq_ref, gk_ref, dq_ref, dk_ref, dgq_ref, dgk_ref):
        i = pl.program_id(1)
        for x_ref, dy_ref, g_ref, dx_ref, dg_ref in ((q_ref, dqn_ref, gq_ref, dq_ref, dgq_ref),
                                                     (k_ref, dkn_ref, gk_ref, dk_ref, dgk_ref)):
            xv = x_ref[...].astype(F32)
            r = lax.rsqrt(jnp.mean(xv * xv, axis=-1, keepdims=True) + RMS_EPS)
            xhat = xv * r
            dy = dy_ref[...].astype(F32)
            dxhat = dy * g_ref[...]
            dx_ref[...] = (r * (dxhat - xhat * jnp.mean(dxhat * xhat, axis=-1, keepdims=True))).astype(BF)

            @pl.when(i == 0)
            def _():
                dg_ref[...] = jnp.zeros_like(dg_ref)

            dg_ref[...] += jnp.sum(dy * xhat, axis=0, keepdims=True)

    qb, kb = q_col // HEAD_DIM, k_col // HEAD_DIM
    tile = pl.BlockSpec((tr, HEAD_DIM), lambda h, i: (i, h))
    gain = pl.BlockSpec((None, 1, HEAD_DIM), lambda h, i: (h, 0, 0))
    in_specs = [pl.BlockSpec((tr, HEAD_DIM), lambda h, i: (i, qb + h)),
                pl.BlockSpec((tr, HEAD_DIM), lambda h, i: (i, kb + h)), tile, tile, gain, gain]
    out_shape = [jax.ShapeDtypeStruct((L, heads * HEAD_DIM), BF)] * 2 + [
        jax.ShapeDtypeStruct((heads, 1, HEAD_DIM), F32)] * 2
    return _call(body, name=name, out_shape=out_shape, grid=(heads, L // tr), in_specs=in_specs,
                 out_specs=[tile, tile, gain, gain], sem=("parallel", "arbitrary"),
                 vmem=16 << 20)(proj, proj, dqn, dkn, gq, gk)


def _tri(cmp):
    r = lax.broadcasted_iota(I32, (LANES, LANES), 0)
    c = lax.broadcasted_iota(I32, (LANES, LANES), 1)
    return jnp.where(cmp(r, c), 1.0, 0.0).astype(BF)


def _cum_fwd(name, fl, bias, n_rows):
    H, nbp, _ = fl.shape

    def body(fl_ref, b_ref, c_ref, tot_ref):
        lf = _log_sigmoid(fl_ref[...] + b_ref[...])
        c_ref[...] = _split_dot(lf, _tri(lambda r, c: r <= c), 3)
        tot_ref[...] = _split_dot(lf, jnp.ones((LANES, LANES), BF), 3)

        def step(r, carry):
            c_ref[pl.ds(r, 1), :] = c_ref[pl.ds(r, 1), :] + carry
            return carry + tot_ref[pl.ds(r, 1), :]

        lax.fori_loop(0, n_rows, step, jnp.zeros((1, LANES), F32))

    blk = pl.BlockSpec((None, nbp, LANES), lambda h: (h, 0, 0))
    vec = pl.BlockSpec((None, 1, LANES), lambda h: (h, 0, 0))
    return _call(body, name=name, out_shape=jax.ShapeDtypeStruct((H, nbp, LANES), F32), grid=(H,),
                 in_specs=[blk, vec], out_specs=blk, scratch=[pltpu.VMEM((nbp, LANES), F32)], sem=("parallel",),
                 vmem=16 << 20)(fl, bias)


def _cum_bwd(name, drs, dcs, fl, bias, n_rows):
    H, nbp, _ = fl.shape

    def body(drs_ref, dcs_ref, fl_ref, b_ref, dfl_ref, db_ref, rin_ref, tot_ref):
        dc = drs_ref[...] - dcs_ref[...]
        rin_ref[...] = _split_dot(dc, _tri(lambda r, c: r >= c), 3)
        tot_ref[...] = _split_dot(dc, jnp.ones((LANES, LANES), BF), 3)
        dfl_ref[...] = jnp.zeros_like(dfl_ref)

        def step(t, carry):
            r = n_rows - 1 - t
            x = fl_ref[pl.ds(r, 1), :] + b_ref[...]
            dfl_ref[pl.ds(r, 1), :] = (rin_ref[pl.ds(r, 1), :] + carry) * _sigmoid(-x)
            return carry + tot_ref[pl.ds(r, 1), :]

        lax.fori_loop(0, n_rows, step, jnp.zeros((1, LANES), F32))
        db_ref[...] = jnp.zeros_like(db_ref) + jnp.sum(dfl_ref[...])

    blk = pl.BlockSpec((None, nbp, LANES), lambda h: (h, 0, 0))
    vec = pl.BlockSpec((None, 1, LANES), lambda h: (h, 0, 0))
    out_shape = [jax.ShapeDtypeStruct((H, nbp, LANES), F32), jax.ShapeDtypeStruct((H, 1, LANES), F32)]
    return _call(body, name=name, out_shape=out_shape, grid=(H,), in_specs=[blk, blk, blk, vec],
                 out_specs=[blk, vec], scratch=[pltpu.VMEM((nbp, LANES), F32)] * 2, sem=("parallel",),
                 vmem=16 << 20)(drs, dcs, fl, bias)


def _att_specs(L, G, q_col, k_col, v_col):
    T = ATT_TILE
    W = G * HEAD_DIM
    assert q_col % W == 0 and k_col % W == 0 and v_col % W == 0
    qb, kb, vb = q_col // W, k_col // W, v_col // W
    q_spec = pl.BlockSpec((T, W), lambda h, i: (i, qb + h))
    k_spec = pl.BlockSpec((L, W), lambda h, i: (0, kb + h), pipeline_mode=pl.Buffered(1))
    v_spec = pl.BlockSpec((L, W), lambda h, i: (0, vb + h), pipeline_mode=pl.Buffered(1))
    return q_spec, k_spec, v_spec


def _head_lanes(G):
    return [slice(g * HEAD_DIM, (g + 1) * HEAD_DIM) for g in range(G)]


def _tile_iotas():
    T = ATT_TILE
    return lax.broadcasted_iota(I32, (T, T), 0), lax.broadcasted_iota(I32, (T, T), 1)


def _rows(j):
    return pl.ds(pl.multiple_of(j * ATT_TILE, ATT_TILE), ATT_TILE)


def _fox_fwd(name, q_arr, k_arr, v_arr, c_row, c_col, heads, G, q_col, k_col, v_col):
    L = q_arr.shape[0]
    T = ATT_TILE
    scale = HEAD_DIM ** -0.5
    lanes = _head_lanes(G)

    def body(q_ref, k_ref, v_ref, crow_ref, ccol_ref, o_ref, lse_ref):
        i = pl.program_id(1)
        qs = [q_ref[:, hl] for hl in lanes]
        cts = [ccol_ref[g] for g in range(G)]
        row, col = _tile_iotas()

        def tile(j, carry, masked):
            qk = [_dot(qs[g], k_ref[_rows(j), hl], 1, 1) for g, hl in enumerate(lanes)]
            stats = []
            for g in range(G):
                m, l, _ = carry[g]
                s = qk[g] * scale + (cts[g] - crow_ref[g, pl.ds(j, 1), :])
                if masked:
                    s = jnp.where(col <= row, s, -jnp.inf)
                m_new = jnp.maximum(m, jnp.max(s, axis=1, keepdims=True))
                alpha = jnp.exp(m - m_new)
                p = jnp.exp(s - m_new)
                stats.append((m_new, alpha, alpha * l + jnp.sum(p, axis=1, keepdims=True), p.astype(BF)))
            pv = [_dot(stats[g][3], v_ref[_rows(j), hl], 1, 0) for g, hl in enumerate(lanes)]
            return tuple((stats[g][0], stats[g][2], stats[g][1] * carry[g][2] + pv[g]) for g in range(G))

        init = tuple((jnp.full((T, 1), -1e30, F32), jnp.zeros((T, 1), F32), jnp.zeros((T, HEAD_DIM), F32))
                     for _ in range(G))
        carry = lax.fori_loop(0, i, lambda j, c: tile(j, c, False), init)
        for g, (m, l, acc) in enumerate(tile(i, carry, True)):
            o_ref[:, lanes[g]] = (acc / l).astype(o_ref.dtype)
            lse_ref[g] = m + jnp.log(l)

    nbp = c_row.shape[1]
    W = G * HEAD_DIM
    q_spec, k_spec, v_spec = _att_specs(L, G, q_col, k_col, v_col)
    crow_spec = pl.BlockSpec((G, nbp, LANES), lambda h, i: (h, 0, 0))
    col_spec = pl.BlockSpec((G, T, 1), lambda h, i: (h, i, 0))
    o_spec = pl.BlockSpec((T, W), lambda h, i: (i, h))
    out_shape = [jax.ShapeDtypeStruct((L, heads * HEAD_DIM), BF), jax.ShapeDtypeStruct((heads, L, 1), F32)]
    vmem = _vmem_limit([((L, W), BF)] * 2, temps=8 << 20)
    return _call(body, name=name, out_shape=out_shape, grid=(heads // G, L // T),
                 in_specs=[q_spec, k_spec, v_spec, crow_spec, col_spec], out_specs=[o_spec, col_spec],
                 sem=("parallel", "parallel"), vmem=vmem)(q_arr, k_arr, v_arr, c_row, c_col)


def _fox_bwd(name, q_arr, k_arr, v_arr, c_row, c_col, o, do, lse, heads, G, q_col, k_col, v_col):
    L = q_arr.shape[0]
    T = ATT_TILE
    nq = L // T
    scale = HEAD_DIM ** -0.5
    lanes = _head_lanes(G)

    def body(q_ref, k_ref, v_ref, crow_ref, ccol_ref, o_ref, do_ref, lse_ref, dq_ref, dk_acc, dv_acc, dcs_ref,
             drs_ref):
        i = pl.program_id(1)

        @pl.when(i == 0)
        def _():
            dk_acc[...] = jnp.zeros_like(dk_acc)
            dv_acc[...] = jnp.zeros_like(dv_acc)
            dcs_ref[...] = jnp.zeros_like(dcs_ref)

        qs = [q_ref[:, hl] for hl in lanes]
        dos = [do_ref[:, hl] for hl in lanes]
        deltas = [jnp.sum(dos[g].astype(F32) * o_ref[:, hl].astype(F32), axis=1, keepdims=True)
                  for g, hl in enumerate(lanes)]
        lses = [lse_ref[g] for g in range(G)]
        cts = [ccol_ref[g] for g in range(G)]
        row, col = _tile_iotas()

        def tile(j, carry, masked):
            ks = [k_ref[_rows(j), hl] for hl in lanes]
            qk = [_dot(qs[g], ks[g], 1, 1) for g in range(G)]
            dp = [_dot(dos[g], v_ref[_rows(j), hl], 1, 1) for g, hl in enumerate(lanes)]
            pbs, dsbs, row_sums = [], [], []
            for g in range(G):
                s = qk[g] * scale + (cts[g] - crow_ref[g, pl.ds(j, 1), :])
                if masked:
                    s = jnp.where(col <= row, s, -jnp.inf)
                p = jnp.exp(s - lses[g])
                ds = p * (dp[g] - deltas[g])
                dcs_ref[g, pl.ds(j, 1), :] += jnp.sum(ds, axis=0, keepdims=True)
                row_sums.append(carry[g][1] + jnp.sum(ds, axis=1, keepdims=True))
                pbs.append(p.astype(BF))
                dsbs.append((ds * scale).astype(BF))
            for g, hl in enumerate(lanes):
                dk_acc[_rows(j), hl] += _dot(dsbs[g], qs[g], 0, 0)
            for g, hl in enumerate(lanes):
                dv_acc[_rows(j), hl] += _dot(pbs[g], dos[g], 0, 0)
            return tuple((carry[g][0] + _dot(dsbs[g], ks[g], 1, 0), row_sums[g]) for g in range(G))

        init = tuple((jnp.zeros((T, HEAD_DIM), F32), jnp.zeros((T, 1), F32)) for _ in range(G))
        carry = lax.fori_loop(0, i, lambda j, c: tile(j, c, False), init)
        for g, (dq, row_sum) in enumerate(tile(i, carry, True)):
            dq_ref[:, lanes[g]] = dq.astype(dq_ref.dtype)
            drs_ref[g] = row_sum

    nbp = c_row.shape[1]
    WG = G * HEAD_DIM
    q_spec, k_spec, v_spec = _att_specs(L, G, q_col, k_col, v_col)
    crow_spec = pl.BlockSpec((G, nbp, LANES), lambda h, i: (h, 0, 0))
    col_spec = pl.BlockSpec((G, T, 1), lambda h, i: (h, i, 0))
    t_spec = pl.BlockSpec((T, WG), lambda h, i: (i, h))
    head_spec = pl.BlockSpec((L, WG), lambda h, i: (0, h), pipeline_mode=pl.Buffered(1))
    W = heads * HEAD_DIM
    out_shape = [jax.ShapeDtypeStruct((L, W), F32)] * 3 + [jax.ShapeDtypeStruct((heads, nbp, LANES), F32),
                                                           jax.ShapeDtypeStruct((heads, L, 1), F32)]
    vmem = _vmem_limit([], [((L, WG), BF)] * 2 + [((L, WG), F32)] * 2, temps=10 << 20)
    return _call(body, name=name, out_shape=out_shape, grid=(heads // G, nq),
                 in_specs=[q_spec, k_spec, v_spec, crow_spec, col_spec, t_spec, t_spec, col_spec],
                 out_specs=[t_spec, head_spec, head_spec, crow_spec, col_spec],
                 sem=("parallel", "arbitrary"), vmem=vmem)(q_arr, k_arr, v_arr, c_row, c_col, o, do, lse)


def _sb_logits(qk, scale, valid):
    z = qk * scale
    lb = jnp.minimum(z, 0.0) - jnp.log1p(jnp.exp(-jnp.abs(z)))
    lom = lb - z
    if valid is not None:
        lom = jnp.where(valid, lom, 0.0)
    return lb, lom


def _sb_fwd(name, proj, heads, G, q_col, k_col, v_col):
    L = proj.shape[0]
    T = ATT_TILE
    scale = HEAD_DIM ** -0.5
    lanes = _head_lanes(G)

    def body(q_ref, k_ref, v_ref, o_ref):
        i = pl.program_id(1)
        qs = [q_ref[:, hl] for hl in lanes]
        row, col = _tile_iotas()
        later_mat = jnp.where(row > col, 1.0, 0.0).astype(BF)

        def tile(j, carry, masked):
            valid = (col < row) if masked else None
            qk = [_dot(qs[g], k_ref[_rows(j), hl], 1, 1) for g, hl in enumerate(lanes)]
            logits = [_sb_logits(qk[g], scale, valid) for g in range(G)]
            pieces = [_split(lom, 2) for _, lom in logits]
            later = [_pieces_dot(pieces[g], later_mat) for g in range(G)]
            ws = []
            for g in range(G):
                w = jnp.exp(logits[g][0] + later[g] + carry[g][0])
                if masked:
                    w = jnp.where(valid, w, 0.0)
                ws.append(w.astype(BF))
            wv = [_dot(ws[g], v_ref[_rows(j), hl], 1, 0) for g, hl in enumerate(lanes)]
            return tuple((carry[g][0] + jnp.sum(logits[g][1], axis=1, keepdims=True), carry[g][1] + wv[g])
                         for g in range(G))

        init = tuple((jnp.zeros((T, 1), F32), jnp.zeros((T, HEAD_DIM), F32)) for _ in range(G))
        carry = tile(i, init, True)
        carry = lax.fori_loop(0, i, lambda t, c: tile(i - 1 - t, c, False), carry)
        for g, (_, acc) in enumerate(carry):
            o_ref[:, lanes[g]] = acc.astype(o_ref.dtype)

    W = G * HEAD_DIM
    q_spec, k_spec, v_spec = _att_specs(L, G, q_col, k_col, v_col)
    o_spec = pl.BlockSpec((T, W), lambda h, i: (i, h))
    vmem = _vmem_limit([((L, W), BF)] * 2, temps=8 << 20)
    return _call(body, name=name, out_shape=jax.ShapeDtypeStruct((L, heads * HEAD_DIM), BF),
                 grid=(heads // G, L // T), in_specs=[q_spec, k_spec, v_spec], out_specs=o_spec,
                 sem=("parallel", "parallel"), vmem=vmem)(proj, proj, proj)


def _sb_bwd(name, proj, do, heads, G, q_col, k_col, v_col):
    L = proj.shape[0]
    T = ATT_TILE
    nq = L // T
    scale = HEAD_DIM ** -0.5
    lanes = _head_lanes(G)

    def body(q_ref, k_ref, v_ref, do_ref, dq_ref, dk_acc, dv_acc, da_buf, beta_buf):
        i = pl.program_id(1)

        @pl.when(i == 0)
        def _():
            dk_acc[...] = jnp.zeros_like(dk_acc)
            dv_acc[...] = jnp.zeros_like(dv_acc)

        qs = [q_ref[:, hl] for hl in lanes]
        dos = [do_ref[:, hl] for hl in lanes]
        row, col = _tile_iotas()
        later_mat = jnp.where(row > col, 1.0, 0.0).astype(BF)
        before_mat = jnp.where(row < col, 1.0, 0.0).astype(BF)

        def pass1(j, runs, masked):
            valid = (col < row) if masked else None
            qk = [_dot(qs[g], k_ref[_rows(j), hl], 1, 1) for g, hl in enumerate(lanes)]
            dw = [_dot(dos[g], v_ref[_rows(j), hl], 1, 1) for g, hl in enumerate(lanes)]
            logits = [_sb_logits(qk[g], scale, valid) for g in range(G)]
            pieces = [_split(lom, 2) for _, lom in logits]
            later = [_pieces_dot(pieces[g], later_mat) for g in range(G)]
            ws = []
            for g in range(G):
                w = jnp.exp(logits[g][0] + later[g] + runs[g])
                if masked:
                    w = jnp.where(valid, w, 0.0)
                da_buf[g * nq + j] = dw[g] * w
                beta_buf[g * nq + j] = jnp.exp(logits[g][0])
                ws.append(w.astype(BF))
            for g, hl in enumerate(lanes):
                dv_acc[_rows(j), hl] += _dot(ws[g], dos[g], 0, 0)
            return tuple(runs[g] + jnp.sum(logits[g][1], axis=1, keepdims=True) for g in range(G))

        runs = pass1(i, tuple(jnp.zeros((T, 1), F32) for _ in range(G)), True)
        lax.fori_loop(0, i, lambda t, c: pass1(i - 1 - t, c, False), runs)

        def pass2(j, carry, masked):
            das = [da_buf[g * nq + j] for g in range(G)]
            pieces = [_split(da, 2) for da in das]
            before = [_pieces_dot(pieces[g], before_mat) for g in range(G)]
            dzbs = []
            for g in range(G):
                beta = beta_buf[g * nq + j]
                dz = das[g] * (1.0 - beta) - (carry[g][0] + before[g]) * beta
                if masked:
                    dz = jnp.where(col < row, dz, 0.0)
                dzbs.append((dz * scale).astype(BF))
            for g, hl in enumerate(lanes):
                dk_acc[_rows(j), hl] += _dot(dzbs[g], qs[g], 0, 0)
            dq = [_dot(dzbs[g], k_ref[_rows(j), hl], 1, 0) for g, hl in enumerate(lanes)]
            return tuple((carry[g][0] + jnp.sum(das[g], axis=1, keepdims=True), carry[g][1] + dq[g])
                         for g in range(G))

        init = tuple((jnp.zeros((T, 1), F32), jnp.zeros((T, HEAD_DIM), F32)) for _ in range(G))
        carry = lax.fori_loop(0, i, lambda j, c: pass2(j, c, False), init)
        for g, (_, dq) in enumerate(pass2(i, carry, True)):
            dq_ref[:, lanes[g]] = dq.astype(dq_ref.dtype)

    WG = G * HEAD_DIM
    q_spec, k_spec, v_spec = _att_specs(L, G, q_col, k_col, v_col)
    t_spec = pl.BlockSpec((T, WG), lambda h, i: (i, h))
    head_spec = pl.BlockSpec((L, WG), lambda h, i: (0, h), pipeline_mode=pl.Buffered(1))
    W = heads * HEAD_DIM
    out_shape = [jax.ShapeDtypeStruct((L, W), BF)] + [jax.ShapeDtypeStruct((L, W), F32)] * 2
    scratch = [pltpu.VMEM((G * nq, T, T), F32)] * 2
    vmem = _vmem_limit([], [((L, WG), BF)] * 2 + [((L, WG), F32)] * 2 + [((G * nq, T, T), F32)] * 2,
                       temps=6 << 20)
    return _call(body, name=name, out_shape=out_shape, grid=(heads // G, nq),
                 in_specs=[q_spec, k_spec, v_spec, t_spec], out_specs=[t_spec, head_spec, head_spec],
                 scratch=scratch, sem=("parallel", "arbitrary"), vmem=vmem)(proj, proj, proj, do)


_ANY = pl.BlockSpec(memory_space=pl.ANY)


def _mesh_pos():
    return lax.axis_index("x"), lax.axis_index("y"), lax.axis_index("c")


def _other_chips(x, y):
    return [(1 - x, y), (x, 1 - y), (1 - x, 1 - y)]


def _shard_window(ref, kind, sidx, r0, nr, cs):
    if kind == "col":
        assert cs % LANES == 0
        return ref.at[pl.ds(r0, nr), pl.ds(pl.multiple_of(sidx * cs, LANES), cs)]
    return ref.at[sidx, pl.ds(r0, nr), :]


def _place_shard(name, pos, shard, kind, dtype):
    R, C = shard.shape
    tr = _row_tile(R, C, 2, 16)

    def body(pos_ref, s_ref, o_ref):
        o_ref[...] = s_ref[...].astype(o_ref.dtype)

    if kind == "col":
        assert C % LANES == 0
        out_shape = jax.ShapeDtypeStruct((R, N_CHIPS * C), dtype)
        out_spec = pl.BlockSpec((tr, C), lambda r, pos_ref: (r, pos_ref[1]))
    else:
        out_shape = jax.ShapeDtypeStruct((N_CHIPS, R, C), dtype)
        out_spec = pl.BlockSpec((None, tr, C), lambda r, pos_ref: (pos_ref[1], r, 0))
    return _call(body, name=name, out_shape=out_shape, grid=(R // tr,),
                 in_specs=[pl.BlockSpec((tr, C), lambda r, pos_ref: (r, 0))], out_specs=out_spec, sem=("parallel",),
                 vmem=32 << 20, prefetch=1)(pos, shard)


_HBM = pl.BlockSpec(memory_space=pltpu.HBM)
_SEM = pl.BlockSpec(memory_space=pltpu.SEMAPHORE)
_KEEP_ORDER = pltpu.SideEffectType.DATAFLOW_SIDE_EFFECTING


def _in_hbm(x):
    return pltpu.with_memory_space_constraint(x, pltpu.HBM)


def _gather_copies(bufs, meta, send_sem, recv_sem):
    x, y, c = _mesh_pos()
    out = []
    for t, (kind, R, cs) in enumerate(meta):
        half = R // 2
        r0 = pl.multiple_of(c * half, SUBLANES)
        mine = _shard_window(bufs[t], kind, 2 * x + y, r0, half, cs)
        for p, (px, py) in enumerate(_other_chips(x, y)):
            k = 3 * t + p
            args = dict(send_sem=send_sem.at[k], recv_sem=recv_sem.at[k], device_id=(px, py, c),
                        device_id_type=MESH_IDS)
            out.append((pltpu.make_async_remote_copy(src_ref=mine, dst_ref=mine, **args),
                        pltpu.make_async_remote_copy(
                            src_ref=mine, dst_ref=_shard_window(bufs[t], kind, 2 * px + py, r0, half, cs), **args)))
    return out


def _all_gather_ici_start(name, gathered, shards, kinds, after):
    n = len(gathered)
    meta = [(kind, s.shape[0], s.shape[1]) for s, kind in zip(shards, kinds)]

    def body(*refs):
        send_sem, recv_sem = refs[n + 1], refs[n + 2]
        bufs = refs[n + 3:2 * n + 3]
        token = refs[2 * n + 3]
        for send, _ in _gather_copies(bufs, meta, send_sem, recv_sem):
            send.start()
        token[...] = jnp.zeros_like(token)

    out_shape = (pltpu.SemaphoreType.DMA((3 * n,)), pltpu.SemaphoreType.DMA((3 * n,)),
                 *[pltpu.HBM(g.shape, g.dtype) for g in gathered], jax.ShapeDtypeStruct((SUBLANES, LANES), F32))
    out_specs = (_SEM, _SEM, *[_HBM] * n, pl.BlockSpec(memory_space=pltpu.VMEM))
    res = pl.pallas_call(body, out_shape=out_shape, in_specs=[_HBM] * n + [_ANY], out_specs=out_specs,
                         input_output_aliases={t: 2 + t for t in range(n)}, name=name,
                         compiler_params=pltpu.CompilerParams(has_side_effects=_KEEP_ORDER))(
        *[_in_hbm(g) for g in gathered], after)
    return res[0], res[1], list(res[2:2 + n]), res[2 + n]


def _all_gather_ici_wait(name, bufs, send_sem, recv_sem, after, shards, kinds):
    n = len(bufs)
    meta = [(kind, s.shape[0], s.shape[1]) for s, kind in zip(shards, kinds)]

    def body(*refs):
        for send, recv in _gather_copies(refs[:n], meta, refs[n], refs[n + 1]):
            send.wait_send()
            recv.wait_recv()

    out_shape = tuple(pltpu.HBM(b.shape, b.dtype) for b in bufs)
    res = pl.pallas_call(body, out_shape=out_shape, in_specs=[_HBM] * n + [_SEM, _SEM, _ANY],
                         out_specs=tuple([_HBM] * n), input_output_aliases={t: t for t in range(n)}, name=name,
                         compiler_params=pltpu.CompilerParams(has_side_effects=_KEEP_ORDER))(
        *bufs, send_sem, recv_sem, after)
    return list(res)


def _all_gather_d2d(name, gathered, shards, kinds):
    n = len(gathered)
    meta = [(kind, s.shape[0], s.shape[1]) for s, kind in zip(shards, kinds)]

    def body(*refs):
        bufs = refs[n:2 * n]
        send_sem, recv_sem = refs[2 * n:]
        x, y, c = _mesh_pos()
        sends, recvs = [], []
        for t, (kind, R, cs) in enumerate(meta):
            half = R // 2
            mine = pl.multiple_of(c * half, SUBLANES)
            theirs = pl.multiple_of((1 - c) * half, SUBLANES)
            for p, (px, py) in enumerate(_other_chips(x, y)):
                k = 3 * t + p
                win = _shard_window(bufs[t], kind, 2 * px + py, mine, half, cs)
                cp = pltpu.make_async_remote_copy(src_ref=win, dst_ref=win, send_sem=send_sem.at[k],
                                                  recv_sem=recv_sem.at[k], device_id=(x, y, 1 - c),
                                                  device_id_type=MESH_IDS)
                cp.start()
                sends.append(cp)
                got = _shard_window(bufs[t], kind, 2 * px + py, theirs, half, cs)
                recvs.append(pltpu.make_async_remote_copy(src_ref=win, dst_ref=got, send_sem=send_sem.at[k],
                                                          recv_sem=recv_sem.at[k], device_id=(x, y, 1 - c),
                                                          device_id_type=MESH_IDS))
        for cp in recvs:
            cp.wait_recv()
        for cp in sends:
            cp.wait_send()

    scratch = [pltpu.SemaphoreType.DMA((3 * n,)), pltpu.SemaphoreType.DMA((3 * n,))]
    out_shape = [jax.ShapeDtypeStruct(g.shape, g.dtype) for g in gathered]
    return _call(body, name=name, out_shape=out_shape, in_specs=[_ANY] * n, out_specs=[_ANY] * n, scratch=scratch,
                 aliases={t: t for t in range(n)})(*gathered)


def _reduce_to_sibling(name, grads):
    n = len(grads)

    def body(*refs):
        srcs, dsts = refs[:n], refs[n:2 * n]
        send_sem, recv_sem = refs[2 * n:]
        x, y, c = _mesh_pos()
        copies = []
        for t, g in enumerate(grads):
            half = g.shape[1] // 2
            theirs = pl.multiple_of((1 - c) * half, SUBLANES)
            cp = pltpu.make_async_remote_copy(src_ref=srcs[t].at[:, pl.ds(theirs, half), :], dst_ref=dsts[t],
                                              send_sem=send_sem.at[t], recv_sem=recv_sem.at[t],
                                              device_id=(x, y, 1 - c), device_id_type=MESH_IDS)
            cp.start()
            copies.append(cp)
        for cp in copies:
            cp.wait()

    out_shape = [jax.ShapeDtypeStruct((N_CHIPS, g.shape[1] // 2, g.shape[2]), g.dtype) for g in grads]
    scratch = [pltpu.SemaphoreType.DMA((n,)), pltpu.SemaphoreType.DMA((n,))]
    return _call(body, name=name, out_shape=out_shape, in_specs=[_ANY] * n, out_specs=[_ANY] * n,
                 scratch=scratch)(*grads)


def _exchange_copies(parts, lands, send_sem, recv_sem):
    x, y, c = _mesh_pos()
    out = []
    for t in range(len(parts)):
        for p, (px, py) in enumerate(_other_chips(x, y)):
            k = 3 * t + p
            out.append(pltpu.make_async_remote_copy(src_ref=parts[t].at[2 * px + py], dst_ref=lands[t].at[p],
                                                    send_sem=send_sem.at[k], recv_sem=recv_sem.at[k],
                                                    device_id=(px, py, c), device_id_type=MESH_IDS))
    return out


def _exchange_start(name, partials):
    n = len(partials)
    lands = [lax.empty((3,) + p.shape[1:], p.dtype) for p in partials]

    def body(*refs):
        send_sem, recv_sem = refs[2 * n], refs[2 * n + 1]
        parts, zones = refs[2 * n + 2:3 * n + 2], refs[3 * n + 2:4 * n + 2]
        token = refs[4 * n + 2]
        for cp in _exchange_copies(parts, zones, send_sem, recv_sem):
            cp.start()
        token[...] = jnp.zeros_like(token)

    out_shape = (pltpu.SemaphoreType.DMA((3 * n,)), pltpu.SemaphoreType.DMA((3 * n,)),
                 *[pltpu.HBM(a.shape, a.dtype) for a in partials + lands],
                 jax.ShapeDtypeStruct((SUBLANES, LANES), F32))
    out_specs = (_SEM, _SEM, *[_HBM] * (2 * n), pl.BlockSpec(memory_space=pltpu.VMEM))
    res = pl.pallas_call(body, out_shape=out_shape, in_specs=[_HBM] * (2 * n), out_specs=out_specs,
                         input_output_aliases={t: 2 + t for t in range(2 * n)}, name=name,
                         compiler_params=pltpu.CompilerParams(has_side_effects=_KEEP_ORDER))(
        *[_in_hbm(a) for a in partials + lands])
    return res[0], res[1], list(res[2:2 + n]), list(res[2 + n:2 + 2 * n]), res[2 + 2 * n]


def _exchange_wait(name, parts, lands, send_sem, recv_sem, after):
    n = len(parts)

    def body(*refs):
        for cp in _exchange_copies(refs[:n], refs[n:2 * n], refs[2 * n], refs[2 * n + 1]):
            cp.wait_send()
            cp.wait_recv()

    out_shape = tuple(pltpu.HBM(a.shape, a.dtype) for a in parts + lands)
    res = pl.pallas_call(body, out_shape=out_shape, in_specs=[_HBM] * (2 * n) + [_SEM, _SEM, _ANY],
                         out_specs=tuple([_HBM] * (2 * n)), input_output_aliases={t: t for t in range(2 * n)},
                         name=name, compiler_params=pltpu.CompilerParams(has_side_effects=_KEEP_ORDER))(
        *parts, *lands, send_sem, recv_sem, after)
    return list(res[n:])


def _share_halves(name, totals):
    n = len(totals)

    def body(*refs):
        bufs = refs[n:2 * n]
        send_sem, recv_sem = refs[2 * n:]
        x, y, c = _mesh_pos()
        sends, recvs = [], []
        for t, g in enumerate(totals):
            half = g.shape[0] // 2
            mine = bufs[t].at[pl.ds(pl.multiple_of(c * half, SUBLANES), half), :]
            theirs = bufs[t].at[pl.ds(pl.multiple_of((1 - c) * half, SUBLANES), half), :]
            cp = pltpu.make_async_remote_copy(src_ref=mine, dst_ref=mine, send_sem=send_sem.at[t],
                                              recv_sem=recv_sem.at[t], device_id=(x, y, 1 - c),
                                              device_id_type=MESH_IDS)
            cp.start()
            sends.append(cp)
            recvs.append(pltpu.make_async_remote_copy(src_ref=mine, dst_ref=theirs, send_sem=send_sem.at[t],
                                                      recv_sem=recv_sem.at[t], device_id=(x, y, 1 - c),
                                                      device_id_type=MESH_IDS))
        for cp in recvs:
            cp.wait_recv()
        for cp in sends:
            cp.wait_send()

    out_shape = [jax.ShapeDtypeStruct(g.shape, g.dtype) for g in totals]
    scratch = [pltpu.SemaphoreType.DMA((n,)), pltpu.SemaphoreType.DMA((n,))]
    return _call(body, name=name, out_shape=out_shape, in_specs=[_ANY] * n, out_specs=[_ANY] * n, scratch=scratch,
                 aliases={t: t for t in range(n)})(*totals)


def _gather_small(name, v):
    def body(v_ref, out_ref, send_sem, recv_sem, local_sem):
        x, y, c = _mesh_pos()
        me = 4 * x + 2 * y + c
        local = pltpu.make_async_copy(v_ref, out_ref.at[me], local_sem)
        local.start()
        sends, recvs = [], []
        for k in range(1, N_DEV):
            px = 1 - x if k & 4 else x
            py = 1 - y if k & 2 else y
            pc = 1 - c if k & 1 else c
            cp = pltpu.make_async_remote_copy(src_ref=v_ref, dst_ref=out_ref.at[me], send_sem=send_sem.at[k],
                                              recv_sem=recv_sem.at[k], device_id=(px, py, pc),
                                              device_id_type=MESH_IDS)
            cp.start()
            sends.append(cp)
            recvs.append(pltpu.make_async_remote_copy(
                src_ref=v_ref, dst_ref=out_ref.at[4 * px + 2 * py + pc], send_sem=send_sem.at[k],
                recv_sem=recv_sem.at[k], device_id=(px, py, pc), device_id_type=MESH_IDS))
        for cp in recvs:
            cp.wait_recv()
        for cp in sends:
            cp.wait_send()
        local.wait()

    scratch = [pltpu.SemaphoreType.DMA((N_DEV,)), pltpu.SemaphoreType.DMA((N_DEV,)), pltpu.SemaphoreType.DMA(())]
    return _call(body, name=name, out_shape=jax.ShapeDtypeStruct((N_DEV,) + v.shape, v.dtype), in_specs=[_ANY],
                 out_specs=_ANY, scratch=scratch)(v)


def _row_tile(rows, cols, n_arrays, mult=SUBLANES):
    budget = (24 << 20) // (2 * n_arrays * _round_up(cols, LANES) * 4)
    for t in (512, 256, 128, 64, 32, 16, 8):
        if t <= max(budget, mult) and rows % t == 0 and t % mult == 0:
            return t
    raise ValueError(f"no row tile for {rows} x {cols}")


def _chip_partial(name, pos, own, recv):
    _, half, C = recv.shape
    tr = _row_tile(half, C, 3, 16)
    nh = half // tr

    def body(pos_ref, own_ref, recv_ref, out_ref):
        out_ref[...] = (own_ref[...] + recv_ref[...]).astype(BF)

    blk = pl.BlockSpec((None, tr, C), lambda s, r, pos_ref: (s, r, 0))
    own_blk = pl.BlockSpec((None, tr, C), lambda s, r, pos_ref: (s, pos_ref[0] * nh + r, 0))
    return _call(body, name=name, out_shape=jax.ShapeDtypeStruct(recv.shape, BF), grid=(N_CHIPS, nh),
                 in_specs=[own_blk, blk], out_specs=blk, sem=("parallel", "parallel"), vmem=40 << 20,
                 prefetch=1)(pos, own, recv)


def _final_half(name, pos, own, recv, others):
    _, half, C = recv.shape
    tr = _row_tile(half, C, 4, 16)
    nh = half // tr

    def body(pos_ref, own_ref, recv_ref, oth_ref, out_ref):
        acc = own_ref[...] + recv_ref[...]
        for p in range(3):
            acc = acc + oth_ref[p].astype(F32)
        out_ref[...] = acc

    own_blk = pl.BlockSpec((None, tr, C), lambda r, pos_ref: (pos_ref[1], pos_ref[0] * nh + r, 0))
    recv_blk = pl.BlockSpec((None, tr, C), lambda r, pos_ref: (pos_ref[1], r, 0))
    oth_blk = pl.BlockSpec((3, tr, C), lambda r, pos_ref: (0, r, 0))
    out_blk = pl.BlockSpec((tr, C), lambda r, pos_ref: (pos_ref[0] * nh + r, 0))
    return _call(body, name=name, out_shape=jax.ShapeDtypeStruct((2 * half, C), F32), grid=(nh,),
                 in_specs=[own_blk, recv_blk, oth_blk], out_specs=out_blk, sem=("parallel",), vmem=40 << 20,
                 prefetch=1)(pos, own, recv, others)


def _adamw(name, w, g, m, v):
    R, C = w.shape
    tr = R if R < SUBLANES or R % SUBLANES else _row_tile(R, C, 7)
    c1 = 1.0 - ADAM_B1 ** ADAM_STEP
    c2 = 1.0 - ADAM_B2 ** ADAM_STEP

    def body(w_ref, g_ref, m_ref, v_ref, d_ref, nm_ref, nv_ref):
        gv = g_ref[...]
        nm = ADAM_B1 * m_ref[...] + (1.0 - ADAM_B1) * gv
        nv = ADAM_B2 * v_ref[...] + (1.0 - ADAM_B2) * (gv * gv)
        d_ref[...] = -ADAM_LR * ((nm / c1) / (jnp.sqrt(nv / c2) + ADAM_EPS) + ADAM_WD * w_ref[...])
        nm_ref[...] = nm
        nv_ref[...] = nv

    blk = pl.BlockSpec((tr, C), lambda r: (r, 0))
    out_shape = [jax.ShapeDtypeStruct((R, C), F32)] * 3
    return _call(body, name=name, out_shape=out_shape, grid=(R // tr,), in_specs=[blk] * 4, out_specs=[blk] * 3,
                 sem=("parallel",), vmem=40 << 20)(w, g, m, v)


def _sum_devices(name, gathered):
    _, R, _ = gathered.shape

    def body(g_ref, o_ref):
        acc = g_ref[0]
        for d in range(1, N_DEV):
            acc = acc + g_ref[d]
        o_ref[...] = acc

    return _call(body, name=name, out_shape=jax.ShapeDtypeStruct((R, LANES), F32), grid=(1,),
                 in_specs=[pl.BlockSpec((N_DEV, R, LANES), lambda i: (0, 0, 0))],
                 out_specs=pl.BlockSpec((R, LANES), lambda i: (0, 0)), sem=("arbitrary",), vmem=16 << 20)(gathered)


def _ffn_fwd(tag, h, gain, wg, wu, wd, tm):
    L, D = h.shape
    F = wg.shape[1]
    tn = _pick(F, (512, 256, 128))
    n = _rmsnorm_fwd(f"{tag}_norm", h, gain, tm)

    def gate_up(accs, _):
        a, u = accs
        return a, u, a * _sigmoid(a) * u

    mn = _mn(tm, tn)
    a, u, s = _matmul(f"{tag}_gate_up", [n], [wg, wu], [(0, 0, 0), (0, 1, 1)], 2, gate_up,
                      [((L, F), BF) + mn] * 3, M=L, N=F, K=D, tm=tm, tn=tn, tk=D, n_outer=True)
    if callable(wd):
        wd = wd(s)
    td = _pick(D, (512, 256, 128))
    (h_out,) = _matmul(f"{tag}_down", [s], [wd], [(0, 0, 0)], 1,
                       lambda accs, ex: [ex[0] + FFN_RESIDUAL_WEIGHT * accs[0]], [((L, D), F32) + _mn(tm, td)],
                       M=L, N=D, K=F, tm=tm, tn=td, tk=F, extras=[(h,) + _mn(tm, td)], n_outer=True)
    return h_out, (h, n, a, u, s), wd


def _ffn_bwd(tag, dh, dh_bf, saved, gain, wg, wu, wd, tm, cs_ff, after=None, on_weight_grads=None):
    h, n, a, u, s = saved
    L, D = h.shape
    F = wg.shape[1]
    tn = _pick(F, (512, 256, 128))
    tkl = _pick(L, (1408, 384, 256, 128))

    def act_grad(accs, ex):
        ds = FFN_RESIDUAL_WEIGHT * accs[0]
        av, uv = ex[0].astype(F32), ex[1].astype(F32)
        sg = _sigmoid(av)
        return ds * uv * sg * (1.0 + av * (1.0 - sg)), ds * av * sg

    mn = _mn(tm, tn)
    da, du = _matmul(f"{tag}_dact", [dh_bf], [wd], [(0, 0, 0)], 1, act_grad, [((L, F), BF) + mn] * 2, M=L, N=F,
                     K=D, tm=tm, tn=tn, tk=D, tb=True, extras=[(a,) + mn, (u,) + mn], n_outer=True, after=after)
    td = _pick(D, (512, 256, 128))
    (dwd,) = _matmul(f"{tag}_dwd", [s], [dh_bf], [(0, 0, 0)], 1, lambda accs, _: [FFN_RESIDUAL_WEIGHT * accs[0]],
                     [((N_CHIPS, cs_ff, D), F32, (None, cs_ff, td), lambda i, j: (i, 0, j))], M=F, N=D, K=L,
                     tm=cs_ff, tn=td, tk=tkl, ta=True, after=after)
    tmw = _pick(D, (512, 256, 128))
    shard_out = ((N_CHIPS, D, cs_ff), F32, (None, tmw, cs_ff), lambda i, j: (j, i, 0))
    dwg, dwu = _matmul(f"{tag}_dwgu", [n], [da, du], [(0, 0, 0), (0, 1, 1)], 2, lambda accs, _: accs,
                       [shard_out] * 2, M=D, N=F, K=L, tm=tmw, tn=cs_ff, tk=tkl, ta=True)
    started = on_weight_grads(dwg, dwu, dwd) if on_weight_grads else None
    tdn = _pick(D, (256, 128))
    (dn,) = _matmul(f"{tag}_dn", [da, du], [wg, wu], [(0, 0, 0), (1, 1, 0)], 1, lambda accs, _: accs,
                    [((L, D), F32) + _mn(tm, tdn)], M=L, N=D, K=F, tm=tm, tn=tdn, tk=F, tb=True, after=started)
    dh_in, dh_in_bf, dgain = _rmsnorm_bwd(f"{tag}_dnorm", dn, h, gain, dh, tm)
    return dh_in, dh_in_bf, dgain


def kernel(x, meta_tokens, ffn1_norm, ffn1_w_gate, ffn1_w_up, ffn1_w_down, mix_norm, w_in, b_forget, fox_q_norm, fox_k_norm, w_branch_fox, w_branch_sb, w_out, ffn2_norm, ffn2_w_gate, ffn2_w_up, ffn2_w_down, loss_target, m_meta_tokens, m_ffn1_norm, m_ffn1_w_gate, m_ffn1_w_up, m_ffn1_w_down, m_mix_norm, m_w_in, m_b_forget, m_fox_q_norm, m_fox_k_norm, m_w_branch_fox, m_w_branch_sb, m_w_out, m_ffn2_norm, m_ffn2_w_gate, m_ffn2_w_up, m_ffn2_w_down, v_meta_tokens, v_ffn1_norm, v_ffn1_w_gate, v_ffn1_w_up, v_ffn1_w_down, v_mix_norm, v_w_in, v_b_forget, v_fox_q_norm, v_fox_k_norm, v_w_branch_fox, v_w_branch_sb, v_w_out, v_ffn2_norm, v_ffn2_w_gate, v_ffn2_w_up, v_ffn2_w_down):
    weights = dict(meta_tokens=meta_tokens, ffn1_norm=ffn1_norm, ffn1_w_gate=ffn1_w_gate, ffn1_w_up=ffn1_w_up,
                   ffn1_w_down=ffn1_w_down, mix_norm=mix_norm, w_in=w_in, b_forget=b_forget, fox_q_norm=fox_q_norm,
                   fox_k_norm=fox_k_norm, w_branch_fox=w_branch_fox, w_branch_sb=w_branch_sb, w_out=w_out,
                   ffn2_norm=ffn2_norm, ffn2_w_gate=ffn2_w_gate, ffn2_w_up=ffn2_w_up, ffn2_w_down=ffn2_w_down)
    moments_m = dict(meta_tokens=m_meta_tokens, ffn1_norm=m_ffn1_norm, ffn1_w_gate=m_ffn1_w_gate,
                     ffn1_w_up=m_ffn1_w_up, ffn1_w_down=m_ffn1_w_down, mix_norm=m_mix_norm, w_in=m_w_in,
                     b_forget=m_b_forget, fox_q_norm=m_fox_q_norm, fox_k_norm=m_fox_k_norm,
                     w_branch_fox=m_w_branch_fox, w_branch_sb=m_w_branch_sb, w_out=m_w_out, ffn2_norm=m_ffn2_norm,
                     ffn2_w_gate=m_ffn2_w_gate, ffn2_w_up=m_ffn2_w_up, ffn2_w_down=m_ffn2_w_down)
    moments_v = dict(meta_tokens=v_meta_tokens, ffn1_norm=v_ffn1_norm, ffn1_w_gate=v_ffn1_w_gate,
                     ffn1_w_up=v_ffn1_w_up, ffn1_w_down=v_ffn1_w_down, mix_norm=v_mix_norm, w_in=v_w_in,
                     b_forget=v_b_forget, fox_q_norm=v_fox_q_norm, fox_k_norm=v_fox_k_norm,
                     w_branch_fox=v_w_branch_fox, w_branch_sb=v_w_branch_sb, w_out=v_w_out, ffn2_norm=v_ffn2_norm,
                     ffn2_w_gate=v_ffn2_w_gate, ffn2_w_up=v_ffn2_w_up, ffn2_w_down=v_ffn2_w_down)
    names = list(weights)

    _, S, D = x.shape
    NM = meta_tokens.shape[0]
    L_real = NM + S
    L = _round_up(L_real, ATT_TILE)
    nblk = L // ATT_TILE
    nbp = _round_up(nblk, SUBLANES)
    cs_ff = ffn1_w_gate.shape[2]
    F = N_CHIPS * cs_ff
    H = b_forget.shape[1]
    FW = w_branch_fox.shape[1]
    SW = w_branch_sb.shape[1]
    HS = SW // HEAD_DIM
    cs_in = w_in.shape[2]
    W_IN = N_CHIPS * cs_in
    assert FW == H * HEAD_DIM and W_IN == 3 * FW + H + 3 * SW + 2 * D
    cs_d = D // N_CHIPS
    tm = _pick(L, (384, 256, 128))

    x_pos, y_pos, c_pos = _mesh_pos()
    pos = jnp.stack([c_pos, 2 * x_pos + y_pos]).astype(I32)

    shard_of = {
        "ffn1_w_gate": (ffn1_w_gate[0], "col"), "ffn1_w_up": (ffn1_w_up[0], "col"),
        "ffn1_w_down": (ffn1_w_down[0], "maj"), "w_in": (w_in[0], "maj"),
        "w_branch_fox": (w_branch_fox[0], "col"), "w_branch_sb": (w_branch_sb[0], "col"),
        "w_out": (w_out[0], "maj"), "ffn2_w_gate": (ffn2_w_gate[0], "col"), "ffn2_w_up": (ffn2_w_up[0], "col"),
        "ffn2_w_down": (ffn2_w_down[0], "maj"),
    }
    g_names = list(shard_of) + ["meta_tokens"]
    shards = [shard_of[k][0] for k in shard_of] + [meta_tokens]
    kinds = [shard_of[k][1] for k in shard_of] + ["col"]
    dtypes = [BF] * len(shard_of) + [F32]
    info = {k: (s, kind) for k, s, kind in zip(g_names, shards, kinds)}
    placed = {k: _place_shard(f"place_{k}", pos, s, kind, dt)
              for k, s, kind, dt in zip(g_names, shards, kinds, dtypes)}
    groups = [["meta_tokens", "ffn1_w_gate", "ffn1_w_up"], ["ffn1_w_down"], ["w_in"],
              ["w_branch_fox", "w_branch_sb", "w_out", "ffn2_w_gate", "ffn2_w_up", "ffn2_w_down"]]
    in_flight = []
    all_started = jnp.zeros((SUBLANES, LANES), F32)
    for gi, grp in enumerate(groups):
        g_shards, g_kinds = [info[k][0] for k in grp], [info[k][1] for k in grp]
        in_flight.append(_all_gather_ici_start(f"gather_start_{gi}", [placed[k] for k in grp], g_shards, g_kinds,
                                               all_started))
        all_started = in_flight[-1][3]
    full = {}

    def arrive(gi, after):
        grp = groups[gi]
        g_shards, g_kinds = [info[k][0] for k in grp], [info[k][1] for k in grp]
        send_sem, recv_sem, bufs, _ = in_flight[gi]
        bufs = _all_gather_ici_wait(f"gather_wait_{gi}", bufs, send_sem, recv_sem, after, g_shards, g_kinds)
        full.update(zip(grp, _all_gather_d2d(f"gather_d2d_{gi}", bufs, g_shards, g_kinds)))

    arrive(0, all_started)
    wg1, wu1 = full["ffn1_w_gate"], full["ffn1_w_up"]
    c_f = 3 * FW
    QKV_S, GATES, FCOL = 3 * FW, 3 * FW + 3 * SW, 3 * FW + 3 * SW + 2 * D
    W_PROJ = FCOL + LANES

    h0 = jnp.concatenate([full["meta_tokens"], x[0], jnp.zeros((L - L_real, D), F32)], axis=0)
    target = jnp.concatenate([jnp.zeros((NM, D), F32), loss_target[0], jnp.zeros((L - L_real, D), F32)], axis=0)

    def late_wd1(s):
        arrive(1, s)
        return full["ffn1_w_down"].reshape(F, D)

    h1, saved1, wd1 = _ffn_fwd("ffn1", h0, ffn1_norm, wg1, wu1, late_wd1, tm)

    arrive(2, h1)
    w_in_full = jnp.transpose(full["w_in"], (1, 0, 2)).reshape(D, W_IN)
    w_proj = jnp.concatenate([w_in_full[:, :c_f], w_in_full[:, c_f + H:],
                              jnp.pad(w_in_full[:, c_f:c_f + H], ((0, 0), (0, LANES - H)))], axis=1)
    n2 = _rmsnorm_fwd("mix_norm", h1, mix_norm, tm)
    tp = _pick(FCOL, (512, 256, 128))
    (proj,) = _matmul("in_proj", [n2], [w_proj], [(0, 0, 0)], 1, lambda accs, _: accs,
                      [((L, FCOL), BF) + _mn(tm, tp)], M=L, N=FCOL, K=D, tm=tm, tn=tp, tk=D, n_outer=True)
    (f_logit,) = _matmul("forget_proj", [n2], [w_proj], [(0, 0, 0)], 1, lambda accs, _: accs,
                         [((L, LANES), F32) + _mn(tm, LANES)], M=L, N=LANES, K=D, tm=tm, tn=LANES, tk=D,
                         b_off=FCOL // LANES)
    fl = jnp.pad(jnp.transpose(f_logit[:, :H]).reshape(H, nblk, LANES), ((0, 0), (0, nbp - nblk), (0, 0)))
    bias = jnp.broadcast_to(b_forget[0][:, None, None], (H, 1, LANES))
    c_row = _cum_fwd("forget_cumsum", fl, bias, nblk)
    c_col = c_row[:, :nblk].reshape(H, L, 1)
    gq, gk = fox_q_norm[0][:, None, :], fox_k_norm[0][:, None, :]
    qn, kn = _qknorm_fwd("fox_qk_norm", proj, gq, gk, H, 0, FW, tm)
    o_fox, lse = _fox_fwd("fox_attention", qn, kn, proj, c_row, c_col, H, ATT_HEADS, 0, 0, 2 * FW)
    o_sb = _sb_fwd("sb_attention", proj, HS, min(HS, ATT_HEADS_SB_FWD), QKV_S, QKV_S + SW, QKV_S + 2 * SW)

    arrive(3, o_sb)
    wg2, wu2 = full["ffn2_w_gate"], full["ffn2_w_up"]
    wd2 = full["ffn2_w_down"].reshape(F, D)
    wbf, wbs = full["w_branch_fox"], full["w_branch_sb"]
    wo = full["w_out"].reshape(D, D)
    td = _pick(D, (512, 256, 128))

    def merge(accs, ex):
        bf_, bs_ = accs
        return _sigmoid(ex[0].astype(F32)) * bf_ + _sigmoid(ex[1].astype(F32)) * bs_, bf_, bs_

    merged, br_f, br_s = _matmul("branch_merge", [o_fox, o_sb], [wbf, wbs], [(0, 0, 0), (1, 1, 1)], 2, merge,
                                 [((L, D), BF) + _mn(tm, td)] * 3, M=L, N=D, K=FW, tm=tm, tn=td, tk=FW,
                                 extras=[(proj,) + _mn(tm, td, GATES), (proj,) + _mn(tm, td, GATES + D)],
                                 n_outer=True)
    (h2,) = _matmul("out_proj", [merged], [wo], [(0, 0, 0)], 1, lambda accs, ex: [ex[0] + accs[0]],
                    [((L, D), F32) + _mn(tm, td)], M=L, N=D, K=D, tm=tm, tn=td, tk=D, extras=[(h1,) + _mn(tm, td)],
                    n_outer=True)

    h3, saved2, _ = _ffn_fwd("ffn2", h2, ffn2_norm, wg2, wu2, wd2, tm)
    dh3, dh3_bf, loss_part = _loss_grad("loss", h3, target, NM, S, tm)

    scatters, totals = {}, {}

    def scatter_begin(tag, keys, local):
        from_sibling = _reduce_to_sibling(f"grads_to_sibling_{tag}", local)
        partials = [_chip_partial(f"chip_sum_{k}", pos, g, r) for k, g, r in zip(keys, local, from_sibling)]
        send_sem, recv_sem, parts, lands, token = _exchange_start(f"grads_to_owner_start_{tag}", partials)
        scatters[tag] = (keys, local, from_sibling, send_sem, recv_sem, parts, lands)
        return token

    def scatter_end(tag, after):
        keys, local, from_sibling, send_sem, recv_sem, parts, lands = scatters[tag]
        from_chips = _exchange_wait(f"grads_to_owner_wait_{tag}", parts, lands, send_sem, recv_sem, after)
        totals.update({k: _final_half(f"total_{k}", pos, g, r, o)
                       for k, g, r, o in zip(keys, local, from_sibling, from_chips)})

    dh2, dh2_bf, dg_ffn2 = _ffn_bwd(
        "ffn2", dh3, dh3_bf, saved2, ffn2_norm, wg2, wu2, wd2, tm, cs_ff,
        on_weight_grads=lambda *dw: scatter_begin("ffn2", ["ffn2_w_gate", "ffn2_w_up", "ffn2_w_down"], list(dw)))

    def gate_grad(accs, ex):
        dm = accs[0]
        gf, gs, bf_, bs_ = [e.astype(F32) for e in ex]
        sf, ss = _sigmoid(gf), _sigmoid(gs)
        return dm * bf_ * sf * (1.0 - sf), dm * bs_ * ss * (1.0 - ss), dm * sf, dm * ss

    mn_d = _mn(tm, td)
    dgf, dgs, dbr_f, dbr_s = _matmul(
        "d_merged", [dh2_bf], [wo], [(0, 0, 0)], 1, gate_grad, [((L, D), BF) + mn_d] * 4, M=L, N=D, K=D, tm=tm,
        tn=td, tk=D, tb=True, extras=[(proj,) + _mn(tm, td, GATES), (proj,) + _mn(tm, td, GATES + D),
                                      (br_f,) + mn_d, (br_s,) + mn_d], n_outer=True)
    tkl = _pick(L, (1408, 384, 256, 128))
    (dwo,) = _matmul("d_w_out", [merged], [dh2_bf], [(0, 0, 0)], 1, lambda accs, _: accs,
                     [((N_CHIPS, cs_d, D), F32, (None, cs_d, td), lambda i, j: (i, 0, j))], M=D, N=D, K=L, tm=cs_d,
                     tn=td, tk=tkl, ta=True)
    tw = _pick(FW, (512, 256, 128))
    do_fox, do_sb = _matmul("d_branch_in", [dbr_f, dbr_s], [wbf, wbs], [(0, 0, 0), (1, 1, 1)], 2,
                            lambda accs, _: accs, [((L, FW), BF) + _mn(tm, tw)] * 2, M=L, N=FW, K=D, tm=tm, tn=tw,
                            tk=D, tb=True)
    tmb = _pick(FW, (1024, 512, 256, 128))
    dwbf, dwbs = _matmul("d_w_branch", [o_fox, o_sb], [dbr_f, dbr_s], [(0, 0, 0), (1, 1, 1)], 2,
                         lambda accs, _: accs,
                         [((N_CHIPS, FW, cs_d), F32, (None, tmb, cs_d), lambda i, j: (j, i, 0))] * 2, M=FW, N=D, K=L,
                         tm=tmb, tn=cs_d, tk=tkl, ta=True)

    dqn, dkn, dfv, dcs, drs = _fox_bwd("fox_attention_bwd", qn, kn, proj, c_row, c_col, o_fox, do_fox, lse, H,
                                       ATT_HEADS, 0, 0, 2 * FW)
    dsq, dsk, dsv = _sb_bwd("sb_attention_bwd", proj, do_sb, HS, ATT_HEADS, QKV_S, QKV_S + SW,
                            QKV_S + 2 * SW)
    dfq, dfk, dgq, dgk = _qknorm_bwd("fox_qk_norm_bwd", proj, dqn, dkn, gq, gk, H, 0, FW, tm)
    drs_row = jnp.pad(drs.reshape(H, nblk, LANES), ((0, 0), (0, nbp - nblk), (0, 0)))
    dfl, dbias = _cum_bwd("forget_cumsum_bwd", drs_row, dcs, fl, bias, nblk)
    dfl_cols = jnp.pad(jnp.transpose(dfl[:, :nblk].reshape(H, L)), ((0, 0), (0, LANES - H))).astype(BF)
    dproj = jnp.concatenate([dfq, dfk, dfv.astype(BF), dsq, dsk.astype(BF), dsv.astype(BF), dgf, dgs, dfl_cols],
                            axis=1)

    tdn = _pick(D, (256, 128))
    (dn2,) = _matmul("d_mix_norm_in", [dproj], [w_proj], [(0, 0, 0)], 1, lambda accs, _: accs,
                     [((L, D), F32) + _mn(tm, tdn)], M=L, N=D, K=W_PROJ, tm=tm, tn=tdn, tk=W_PROJ, tb=True)
    tmw = _pick(D, (1024, 512, 256, 128))
    tnp = _pick(W_PROJ, (1152, 640, 512, 384, 256, 128))
    (dw_proj,) = _matmul("d_w_in", [n2], [dproj], [(0, 0, 0)], 1, lambda accs, _: accs,
                         [((D, W_PROJ), F32) + _mn(tmw, tnp)], M=D, N=W_PROJ, K=L, tm=tmw, tn=tnp, tk=tkl,
                         ta=True)
    dh1, dh1_bf, dg_mix = _rmsnorm_bwd("mix_dnorm", dn2, h1, mix_norm, dh2, tm)
    dw_in_ref = jnp.concatenate([dw_proj[:, :c_f], dw_proj[:, FCOL:FCOL + H], dw_proj[:, c_f:FCOL]], axis=1)
    dw_in = jnp.transpose(dw_in_ref.reshape(D, N_CHIPS, cs_in), (1, 0, 2))
    scatter_end("ffn2", dh1)
    mix_started = scatter_begin("mix", ["w_in", "w_branch_fox", "w_branch_sb", "w_out"], [dw_in, dwbf, dwbs, dwo])

    dh0, _, dg_ffn1 = _ffn_bwd(
        "ffn1", dh1, dh1_bf, saved1, ffn1_norm, wg1, wu1, wd1, tm, cs_ff, after=mix_started,
        on_weight_grads=lambda *dw: scatter_begin("ffn1", ["ffn1_w_gate", "ffn1_w_up", "ffn1_w_down"], list(dw)))
    grad_x = dh0[NM:L_real][None]
    scatter_end("mix", dh0)
    scatter_end("ffn1", dh0)
    big = list(totals)
    grads = dict(zip(big, _share_halves("grads_to_core_pair", [totals[k] for k in big])))

    small = [loss_part[:, :1].reshape(1), dh0[:NM].reshape(-1), dg_ffn1.reshape(-1), dg_mix.reshape(-1),
             dg_ffn2.reshape(-1), dbias[:, 0, 0], dgq.reshape(-1), dgk.reshape(-1)]
    sizes = [s.shape[0] for s in small]
    flat = jnp.concatenate(small)
    rows = _round_up(-(-flat.shape[0] // LANES), SUBLANES)
    packed = jnp.pad(flat, (0, rows * LANES - flat.shape[0])).reshape(rows, LANES)
    total = _sum_devices("sum_small", _gather_small("gather_small", packed)).reshape(-1)
    pieces, off = [], 0
    for n_el in sizes:
        pieces.append(total[off:off + n_el])
        off += n_el
    loss = pieces[0][0]
    d_meta = lax.dynamic_slice_in_dim(pieces[1].reshape(NM, D), pos[1] * cs_d, cs_d, axis=1)
    grads.update(meta_tokens=d_meta, ffn1_norm=pieces[2].reshape(1, D), mix_norm=pieces[3].reshape(1, D),
                 ffn2_norm=pieces[4].reshape(1, D), b_forget=pieces[5].reshape(1, H),
                 fox_q_norm=pieces[6].reshape(1, H, HEAD_DIM), fox_k_norm=pieces[7].reshape(1, H, HEAD_DIM))

    out_g, out_d, out_m, out_v = [], [], [], []
    for k in names:
        w = weights[k]
        shape2 = (1, w.size) if w.size < LANES * SUBLANES else (w.size // w.shape[-1], w.shape[-1])
        g2 = grads[k].reshape(shape2)
        d, nm, nv = _adamw(f"adamw_{k}", w.reshape(shape2), g2, moments_m[k].reshape(shape2),
                           moments_v[k].reshape(shape2))
        out_g.append(g2.reshape(w.shape))
        out_d.append(d.reshape(w.shape))
        out_m.append(nm.reshape(w.shape))
        out_v.append(nv.reshape(w.shape))
    return (loss, grad_x, *out_g, *out_d, *out_m, *out_v)
```

```python
import functools

import jax
import jax.numpy as jnp
from jax import lax
from jax.experimental import pallas as pl
from jax.experimental.pallas import tpu as pltpu

F32 = jnp.float32
BF = jnp.bfloat16
I32 = jnp.int32

HEAD_DIM = 128
RMS_EPS = 1e-6
FFN_RESIDUAL_WEIGHT = 0.5
ADAM_LR = 0.001
ADAM_B1 = 0.9
ADAM_B2 = 0.999
ADAM_EPS = 1e-08
ADAM_WD = 0.01
ADAM_STEP = 10

LANES = 128
SUBLANES = 8
ATT_TILE = 128
ATT_HEADS = 4
ATT_HEADS_SB_FWD = 8
VMEM_CAP = 56 * 1024 * 1024
MESH_IDS = pl.DeviceIdType.MESH
N_CHIPS = 4
N_DEV = 8


def _pick(n, cands):
    for c in cands:
        if c <= n and n % c == 0:
            return c
    raise ValueError(f"no tile for {n} among {cands}")


def _round_up(n, m):
    return (n + m - 1) // m * m


def _tile_bytes(shape, dtype):
    item = jnp.dtype(dtype).itemsize
    dims = [d for d in shape if d is not None]
    if not dims:
        return 4 * LANES * SUBLANES
    last = _round_up(dims[-1], LANES)
    sub = _round_up(dims[-2], SUBLANES * (4 // item)) if len(dims) > 1 else 1
    lead = 1
    for d in dims[:-2]:
        lead *= d
    return lead * sub * last * item


def _vmem_limit(blocks, scratch=(), temps=0):
    need = 2 * sum(_tile_bytes(s, d) for s, d in blocks) + sum(_tile_bytes(s, d) for s, d in scratch) + temps
    return int(min(VMEM_CAP, max(need + (4 << 20), 16 << 20)))


def _call(body, *, name, out_shape, grid=(), in_specs=None, out_specs=None, scratch=(), sem=None, vmem=None,
          aliases=None, prefetch=0):
    params = pltpu.CompilerParams(dimension_semantics=sem, vmem_limit_bytes=vmem)
    if prefetch:
        grid_spec = pltpu.PrefetchScalarGridSpec(num_scalar_prefetch=prefetch, grid=grid, in_specs=in_specs,
                                                 out_specs=out_specs, scratch_shapes=scratch)
        return pl.pallas_call(body, out_shape=out_shape, grid_spec=grid_spec, name=name, compiler_params=params,
                              input_output_aliases=aliases or {})
    return pl.pallas_call(body, out_shape=out_shape, grid=grid, in_specs=in_specs, out_specs=out_specs,
                          scratch_shapes=scratch, name=name, compiler_params=params,
                          input_output_aliases=aliases or {})


def _dot(a, b, ca, cb):
    return lax.dot_general(a, b, (((ca,), (cb,)), ((), ())), preferred_element_type=F32)


def _sigmoid(x):
    return 1.0 / (1.0 + jnp.exp(-x))


def _log_sigmoid(x):
    return jnp.minimum(x, 0.0) - jnp.log1p(jnp.exp(-jnp.abs(x)))


def _split(x, parts):
    pieces = []
    rem = x
    for p in range(parts):
        piece = rem.astype(BF)
        pieces.append(piece)
        if p + 1 < parts:
            rem = rem - piece.astype(F32)
    return pieces


def _pieces_dot(pieces, ones_bf):
    out = None
    for piece in pieces:
        d = _dot(piece, ones_bf, 1, 0)
        out = d if out is None else out + d
    return out


def _split_dot(x, ones_bf, parts):
    return _pieces_dot(_split(x, parts), ones_bf)


def _matmul(name, a_list, b_list, pairs, n_acc, epi, outs, *, M, N, K, tm, tn, tk, ta=False, tb=False,
            extras=(), n_outer=False, b_off=0, after=None):
    if after is not None:
        user_epi = epi
        extras = list(extras) + [(after, (SUBLANES, LANES), lambda i, j: (0, 0))]
        epi = lambda accs, ex: user_epi(accs, ex[:-1])
    gi, gj, nk = M // tm, N // tn, K // tk
    assert gi * tm == M and gj * tn == N and nk * tk == K, (name, M, N, K, tm, tn, tk)
    n_a, n_b, n_e, n_o = len(a_list), len(b_list), len(extras), len(outs)

    def ij(g0, g1):
        return (g1, g0) if n_outer else (g0, g1)

    def a_map(g0, g1, k):
        i, _ = ij(g0, g1)
        return (k, i) if ta else (i, k)

    def b_map(g0, g1, k):
        _, j = ij(g0, g1)
        return (j + b_off, k) if tb else (k, j + b_off)

    def tile_map(fn):
        return lambda g0, g1, k: fn(*ij(g0, g1))

    a_block = (tk, tm) if ta else (tm, tk)
    b_block = (tn, tk) if tb else (tk, tn)
    in_specs = ([pl.BlockSpec(a_block, a_map)] * n_a + [pl.BlockSpec(b_block, b_map)] * n_b
                + [pl.BlockSpec(bs, tile_map(fn)) for _, bs, fn in extras])
    out_specs = [pl.BlockSpec(bs, tile_map(fn)) for _, _, bs, fn in outs]
    out_shape = [jax.ShapeDtypeStruct(s, d) for s, d, _, _ in outs]
    scratch = [pltpu.VMEM((tm, tn), F32) for _ in range(n_acc)] if nk > 1 else []

    def body(*refs):
        a_refs = refs[:n_a]
        b_refs = refs[n_a:n_a + n_b]
        e_refs = refs[n_a + n_b:n_a + n_b + n_e]
        o_refs = refs[n_a + n_b + n_e:n_a + n_b + n_e + n_o]
        acc_refs = refs[n_a + n_b + n_e + n_o:]

        def products():
            accs = [None] * n_acc
            for ai, bi, ci in pairs:
                a = a_refs[ai][...]
                b = b_refs[bi][...]
                d = _dot(a.astype(BF), b.astype(BF), 0 if ta else 1, 1 if tb else 0)
                accs[ci] = d if accs[ci] is None else accs[ci] + d
            return accs

        def finish(accs):
            res = epi(accs, [e[...] for e in e_refs])
            for o_ref, r in zip(o_refs, res):
                o_ref[...] = r.reshape(o_ref.shape).astype(o_ref.dtype)

        if nk == 1:
            finish(products())
        else:
            k = pl.program_id(2)

            @pl.when(k == 0)
            def _():
                for acc in acc_refs:
                    acc[...] = jnp.zeros_like(acc)

            for acc, d in zip(acc_refs, products()):
                acc[...] += d

            @pl.when(k == nk - 1)
            def _():
                finish([acc[...] for acc in acc_refs])

    blocks = ([(a_block, a.dtype) for a in a_list] + [(b_block, b.dtype) for b in b_list]
              + [(bs, e.dtype) for e, bs, _ in extras] + [(bs, d) for _, d, bs, _ in outs])
    vmem = _vmem_limit(blocks, [((tm, tn), F32)] * (n_acc if nk > 1 else 0), temps=6 * tm * tn * 4)
    grid = (gj, gi, nk) if n_outer else (gi, gj, nk)
    fn = _call(body, name=name, out_shape=out_shape, grid=grid, in_specs=in_specs, out_specs=out_specs,
               scratch=scratch, sem=("parallel", "parallel", "arbitrary"), vmem=vmem)
    return fn(*a_list, *b_list, *[e for e, _, _ in extras])


def _mn(tm, tn, col0=0):
    assert col0 % tn == 0
    off = col0 // tn
    return (tm, tn), (lambda i, j: (i, j + off))


def _rmsnorm_fwd(name, x, gain, tr):
    L, D = x.shape

    def body(x_ref, g_ref, o_ref):
        xv = x_ref[...]
        r = lax.rsqrt(jnp.mean(xv * xv, axis=-1, keepdims=True) + RMS_EPS)
        o_ref[...] = (xv * r * g_ref[...]).astype(BF)

    row = pl.BlockSpec((tr, D), lambda i: (i, 0))
    vec = pl.BlockSpec((1, D), lambda i: (0, 0))
    vmem = _vmem_limit([((tr, D), F32), ((tr, D), BF)], temps=3 * tr * D * 4)
    return _call(body, name=name, out_shape=jax.ShapeDtypeStruct((L, D), BF), grid=(L // tr,), in_specs=[row, vec],
                 out_specs=row, sem=("parallel",), vmem=vmem)(x, gain)


def _rmsnorm_bwd(name, dn, x, gain, dres, tr):
    L, D = x.shape
    steps = L // tr

    def body(dn_ref, x_ref, g_ref, dres_ref, dx_ref, dxb_ref, dg_ref):
        i = pl.program_id(0)
        xv = x_ref[...]
        r = lax.rsqrt(jnp.mean(xv * xv, axis=-1, keepdims=True) + RMS_EPS)
        xhat = xv * r
        dy = dn_ref[...]
        dxhat = dy * g_ref[...]
        dx = dres_ref[...] + r * (dxhat - xhat * jnp.mean(dxhat * xhat, axis=-1, keepdims=True))
        dx_ref[...] = dx
        dxb_ref[...] = dx.astype(BF)

        @pl.when(i == 0)
        def _():
            dg_ref[...] = jnp.zeros_like(dg_ref)

        dg_ref[...] += jnp.sum(dy * xhat, axis=0, keepdims=True)

    row = pl.BlockSpec((tr, D), lambda i: (i, 0))
    vec = pl.BlockSpec((1, D), lambda i: (0, 0))
    vmem = _vmem_limit([((tr, D), F32)] * 4 + [((tr, D), BF)], temps=4 * tr * D * 4)
    out_shape = [jax.ShapeDtypeStruct((L, D), F32), jax.ShapeDtypeStruct((L, D), BF),
                 jax.ShapeDtypeStruct((1, D), F32)]
    return _call(body, name=name, out_shape=out_shape, grid=(steps,), in_specs=[row, row, vec, row],
                 out_specs=[row, row, vec], sem=("arbitrary",), vmem=vmem)(dn, x, gain, dres)


def _loss_grad(name, h, target, n_meta, n_seq, tr):
    L, D = h.shape

    def body(h_ref, t_ref, dh_ref, dhb_ref, loss_ref):
        i = pl.program_id(0)
        rows = i * tr + lax.broadcasted_iota(I32, (tr, 1), 0)
        valid = (rows >= n_meta) & (rows < n_meta + n_seq)
        diff = jnp.where(valid, h_ref[...] - t_ref[...], 0.0)
        dh = diff * (1.0 / D)
        dh_ref[...] = dh
        dhb_ref[...] = dh.astype(BF)

        @pl.when(i == 0)
        def _():
            loss_ref[...] = jnp.zeros_like(loss_ref)

        loss_ref[...] += jnp.sum(diff * diff) * (0.5 / D)

    row = pl.BlockSpec((tr, D), lambda i: (i, 0))
    acc = pl.BlockSpec((1, LANES), lambda i: (0, 0))
    vmem = _vmem_limit([((tr, D), F32)] * 3 + [((tr, D), BF)], temps=3 * tr * D * 4)
    out_shape = [jax.ShapeDtypeStruct((L, D), F32), jax.ShapeDtypeStruct((L, D), BF),
                 jax.ShapeDtypeStruct((1, LANES), F32)]
    return _call(body, name=name, out_shape=out_shape, grid=(L // tr,), in_specs=[row, row],
                 out_specs=[row, row, acc], sem=("arbitrary",), vmem=vmem)(h, target)


def _qknorm_fwd(name, proj, gq, gk, heads, q_col, k_col, tr):
    L = proj.shape[0]

    def body(q_ref, k_ref, gq_ref, gk_ref, qn_ref, kn_ref):
        for x_ref, g_ref, o_ref in ((q_ref, gq_ref, qn_ref), (k_ref, gk_ref, kn_ref)):
            xv = x_ref[...].astype(F32)
            r = lax.rsqrt(jnp.mean(xv * xv, axis=-1, keepdims=True) + RMS_EPS)
            o_ref[...] = (xv * r * g_ref[...]).astype(BF)

    qb, kb = q_col // HEAD_DIM, k_col // HEAD_DIM
    in_specs = [pl.BlockSpec((tr, HEAD_DIM), lambda h, i: (i, qb + h)),
                pl.BlockSpec((tr, HEAD_DIM), lambda h, i: (i, kb + h)),
                pl.BlockSpec((None, 1, HEAD_DIM), lambda h, i: (h, 0, 0)),
                pl.BlockSpec((None, 1, HEAD_DIM), lambda h, i: (h, 0, 0))]
    out = pl.BlockSpec((tr, HEAD_DIM), lambda h, i: (i, h))
    out_shape = [jax.ShapeDtypeStruct((L, heads * HEAD_DIM), BF)] * 2
    return _call(body, name=name, out_shape=out_shape, grid=(heads, L // tr), in_specs=in_specs,
                 out_specs=[out, out], sem=("parallel", "parallel"), vmem=16 << 20)(proj, proj, gq, gk)


def _qknorm_bwd(name, proj, dqn, dkn, gq, gk, heads, q_col, k_col, tr):
    L = proj.shape[0]

    def body(q_ref, k_ref, dqn_ref, dkn_ref, gq_ref, gk_ref, dq_ref, dk_ref, dgq_ref, dgk_ref):
        i = pl.program_id(1)
        for x_ref, dy_ref, g_ref, dx_ref, dg_ref in ((q_ref, dqn_ref, gq_ref, dq_ref, dgq_ref),
                                                     (k_ref, dkn_ref, gk_ref, dk_ref, dgk_ref)):
            xv = x_ref[...].astype(F32)
            r = lax.rsqrt(jnp.mean(xv * xv, axis=-1, keepdims=True) + RMS_EPS)
            xhat = xv * r
            dy = dy_ref[...].astype(F32)
            dxhat = dy * g_ref[...]
            dx_ref[...] = (r * (dxhat - xhat * jnp.mean(dxhat * xhat, axis=-1, keepdims=True))).astype(BF)

            @pl.when(i == 0)
            def _():
                dg_ref[...] = jnp.zeros_like(dg_ref)

            dg_ref[...] += jnp.sum(dy * xhat, axis=0, keepdims=True)

    qb, kb = q_col // HEAD_DIM, k_col // HEAD_DIM
    tile = pl.BlockSpec((tr, HEAD_DIM), lambda h, i: (i, h))
    gain = pl.BlockSpec((None, 1, HEAD_DIM), lambda h, i: (h, 0, 0))
    in_specs = [pl.BlockSpec((tr, HEAD_DIM), lambda h, i: (i, qb + h)),
                pl.BlockSpec((tr, HEAD_DIM), lambda h, i: (i, kb + h)), tile, tile, gain, gain]
    out_shape = [jax.ShapeDtypeStruct((L, heads * HEAD_DIM), BF)] * 2 + [
        jax.ShapeDtypeStruct((heads, 1, HEAD_DIM), F32)] * 2
    return _call(body, name=name, out_shape=out_shape, grid=(heads, L // tr), in_specs=in_specs,
                 out_specs=[tile, tile, gain, gain], sem=("parallel", "arbitrary"),
                 vmem=16 << 20)(proj, proj, dqn, dkn, gq, gk)


def _tri(cmp):
    r = lax.broadcasted_iota(I32, (LANES, LANES), 0)
    c = lax.broadcasted_iota(I32, (LANES, LANES), 1)
    return jnp.where(cmp(r, c), 1.0, 0.0).astype(BF)


def _cum_fwd(name, fl, bias, n_rows):
    H, nbp, _ = fl.shape

    def body(fl_ref, b_ref, c_ref, tot_ref):
        lf = _log_sigmoid(fl_ref[...] + b_ref[...])
        c_ref[...] = _split_dot(lf, _tri(lambda r, c: r <= c), 3)
        tot_ref[...] = _split_dot(lf, jnp.ones((LANES, LANES), BF), 3)

        def step(r, carry):
            c_ref[pl.ds(r, 1), :] = c_ref[pl.ds(r, 1), :] + carry
            return carry + tot_ref[pl.ds(r, 1), :]

        lax.fori_loop(0, n_rows, step, jnp.zeros((1, LANES), F32))

    blk = pl.BlockSpec((None, nbp, LANES), lambda h: (h, 0, 0))
    vec = pl.BlockSpec((None, 1, LANES), lambda h: (h, 0, 0))
    return _call(body, name=name, out_shape=jax.ShapeDtypeStruct((H, nbp, LANES), F32), grid=(H,),
                 in_specs=[blk, vec], out_specs=blk, scratch=[pltpu.VMEM((nbp, LANES), F32)], sem=("parallel",),
                 vmem=16 << 20)(fl, bias)


def _cum_bwd(name, drs, dcs, fl, bias, n_rows):
    H, nbp, _ = fl.shape

    def body(drs_ref, dcs_ref, fl_ref, b_ref, dfl_ref, db_ref, rin_ref, tot_ref):
        dc = drs_ref[...] - dcs_ref[...]
        rin_ref[...] = _split_dot(dc, _tri(lambda r, c: r >= c), 3)
        tot_ref[...] = _split_dot(dc, jnp.ones((LANES, LANES), BF), 3)
        dfl_ref[...] = jnp.zeros_like(dfl_ref)

        def step(t, carry):
            r = n_rows - 1 - t
            x = fl_ref[pl.ds(r, 1), :] + b_ref[...]
            dfl_ref[pl.ds(r, 1), :] = (rin_ref[pl.ds(r, 1), :] + carry) * _sigmoid(-x)
            return carry + tot_ref[pl.ds(r, 1), :]

        lax.fori_loop(0, n_rows, step, jnp.zeros((1, LANES), F32))
        db_ref[...] = jnp.zeros_like(db_ref) + jnp.sum(dfl_ref[...])

    blk = pl.BlockSpec((None, nbp, LANES), lambda h: (h, 0, 0))
    vec = pl.BlockSpec((None, 1, LANES), lambda h: (h, 0, 0))
    out_shape = [jax.ShapeDtypeStruct((H, nbp, LANES), F32), jax.ShapeDtypeStruct((H, 1, LANES), F32)]
    return _call(body, name=name, out_shape=out_shape, grid=(H,), in_specs=[blk, blk, blk, vec],
                 out_specs=[blk, vec], scratch=[pltpu.VMEM((nbp, LANES), F32)] * 2, sem=("parallel",),
                 vmem=16 << 20)(drs, dcs, fl, bias)


def _att_specs(L, G, q_col, k_col, v_col):
    T = ATT_TILE
    W = G * HEAD_DIM
    assert q_col % W == 0 and k_col % W == 0 and v_col % W == 0
    qb, kb, vb = q_col // W, k_col // W, v_col // W
    q_spec = pl.BlockSpec((T, W), lambda h, i: (i, qb + h))
    k_spec = pl.BlockSpec((L, W), lambda h, i: (0, kb + h), pipeline_mode=pl.Buffered(1))
    v_spec = pl.BlockSpec((L, W), lambda h, i: (0, vb + h), pipeline_mode=pl.Buffered(1))
    return q_spec, k_spec, v_spec


def _head_lanes(G):
    return [slice(g * HEAD_DIM, (g + 1) * HEAD_DIM) for g in range(G)]


def _tile_iotas():
    T = ATT_TILE
    return lax.broadcasted_iota(I32, (T, T), 0), lax.broadcasted_iota(I32, (T, T), 1)


def _rows(j):
    return pl.ds(pl.multiple_of(j * ATT_TILE, ATT_TILE), ATT_TILE)


def _fox_fwd(name, q_arr, k_arr, v_arr, c_row, c_col, heads, G, q_col, k_col, v_col):
    L = q_arr.shape[0]
    T = ATT_TILE
    scale = HEAD_DIM ** -0.5
    lanes = _head_lanes(G)

    def body(q_ref, k_ref, v_ref, crow_ref, ccol_ref, o_ref, lse_ref):
        i = pl.program_id(1)
        qs = [q_ref[:, hl] for hl in lanes]
        cts = [jnp.broadcast_to(ccol_ref[g], (T, T)) for g in range(G)]
        row, col = _tile_iotas()

        def tile(j, carry, masked):
            qk = [_dot(qs[g], k_ref[_rows(j), hl], 1, 1) for g, hl in enumerate(lanes)]
            stats = []
            for g in range(G):
                m, l, _ = carry[g]
                s = qk[g] * scale + (cts[g] - crow_ref[g, pl.ds(j, 1), :])
                if masked:
                    s = jnp.where(col <= row, s, -jnp.inf)
                m_new = jnp.maximum(m, jnp.max(s, axis=1, keepdims=True))
                alpha = jnp.exp(m - m_new)
                p = jnp.exp(s - m_new)
                stats.append((m_new, alpha, alpha * l + jnp.sum(p, axis=1, keepdims=True), p.astype(BF)))
            pv = [_dot(stats[g][3], v_ref[_rows(j), hl], 1, 0) for g, hl in enumerate(lanes)]
            return tuple((stats[g][0], stats[g][2], stats[g][1] * carry[g][2] + pv[g]) for g in range(G))

        init = tuple((jnp.full((T, 1), -1e30, F32), jnp.zeros((T, 1), F32), jnp.zeros((T, HEAD_DIM), F32))
                     for _ in range(G))
        carry = lax.fori_loop(0, i, lambda j, c: tile(j, c, False), init)
        for g, (m, l, acc) in enumerate(tile(i, carry, True)):
            o_ref[:, lanes[g]] = (acc / l).astype(o_ref.dtype)
            lse_ref[g] = m + jnp.log(l)

    nbp = c_row.shape[1]
    W = G * HEAD_DIM
    q_spec, k_spec, v_spec = _att_specs(L, G, q_col, k_col, v_col)
    crow_spec = pl.BlockSpec((G, nbp, LANES), lambda h, i: (h, 0, 0))
    col_spec = pl.BlockSpec((G, T, 1), lambda h, i: (h, i, 0))
    o_spec = pl.BlockSpec((T, W), lambda h, i: (i, h))
    out_shape = [jax.ShapeDtypeStruct((L, heads * HEAD_DIM), BF), jax.ShapeDtypeStruct((heads, L, 1), F32)]
    vmem = _vmem_limit([((L, W), BF)] * 2, temps=8 << 20)
    return _call(body, name=name, out_shape=out_shape, grid=(heads // G, L // T),
                 in_specs=[q_spec, k_spec, v_spec, crow_spec, col_spec], out_specs=[o_spec, col_spec],
                 sem=("parallel", "parallel"), vmem=vmem)(q_arr, k_arr, v_arr, c_row, c_col)


def _fox_bwd(name, q_arr, k_arr, v_arr, c_row, c_col, o, do, lse, heads, G, q_col, k_col, v_col):
    L = q_arr.shape[0]
    T = ATT_TILE
    nq = L // T
    scale = HEAD_DIM ** -0.5
    lanes = _head_lanes(G)

    def body(q_ref, k_ref, v_ref, crow_ref, ccol_ref, o_ref, do_ref, lse_ref, dq_ref, dk_ref, dv_ref, dcs_ref,
             drs_ref, dk_acc, dv_acc):
        i = pl.program_id(1)

        @pl.when(i == 0)
        def _():
            dk_acc[...] = jnp.zeros_like(dk_acc)
            dv_acc[...] = jnp.zeros_like(dv_acc)
            dcs_ref[...] = jnp.zeros_like(dcs_ref)

        qs = [q_ref[:, hl] for hl in lanes]
        dos = [do_ref[:, hl] for hl in lanes]
        q_ts = [qs[g].T for g in range(G)]
        do_ts = [dos[g].T for g in range(G)]
        deltas = [jnp.broadcast_to(jnp.sum(dos[g].astype(F32) * o_ref[:, hl].astype(F32), axis=1, keepdims=True),
                                   (T, T)) for g, hl in enumerate(lanes)]
        lses = [jnp.broadcast_to(lse_ref[g], (T, T)) for g in range(G)]
        cts = [jnp.broadcast_to(ccol_ref[g], (T, T)) for g in range(G)]
        row, col = _tile_iotas()

        def tile(j, carry, masked):
            ks = [k_ref[_rows(j), hl] for hl in lanes]
            qk = [_dot(qs[g], ks[g], 1, 1) for g in range(G)]
            dp = [_dot(dos[g], v_ref[_rows(j), hl], 1, 1) for g, hl in enumerate(lanes)]
            pbs, dsbs, row_sums = [], [], []
            for g in range(G):
                s = qk[g] * scale + (cts[g] - crow_ref[g, pl.ds(j, 1), :])
                if masked:
                    s = jnp.where(col <= row, s, -jnp.inf)
                p = jnp.exp(s - lses[g])
                ds = p * (dp[g] - deltas[g])
                dcs_ref[g, pl.ds(j, 1), :] += jnp.sum(ds, axis=0, keepdims=True)
                row_sums.append(carry[g][1] + jnp.sum(ds, axis=1, keepdims=True))
                pbs.append(p.astype(BF))
                dsbs.append((ds * scale).astype(BF))
            for g, hl in enumerate(lanes):
                dk_acc[j, hl, :] += _dot(q_ts[g], dsbs[g], 1, 0)
            for g, hl in enumerate(lanes):
                dv_acc[j, hl, :] += _dot(do_ts[g], pbs[g], 1, 0)
            return tuple((carry[g][0] + _dot(dsbs[g], ks[g], 1, 0), row_sums[g]) for g in range(G))

        init = tuple((jnp.zeros((T, HEAD_DIM), F32), jnp.zeros((T, 1), F32)) for _ in range(G))
        carry = lax.fori_loop(0, i, lambda j, c: tile(j, c, False), init)
        for g, (dq, row_sum) in enumerate(tile(i, carry, True)):
            dq_ref[:, lanes[g]] = dq.astype(dq_ref.dtype)
            drs_ref[g] = row_sum

        @pl.when(i == nq - 1)
        def _():
            for r in range(nq):
                for hl in lanes:
                    dk_ref[r * T:(r + 1) * T, hl] = dk_acc[r, hl, :].T.astype(dk_ref.dtype)
                    dv_ref[r * T:(r + 1) * T, hl] = dv_acc[r, hl, :].T.astype(dv_ref.dtype)

    nbp = c_row.shape[1]
    WG = G * HEAD_DIM
    q_spec, k_spec, v_spec = _att_specs(L, G, q_col, k_col, v_col)
    crow_spec = pl.BlockSpec((G, nbp, LANES), lambda h, i: (h, 0, 0))
    col_spec = pl.BlockSpec((G, T, 1), lambda h, i: (h, i, 0))
    t_spec = pl.BlockSpec((T, WG), lambda h, i: (i, h))
    head_spec = pl.BlockSpec((L, WG), lambda h, i: (0, h), pipeline_mode=pl.Buffered(1))
    W = heads * HEAD_DIM
    out_shape = [jax.ShapeDtypeStruct((L, W), F32)] * 3 + [jax.ShapeDtypeStruct((heads, nbp, LANES), F32),
                                                           jax.ShapeDtypeStruct((heads, L, 1), F32)]
    scratch = [pltpu.VMEM((nq, WG, T), F32)] * 2
    vmem = _vmem_limit([], [((L, WG), BF)] * 2 + [((L, WG), F32)] * 4, temps=8 << 20)
    return _call(body, name=name, out_shape=out_shape, grid=(heads // G, nq),
                 in_specs=[q_spec, k_spec, v_spec, crow_spec, col_spec, t_spec, t_spec, col_spec],
                 out_specs=[t_spec, head_spec, head_spec, crow_spec, col_spec], scratch=scratch,
                 sem=("parallel", "arbitrary"), vmem=vmem)(q_arr, k_arr, v_arr, c_row, c_col, o, do, lse)


def _sb_logits(qk, scale, valid):
    z = qk * scale
    lb = jnp.minimum(z, 0.0) - jnp.log1p(jnp.exp(-jnp.abs(z)))
    lom = lb - z
    if valid is not None:
        lom = jnp.where(valid, lom, 0.0)
    return lb, lom


def _sb_fwd(name, proj, heads, G, q_col, k_col, v_col):
    L = proj.shape[0]
    T = ATT_TILE
    scale = HEAD_DIM ** -0.5
    lanes = _head_lanes(G)

    def body(q_ref, k_ref, v_ref, o_ref):
        i = pl.program_id(1)
        qs = [q_ref[:, hl] for hl in lanes]
        row, col = _tile_iotas()
        later_mat = jnp.where(row > col, 1.0, 0.0).astype(BF)

        def tile(j, carry, masked):
            valid = (col < row) if masked else None
            qk = [_dot(qs[g], k_ref[_rows(j), hl], 1, 1) for g, hl in enumerate(lanes)]
            logits = [_sb_logits(qk[g], scale, valid) for g in range(G)]
            pieces = [_split(lom, 2) for _, lom in logits]
            later = [_pieces_dot(pieces[g], later_mat) for g in range(G)]
            ws = []
            for g in range(G):
                w = jnp.exp(logits[g][0] + later[g] + carry[g][0])
                if masked:
                    w = jnp.where(valid, w, 0.0)
                ws.append(w.astype(BF))
            wv = [_dot(ws[g], v_ref[_rows(j), hl], 1, 0) for g, hl in enumerate(lanes)]
            return tuple((carry[g][0] + jnp.sum(logits[g][1], axis=1, keepdims=True), carry[g][1] + wv[g])
                         for g in range(G))

        init = tuple((jnp.zeros((T, 1), F32), jnp.zeros((T, HEAD_DIM), F32)) for _ in range(G))
        carry = tile(i, init, True)
        carry = lax.fori_loop(0, i, lambda t, c: tile(i - 1 - t, c, False), carry)
        for g, (_, acc) in enumerate(carry):
            o_ref[:, lanes[g]] = acc.astype(o_ref.dtype)

    W = G * HEAD_DIM
    q_spec, k_spec, v_spec = _att_specs(L, G, q_col, k_col, v_col)
    o_spec = pl.BlockSpec((T, W), lambda h, i: (i, h))
    vmem = _vmem_limit([((L, W), BF)] * 2, temps=8 << 20)
    return _call(body, name=name, out_shape=jax.ShapeDtypeStruct((L, heads * HEAD_DIM), BF),
                 grid=(heads // G, L // T), in_specs=[q_spec, k_spec, v_spec], out_specs=o_spec,
                 sem=("parallel", "parallel"), vmem=vmem)(proj, proj, proj)


def _sb_bwd(name, proj, do, heads, G, q_col, k_col, v_col):
    L = proj.shape[0]
    T = ATT_TILE
    nq = L // T
    scale = HEAD_DIM ** -0.5
    lanes = _head_lanes(G)

    def body(q_ref, k_ref, v_ref, do_ref, dq_ref, dk_acc, dv_acc, da_buf, beta_buf):
        i = pl.program_id(1)

        @pl.when(i == 0)
        def _():
            dk_acc[...] = jnp.zeros_like(dk_acc)
            dv_acc[...] = jnp.zeros_like(dv_acc)

        qs = [q_ref[:, hl] for hl in lanes]
        dos = [do_ref[:, hl] for hl in lanes]
        q_ts = [qs[g].T for g in range(G)]
        do_ts = [dos[g].T for g in range(G)]
        row, col = _tile_iotas()
        later_mat = jnp.where(row > col, 1.0, 0.0).astype(BF)
        before_mat = jnp.where(row < col, 1.0, 0.0).astype(BF)

        def pass1(j, runs, masked):
            valid = (col < row) if masked else None
            qk = [_dot(qs[g], k_ref[_rows(j), hl], 1, 1) for g, hl in enumerate(lanes)]
            dw = [_dot(dos[g], v_ref[_rows(j), hl], 1, 1) for g, hl in enumerate(lanes)]
            logits = [_sb_logits(qk[g], scale, valid) for g in range(G)]
            pieces = [_split(lom, 2) for _, lom in logits]
            later = [_pieces_dot(pieces[g], later_mat) for g in range(G)]
            ws = []
            for g in range(G):
                w = jnp.exp(logits[g][0] + later[g] + runs[g])
                if masked:
                    w = jnp.where(valid, w, 0.0)
                da_buf[g * nq + j] = dw[g] * w
                beta_buf[g * nq + j] = jnp.exp(logits[g][0])
                ws.append(w.astype(BF))
            for g, hl in enumerate(lanes):
                dv_acc[j, hl, :] += _dot(do_ts[g], ws[g], 1, 0)
            return tuple(runs[g] + jnp.sum(logits[g][1], axis=1, keepdims=True) for g in range(G))

        runs = pass1(i, tuple(jnp.zeros((T, 1), F32) for _ in range(G)), True)
        lax.fori_loop(0, i, lambda t, c: pass1(i - 1 - t, c, False), runs)

        def pass2(j, carry, masked):
            das = [da_buf[g * nq + j] for g in range(G)]
            pieces = [_split(da, 2) for da in das]
            before = [_pieces_dot(pieces[g], before_mat) for g in range(G)]
            dzbs = []
            for g in range(G):
                beta = beta_buf[g * nq + j]
                dz = das[g] * (1.0 - beta) - (carry[g][0] + before[g]) * beta
                if masked:
                    dz = jnp.where(col < row, dz, 0.0)
                dzbs.append((dz * scale).astype(BF))
            for g, hl in enumerate(lanes):
                dk_acc[j, hl, :] += _dot(q_ts[g], dzbs[g], 1, 0)
            dq = [_dot(dzbs[g], k_ref[_rows(j), hl], 1, 0) for g, hl in enumerate(lanes)]
            return tuple((carry[g][0] + jnp.sum(das[g], axis=1, keepdims=True), carry[g][1] + dq[g])
                         for g in range(G))

        init = tuple((jnp.zeros((T, 1), F32), jnp.zeros((T, HEAD_DIM), F32)) for _ in range(G))
        carry = lax.fori_loop(0, i, lambda j, c: pass2(j, c, False), init)
        for g, (_, dq) in enumerate(pass2(i, carry, True)):
            dq_ref[:, lanes[g]] = dq.astype(dq_ref.dtype)

    WG = G * HEAD_DIM
    q_spec, k_spec, v_spec = _att_specs(L, G, q_col, k_col, v_col)
    t_spec = pl.BlockSpec((T, WG), lambda h, i: (i, h))
    head_spec = pl.BlockSpec((nq, WG, T), lambda h, i: (0, h, 0), pipeline_mode=pl.Buffered(1))
    W = heads * HEAD_DIM
    out_shape = [jax.ShapeDtypeStruct((L, W), BF)] + [jax.ShapeDtypeStruct((nq, W, T), F32)] * 2
    scratch = [pltpu.VMEM((G * nq, T, T), F32)] * 2
    vmem = _vmem_limit([], [((L, WG), BF)] * 2 + [((L, WG), F32)] * 2 + [((G * nq, T, T), F32)] * 2,
                       temps=6 << 20)
    return _call(body, name=name, out_shape=out_shape, grid=(heads // G, nq),
                 in_specs=[q_spec, k_spec, v_spec, t_spec], out_specs=[t_spec, head_spec, head_spec],
                 scratch=scratch, sem=("parallel", "arbitrary"), vmem=vmem)(proj, proj, proj, do)


_ANY = pl.BlockSpec(memory_space=pl.ANY)


def _mesh_pos():
    return lax.axis_index("x"), lax.axis_index("y"), lax.axis_index("c")


def _other_chips(x, y):
    return [(1 - x, y), (x, 1 - y), (1 - x, 1 - y)]


def _shard_window(ref, kind, sidx, r0, nr, cs):
    if kind == "col":
        assert cs % LANES == 0
        return ref.at[pl.ds(r0, nr), pl.ds(pl.multiple_of(sidx * cs, LANES), cs)]
    return ref.at[sidx, pl.ds(r0, nr), :]


def _place_shard(name, pos, shard, kind, dtype):
    R, C = shard.shape
    tr = _row_tile(R, C, 2, 16)

    def body(pos_ref, s_ref, o_ref):
        o_ref[...] = s_ref[...].astype(o_ref.dtype)

    if kind == "col":
        assert C % LANES == 0
        out_shape = jax.ShapeDtypeStruct((R, N_CHIPS * C), dtype)
        out_spec = pl.BlockSpec((tr, C), lambda r, pos_ref: (r, pos_ref[1]))
    else:
        out_shape = jax.ShapeDtypeStruct((N_CHIPS, R, C), dtype)
        out_spec = pl.BlockSpec((None, tr, C), lambda r, pos_ref: (pos_ref[1], r, 0))
    return _call(body, name=name, out_shape=out_shape, grid=(R // tr,),
                 in_specs=[pl.BlockSpec((tr, C), lambda r, pos_ref: (r, 0))], out_specs=out_spec, sem=("parallel",),
                 vmem=32 << 20, prefetch=1)(pos, shard)


_HBM = pl.BlockSpec(memory_space=pltpu.HBM)
_SEM = pl.BlockSpec(memory_space=pltpu.SEMAPHORE)
_KEEP_ORDER = pltpu.SideEffectType.DATAFLOW_SIDE_EFFECTING


def _in_hbm(x):
    return pltpu.with_memory_space_constraint(x, pltpu.HBM)


def _gather_copies(bufs, meta, send_sem, recv_sem):
    x, y, c = _mesh_pos()
    out = []
    for t, (kind, R, cs) in enumerate(meta):
        half = R // 2
        r0 = pl.multiple_of(c * half, SUBLANES)
        mine = _shard_window(bufs[t], kind, 2 * x + y, r0, half, cs)
        for p, (px, py) in enumerate(_other_chips(x, y)):
            k = 3 * t + p
            args = dict(send_sem=send_sem.at[k], recv_sem=recv_sem.at[k], device_id=(px, py, c),
                        device_id_type=MESH_IDS)
            out.append((pltpu.make_async_remote_copy(src_ref=mine, dst_ref=mine, **args),
                        pltpu.make_async_remote_copy(
                            src_ref=mine, dst_ref=_shard_window(bufs[t], kind, 2 * px + py, r0, half, cs), **args)))
    return out


def _all_gather_ici_start(name, gathered, shards, kinds, after):
    n = len(gathered)
    meta = [(kind, s.shape[0], s.shape[1]) for s, kind in zip(shards, kinds)]

    def body(*refs):
        send_sem, recv_sem = refs[n + 1], refs[n + 2]
        bufs = refs[n + 3:2 * n + 3]
        token = refs[2 * n + 3]
        for send, _ in _gather_copies(bufs, meta, send_sem, recv_sem):
            send.start()
        token[...] = jnp.zeros_like(token)

    out_shape = (pltpu.SemaphoreType.DMA((3 * n,)), pltpu.SemaphoreType.DMA((3 * n,)),
                 *[pltpu.HBM(g.shape, g.dtype) for g in gathered], jax.ShapeDtypeStruct((SUBLANES, LANES), F32))
    out_specs = (_SEM, _SEM, *[_HBM] * n, pl.BlockSpec(memory_space=pltpu.VMEM))
    res = pl.pallas_call(body, out_shape=out_shape, in_specs=[_HBM] * n + [_ANY], out_specs=out_specs,
                         input_output_aliases={t: 2 + t for t in range(n)}, name=name,
                         compiler_params=pltpu.CompilerParams(has_side_effects=_KEEP_ORDER))(
        *[_in_hbm(g) for g in gathered], after)
    return res[0], res[1], list(res[2:2 + n]), res[2 + n]


def _all_gather_ici_wait(name, bufs, send_sem, recv_sem, after, shards, kinds):
    n = len(bufs)
    meta = [(kind, s.shape[0], s.shape[1]) for s, kind in zip(shards, kinds)]

    def body(*refs):
        for send, recv in _gather_copies(refs[:n], meta, refs[n], refs[n + 1]):
            send.wait_send()
            recv.wait_recv()

    out_shape = tuple(pltpu.HBM(b.shape, b.dtype) for b in bufs)
    res = pl.pallas_call(body, out_shape=out_shape, in_specs=[_HBM] * n + [_SEM, _SEM, _ANY],
                         out_specs=tuple([_HBM] * n), input_output_aliases={t: t for t in range(n)}, name=name,
                         compiler_params=pltpu.CompilerParams(has_side_effects=_KEEP_ORDER))(
        *bufs, send_sem, recv_sem, after)
    return list(res)


def _all_gather_d2d(name, gathered, shards, kinds):
    n = len(gathered)
    meta = [(kind, s.shape[0], s.shape[1]) for s, kind in zip(shards, kinds)]

    def body(*refs):
        bufs = refs[n:2 * n]
        send_sem, recv_sem = refs[2 * n:]
        x, y, c = _mesh_pos()
        sends, recvs = [], []
        for t, (kind, R, cs) in enumerate(meta):
            half = R // 2
            mine = pl.multiple_of(c * half, SUBLANES)
            theirs = pl.multiple_of((1 - c) * half, SUBLANES)
            for p, (px, py) in enumerate(_other_chips(x, y)):
                k = 3 * t + p
                win = _shard_window(bufs[t], kind, 2 * px + py, mine, half, cs)
                cp = pltpu.make_async_remote_copy(src_ref=win, dst_ref=win, send_sem=send_sem.at[k],
                                                  recv_sem=recv_sem.at[k], device_id=(x, y, 1 - c),
                                                  device_id_type=MESH_IDS)
                cp.start()
                sends.append(cp)
                got = _shard_window(bufs[t], kind, 2 * px + py, theirs, half, cs)
                recvs.append(pltpu.make_async_remote_copy(src_ref=win, dst_ref=got, send_sem=send_sem.at[k],
                                                          recv_sem=recv_sem.at[k], device_id=(x, y, 1 - c),
                                                          device_id_type=MESH_IDS))
        for cp in recvs:
            cp.wait_recv()
        for cp in sends:
            cp.wait_send()

    scratch = [pltpu.SemaphoreType.DMA((3 * n,)), pltpu.SemaphoreType.DMA((3 * n,))]
    out_shape = [jax.ShapeDtypeStruct(g.shape, g.dtype) for g in gathered]
    return _call(body, name=name, out_shape=out_shape, in_specs=[_ANY] * n, out_specs=[_ANY] * n, scratch=scratch,
                 aliases={t: t for t in range(n)})(*gathered)


def _sibling_copies(grads, lands, send_sem, recv_sem):
    x, y, c = _mesh_pos()
    out = []
    for t in range(len(grads)):
        half = lands[t].shape[1]
        theirs = pl.multiple_of((1 - c) * half, SUBLANES)
        out.append(pltpu.make_async_remote_copy(src_ref=grads[t].at[:, pl.ds(theirs, half), :], dst_ref=lands[t],
                                                send_sem=send_sem.at[t], recv_sem=recv_sem.at[t],
                                                device_id=(x, y, 1 - c), device_id_type=MESH_IDS))
    return out


def _sibling_start(name, grads):
    n = len(grads)
    lands = [lax.empty((N_CHIPS, g.shape[1] // 2, g.shape[2]), g.dtype) for g in grads]

    def body(*refs):
        send_sem, recv_sem = refs[2 * n], refs[2 * n + 1]
        srcs, zones = refs[2 * n + 2:3 * n + 2], refs[3 * n + 2:4 * n + 2]
        token = refs[4 * n + 2]
        for cp in _sibling_copies(srcs, zones, send_sem, recv_sem):
            cp.start()
        token[...] = jnp.zeros_like(token)

    out_shape = (pltpu.SemaphoreType.DMA((n,)), pltpu.SemaphoreType.DMA((n,)),
                 *[pltpu.HBM(a.shape, a.dtype) for a in list(grads) + lands],
                 jax.ShapeDtypeStruct((SUBLANES, LANES), F32))
    out_specs = (_SEM, _SEM, *[_HBM] * (2 * n), pl.BlockSpec(memory_space=pltpu.VMEM))
    res = pl.pallas_call(body, out_shape=out_shape, in_specs=[_HBM] * (2 * n), out_specs=out_specs,
                         input_output_aliases={t: 2 + t for t in range(2 * n)}, name=name,
                         compiler_params=pltpu.CompilerParams(has_side_effects=_KEEP_ORDER))(
        *[_in_hbm(a) for a in list(grads) + lands])
    return res[0], res[1], list(res[2:2 + n]), list(res[2 + n:2 + 2 * n]), res[2 + 2 * n]


def _sibling_wait(name, grads, lands, send_sem, recv_sem, after):
    n = len(grads)

    def body(*refs):
        for cp in _sibling_copies(refs[:n], refs[n:2 * n], refs[2 * n], refs[2 * n + 1]):
            cp.wait_send()
            cp.wait_recv()

    out_shape = tuple(pltpu.HBM(a.shape, a.dtype) for a in list(grads) + list(lands))
    res = pl.pallas_call(body, out_shape=out_shape, in_specs=[_HBM] * (2 * n) + [_SEM, _SEM, _ANY],
                         out_specs=tuple([_HBM] * (2 * n)), input_output_aliases={t: t for t in range(2 * n)},
                         name=name, compiler_params=pltpu.CompilerParams(has_side_effects=_KEEP_ORDER))(
        *grads, *lands, send_sem, recv_sem, after)
    return list(res[:n]), list(res[n:])


def _exchange_copies(parts, lands, send_sem, recv_sem):
    x, y, c = _mesh_pos()
    out = []
    for t in range(len(parts)):
        for p, (px, py) in enumerate(_other_chips(x, y)):
            k = 3 * t + p
            out.append(pltpu.make_async_remote_copy(src_ref=parts[t].at[2 * px + py], dst_ref=lands[t].at[p],
                                                    send_sem=send_sem.at[k], recv_sem=recv_sem.at[k],
                                                    device_id=(px, py, c), device_id_type=MESH_IDS))
    return out


def _exchange_start(name, partials):
    n = len(partials)
    lands = [lax.empty((3,) + p.shape[1:], p.dtype) for p in partials]

    def body(*refs):
        send_sem, recv_sem = refs[2 * n], refs[2 * n + 1]
        parts, zones = refs[2 * n + 2:3 * n + 2], refs[3 * n + 2:4 * n + 2]
        token = refs[4 * n + 2]
        for cp in _exchange_copies(parts, zones, send_sem, recv_sem):
            cp.start()
        token[...] = jnp.zeros_like(token)

    out_shape = (pltpu.SemaphoreType.DMA((3 * n,)), pltpu.SemaphoreType.DMA((3 * n,)),
                 *[pltpu.HBM(a.shape, a.dtype) for a in partials + lands],
                 jax.ShapeDtypeStruct((SUBLANES, LANES), F32))
    out_specs = (_SEM, _SEM, *[_HBM] * (2 * n), pl.BlockSpec(memory_space=pltpu.VMEM))
    res = pl.pallas_call(body, out_shape=out_shape, in_specs=[_HBM] * (2 * n), out_specs=out_specs,
                         input_output_aliases={t: 2 + t for t in range(2 * n)}, name=name,
                         compiler_params=pltpu.CompilerParams(has_side_effects=_KEEP_ORDER))(
        *[_in_hbm(a) for a in partials + lands])
    return res[0], res[1], list(res[2:2 + n]), list(res[2 + n:2 + 2 * n]), res[2 + 2 * n]


def _exchange_wait(name, parts, lands, send_sem, recv_sem, after):
    n = len(parts)

    def body(*refs):
        for cp in _exchange_copies(refs[:n], refs[n:2 * n], refs[2 * n], refs[2 * n + 1]):
            cp.wait_send()
            cp.wait_recv()

    out_shape = tuple(pltpu.HBM(a.shape, a.dtype) for a in parts + lands)
    res = pl.pallas_call(body, out_shape=out_shape, in_specs=[_HBM] * (2 * n) + [_SEM, _SEM, _ANY],
                         out_specs=tuple([_HBM] * (2 * n)), input_output_aliases={t: t for t in range(2 * n)},
                         name=name, compiler_params=pltpu.CompilerParams(has_side_effects=_KEEP_ORDER))(
        *parts, *lands, send_sem, recv_sem, after)
    return list(res[n:])


def _share_halves(name, totals):
    n = len(totals)

    def body(*refs):
        bufs = refs[n:2 * n]
        send_sem, recv_sem = refs[2 * n:]
        x, y, c = _mesh_pos()
        sends, recvs = [], []
        for t, g in enumerate(totals):
            half = g.shape[0] // 2
            mine = bufs[t].at[pl.ds(pl.multiple_of(c * half, SUBLANES), half), :]
            theirs = bufs[t].at[pl.ds(pl.multiple_of((1 - c) * half, SUBLANES), half), :]
            cp = pltpu.make_async_remote_copy(src_ref=mine, dst_ref=mine, send_sem=send_sem.at[t],
                                              recv_sem=recv_sem.at[t], device_id=(x, y, 1 - c),
                                              device_id_type=MESH_IDS)
            cp.start()
            sends.append(cp)
            recvs.append(pltpu.make_async_remote_copy(src_ref=mine, dst_ref=theirs, send_sem=send_sem.at[t],
                                                      recv_sem=recv_sem.at[t], device_id=(x, y, 1 - c),
                                                      device_id_type=MESH_IDS))
        for cp in recvs:
            cp.wait_recv()
        for cp in sends:
            cp.wait_send()

    out_shape = [jax.ShapeDtypeStruct(g.shape, g.dtype) for g in totals]
    scratch = [pltpu.SemaphoreType.DMA((n,)), pltpu.SemaphoreType.DMA((n,))]
    return _call(body, name=name, out_shape=out_shape, in_specs=[_ANY] * n, out_specs=[_ANY] * n, scratch=scratch,
                 aliases={t: t for t in range(n)})(*totals)


def _gather_small(name, v):
    def body(v_ref, out_ref, send_sem, recv_sem, local_sem):
        x, y, c = _mesh_pos()
        me = 4 * x + 2 * y + c
        local = pltpu.make_async_copy(v_ref, out_ref.at[me], local_sem)
        local.start()
        sends, recvs = [], []
        for k in range(1, N_DEV):
            px = 1 - x if k & 4 else x
            py = 1 - y if k & 2 else y
            pc = 1 - c if k & 1 else c
            cp = pltpu.make_async_remote_copy(src_ref=v_ref, dst_ref=out_ref.at[me], send_sem=send_sem.at[k],
                                              recv_sem=recv_sem.at[k], device_id=(px, py, pc),
                                              device_id_type=MESH_IDS)
            cp.start()
            sends.append(cp)
            recvs.append(pltpu.make_async_remote_copy(
                src_ref=v_ref, dst_ref=out_ref.at[4 * px + 2 * py + pc], send_sem=send_sem.at[k],
                recv_sem=recv_sem.at[k], device_id=(px, py, pc), device_id_type=MESH_IDS))
        for cp in recvs:
            cp.wait_recv()
        for cp in sends:
            cp.wait_send()
        local.wait()

    scratch = [pltpu.SemaphoreType.DMA((N_DEV,)), pltpu.SemaphoreType.DMA((N_DEV,)), pltpu.SemaphoreType.DMA(())]
    return _call(body, name=name, out_shape=jax.ShapeDtypeStruct((N_DEV,) + v.shape, v.dtype), in_specs=[_ANY],
                 out_specs=_ANY, scratch=scratch)(v)


def _row_tile(rows, cols, n_arrays, mult=SUBLANES):
    budget = (24 << 20) // (2 * n_arrays * _round_up(cols, LANES) * 4)
    for t in (512, 256, 128, 64, 32, 16, 8):
        if t <= max(budget, mult) and rows % t == 0 and t % mult == 0:
            return t
    raise ValueError(f"no row tile for {rows} x {cols}")


def _chip_partial(name, pos, own, recv):
    _, half, C = recv.shape
    tr = _row_tile(half, C, 3, 16)
    nh = half // tr

    def body(pos_ref, own_ref, recv_ref, out_ref):
        out_ref[...] = (own_ref[...] + recv_ref[...]).astype(BF)

    blk = pl.BlockSpec((None, tr, C), lambda s, r, pos_ref: (s, r, 0))
    own_blk = pl.BlockSpec((None, tr, C), lambda s, r, pos_ref: (s, pos_ref[0] * nh + r, 0))
    return _call(body, name=name, out_shape=jax.ShapeDtypeStruct(recv.shape, BF), grid=(N_CHIPS, nh),
                 in_specs=[own_blk, blk], out_specs=blk, sem=("parallel", "parallel"), vmem=40 << 20,
                 prefetch=1)(pos, own, recv)


def _final_half(name, pos, own, recv, others):
    _, half, C = recv.shape
    tr = _row_tile(half, C, 4, 16)
    nh = half // tr

    def body(pos_ref, own_ref, recv_ref, oth_ref, out_ref):
        acc = own_ref[...] + recv_ref[...]
        for p in range(3):
            acc = acc + oth_ref[p].astype(F32)
        out_ref[...] = acc

    own_blk = pl.BlockSpec((None, tr, C), lambda r, pos_ref: (pos_ref[1], pos_ref[0] * nh + r, 0))
    recv_blk = pl.BlockSpec((None, tr, C), lambda r, pos_ref: (pos_ref[1], r, 0))
    oth_blk = pl.BlockSpec((3, tr, C), lambda r, pos_ref: (0, r, 0))
    out_blk = pl.BlockSpec((tr, C), lambda r, pos_ref: (pos_ref[0] * nh + r, 0))
    return _call(body, name=name, out_shape=jax.ShapeDtypeStruct((2 * half, C), F32), grid=(nh,),
                 in_specs=[own_blk, recv_blk, oth_blk], out_specs=out_blk, sem=("parallel",), vmem=40 << 20,
                 prefetch=1)(pos, own, recv, others)


def _adamw(name, w, g, m, v):
    R, C = w.shape
    tr = R if R < SUBLANES or R % SUBLANES else _row_tile(R, C, 7)
    c1 = 1.0 - ADAM_B1 ** ADAM_STEP
    c2 = 1.0 - ADAM_B2 ** ADAM_STEP

    def body(w_ref, g_ref, m_ref, v_ref, d_ref, nm_ref, nv_ref):
        gv = g_ref[...]
        nm = ADAM_B1 * m_ref[...] + (1.0 - ADAM_B1) * gv
        nv = ADAM_B2 * v_ref[...] + (1.0 - ADAM_B2) * (gv * gv)
        d_ref[...] = -ADAM_LR * ((nm / c1) / (jnp.sqrt(nv / c2) + ADAM_EPS) + ADAM_WD * w_ref[...])
        nm_ref[...] = nm
        nv_ref[...] = nv

    blk = pl.BlockSpec((tr, C), lambda r: (r, 0))
    out_shape = [jax.ShapeDtypeStruct((R, C), F32)] * 3
    return _call(body, name=name, out_shape=out_shape, grid=(R // tr,), in_specs=[blk] * 4, out_specs=[blk] * 3,
                 sem=("parallel",), vmem=40 << 20)(w, g, m, v)


def _sum_devices(name, gathered):
    _, R, _ = gathered.shape

    def body(g_ref, o_ref):
        acc = g_ref[0]
        for d in range(1, N_DEV):
            acc = acc + g_ref[d]
        o_ref[...] = acc

    return _call(body, name=name, out_shape=jax.ShapeDtypeStruct((R, LANES), F32), grid=(1,),
                 in_specs=[pl.BlockSpec((N_DEV, R, LANES), lambda i: (0, 0, 0))],
                 out_specs=pl.BlockSpec((R, LANES), lambda i: (0, 0)), sem=("arbitrary",), vmem=16 << 20)(gathered)


def _ffn_fwd(tag, h, gain, wg, wu, wd, tm):
    L, D = h.shape
    F = wg.shape[1]
    tn = _pick(F, (512, 256, 128))
    n = _rmsnorm_fwd(f"{tag}_norm", h, gain, tm)

    def gate_up(accs, _):
        a, u = accs
        return a, u, a * _sigmoid(a) * u

    mn = _mn(tm, tn)
    a, u, s = _matmul(f"{tag}_gate_up", [n], [wg, wu], [(0, 0, 0), (0, 1, 1)], 2, gate_up,
                      [((L, F), BF) + mn] * 3, M=L, N=F, K=D, tm=tm, tn=tn, tk=D, n_outer=True)
    if callable(wd):
        wd = wd(s)
    td = _pick(D, (512, 256, 128))
    (h_out,) = _matmul(f"{tag}_down", [s], [wd], [(0, 0, 0)], 1,
                       lambda accs, ex: [ex[0] + FFN_RESIDUAL_WEIGHT * accs[0]], [((L, D), F32) + _mn(tm, td)],
                       M=L, N=D, K=F, tm=tm, tn=td, tk=F, extras=[(h,) + _mn(tm, td)], n_outer=True)
    return h_out, (h, n, a, u, s), wd


def _ffn_bwd(tag, dh, dh_bf, saved, gain, wg, wu, wd, tm, cs_ff, after=None, midway=None, on_weight_grads=None):
    h, n, a, u, s = saved
    L, D = h.shape
    F = wg.shape[1]
    tn = _pick(F, (512, 256, 128))
    tkl = _pick(L, (1408, 384, 256, 128))

    def act_grad(accs, ex):
        ds = FFN_RESIDUAL_WEIGHT * accs[0]
        av, uv = ex[0].astype(F32), ex[1].astype(F32)
        sg = _sigmoid(av)
        return ds * uv * sg * (1.0 + av * (1.0 - sg)), ds * av * sg

    mn = _mn(tm, tn)
    da, du = _matmul(f"{tag}_dact", [dh_bf], [wd], [(0, 0, 0)], 1, act_grad, [((L, F), BF) + mn] * 2, M=L, N=F,
                     K=D, tm=tm, tn=tn, tk=D, tb=True, extras=[(a,) + mn, (u,) + mn], n_outer=True, after=after)
    td = _pick(D, (512, 256, 128))
    (dwd,) = _matmul(f"{tag}_dwd", [s], [dh_bf], [(0, 0, 0)], 1, lambda accs, _: [FFN_RESIDUAL_WEIGHT * accs[0]],
                     [((N_CHIPS, cs_ff, D), F32, (None, cs_ff, td), lambda i, j: (i, 0, j))], M=F, N=D, K=L,
                     tm=cs_ff, tn=td, tk=tkl, ta=True, after=after)
    tmw = _pick(D, (512, 256, 128))
    shard_out = ((N_CHIPS, D, cs_ff), F32, (None, tmw, cs_ff), lambda i, j: (j, i, 0))
    dwg, dwu = _matmul(f"{tag}_dwgu", [n], [da, du], [(0, 0, 0), (0, 1, 1)], 2, lambda accs, _: accs,
                       [shard_out] * 2, M=D, N=F, K=L, tm=tmw, tn=cs_ff, tk=tkl, ta=True,
                       after=midway(dwd) if midway else None)
    started = on_weight_grads(dwg, dwu, dwd) if on_weight_grads else None
    tdn = _pick(D, (256, 128))
    (dn,) = _matmul(f"{tag}_dn", [da, du], [wg, wu], [(0, 0, 0), (1, 1, 0)], 1, lambda accs, _: accs,
                    [((L, D), F32) + _mn(tm, tdn)], M=L, N=D, K=F, tm=tm, tn=tdn, tk=F, tb=True, after=started)
    dh_in, dh_in_bf, dgain = _rmsnorm_bwd(f"{tag}_dnorm", dn, h, gain, dh, tm)
    return dh_in, dh_in_bf, dgain


def kernel(x, meta_tokens, ffn1_norm, ffn1_w_gate, ffn1_w_up, ffn1_w_down, mix_norm, w_in, b_forget, fox_q_norm, fox_k_norm, w_branch_fox, w_branch_sb, w_out, ffn2_norm, ffn2_w_gate, ffn2_w_up, ffn2_w_down, loss_target, m_meta_tokens, m_ffn1_norm, m_ffn1_w_gate, m_ffn1_w_up, m_ffn1_w_down, m_mix_norm, m_w_in, m_b_forget, m_fox_q_norm, m_fox_k_norm, m_w_branch_fox, m_w_branch_sb, m_w_out, m_ffn2_norm, m_ffn2_w_gate, m_ffn2_w_up, m_ffn2_w_down, v_meta_tokens, v_ffn1_norm, v_ffn1_w_gate, v_ffn1_w_up, v_ffn1_w_down, v_mix_norm, v_w_in, v_b_forget, v_fox_q_norm, v_fox_k_norm, v_w_branch_fox, v_w_branch_sb, v_w_out, v_ffn2_norm, v_ffn2_w_gate, v_ffn2_w_up, v_ffn2_w_down):
    weights = dict(meta_tokens=meta_tokens, ffn1_norm=ffn1_norm, ffn1_w_gate=ffn1_w_gate, ffn1_w_up=ffn1_w_up,
                   ffn1_w_down=ffn1_w_down, mix_norm=mix_norm, w_in=w_in, b_forget=b_forget, fox_q_norm=fox_q_norm,
                   fox_k_norm=fox_k_norm, w_branch_fox=w_branch_fox, w_branch_sb=w_branch_sb, w_out=w_out,
                   ffn2_norm=ffn2_norm, ffn2_w_gate=ffn2_w_gate, ffn2_w_up=ffn2_w_up, ffn2_w_down=ffn2_w_down)
    moments_m = dict(meta_tokens=m_meta_tokens, ffn1_norm=m_ffn1_norm, ffn1_w_gate=m_ffn1_w_gate,
                     ffn1_w_up=m_ffn1_w_up, ffn1_w_down=m_ffn1_w_down, mix_norm=m_mix_norm, w_in=m_w_in,
                     b_forget=m_b_forget, fox_q_norm=m_fox_q_norm, fox_k_norm=m_fox_k_norm,
                     w_branch_fox=m_w_branch_fox, w_branch_sb=m_w_branch_sb, w_out=m_w_out, ffn2_norm=m_ffn2_norm,
                     ffn2_w_gate=m_ffn2_w_gate, ffn2_w_up=m_ffn2_w_up, ffn2_w_down=m_ffn2_w_down)
    moments_v = dict(meta_tokens=v_meta_tokens, ffn1_norm=v_ffn1_norm, ffn1_w_gate=v_ffn1_w_gate,
                     ffn1_w_up=v_ffn1_w_up, ffn1_w_down=v_ffn1_w_down, mix_norm=v_mix_norm, w_in=v_w_in,
                     b_forget=v_b_forget, fox_q_norm=v_fox_q_norm, fox_k_norm=v_fox_k_norm,
                     w_branch_fox=v_w_branch_fox, w_branch_sb=v_w_branch_sb, w_out=v_w_out, ffn2_norm=v_ffn2_norm,
                     ffn2_w_gate=v_ffn2_w_gate, ffn2_w_up=v_ffn2_w_up, ffn2_w_down=v_ffn2_w_down)
    names = list(weights)

    _, S, D = x.shape
    NM = meta_tokens.shape[0]
    L_real = NM + S
    L = _round_up(L_real, ATT_TILE)
    nblk = L // ATT_TILE
    nbp = _round_up(nblk, SUBLANES)
    cs_ff = ffn1_w_gate.shape[2]
    F = N_CHIPS * cs_ff
    H = b_forget.shape[1]
    FW = w_branch_fox.shape[1]
    SW = w_branch_sb.shape[1]
    HS = SW // HEAD_DIM
    cs_in = w_in.shape[2]
    W_IN = N_CHIPS * cs_in
    assert FW == H * HEAD_DIM and W_IN == 3 * FW + H + 3 * SW + 2 * D
    cs_d = D // N_CHIPS
    tm = _pick(L, (384, 256, 128))

    x_pos, y_pos, c_pos = _mesh_pos()
    pos = jnp.stack([c_pos, 2 * x_pos + y_pos]).astype(I32)

    shard_of = {
        "ffn1_w_gate": (ffn1_w_gate[0], "col"), "ffn1_w_up": (ffn1_w_up[0], "col"),
        "ffn1_w_down": (ffn1_w_down[0], "maj"), "w_in": (w_in[0], "maj"),
        "w_branch_fox": (w_branch_fox[0], "col"), "w_branch_sb": (w_branch_sb[0], "col"),
        "w_out": (w_out[0], "maj"), "ffn2_w_gate": (ffn2_w_gate[0], "col"), "ffn2_w_up": (ffn2_w_up[0], "col"),
        "ffn2_w_down": (ffn2_w_down[0], "maj"),
    }
    g_names = list(shard_of) + ["meta_tokens"]
    shards = [shard_of[k][0] for k in shard_of] + [meta_tokens]
    kinds = [shard_of[k][1] for k in shard_of] + ["col"]
    dtypes = [BF] * len(shard_of) + [F32]
    info = {k: (s, kind) for k, s, kind in zip(g_names, shards, kinds)}
    placed = {k: _place_shard(f"place_{k}", pos, s, kind, dt)
              for k, s, kind, dt in zip(g_names, shards, kinds, dtypes)}
    groups = [["meta_tokens", "ffn1_w_gate", "ffn1_w_up"], ["ffn1_w_down"], ["w_in"],
              ["w_branch_fox", "w_branch_sb", "w_out", "ffn2_w_gate", "ffn2_w_up", "ffn2_w_down"]]
    in_flight = []
    all_started = jnp.zeros((SUBLANES, LANES), F32)
    for gi, grp in enumerate(groups):
        g_shards, g_kinds = [info[k][0] for k in grp], [info[k][1] for k in grp]
        in_flight.append(_all_gather_ici_start(f"gather_start_{gi}", [placed[k] for k in grp], g_shards, g_kinds,
                                               all_started))
        all_started = in_flight[-1][3]
    full = {}

    def arrive(gi, after):
        grp = groups[gi]
        g_shards, g_kinds = [info[k][0] for k in grp], [info[k][1] for k in grp]
        send_sem, recv_sem, bufs, _ = in_flight[gi]
        bufs = _all_gather_ici_wait(f"gather_wait_{gi}", bufs, send_sem, recv_sem, after, g_shards, g_kinds)
        full.update(zip(grp, _all_gather_d2d(f"gather_d2d_{gi}", bufs, g_shards, g_kinds)))

    arrive(0, all_started)
    wg1, wu1 = full["ffn1_w_gate"], full["ffn1_w_up"]
    c_f = 3 * FW
    QKV_S, GATES, FCOL = 3 * FW, 3 * FW + 3 * SW, 3 * FW + 3 * SW + 2 * D
    W_PROJ = FCOL + LANES

    h0 = jnp.concatenate([full["meta_tokens"], x[0], jnp.zeros((L - L_real, D), F32)], axis=0)
    target = jnp.concatenate([jnp.zeros((NM, D), F32), loss_target[0], jnp.zeros((L - L_real, D), F32)], axis=0)

    def late_wd1(s):
        arrive(1, s)
        return full["ffn1_w_down"].reshape(F, D)

    h1, saved1, wd1 = _ffn_fwd("ffn1", h0, ffn1_norm, wg1, wu1, late_wd1, tm)

    arrive(2, h1)
    w_in_full = jnp.transpose(full["w_in"], (1, 0, 2)).reshape(D, W_IN)
    w_proj = jnp.concatenate([w_in_full[:, :c_f], w_in_full[:, c_f + H:],
                              jnp.pad(w_in_full[:, c_f:c_f + H], ((0, 0), (0, LANES - H)))], axis=1)
    n2 = _rmsnorm_fwd("mix_norm", h1, mix_norm, tm)
    tp = _pick(FCOL, (512, 256, 128))
    (proj,) = _matmul("in_proj", [n2], [w_proj], [(0, 0, 0)], 1, lambda accs, _: accs,
                      [((L, FCOL), BF) + _mn(tm, tp)], M=L, N=FCOL, K=D, tm=tm, tn=tp, tk=D, n_outer=True)
    (f_logit,) = _matmul("forget_proj", [n2], [w_proj], [(0, 0, 0)], 1, lambda accs, _: accs,
                         [((L, LANES), F32) + _mn(tm, LANES)], M=L, N=LANES, K=D, tm=tm, tn=LANES, tk=D,
                         b_off=FCOL // LANES)
    fl = jnp.pad(jnp.transpose(f_logit[:, :H]).reshape(H, nblk, LANES), ((0, 0), (0, nbp - nblk), (0, 0)))
    bias = jnp.broadcast_to(b_forget[0][:, None, None], (H, 1, LANES))
    c_row = _cum_fwd("forget_cumsum", fl, bias, nblk)
    c_col = c_row[:, :nblk].reshape(H, L, 1)
    gq, gk = fox_q_norm[0][:, None, :], fox_k_norm[0][:, None, :]
    qn, kn = _qknorm_fwd("fox_qk_norm", proj, gq, gk, H, 0, FW, tm)
    o_fox, lse = _fox_fwd("fox_attention", qn, kn, proj, c_row, c_col, H, ATT_HEADS, 0, 0, 2 * FW)
    o_sb = _sb_fwd("sb_attention", proj, HS, min(HS, ATT_HEADS_SB_FWD), QKV_S, QKV_S + SW, QKV_S + 2 * SW)

    arrive(3, o_sb)
    wg2, wu2 = full["ffn2_w_gate"], full["ffn2_w_up"]
    wd2 = full["ffn2_w_down"].reshape(F, D)
    wbf, wbs = full["w_branch_fox"], full["w_branch_sb"]
    wo = full["w_out"].reshape(D, D)
    td = _pick(D, (512, 256, 128))

    def merge(accs, ex):
        bf_, bs_ = accs
        return _sigmoid(ex[0].astype(F32)) * bf_ + _sigmoid(ex[1].astype(F32)) * bs_, bf_, bs_

    merged, br_f, br_s = _matmul("branch_merge", [o_fox, o_sb], [wbf, wbs], [(0, 0, 0), (1, 1, 1)], 2, merge,
                                 [((L, D), BF) + _mn(tm, td)] * 3, M=L, N=D, K=FW, tm=tm, tn=td, tk=FW,
                                 extras=[(proj,) + _mn(tm, td, GATES), (proj,) + _mn(tm, td, GATES + D)],
                                 n_outer=True)
    (h2,) = _matmul("out_proj", [merged], [wo], [(0, 0, 0)], 1, lambda accs, ex: [ex[0] + accs[0]],
                    [((L, D), F32) + _mn(tm, td)], M=L, N=D, K=D, tm=tm, tn=td, tk=D, extras=[(h1,) + _mn(tm, td)],
                    n_outer=True)

    h3, saved2, _ = _ffn_fwd("ffn2", h2, ffn2_norm, wg2, wu2, wd2, tm)
    dh3, dh3_bf, loss_part = _loss_grad("loss", h3, target, NM, S, tm)

    scatters, totals = {}, {}

    def swap_begin(tag, keys, local):
        send_sem, recv_sem, grads, lands, token = _sibling_start(f"grads_to_sibling_start_{tag}", list(local))
        scatters[tag] = (keys, send_sem, recv_sem, grads, lands)
        return token

    def exchange_begin(tag, after):
        keys, send_sem, recv_sem, grads, lands = scatters[tag]
        local, from_sibling = _sibling_wait(f"grads_to_sibling_wait_{tag}", grads, lands, send_sem, recv_sem, after)
        partials = [_chip_partial(f"chip_sum_{k}", pos, g, r) for k, g, r in zip(keys, local, from_sibling)]
        send_sem, recv_sem, parts, lands, token = _exchange_start(f"grads_to_owner_start_{tag}", partials)
        scatters[tag] = (keys, local, from_sibling, send_sem, recv_sem, parts, lands)
        return token

    def scatter_end(tag, after):
        keys, local, from_sibling, send_sem, recv_sem, parts, lands = scatters[tag]
        from_chips = _exchange_wait(f"grads_to_owner_wait_{tag}", parts, lands, send_sem, recv_sem, after)
        totals.update({k: _final_half(f"total_{k}", pos, g, r, o)
                       for k, g, r, o in zip(keys, local, from_sibling, from_chips)})

    dh2, dh2_bf, dg_ffn2 = _ffn_bwd(
        "ffn2", dh3, dh3_bf, saved2, ffn2_norm, wg2, wu2, wd2, tm, cs_ff,
        on_weight_grads=lambda *dw: swap_begin("ffn2", ["ffn2_w_gate", "ffn2_w_up", "ffn2_w_down"], dw))
    ffn2_exchanging = exchange_begin("ffn2", dh2)

    def gate_grad(accs, ex):
        dm = accs[0]
        gf, gs, bf_, bs_ = [e.astype(F32) for e in ex]
        sf, ss = _sigmoid(gf), _sigmoid(gs)
        return dm * bf_ * sf * (1.0 - sf), dm * bs_ * ss * (1.0 - ss), dm * sf, dm * ss

    mn_d = _mn(tm, td)
    dgf, dgs, dbr_f, dbr_s = _matmul(
        "d_merged", [dh2_bf], [wo], [(0, 0, 0)], 1, gate_grad, [((L, D), BF) + mn_d] * 4, M=L, N=D, K=D, tm=tm,
        tn=td, tk=D, tb=True, extras=[(proj,) + _mn(tm, td, GATES), (proj,) + _mn(tm, td, GATES + D),
                                      (br_f,) + mn_d, (br_s,) + mn_d], n_outer=True, after=ffn2_exchanging)
    tkl = _pick(L, (1408, 384, 256, 128))
    (dwo,) = _matmul("d_w_out", [merged], [dh2_bf], [(0, 0, 0)], 1, lambda accs, _: accs,
                     [((N_CHIPS, cs_d, D), F32, (None, cs_d, td), lambda i, j: (i, 0, j))], M=D, N=D, K=L, tm=cs_d,
                     tn=td, tk=tkl, ta=True)
    tw = _pick(FW, (512, 256, 128))
    do_fox, do_sb = _matmul("d_branch_in", [dbr_f, dbr_s], [wbf, wbs], [(0, 0, 0), (1, 1, 1)], 2,
                            lambda accs, _: accs, [((L, FW), BF) + _mn(tm, tw)] * 2, M=L, N=FW, K=D, tm=tm, tn=tw,
                            tk=D, tb=True)
    tmb = _pick(FW, (1024, 512, 256, 128))
    dwbf, dwbs = _matmul("d_w_branch", [o_fox, o_sb], [dbr_f, dbr_s], [(0, 0, 0), (1, 1, 1)], 2,
                         lambda accs, _: accs,
                         [((N_CHIPS, FW, cs_d), F32, (None, tmb, cs_d), lambda i, j: (j, i, 0))] * 2, M=FW, N=D, K=L,
                         tm=tmb, tn=cs_d, tk=tkl, ta=True)

    dqn, dkn, dfv, dcs, drs = _fox_bwd("fox_attention_bwd", qn, kn, proj, c_row, c_col, o_fox, do_fox, lse, H,
                                       ATT_HEADS, 0, 0, 2 * FW)
    dsq, dsk_t, dsv_t = _sb_bwd("sb_attention_bwd", proj, do_sb, HS, ATT_HEADS, QKV_S, QKV_S + SW,
                                QKV_S + 2 * SW)
    dsk = jnp.transpose(dsk_t, (0, 2, 1)).reshape(L, SW)
    dsv = jnp.transpose(dsv_t, (0, 2, 1)).reshape(L, SW)
    dfq, dfk, dgq, dgk = _qknorm_bwd("fox_qk_norm_bwd", proj, dqn, dkn, gq, gk, H, 0, FW, tm)
    drs_row = jnp.pad(drs.reshape(H, nblk, LANES), ((0, 0), (0, nbp - nblk), (0, 0)))
    dfl, dbias = _cum_bwd("forget_cumsum_bwd", drs_row, dcs, fl, bias, nblk)
    dfl_cols = jnp.pad(jnp.transpose(dfl[:, :nblk].reshape(H, L)), ((0, 0), (0, LANES - H))).astype(BF)
    dproj = jnp.concatenate([dfq, dfk, dfv.astype(BF), dsq, dsk.astype(BF), dsv.astype(BF), dgf, dgs, dfl_cols],
                            axis=1)

    tdn = _pick(D, (256, 128))
    (dn2,) = _matmul("d_mix_norm_in", [dproj], [w_proj], [(0, 0, 0)], 1, lambda accs, _: accs,
                     [((L, D), F32) + _mn(tm, tdn)], M=L, N=D, K=W_PROJ, tm=tm, tn=tdn, tk=W_PROJ, tb=True)
    tmw = _pick(D, (1024, 512, 256, 128))
    tnp = _pick(W_PROJ, (1152, 640, 512, 384, 256, 128))
    (dw_proj,) = _matmul("d_w_in", [n2], [dproj], [(0, 0, 0)], 1, lambda accs, _: accs,
                         [((D, W_PROJ), F32) + _mn(tmw, tnp)], M=D, N=W_PROJ, K=L, tm=tmw, tn=tnp, tk=tkl,
                         ta=True)
    dh1, dh1_bf, dg_mix = _rmsnorm_bwd("mix_dnorm", dn2, h1, mix_norm, dh2, tm)
    dw_in_ref = jnp.concatenate([dw_proj[:, :c_f], dw_proj[:, FCOL:FCOL + H], dw_proj[:, c_f:FCOL]], axis=1)
    dw_in = jnp.transpose(dw_in_ref.reshape(D, N_CHIPS, cs_in), (1, 0, 2))
    scatter_end("ffn2", dh1)
    mix_swapping = swap_begin("mix", ["w_in", "w_branch_fox", "w_branch_sb", "w_out"], [dw_in, dwbf, dwbs, dwo])

    def ffn1_grads(*dw):
        swapping = swap_begin("ffn1", ["ffn1_w_gate", "ffn1_w_up", "ffn1_w_down"], dw)
        return exchange_begin("ffn1", swapping)

    dh0, _, dg_ffn1 = _ffn_bwd("ffn1", dh1, dh1_bf, saved1, ffn1_norm, wg1, wu1, wd1, tm, cs_ff, after=mix_swapping,
                               midway=lambda done: exchange_begin("mix", done), on_weight_grads=ffn1_grads)
    grad_x = dh0[NM:L_real][None]
    scatter_end("mix", dh0)
    scatter_end("ffn1", dh0)
    big = list(totals)
    grads = dict(zip(big, _share_halves("grads_to_core_pair", [totals[k] for k in big])))

    small = [loss_part[:, :1].reshape(1), dh0[:NM].reshape(-1), dg_ffn1.reshape(-1), dg_mix.reshape(-1),
             dg_ffn2.reshape(-1), dbias[:, 0, 0], dgq.reshape(-1), dgk.reshape(-1)]
    sizes = [s.shape[0] for s in small]
    flat = jnp.concatenate(small)
    rows = _round_up(-(-flat.shape[0] // LANES), SUBLANES)
    packed = jnp.pad(flat, (0, rows * LANES - flat.shape[0])).reshape(rows, LANES)
    total = _sum_devices("sum_small", _gather_small("gather_small", packed)).reshape(-1)
    pieces, off = [], 0
    for n_el in sizes:
        pieces.append(total[off:off + n_el])
        off += n_el
    loss = pieces[0][0]
    d_meta = lax.dynamic_slice_in_dim(pieces[1].reshape(NM, D), pos[1] * cs_d, cs_d, axis=1)
    grads.update(meta_tokens=d_meta, ffn1_norm=pieces[2].reshape(1, D), mix_norm=pieces[3].reshape(1, D),
                 ffn2_norm=pieces[4].reshape(1, D), b_forget=pieces[5].reshape(1, H),
                 fox_q_norm=pieces[6].reshape(1, H, HEAD_DIM), fox_k_norm=pieces[7].reshape(1, H, HEAD_DIM))

    out_g, out_d, out_m, out_v = [], [], [], []
    for k in names:
        w = weights[k]
        shape2 = (1, w.size) if w.size < LANES * SUBLANES else (w.size // w.shape[-1], w.shape[-1])
        g2 = grads[k].reshape(shape2)
        d, nm, nv = _adamw(f"adamw_{k}", w.reshape(shape2), g2, moments_m[k].reshape(shape2),
                           moments_v[k].reshape(shape2))
        out_g.append(g2.reshape(w.shape))
        out_d.append(d.reshape(w.shape))
        out_m.append(nm.reshape(w.shape))
        out_v.append(nv.reshape(w.shape))
    return (loss, grad_x, *out_g, *out_d, *out_m, *out_v)
```

```python
import functools

import jax
import jax.numpy as jnp
from jax import lax
from jax.experimental import pallas as pl
from jax.experimental.pallas import tpu as pltpu

F32 = jnp.float32
BF = jnp.bfloat16
I32 = jnp.int32

HEAD_DIM = 128
RMS_EPS = 1e-6
FFN_RESIDUAL_WEIGHT = 0.5
ADAM_LR = 0.001
ADAM_B1 = 0.9
ADAM_B2 = 0.999
ADAM_EPS = 1e-08
ADAM_WD = 0.01
ADAM_STEP = 10

LANES = 128
SUBLANES = 8
ATT_TILE = 128
ATT_HEADS = 4
ATT_HEADS_SB_FWD = 8
VMEM_CAP = 56 * 1024 * 1024
MESH_IDS = pl.DeviceIdType.MESH
N_CHIPS = 4
N_DEV = 8


def _pick(n, cands):
    for c in cands:
        if c <= n and n % c == 0:
            return c
    raise ValueError(f"no tile for {n} among {cands}")


def _round_up(n, m):
    return (n + m - 1) // m * m


def _tile_bytes(shape, dtype):
    item = jnp.dtype(dtype).itemsize
    dims = [d for d in shape if d is not None]
    if not dims:
        return 4 * LANES * SUBLANES
    last = _round_up(dims[-1], LANES)
    sub = _round_up(dims[-2], SUBLANES * (4 // item)) if len(dims) > 1 else 1
    lead = 1
    for d in dims[:-2]:
        lead *= d
    return lead * sub * last * item


def _vmem_limit(blocks, scratch=(), temps=0):
    need = 2 * sum(_tile_bytes(s, d) for s, d in blocks) + sum(_tile_bytes(s, d) for s, d in scratch) + temps
    return int(min(VMEM_CAP, max(need + (4 << 20), 16 << 20)))


def _call(body, *, name, out_shape, grid=(), in_specs=None, out_specs=None, scratch=(), sem=None, vmem=None,
          aliases=None, prefetch=0):
    params = pltpu.CompilerParams(dimension_semantics=sem, vmem_limit_bytes=vmem)
    if prefetch:
        grid_spec = pltpu.PrefetchScalarGridSpec(num_scalar_prefetch=prefetch, grid=grid, in_specs=in_specs,
                                                 out_specs=out_specs, scratch_shapes=scratch)
        return pl.pallas_call(body, out_shape=out_shape, grid_spec=grid_spec, name=name, compiler_params=params,
                              input_output_aliases=aliases or {})
    return pl.pallas_call(body, out_shape=out_shape, grid=grid, in_specs=in_specs, out_specs=out_specs,
                          scratch_shapes=scratch, name=name, compiler_params=params,
                          input_output_aliases=aliases or {})


def _dot(a, b, ca, cb):
    return lax.dot_general(a, b, (((ca,), (cb,)), ((), ())), preferred_element_type=F32)


def _sigmoid(x):
    return 1.0 / (1.0 + jnp.exp(-x))


def _log_sigmoid(x):
    return jnp.minimum(x, 0.0) - jnp.log1p(jnp.exp(-jnp.abs(x)))


def _split(x, parts):
    pieces = []
    rem = x
    for p in range(parts):
        piece = rem.astype(BF)
        pieces.append(piece)
        if p + 1 < parts:
            rem = rem - piece.astype(F32)
    return pieces


def _pieces_dot(pieces, ones_bf):
    out = None
    for piece in pieces:
        d = _dot(piece, ones_bf, 1, 0)
        out = d if out is None else out + d
    return out


def _split_dot(x, ones_bf, parts):
    return _pieces_dot(_split(x, parts), ones_bf)


def _matmul(name, a_list, b_list, pairs, n_acc, epi, outs, *, M, N, K, tm, tn, tk, ta=False, tb=False,
            extras=(), n_outer=False, b_off=0, after=None):
    if after is not None:
        user_epi = epi
        extras = list(extras) + [(after, (SUBLANES, LANES), lambda i, j: (0, 0))]
        epi = lambda accs, ex: user_epi(accs, ex[:-1])
    gi, gj, nk = M // tm, N // tn, K // tk
    assert gi * tm == M and gj * tn == N and nk * tk == K, (name, M, N, K, tm, tn, tk)
    n_a, n_b, n_e, n_o = len(a_list), len(b_list), len(extras), len(outs)

    def ij(g0, g1):
        return (g1, g0) if n_outer else (g0, g1)

    def a_map(g0, g1, k):
        i, _ = ij(g0, g1)
        return (k, i) if ta else (i, k)

    def b_map(g0, g1, k):
        _, j = ij(g0, g1)
        return (j + b_off, k) if tb else (k, j + b_off)

    def tile_map(fn):
        return lambda g0, g1, k: fn(*ij(g0, g1))

    a_block = (tk, tm) if ta else (tm, tk)
    b_block = (tn, tk) if tb else (tk, tn)
    in_specs = ([pl.BlockSpec(a_block, a_map)] * n_a + [pl.BlockSpec(b_block, b_map)] * n_b
                + [pl.BlockSpec(bs, tile_map(fn)) for _, bs, fn in extras])
    out_specs = [pl.BlockSpec(bs, tile_map(fn)) for _, _, bs, fn in outs]
    out_shape = [jax.ShapeDtypeStruct(s, d) for s, d, _, _ in outs]
    scratch = [pltpu.VMEM((tm, tn), F32) for _ in range(n_acc)] if nk > 1 else []

    def body(*refs):
        a_refs = refs[:n_a]
        b_refs = refs[n_a:n_a + n_b]
        e_refs = refs[n_a + n_b:n_a + n_b + n_e]
        o_refs = refs[n_a + n_b + n_e:n_a + n_b + n_e + n_o]
        acc_refs = refs[n_a + n_b + n_e + n_o:]

        def products():
            accs = [None] * n_acc
            for ai, bi, ci in pairs:
                a = a_refs[ai][...]
                b = b_refs[bi][...]
                d = _dot(a.astype(BF), b.astype(BF), 0 if ta else 1, 1 if tb else 0)
                accs[ci] = d if accs[ci] is None else accs[ci] + d
            return accs

        def finish(accs):
            res = epi(accs, [e[...] for e in e_refs])
            for o_ref, r in zip(o_refs, res):
                o_ref[...] = r.reshape(o_ref.shape).astype(o_ref.dtype)

        if nk == 1:
            finish(products())
        else:
            k = pl.program_id(2)

            @pl.when(k == 0)
            def _():
                for acc in acc_refs:
                    acc[...] = jnp.zeros_like(acc)

            for acc, d in zip(acc_refs, products()):
                acc[...] += d

            @pl.when(k == nk - 1)
            def _():
                finish([acc[...] for acc in acc_refs])

    blocks = ([(a_block, a.dtype) for a in a_list] + [(b_block, b.dtype) for b in b_list]
              + [(bs, e.dtype) for e, bs, _ in extras] + [(bs, d) for _, d, bs, _ in outs])
    vmem = _vmem_limit(blocks, [((tm, tn), F32)] * (n_acc if nk > 1 else 0), temps=6 * tm * tn * 4)
    grid = (gj, gi, nk) if n_outer else (gi, gj, nk)
    fn = _call(body, name=name, out_shape=out_shape, grid=grid, in_specs=in_specs, out_specs=out_specs,
               scratch=scratch, sem=("parallel", "parallel", "arbitrary"), vmem=vmem)
    return fn(*a_list, *b_list, *[e for e, _, _ in extras])


def _mn(tm, tn, col0=0):
    assert col0 % tn == 0
    off = col0 // tn
    return (tm, tn), (lambda i, j: (i, j + off))


def _rmsnorm_fwd(name, x, gain, tr):
    L, D = x.shape

    def body(x_ref, g_ref, o_ref):
        xv = x_ref[...]
        r = lax.rsqrt(jnp.mean(xv * xv, axis=-1, keepdims=True) + RMS_EPS)
        o_ref[...] = (xv * r * g_ref[...]).astype(BF)

    row = pl.BlockSpec((tr, D), lambda i: (i, 0))
    vec = pl.BlockSpec((1, D), lambda i: (0, 0))
    vmem = _vmem_limit([((tr, D), F32), ((tr, D), BF)], temps=3 * tr * D * 4)
    return _call(body, name=name, out_shape=jax.ShapeDtypeStruct((L, D), BF), grid=(L // tr,), in_specs=[row, vec],
                 out_specs=row, sem=("parallel",), vmem=vmem)(x, gain)


def _rmsnorm_bwd(name, dn, x, gain, dres, tr):
    L, D = x.shape
    steps = L // tr

    def body(dn_ref, x_ref, g_ref, dres_ref, dx_ref, dxb_ref, dg_ref):
        i = pl.program_id(0)
        xv = x_ref[...]
        r = lax.rsqrt(jnp.mean(xv * xv, axis=-1, keepdims=True) + RMS_EPS)
        xhat = xv * r
        dy = dn_ref[...]
        dxhat = dy * g_ref[...]
        dx = dres_ref[...] + r * (dxhat - xhat * jnp.mean(dxhat * xhat, axis=-1, keepdims=True))
        dx_ref[...] = dx
        dxb_ref[...] = dx.astype(BF)

        @pl.when(i == 0)
        def _():
            dg_ref[...] = jnp.zeros_like(dg_ref)

        dg_ref[...] += jnp.sum(dy * xhat, axis=0, keepdims=True)

    row = pl.BlockSpec((tr, D), lambda i: (i, 0))
    vec = pl.BlockSpec((1, D), lambda i: (0, 0))
    vmem = _vmem_limit([((tr, D), F32)] * 4 + [((tr, D), BF)], temps=4 * tr * D * 4)
    out_shape = [jax.ShapeDtypeStruct((L, D), F32), jax.ShapeDtypeStruct((L, D), BF),
                 jax.ShapeDtypeStruct((1, D), F32)]
    return _call(body, name=name, out_shape=out_shape, grid=(steps,), in_specs=[row, row, vec, row],
                 out_specs=[row, row, vec], sem=("arbitrary",), vmem=vmem)(dn, x, gain, dres)


def _loss_grad(name, h, target, n_meta, n_seq, tr):
    L, D = h.shape

    def body(h_ref, t_ref, dh_ref, dhb_ref, loss_ref):
        i = pl.program_id(0)
        rows = i * tr + lax.broadcasted_iota(I32, (tr, 1), 0)
        valid = (rows >= n_meta) & (rows < n_meta + n_seq)
        diff = jnp.where(valid, h_ref[...] - t_ref[...], 0.0)
        dh = diff * (1.0 / D)
        dh_ref[...] = dh
        dhb_ref[...] = dh.astype(BF)

        @pl.when(i == 0)
        def _():
            loss_ref[...] = jnp.zeros_like(loss_ref)

        loss_ref[...] += jnp.sum(diff * diff) * (0.5 / D)

    row = pl.BlockSpec((tr, D), lambda i: (i, 0))
    acc = pl.BlockSpec((1, LANES), lambda i: (0, 0))
    vmem = _vmem_limit([((tr, D), F32)] * 3 + [((tr, D), BF)], temps=3 * tr * D * 4)
    out_shape = [jax.ShapeDtypeStruct((L, D), F32), jax.ShapeDtypeStruct((L, D), BF),
                 jax.ShapeDtypeStruct((1, LANES), F32)]
    return _call(body, name=name, out_shape=out_shape, grid=(L // tr,), in_specs=[row, row],
                 out_specs=[row, row, acc], sem=("arbitrary",), vmem=vmem)(h, target)


def _qknorm_fwd(name, proj, gq, gk, heads, q_col, k_col, tr):
    L = proj.shape[0]

    def body(q_ref, k_ref, gq_ref, gk_ref, qn_ref, kn_ref):
        for x_ref, g_ref, o_ref in ((q_ref, gq_ref, qn_ref), (k_ref, gk_ref, kn_ref)):
            xv = x_ref[...].astype(F32)
            r = lax.rsqrt(jnp.mean(xv * xv, axis=-1, keepdims=True) + RMS_EPS)
            o_ref[...] = (xv * r * g_ref[...]).astype(BF)

    qb, kb = q_col // HEAD_DIM, k_col // HEAD_DIM
    in_specs = [pl.BlockSpec((tr, HEAD_DIM), lambda h, i: (i, qb + h)),
                pl.BlockSpec((tr, HEAD_DIM), lambda h, i: (i, kb + h)),
                pl.BlockSpec((None, 1, HEAD_DIM), lambda h, i: (h, 0, 0)),
                pl.BlockSpec((None, 1, HEAD_DIM), lambda h, i: (h, 0, 0))]
    out = pl.BlockSpec((tr, HEAD_DIM), lambda h, i: (i, h))
    out_shape = [jax.ShapeDtypeStruct((L, heads * HEAD_DIM), BF)] * 2
    return _call(body, name=name, out_shape=out_shape, grid=(heads, L // tr), in_specs=in_specs,
                 out_specs=[out, out], sem=("parallel", "parallel"), vmem=16 << 20)(proj, proj, gq, gk)


def _qknorm_bwd(name, proj, dqn, dkn, gq, gk, heads, q_col, k_col, tr):
    L = proj.shape[0]

    def body(q_ref, k_ref, dqn_ref, dkn_ref, gq_ref, gk_ref, dq_ref, dk_ref, dgq_ref, dgk_ref):
        i = pl.program_id(1)
        for x_ref, dy_ref, g_ref, dx_ref, dg_ref in ((q_ref, dqn_ref, gq_ref, dq_ref, dgq_ref),
                                                     (k_ref, dkn_ref, gk_ref, dk_ref, dgk_ref)):
            xv = x_ref[...].astype(F32)
            r = lax.rsqrt(jnp.mean(xv * xv, axis=-1, keepdims=True) + RMS_EPS)
            xhat = xv * r
            dy = dy_ref[...].astype(F32)
            dxhat = dy * g_ref[...]
            dx_ref[...] = (r * (dxhat - xhat * jnp.mean(dxhat * xhat, axis=-1, keepdims=True))).astype(BF)

            @pl.when(i == 0)
            def _():
                dg_ref[...] = jnp.zeros_like(dg_ref)

            dg_ref[...] += jnp.sum(dy * xhat, axis=0, keepdims=True)

    qb, kb = q_col // HEAD_DIM, k_col // HEAD_DIM
    tile = pl.BlockSpec((tr, HEAD_DIM), lambda h, i: (i, h))
    gain = pl.BlockSpec((None, 1, HEAD_DIM), lambda h, i: (h, 0, 0))
    in_specs = [pl.BlockSpec((tr, HEAD_DIM), lambda h, i: (i, qb + h)),
                pl.BlockSpec((tr, HEAD_DIM), lambda h, i: (i, kb + h)), tile, tile, gain, gain]
    out_shape = [jax.ShapeDtypeStruct((L, heads * HEAD_DIM), BF)] * 2 + [
        jax.ShapeDtypeStruct((heads, 1, HEAD_DIM), F32)] * 2
    return _call(body, name=name, out_shape=out_shape, grid=(heads, L // tr), in_specs=in_specs,
                 out_specs=[tile, tile, gain, gain], sem=("parallel", "arbitrary"),
                 vmem=16 << 20)(proj, proj, dqn, dkn, gq, gk)


def _tri(cmp):
    r = lax.broadcasted_iota(I32, (LANES, LANES), 0)
    c = lax.broadcasted_iota(I32, (LANES, LANES), 1)
    return jnp.where(cmp(r, c), 1.0, 0.0).astype(BF)


def _cum_fwd(name, fl, bias, n_rows):
    H, nbp, _ = fl.shape

    def body(fl_ref, b_ref, c_ref, tot_ref):
        lf = _log_sigmoid(fl_ref[...] + b_ref[...])
        c_ref[...] = _split_dot(lf, _tri(lambda r, c: r <= c), 3)
        tot_ref[...] = _split_dot(lf, jnp.ones((LANES, LANES), BF), 3)

        def step(r, carry):
            c_ref[pl.ds(r, 1), :] = c_ref[pl.ds(r, 1), :] + carry
            return carry + tot_ref[pl.ds(r, 1), :]

        lax.fori_loop(0, n_rows, step, jnp.zeros((1, LANES), F32))

    blk = pl.BlockSpec((None, nbp, LANES), lambda h: (h, 0, 0))
    vec = pl.BlockSpec((None, 1, LANES), lambda h: (h, 0, 0))
    return _call(body, name=name, out_shape=jax.ShapeDtypeStruct((H, nbp, LANES), F32), grid=(H,),
                 in_specs=[blk, vec], out_specs=blk, scratch=[pltpu.VMEM((nbp, LANES), F32)], sem=("parallel",),
                 vmem=16 << 20)(fl, bias)


def _cum_bwd(name, drs, dcs, fl, bias, n_rows):
    H, nbp, _ = fl.shape

    def body(drs_ref, dcs_ref, fl_ref, b_ref, dfl_ref, db_ref, rin_ref, tot_ref):
        dc = drs_ref[...] - dcs_ref[...]
        rin_ref[...] = _split_dot(dc, _tri(lambda r, c: r >= c), 3)
        tot_ref[...] = _split_dot(dc, jnp.ones((LANES, LANES), BF), 3)
        dfl_ref[...] = jnp.zeros_like(dfl_ref)

        def step(t, carry):
            r = n_rows - 1 - t
            x = fl_ref[pl.ds(r, 1), :] + b_ref[...]
            dfl_ref[pl.ds(r, 1), :] = (rin_ref[pl.ds(r, 1), :] + carry) * _sigmoid(-x)
            return carry + tot_ref[pl.ds(r, 1), :]

        lax.fori_loop(0, n_rows, step, jnp.zeros((1, LANES), F32))
        db_ref[...] = jnp.zeros_like(db_ref) + jnp.sum(dfl_ref[...])

    blk = pl.BlockSpec((None, nbp, LANES), lambda h: (h, 0, 0))
    vec = pl.BlockSpec((None, 1, LANES), lambda h: (h, 0, 0))
    out_shape = [jax.ShapeDtypeStruct((H, nbp, LANES), F32), jax.ShapeDtypeStruct((H, 1, LANES), F32)]
    return _call(body, name=name, out_shape=out_shape, grid=(H,), in_specs=[blk, blk, blk, vec],
                 out_specs=[blk, vec], scratch=[pltpu.VMEM((nbp, LANES), F32)] * 2, sem=("parallel",),
                 vmem=16 << 20)(drs, dcs, fl, bias)


def _att_specs(L, G, q_col, k_col, v_col):
    T = ATT_TILE
    W = G * HEAD_DIM
    assert q_col % W == 0 and k_col % W == 0 and v_col % W == 0
    qb, kb, vb = q_col // W, k_col // W, v_col // W
    q_spec = pl.BlockSpec((T, W), lambda h, i: (i, qb + h))
    k_spec = pl.BlockSpec((L, W), lambda h, i: (0, kb + h), pipeline_mode=pl.Buffered(1))
    v_spec = pl.BlockSpec((L, W), lambda h, i: (0, vb + h), pipeline_mode=pl.Buffered(1))
    return q_spec, k_spec, v_spec


def _head_lanes(G):
    return [slice(g * HEAD_DIM, (g + 1) * HEAD_DIM) for g in range(G)]


def _tile_iotas():
    T = ATT_TILE
    return lax.broadcasted_iota(I32, (T, T), 0), lax.broadcasted_iota(I32, (T, T), 1)


def _rows(j):
    return pl.ds(pl.multiple_of(j * ATT_TILE, ATT_TILE), ATT_TILE)


def _fox_fwd(name, q_arr, k_arr, v_arr, c_row, c_col, heads, G, q_col, k_col, v_col):
    L = q_arr.shape[0]
    T = ATT_TILE
    scale = HEAD_DIM ** -0.5
    lanes = _head_lanes(G)

    def body(q_ref, k_ref, v_ref, crow_ref, ccol_ref, o_ref, lse_ref):
        i = pl.program_id(1)
        qs = [q_ref[:, hl] for hl in lanes]
        cts = [jnp.broadcast_to(ccol_ref[g], (T, T)) for g in range(G)]
        row, col = _tile_iotas()

        def tile(j, carry, masked):
            qk = [_dot(qs[g], k_ref[_rows(j), hl], 1, 1) for g, hl in enumerate(lanes)]
            stats = []
            for g in range(G):
                m, l, _ = carry[g]
                s = qk[g] * scale + (cts[g] - crow_ref[g, pl.ds(j, 1), :])
                if masked:
                    s = jnp.where(col <= row, s, -jnp.inf)
                m_new = jnp.maximum(m, jnp.max(s, axis=1, keepdims=True))
                alpha = jnp.exp(m - m_new)
                p = jnp.exp(s - m_new)
                stats.append((m_new, alpha, alpha * l + jnp.sum(p, axis=1, keepdims=True), p.astype(BF)))
            pv = [_dot(stats[g][3], v_ref[_rows(j), hl], 1, 0) for g, hl in enumerate(lanes)]
            return tuple((stats[g][0], stats[g][2], stats[g][1] * carry[g][2] + pv[g]) for g in range(G))

        init = tuple((jnp.full((T, 1), -1e30, F32), jnp.zeros((T, 1), F32), jnp.zeros((T, HEAD_DIM), F32))
                     for _ in range(G))
        carry = lax.fori_loop(0, i, lambda j, c: tile(j, c, False), init)
        for g, (m, l, acc) in enumerate(tile(i, carry, True)):
            o_ref[:, lanes[g]] = (acc / l).astype(o_ref.dtype)
            lse_ref[g] = m + jnp.log(l)

    nbp = c_row.shape[1]
    W = G * HEAD_DIM
    q_spec, k_spec, v_spec = _att_specs(L, G, q_col, k_col, v_col)
    crow_spec = pl.BlockSpec((G, nbp, LANES), lambda h, i: (h, 0, 0))
    col_spec = pl.BlockSpec((G, T, 1), lambda h, i: (h, i, 0))
    o_spec = pl.BlockSpec((T, W), lambda h, i: (i, h))
    out_shape = [jax.ShapeDtypeStruct((L, heads * HEAD_DIM), BF), jax.ShapeDtypeStruct((heads, L, 1), F32)]
    vmem = _vmem_limit([((L, W), BF)] * 2, temps=8 << 20)
    return _call(body, name=name, out_shape=out_shape, grid=(heads // G, L // T),
                 in_specs=[q_spec, k_spec, v_spec, crow_spec, col_spec], out_specs=[o_spec, col_spec],
                 sem=("parallel", "parallel"), vmem=vmem)(q_arr, k_arr, v_arr, c_row, c_col)


def _fox_bwd(name, q_arr, k_arr, v_arr, c_row, c_col, o, do, lse, heads, G, q_col, k_col, v_col):
    L = q_arr.shape[0]
    T = ATT_TILE
    nq = L // T
    scale = HEAD_DIM ** -0.5
    lanes = _head_lanes(G)

    def body(q_ref, k_ref, v_ref, crow_ref, ccol_ref, o_ref, do_ref, lse_ref, dq_ref, dk_ref, dv_ref, dcs_ref,
             drs_ref, dk_acc, dv_acc):
        i = pl.program_id(1)

        @pl.when(i == 0)
        def _():
            dk_acc[...] = jnp.zeros_like(dk_acc)
            dv_acc[...] = jnp.zeros_like(dv_acc)
            dcs_ref[...] = jnp.zeros_like(dcs_ref)

        qs = [q_ref[:, hl] for hl in lanes]
        dos = [do_ref[:, hl] for hl in lanes]
        q_ts = [qs[g].T for g in range(G)]
        do_ts = [dos[g].T for g in range(G)]
        deltas = [jnp.broadcast_to(jnp.sum(dos[g].astype(F32) * o_ref[:, hl].astype(F32), axis=1, keepdims=True),
                                   (T, T)) for g, hl in enumerate(lanes)]
        lses = [jnp.broadcast_to(lse_ref[g], (T, T)) for g in range(G)]
        cts = [jnp.broadcast_to(ccol_ref[g], (T, T)) for g in range(G)]
        row, col = _tile_iotas()

        def tile(j, carry, masked):
            ks = [k_ref[_rows(j), hl] for hl in lanes]
            qk = [_dot(qs[g], ks[g], 1, 1) for g in range(G)]
            dp = [_dot(dos[g], v_ref[_rows(j), hl], 1, 1) for g, hl in enumerate(lanes)]
            pbs, dsbs, row_sums = [], [], []
            for g in range(G):
                s = qk[g] * scale + (cts[g] - crow_ref[g, pl.ds(j, 1), :])
                if masked:
                    s = jnp.where(col <= row, s, -jnp.inf)
                p = jnp.exp(s - lses[g])
                ds = p * (dp[g] - deltas[g])
                dcs_ref[g, pl.ds(j, 1), :] += jnp.sum(ds, axis=0, keepdims=True)
                row_sums.append(carry[g][1] + jnp.sum(ds, axis=1, keepdims=True))
                pbs.append(p.astype(BF))
                dsbs.append((ds * scale).astype(BF))
            for g, hl in enumerate(lanes):
                dk_acc[j, hl, :] += _dot(q_ts[g], dsbs[g], 1, 0)
            for g, hl in enumerate(lanes):
                dv_acc[j, hl, :] += _dot(do_ts[g], pbs[g], 1, 0)
            return tuple((carry[g][0] + _dot(dsbs[g], ks[g], 1, 0), row_sums[g]) for g in range(G))

        init = tuple((jnp.zeros((T, HEAD_DIM), F32), jnp.zeros((T, 1), F32)) for _ in range(G))
        carry = lax.fori_loop(0, i, lambda j, c: tile(j, c, False), init)
        for g, (dq, row_sum) in enumerate(tile(i, carry, True)):
            dq_ref[:, lanes[g]] = dq.astype(dq_ref.dtype)
            drs_ref[g] = row_sum

        @pl.when(i == nq - 1)
        def _():
            for r in range(nq):
                for hl in lanes:
                    dk_ref[r * T:(r + 1) * T, hl] = dk_acc[r, hl, :].T.astype(dk_ref.dtype)
                    dv_ref[r * T:(r + 1) * T, hl] = dv_acc[r, hl, :].T.astype(dv_ref.dtype)

    nbp = c_row.shape[1]
    WG = G * HEAD_DIM
    q_spec, k_spec, v_spec = _att_specs(L, G, q_col, k_col, v_col)
    crow_spec = pl.BlockSpec((G, nbp, LANES), lambda h, i: (h, 0, 0))
    col_spec = pl.BlockSpec((G, T, 1), lambda h, i: (h, i, 0))
    t_spec = pl.BlockSpec((T, WG), lambda h, i: (i, h))
    head_spec = pl.BlockSpec((L, WG), lambda h, i: (0, h), pipeline_mode=pl.Buffered(1))
    W = heads * HEAD_DIM
    out_shape = [jax.ShapeDtypeStruct((L, W), F32)] * 3 + [jax.ShapeDtypeStruct((heads, nbp, LANES), F32),
                                                           jax.ShapeDtypeStruct((heads, L, 1), F32)]
    scratch = [pltpu.VMEM((nq, WG, T), F32)] * 2
    vmem = _vmem_limit([], [((L, WG), BF)] * 2 + [((L, WG), F32)] * 4, temps=8 << 20)
    return _call(body, name=name, out_shape=out_shape, grid=(heads // G, nq),
                 in_specs=[q_spec, k_spec, v_spec, crow_spec, col_spec, t_spec, t_spec, col_spec],
                 out_specs=[t_spec, head_spec, head_spec, crow_spec, col_spec], scratch=scratch,
                 sem=("parallel", "arbitrary"), vmem=vmem)(q_arr, k_arr, v_arr, c_row, c_col, o, do, lse)


def _sb_logits(qk, scale, valid):
    z = qk * scale
    lb = jnp.minimum(z, 0.0) - jnp.log1p(jnp.exp(-jnp.abs(z)))
    lom = lb - z
    if valid is not None:
        lom = jnp.where(valid, lom, 0.0)
    return lb, lom


def _sb_fwd(name, proj, heads, G, q_col, k_col, v_col):
    L = proj.shape[0]
    T = ATT_TILE
    scale = HEAD_DIM ** -0.5
    lanes = _head_lanes(G)

    def body(q_ref, k_ref, v_ref, o_ref):
        i = pl.program_id(1)
        qs = [q_ref[:, hl] for hl in lanes]
        row, col = _tile_iotas()
        later_mat = jnp.where(row > col, 1.0, 0.0).astype(BF)

        def tile(j, carry, masked):
            valid = (col < row) if masked else None
            qk = [_dot(qs[g], k_ref[_rows(j), hl], 1, 1) for g, hl in enumerate(lanes)]
            logits = [_sb_logits(qk[g], scale, valid) for g in range(G)]
            pieces = [_split(lom, 2) for _, lom in logits]
            later = [_pieces_dot(pieces[g], later_mat) for g in range(G)]
            ws = []
            for g in range(G):
                w = jnp.exp(logits[g][0] + later[g] + carry[g][0])
                if masked:
                    w = jnp.where(valid, w, 0.0)
                ws.append(w.astype(BF))
            wv = [_dot(ws[g], v_ref[_rows(j), hl], 1, 0) for g, hl in enumerate(lanes)]
            return tuple((carry[g][0] + jnp.sum(logits[g][1], axis=1, keepdims=True), carry[g][1] + wv[g])
                         for g in range(G))

        init = tuple((jnp.zeros((T, 1), F32), jnp.zeros((T, HEAD_DIM), F32)) for _ in range(G))
        carry = tile(i, init, True)
        carry = lax.fori_loop(0, i, lambda t, c: tile(i - 1 - t, c, False), carry)
        for g, (_, acc) in enumerate(carry):
            o_ref[:, lanes[g]] = acc.astype(o_ref.dtype)

    W = G * HEAD_DIM
    q_spec, k_spec, v_spec = _att_specs(L, G, q_col, k_col, v_col)
    o_spec = pl.BlockSpec((T, W), lambda h, i: (i, h))
    vmem = _vmem_limit([((L, W), BF)] * 2, temps=8 << 20)
    return _call(body, name=name, out_shape=jax.ShapeDtypeStruct((L, heads * HEAD_DIM), BF),
                 grid=(heads // G, L // T), in_specs=[q_spec, k_spec, v_spec], out_specs=o_spec,
                 sem=("parallel", "parallel"), vmem=vmem)(proj, proj, proj)


def _sb_bwd(name, proj, do, heads, G, q_col, k_col, v_col):
    L = proj.shape[0]
    T = ATT_TILE
    nq = L // T
    scale = HEAD_DIM ** -0.5
    lanes = _head_lanes(G)

    def body(q_ref, k_ref, v_ref, do_ref, dq_ref, dk_acc, dv_acc, da_buf, beta_buf):
        i = pl.program_id(1)

        @pl.when(i == 0)
        def _():
            dk_acc[...] = jnp.zeros_like(dk_acc)
            dv_acc[...] = jnp.zeros_like(dv_acc)

        qs = [q_ref[:, hl] for hl in lanes]
        dos = [do_ref[:, hl] for hl in lanes]
        q_ts = [qs[g].T for g in range(G)]
        do_ts = [dos[g].T for g in range(G)]
        row, col = _tile_iotas()
        later_mat = jnp.where(row > col, 1.0, 0.0).astype(BF)
        before_mat = jnp.where(row < col, 1.0, 0.0).astype(BF)

        def pass1(j, runs, masked):
            valid = (col < row) if masked else None
            qk = [_dot(qs[g], k_ref[_rows(j), hl], 1, 1) for g, hl in enumerate(lanes)]
            dw = [_dot(dos[g], v_ref[_rows(j), hl], 1, 1) for g, hl in enumerate(lanes)]
            logits = [_sb_logits(qk[g], scale, valid) for g in range(G)]
            pieces = [_split(lom, 2) for _, lom in logits]
            later = [_pieces_dot(pieces[g], later_mat) for g in range(G)]
            ws = []
            for g in range(G):
                w = jnp.exp(logits[g][0] + later[g] + runs[g])
                if masked:
                    w = jnp.where(valid, w, 0.0)
                da_buf[g * nq + j] = dw[g] * w
                beta_buf[g * nq + j] = jnp.exp(logits[g][0])
                ws.append(w.astype(BF))
            for g, hl in enumerate(lanes):
                dv_acc[j, hl, :] += _dot(do_ts[g], ws[g], 1, 0)
            return tuple(runs[g] + jnp.sum(logits[g][1], axis=1, keepdims=True) for g in range(G))

        runs = pass1(i, tuple(jnp.zeros((T, 1), F32) for _ in range(G)), True)
        lax.fori_loop(0, i, lambda t, c: pass1(i - 1 - t, c, False), runs)

        def pass2(j, carry, masked):
            das = [da_buf[g * nq + j] for g in range(G)]
            pieces = [_split(da, 2) for da in das]
            before = [_pieces_dot(pieces[g], before_mat) for g in range(G)]
            dzbs = []
            for g in range(G):
                beta = beta_buf[g * nq + j]
                dz = das[g] * (1.0 - beta) - (carry[g][0] + before[g]) * beta
                if masked:
                    dz = jnp.where(col < row, dz, 0.0)
                dzbs.append((dz * scale).astype(BF))
            for g, hl in enumerate(lanes):
                dk_acc[j, hl, :] += _dot(q_ts[g], dzbs[g], 1, 0)
            dq = [_dot(dzbs[g], k_ref[_rows(j), hl], 1, 0) for g, hl in enumerate(lanes)]
            return tuple((carry[g][0] + jnp.sum(das[g], axis=1, keepdims=True), carry[g][1] + dq[g])
                         for g in range(G))

        init = tuple((jnp.zeros((T, 1), F32), jnp.zeros((T, HEAD_DIM), F32)) for _ in range(G))
        carry = lax.fori_loop(0, i, lambda j, c: pass2(j, c, False), init)
        for g, (_, dq) in enumerate(pass2(i, carry, True)):
            dq_ref[:, lanes[g]] = dq.astype(dq_ref.dtype)

    WG = G * HEAD_DIM
    q_spec, k_spec, v_spec = _att_specs(L, G, q_col, k_col, v_col)
    t_spec = pl.BlockSpec((T, WG), lambda h, i: (i, h))
    head_spec = pl.BlockSpec((nq, WG, T), lambda h, i: (0, h, 0), pipeline_mode=pl.Buffered(1))
    W = heads * HEAD_DIM
    out_shape = [jax.ShapeDtypeStruct((L, W), BF)] + [jax.ShapeDtypeStruct((nq, W, T), F32)] * 2
    scratch = [pltpu.VMEM((G * nq, T, T), F32)] * 2
    vmem = _vmem_limit([], [((L, WG), BF)] * 2 + [((L, WG), F32)] * 2 + [((G * nq, T, T), F32)] * 2,
                       temps=6 << 20)
    return _call(body, name=name, out_shape=out_shape, grid=(heads // G, nq),
                 in_specs=[q_spec, k_spec, v_spec, t_spec], out_specs=[t_spec, head_spec, head_spec],
                 scratch=scratch, sem=("parallel", "arbitrary"), vmem=vmem)(proj, proj, proj, do)


_ANY = pl.BlockSpec(memory_space=pl.ANY)


def _mesh_pos():
    return lax.axis_index("x"), lax.axis_index("y"), lax.axis_index("c")


def _other_chips(x, y):
    return [(1 - x, y), (x, 1 - y), (1 - x, 1 - y)]


def _shard_window(ref, kind, sidx, r0, nr, cs):
    if kind == "col":
        assert cs % LANES == 0
        return ref.at[pl.ds(r0, nr), pl.ds(pl.multiple_of(sidx * cs, LANES), cs)]
    return ref.at[sidx, pl.ds(r0, nr), :]


def _place_shard(name, pos, shard, kind, dtype):
    R, C = shard.shape
    tr = _row_tile(R, C, 2, 16)

    def body(pos_ref, s_ref, o_ref):
        o_ref[...] = s_ref[...].astype(o_ref.dtype)

    if kind == "col":
        assert C % LANES == 0
        out_shape = jax.ShapeDtypeStruct((R, N_CHIPS * C), dtype)
        out_spec = pl.BlockSpec((tr, C), lambda r, pos_ref: (r, pos_ref[1]))
    else:
        out_shape = jax.ShapeDtypeStruct((N_CHIPS, R, C), dtype)
        out_spec = pl.BlockSpec((None, tr, C), lambda r, pos_ref: (pos_ref[1], r, 0))
    return _call(body, name=name, out_shape=out_shape, grid=(R // tr,),
                 in_specs=[pl.BlockSpec((tr, C), lambda r, pos_ref: (r, 0))], out_specs=out_spec, sem=("parallel",),
                 vmem=32 << 20, prefetch=1)(pos, shard)


_HBM = pl.BlockSpec(memory_space=pltpu.HBM)
_SEM = pl.BlockSpec(memory_space=pltpu.SEMAPHORE)
_KEEP_ORDER = pltpu.SideEffectType.DATAFLOW_SIDE_EFFECTING


def _in_hbm(x):
    return pltpu.with_memory_space_constraint(x, pltpu.HBM)


def _gather_copies(bufs, meta, send_sem, recv_sem):
    x, y, c = _mesh_pos()
    out = []
    for t, (kind, R, cs) in enumerate(meta):
        half = R // 2
        r0 = pl.multiple_of(c * half, SUBLANES)
        mine = _shard_window(bufs[t], kind, 2 * x + y, r0, half, cs)
        for p, (px, py) in enumerate(_other_chips(x, y)):
            k = 3 * t + p
            args = dict(send_sem=send_sem.at[k], recv_sem=recv_sem.at[k], device_id=(px, py, c),
                        device_id_type=MESH_IDS)
            out.append((pltpu.make_async_remote_copy(src_ref=mine, dst_ref=mine, **args),
                        pltpu.make_async_remote_copy(
                            src_ref=mine, dst_ref=_shard_window(bufs[t], kind, 2 * px + py, r0, half, cs), **args)))
    return out


def _all_gather_ici_start(name, gathered, shards, kinds, after):
    n = len(gathered)
    meta = [(kind, s.shape[0], s.shape[1]) for s, kind in zip(shards, kinds)]

    def body(*refs):
        send_sem, recv_sem = refs[n + 1], refs[n + 2]
        bufs = refs[n + 3:2 * n + 3]
        token = refs[2 * n + 3]
        for send, _ in _gather_copies(bufs, meta, send_sem, recv_sem):
            send.start()
        token[...] = jnp.zeros_like(token)

    out_shape = (pltpu.SemaphoreType.DMA((3 * n,)), pltpu.SemaphoreType.DMA((3 * n,)),
                 *[pltpu.HBM(g.shape, g.dtype) for g in gathered], jax.ShapeDtypeStruct((SUBLANES, LANES), F32))
    out_specs = (_SEM, _SEM, *[_HBM] * n, pl.BlockSpec(memory_space=pltpu.VMEM))
    res = pl.pallas_call(body, out_shape=out_shape, in_specs=[_HBM] * n + [_ANY], out_specs=out_specs,
                         input_output_aliases={t: 2 + t for t in range(n)}, name=name,
                         compiler_params=pltpu.CompilerParams(has_side_effects=_KEEP_ORDER))(
        *[_in_hbm(g) for g in gathered], after)
    return res[0], res[1], list(res[2:2 + n]), res[2 + n]


def _all_gather_ici_wait(name, bufs, send_sem, recv_sem, after, shards, kinds):
    n = len(bufs)
    meta = [(kind, s.shape[0], s.shape[1]) for s, kind in zip(shards, kinds)]
    after = list(after) if isinstance(after, (list, tuple)) else [after]

    def body(*refs):
        for send, recv in _gather_copies(refs[:n], meta, refs[n], refs[n + 1]):
            send.wait_send()
            recv.wait_recv()

    out_shape = tuple(pltpu.HBM(b.shape, b.dtype) for b in bufs)
    res = pl.pallas_call(body, out_shape=out_shape, in_specs=[_HBM] * n + [_SEM, _SEM] + [_ANY] * len(after),
                         out_specs=tuple([_HBM] * n), input_output_aliases={t: t for t in range(n)}, name=name,
                         compiler_params=pltpu.CompilerParams(has_side_effects=_KEEP_ORDER))(
        *bufs, send_sem, recv_sem, *after)
    return list(res)


def _all_gather_d2d(name, gathered, shards, kinds):
    n = len(gathered)
    meta = [(kind, s.shape[0], s.shape[1]) for s, kind in zip(shards, kinds)]

    def body(*refs):
        bufs = refs[n:2 * n]
        send_sem, recv_sem = refs[2 * n:]
        x, y, c = _mesh_pos()
        sends, recvs = [], []
        for t, (kind, R, cs) in enumerate(meta):
            half = R // 2
            mine = pl.multiple_of(c * half, SUBLANES)
            theirs = pl.multiple_of((1 - c) * half, SUBLANES)
            for p, (px, py) in enumerate(_other_chips(x, y)):
                k = 3 * t + p
                win = _shard_window(bufs[t], kind, 2 * px + py, mine, half, cs)
                cp = pltpu.make_async_remote_copy(src_ref=win, dst_ref=win, send_sem=send_sem.at[k],
                                                  recv_sem=recv_sem.at[k], device_id=(x, y, 1 - c),
                                                  device_id_type=MESH_IDS)
                cp.start()
                sends.append(cp)
                got = _shard_window(bufs[t], kind, 2 * px + py, theirs, half, cs)
                recvs.append(pltpu.make_async_remote_copy(src_ref=win, dst_ref=got, send_sem=send_sem.at[k],
                                                          recv_sem=recv_sem.at[k], device_id=(x, y, 1 - c),
                                                          device_id_type=MESH_IDS))
        for cp in recvs:
            cp.wait_recv()
        for cp in sends:
            cp.wait_send()

    scratch = [pltpu.SemaphoreType.DMA((3 * n,)), pltpu.SemaphoreType.DMA((3 * n,))]
    out_shape = [jax.ShapeDtypeStruct(g.shape, g.dtype) for g in gathered]
    return _call(body, name=name, out_shape=out_shape, in_specs=[_ANY] * n, out_specs=[_ANY] * n, scratch=scratch,
                 aliases={t: t for t in range(n)})(*gathered)


def _sibling_copies(grads, lands, send_sem, recv_sem):
    x, y, c = _mesh_pos()
    out = []
    for t in range(len(grads)):
        half = lands[t].shape[1]
        theirs = pl.multiple_of((1 - c) * half, SUBLANES)
        out.append(pltpu.make_async_remote_copy(src_ref=grads[t].at[:, pl.ds(theirs, half), :], dst_ref=lands[t],
                                                send_sem=send_sem.at[t], recv_sem=recv_sem.at[t],
                                                device_id=(x, y, 1 - c), device_id_type=MESH_IDS))
    return out


def _sibling_start(name, grads):
    n = len(grads)
    lands = [lax.empty((N_CHIPS, g.shape[1] // 2, g.shape[2]), g.dtype) for g in grads]

    def body(*refs):
        send_sem, recv_sem = refs[2 * n], refs[2 * n + 1]
        srcs, zones = refs[2 * n + 2:3 * n + 2], refs[3 * n + 2:4 * n + 2]
        token = refs[4 * n + 2]
        for cp in _sibling_copies(srcs, zones, send_sem, recv_sem):
            cp.start()
        token[...] = jnp.zeros_like(token)

    out_shape = (pltpu.SemaphoreType.DMA((n,)), pltpu.SemaphoreType.DMA((n,)),
                 *[pltpu.HBM(a.shape, a.dtype) for a in list(grads) + lands],
                 jax.ShapeDtypeStruct((SUBLANES, LANES), F32))
    out_specs = (_SEM, _SEM, *[_HBM] * (2 * n), pl.BlockSpec(memory_space=pltpu.VMEM))
    res = pl.pallas_call(body, out_shape=out_shape, in_specs=[_HBM] * (2 * n), out_specs=out_specs,
                         input_output_aliases={t: 2 + t for t in range(2 * n)}, name=name,
                         compiler_params=pltpu.CompilerParams(has_side_effects=_KEEP_ORDER))(
        *[_in_hbm(a) for a in list(grads) + lands])
    return res[0], res[1], list(res[2:2 + n]), list(res[2 + n:2 + 2 * n]), res[2 + 2 * n]


def _sibling_wait(name, grads, lands, send_sem, recv_sem, after):
    n = len(grads)
    after = list(after) if isinstance(after, (list, tuple)) else [after]

    def body(*refs):
        for cp in _sibling_copies(refs[:n], refs[n:2 * n], refs[2 * n], refs[2 * n + 1]):
            cp.wait_send()
            cp.wait_recv()

    out_shape = tuple(pltpu.HBM(a.shape, a.dtype) for a in list(grads) + list(lands))
    res = pl.pallas_call(body, out_shape=out_shape, in_specs=[_HBM] * (2 * n) + [_SEM, _SEM] + [_ANY] * len(after),
                         out_specs=tuple([_HBM] * (2 * n)), input_output_aliases={t: t for t in range(2 * n)},
                         name=name, compiler_params=pltpu.CompilerParams(has_side_effects=_KEEP_ORDER))(
        *grads, *lands, send_sem, recv_sem, *after)
    return list(res[:n]), list(res[n:])


def _exchange_copies(parts, lands, send_sem, recv_sem):
    x, y, c = _mesh_pos()
    out = []
    for t in range(len(parts)):
        for p, (px, py) in enumerate(_other_chips(x, y)):
            k = 3 * t + p
            out.append(pltpu.make_async_remote_copy(src_ref=parts[t].at[2 * px + py], dst_ref=lands[t].at[p],
                                                    send_sem=send_sem.at[k], recv_sem=recv_sem.at[k],
                                                    device_id=(px, py, c), device_id_type=MESH_IDS))
    return out


def _exchange_start(name, partials):
    n = len(partials)
    lands = [lax.empty((3,) + p.shape[1:], p.dtype) for p in partials]

    def body(*refs):
        send_sem, recv_sem = refs[2 * n], refs[2 * n + 1]
        parts, zones = refs[2 * n + 2:3 * n + 2], refs[3 * n + 2:4 * n + 2]
        token = refs[4 * n + 2]
        for cp in _exchange_copies(parts, zones, send_sem, recv_sem):
            cp.start()
        token[...] = jnp.zeros_like(token)

    out_shape = (pltpu.SemaphoreType.DMA((3 * n,)), pltpu.SemaphoreType.DMA((3 * n,)),
                 *[pltpu.HBM(a.shape, a.dtype) for a in partials + lands],
                 jax.ShapeDtypeStruct((SUBLANES, LANES), F32))
    out_specs = (_SEM, _SEM, *[_HBM] * (2 * n), pl.BlockSpec(memory_space=pltpu.VMEM))
    res = pl.pallas_call(body, out_shape=out_shape, in_specs=[_HBM] * (2 * n), out_specs=out_specs,
                         input_output_aliases={t: 2 + t for t in range(2 * n)}, name=name,
                         compiler_params=pltpu.CompilerParams(has_side_effects=_KEEP_ORDER))(
        *[_in_hbm(a) for a in partials + lands])
    return res[0], res[1], list(res[2:2 + n]), list(res[2 + n:2 + 2 * n]), res[2 + 2 * n]


def _exchange_wait(name, parts, lands, send_sem, recv_sem, after):
    n = len(parts)

    def body(*refs):
        for cp in _exchange_copies(refs[:n], refs[n:2 * n], refs[2 * n], refs[2 * n + 1]):
            cp.wait_send()
            cp.wait_recv()

    out_shape = tuple(pltpu.HBM(a.shape, a.dtype) for a in parts + lands)
    res = pl.pallas_call(body, out_shape=out_shape, in_specs=[_HBM] * (2 * n) + [_SEM, _SEM, _ANY],
                         out_specs=tuple([_HBM] * (2 * n)), input_output_aliases={t: t for t in range(2 * n)},
                         name=name, compiler_params=pltpu.CompilerParams(has_side_effects=_KEEP_ORDER))(
        *parts, *lands, send_sem, recv_sem, after)
    return list(res[n:])


def _share_halves(name, totals):
    n = len(totals)

    def body(*refs):
        bufs = refs[n:2 * n]
        send_sem, recv_sem = refs[2 * n:]
        x, y, c = _mesh_pos()
        sends, recvs = [], []
        for t, g in enumerate(totals):
            half = g.shape[0] // 2
            mine = bufs[t].at[pl.ds(pl.multiple_of(c * half, SUBLANES), half), :]
            theirs = bufs[t].at[pl.ds(pl.multiple_of((1 - c) * half, SUBLANES), half), :]
            cp = pltpu.make_async_remote_copy(src_ref=mine, dst_ref=mine, send_sem=send_sem.at[t],
                                              recv_sem=recv_sem.at[t], device_id=(x, y, 1 - c),
                                              device_id_type=MESH_IDS)
            cp.start()
            sends.append(cp)
            recvs.append(pltpu.make_async_remote_copy(src_ref=mine, dst_ref=theirs, send_sem=send_sem.at[t],
                                                      recv_sem=recv_sem.at[t], device_id=(x, y, 1 - c),
                                                      device_id_type=MESH_IDS))
        for cp in recvs:
            cp.wait_recv()
        for cp in sends:
            cp.wait_send()

    out_shape = [jax.ShapeDtypeStruct(g.shape, g.dtype) for g in totals]
    scratch = [pltpu.SemaphoreType.DMA((n,)), pltpu.SemaphoreType.DMA((n,))]
    return _call(body, name=name, out_shape=out_shape, in_specs=[_ANY] * n, out_specs=[_ANY] * n, scratch=scratch,
                 aliases={t: t for t in range(n)})(*totals)


def _gather_small(name, v):
    def body(v_ref, out_ref, send_sem, recv_sem, local_sem):
        x, y, c = _mesh_pos()
        me = 4 * x + 2 * y + c
        local = pltpu.make_async_copy(v_ref, out_ref.at[me], local_sem)
        local.start()
        sends, recvs = [], []
        for k in range(1, N_DEV):
            px = 1 - x if k & 4 else x
            py = 1 - y if k & 2 else y
            pc = 1 - c if k & 1 else c
            cp = pltpu.make_async_remote_copy(src_ref=v_ref, dst_ref=out_ref.at[me], send_sem=send_sem.at[k],
                                              recv_sem=recv_sem.at[k], device_id=(px, py, pc),
                                              device_id_type=MESH_IDS)
            cp.start()
            sends.append(cp)
            recvs.append(pltpu.make_async_remote_copy(
                src_ref=v_ref, dst_ref=out_ref.at[4 * px + 2 * py + pc], send_sem=send_sem.at[k],
                recv_sem=recv_sem.at[k], device_id=(px, py, pc), device_id_type=MESH_IDS))
        for cp in recvs:
            cp.wait_recv()
        for cp in sends:
            cp.wait_send()
        local.wait()

    scratch = [pltpu.SemaphoreType.DMA((N_DEV,)), pltpu.SemaphoreType.DMA((N_DEV,)), pltpu.SemaphoreType.DMA(())]
    return _call(body, name=name, out_shape=jax.ShapeDtypeStruct((N_DEV,) + v.shape, v.dtype), in_specs=[_ANY],
                 out_specs=_ANY, scratch=scratch)(v)


def _row_tile(rows, cols, n_arrays, mult=SUBLANES):
    budget = (24 << 20) // (2 * n_arrays * _round_up(cols, LANES) * 4)
    for t in (512, 256, 128, 64, 32, 16, 8):
        if t <= max(budget, mult) and rows % t == 0 and t % mult == 0:
            return t
    raise ValueError(f"no row tile for {rows} x {cols}")


def _chip_partial(name, pos, own, recv):
    _, half, C = recv.shape
    tr = _row_tile(half, C, 3, 16)
    nh = half // tr

    def body(pos_ref, own_ref, recv_ref, out_ref):
        out_ref[...] = (own_ref[...] + recv_ref[...]).astype(BF)

    blk = pl.BlockSpec((None, tr, C), lambda s, r, pos_ref: (s, r, 0))
    own_blk = pl.BlockSpec((None, tr, C), lambda s, r, pos_ref: (s, pos_ref[0] * nh + r, 0))
    return _call(body, name=name, out_shape=jax.ShapeDtypeStruct(recv.shape, BF), grid=(N_CHIPS, nh),
                 in_specs=[own_blk, blk], out_specs=blk, sem=("parallel", "parallel"), vmem=40 << 20,
                 prefetch=1)(pos, own, recv)


def _final_half(name, pos, own, recv, others):
    _, half, C = recv.shape
    tr = _row_tile(half, C, 4, 16)
    nh = half // tr

    def body(pos_ref, own_ref, recv_ref, oth_ref, out_ref):
        acc = own_ref[...] + recv_ref[...]
        for p in range(3):
            acc = acc + oth_ref[p].astype(F32)
        out_ref[...] = acc

    own_blk = pl.BlockSpec((None, tr, C), lambda r, pos_ref: (pos_ref[1], pos_ref[0] * nh + r, 0))
    recv_blk = pl.BlockSpec((None, tr, C), lambda r, pos_ref: (pos_ref[1], r, 0))
    oth_blk = pl.BlockSpec((3, tr, C), lambda r, pos_ref: (0, r, 0))
    out_blk = pl.BlockSpec((tr, C), lambda r, pos_ref: (pos_ref[0] * nh + r, 0))
    return _call(body, name=name, out_shape=jax.ShapeDtypeStruct((2 * half, C), F32), grid=(nh,),
                 in_specs=[own_blk, recv_blk, oth_blk], out_specs=out_blk, sem=("parallel",), vmem=40 << 20,
                 prefetch=1)(pos, own, recv, others)


def _adamw(name, w, g, m, v):
    R, C = w.shape
    tr = R if R < SUBLANES or R % SUBLANES else _row_tile(R, C, 7)
    c1 = 1.0 - ADAM_B1 ** ADAM_STEP
    c2 = 1.0 - ADAM_B2 ** ADAM_STEP

    def body(w_ref, g_ref, m_ref, v_ref, d_ref, nm_ref, nv_ref):
        gv = g_ref[...]
        nm = ADAM_B1 * m_ref[...] + (1.0 - ADAM_B1) * gv
        nv = ADAM_B2 * v_ref[...] + (1.0 - ADAM_B2) * (gv * gv)
        d_ref[...] = -ADAM_LR * ((nm / c1) / (jnp.sqrt(nv / c2) + ADAM_EPS) + ADAM_WD * w_ref[...])
        nm_ref[...] = nm
        nv_ref[...] = nv

    blk = pl.BlockSpec((tr, C), lambda r: (r, 0))
    out_shape = [jax.ShapeDtypeStruct((R, C), F32)] * 3
    return _call(body, name=name, out_shape=out_shape, grid=(R // tr,), in_specs=[blk] * 4, out_specs=[blk] * 3,
                 sem=("parallel",), vmem=40 << 20)(w, g, m, v)


def _sum_devices(name, gathered):
    _, R, _ = gathered.shape

    def body(g_ref, o_ref):
        acc = g_ref[0]
        for d in range(1, N_DEV):
            acc = acc + g_ref[d]
        o_ref[...] = acc

    return _call(body, name=name, out_shape=jax.ShapeDtypeStruct((R, LANES), F32), grid=(1,),
                 in_specs=[pl.BlockSpec((N_DEV, R, LANES), lambda i: (0, 0, 0))],
                 out_specs=pl.BlockSpec((R, LANES), lambda i: (0, 0)), sem=("arbitrary",), vmem=16 << 20)(gathered)


def _ffn_fwd(tag, h, gain, wg, wu, wd, tm):
    L, D = h.shape
    F = wg.shape[1]
    tn = _pick(F, (512, 256, 128))
    n = _rmsnorm_fwd(f"{tag}_norm", h, gain, tm)

    def gate_up(accs, _):
        a, u = accs
        return a, u, a * _sigmoid(a) * u

    mn = _mn(tm, tn)
    a, u, s = _matmul(f"{tag}_gate_up", [n], [wg, wu], [(0, 0, 0), (0, 1, 1)], 2, gate_up,
                      [((L, F), BF) + mn] * 3, M=L, N=F, K=D, tm=tm, tn=tn, tk=D, n_outer=True)
    if callable(wd):
        wd = wd(s)
    td = _pick(D, (512, 256, 128))
    (h_out,) = _matmul(f"{tag}_down", [s], [wd], [(0, 0, 0)], 1,
                       lambda accs, ex: [ex[0] + FFN_RESIDUAL_WEIGHT * accs[0]], [((L, D), F32) + _mn(tm, td)],
                       M=L, N=D, K=F, tm=tm, tn=td, tk=F, extras=[(h,) + _mn(tm, td)], n_outer=True)
    return h_out, (h, n, a, u, s), wd


def _ffn_bwd(tag, dh, dh_bf, saved, gain, wg, wu, wd, tm, cs_ff, after=None, midway=None, on_weight_grads=None):
    h, n, a, u, s = saved
    L, D = h.shape
    F = wg.shape[1]
    tn = _pick(F, (512, 256, 128))
    tkl = _pick(L, (1408, 384, 256, 128))

    def act_grad(accs, ex):
        ds = FFN_RESIDUAL_WEIGHT * accs[0]
        av, uv = ex[0].astype(F32), ex[1].astype(F32)
        sg = _sigmoid(av)
        return ds * uv * sg * (1.0 + av * (1.0 - sg)), ds * av * sg

    mn = _mn(tm, tn)
    da, du = _matmul(f"{tag}_dact", [dh_bf], [wd], [(0, 0, 0)], 1, act_grad, [((L, F), BF) + mn] * 2, M=L, N=F,
                     K=D, tm=tm, tn=tn, tk=D, tb=True, extras=[(a,) + mn, (u,) + mn], n_outer=True, after=after)
    td = _pick(D, (512, 256, 128))
    (dwd,) = _matmul(f"{tag}_dwd", [s], [dh_bf], [(0, 0, 0)], 1, lambda accs, _: [FFN_RESIDUAL_WEIGHT * accs[0]],
                     [((N_CHIPS, cs_ff, D), F32, (None, cs_ff, td), lambda i, j: (i, 0, j))], M=F, N=D, K=L,
                     tm=cs_ff, tn=td, tk=tkl, ta=True, after=after)
    tmw = _pick(D, (512, 256, 128))
    shard_out = ((N_CHIPS, D, cs_ff), F32, (None, tmw, cs_ff), lambda i, j: (j, i, 0))
    dwg, dwu = _matmul(f"{tag}_dwgu", [n], [da, du], [(0, 0, 0), (0, 1, 1)], 2, lambda accs, _: accs,
                       [shard_out] * 2, M=D, N=F, K=L, tm=tmw, tn=cs_ff, tk=tkl, ta=True,
                       after=midway(dwd) if midway else None)
    started = on_weight_grads(dwg, dwu, dwd) if on_weight_grads else None
    tdn = _pick(D, (256, 128))
    (dn,) = _matmul(f"{tag}_dn", [da, du], [wg, wu], [(0, 0, 0), (1, 1, 0)], 1, lambda accs, _: accs,
                    [((L, D), F32) + _mn(tm, tdn)], M=L, N=D, K=F, tm=tm, tn=tdn, tk=F, tb=True, after=started)
    dh_in, dh_in_bf, dgain = _rmsnorm_bwd(f"{tag}_dnorm", dn, h, gain, dh, tm)
    return dh_in, dh_in_bf, dgain


def kernel(x, meta_tokens, ffn1_norm, ffn1_w_gate, ffn1_w_up, ffn1_w_down, mix_norm, w_in, b_forget, fox_q_norm, fox_k_norm, w_branch_fox, w_branch_sb, w_out, ffn2_norm, ffn2_w_gate, ffn2_w_up, ffn2_w_down, loss_target, m_meta_tokens, m_ffn1_norm, m_ffn1_w_gate, m_ffn1_w_up, m_ffn1_w_down, m_mix_norm, m_w_in, m_b_forget, m_fox_q_norm, m_fox_k_norm, m_w_branch_fox, m_w_branch_sb, m_w_out, m_ffn2_norm, m_ffn2_w_gate, m_ffn2_w_up, m_ffn2_w_down, v_meta_tokens, v_ffn1_norm, v_ffn1_w_gate, v_ffn1_w_up, v_ffn1_w_down, v_mix_norm, v_w_in, v_b_forget, v_fox_q_norm, v_fox_k_norm, v_w_branch_fox, v_w_branch_sb, v_w_out, v_ffn2_norm, v_ffn2_w_gate, v_ffn2_w_up, v_ffn2_w_down):
    weights = dict(meta_tokens=meta_tokens, ffn1_norm=ffn1_norm, ffn1_w_gate=ffn1_w_gate, ffn1_w_up=ffn1_w_up,
                   ffn1_w_down=ffn1_w_down, mix_norm=mix_norm, w_in=w_in, b_forget=b_forget, fox_q_norm=fox_q_norm,
                   fox_k_norm=fox_k_norm, w_branch_fox=w_branch_fox, w_branch_sb=w_branch_sb, w_out=w_out,
                   ffn2_norm=ffn2_norm, ffn2_w_gate=ffn2_w_gate, ffn2_w_up=ffn2_w_up, ffn2_w_down=ffn2_w_down)
    moments_m = dict(meta_tokens=m_meta_tokens, ffn1_norm=m_ffn1_norm, ffn1_w_gate=m_ffn1_w_gate,
                     ffn1_w_up=m_ffn1_w_up, ffn1_w_down=m_ffn1_w_down, mix_norm=m_mix_norm, w_in=m_w_in,
                     b_forget=m_b_forget, fox_q_norm=m_fox_q_norm, fox_k_norm=m_fox_k_norm,
                     w_branch_fox=m_w_branch_fox, w_branch_sb=m_w_branch_sb, w_out=m_w_out, ffn2_norm=m_ffn2_norm,
                     ffn2_w_gate=m_ffn2_w_gate, ffn2_w_up=m_ffn2_w_up, ffn2_w_down=m_ffn2_w_down)
    moments_v = dict(meta_tokens=v_meta_tokens, ffn1_norm=v_ffn1_norm, ffn1_w_gate=v_ffn1_w_gate,
                     ffn1_w_up=v_ffn1_w_up, ffn1_w_down=v_ffn1_w_down, mix_norm=v_mix_norm, w_in=v_w_in,
                     b_forget=v_b_forget, fox_q_norm=v_fox_q_norm, fox_k_norm=v_fox_k_norm,
                     w_branch_fox=v_w_branch_fox, w_branch_sb=v_w_branch_sb, w_out=v_w_out, ffn2_norm=v_ffn2_norm,
                     ffn2_w_gate=v_ffn2_w_gate, ffn2_w_up=v_ffn2_w_up, ffn2_w_down=v_ffn2_w_down)
    names = list(weights)

    _, S, D = x.shape
    NM = meta_tokens.shape[0]
    L_real = NM + S
    L = _round_up(L_real, ATT_TILE)
    nblk = L // ATT_TILE
    nbp = _round_up(nblk, SUBLANES)
    cs_ff = ffn1_w_gate.shape[2]
    F = N_CHIPS * cs_ff
    H = b_forget.shape[1]
    FW = w_branch_fox.shape[1]
    SW = w_branch_sb.shape[1]
    HS = SW // HEAD_DIM
    cs_in = w_in.shape[2]
    W_IN = N_CHIPS * cs_in
    assert FW == H * HEAD_DIM and W_IN == 3 * FW + H + 3 * SW + 2 * D
    cs_d = D // N_CHIPS
    tm = _pick(L, (384, 256, 128))

    x_pos, y_pos, c_pos = _mesh_pos()
    pos = jnp.stack([c_pos, 2 * x_pos + y_pos]).astype(I32)

    shard_of = {
        "ffn1_w_gate": (ffn1_w_gate[0], "col"), "ffn1_w_up": (ffn1_w_up[0], "col"),
        "ffn1_w_down": (ffn1_w_down[0], "maj"), "w_in": (w_in[0], "maj"),
        "w_branch_fox": (w_branch_fox[0], "col"), "w_branch_sb": (w_branch_sb[0], "col"),
        "w_out": (w_out[0], "maj"), "ffn2_w_gate": (ffn2_w_gate[0], "col"), "ffn2_w_up": (ffn2_w_up[0], "col"),
        "ffn2_w_down": (ffn2_w_down[0], "maj"),
    }
    g_names = list(shard_of) + ["meta_tokens"]
    shards = [shard_of[k][0] for k in shard_of] + [meta_tokens]
    kinds = [shard_of[k][1] for k in shard_of] + ["col"]
    dtypes = [BF] * len(shard_of) + [F32]
    info = {k: (s, kind) for k, s, kind in zip(g_names, shards, kinds)}
    placed = {k: _place_shard(f"place_{k}", pos, s, kind, dt)
              for k, s, kind, dt in zip(g_names, shards, kinds, dtypes)}
    groups = [["meta_tokens", "ffn1_w_gate", "ffn1_w_up"], ["ffn1_w_down"], ["w_in"],
              ["w_branch_fox", "w_branch_sb", "w_out", "ffn2_w_gate", "ffn2_w_up", "ffn2_w_down"]]
    in_flight = []
    all_started = jnp.zeros((SUBLANES, LANES), F32)
    for gi, grp in enumerate(groups):
        g_shards, g_kinds = [info[k][0] for k in grp], [info[k][1] for k in grp]
        in_flight.append(_all_gather_ici_start(f"gather_start_{gi}", [placed[k] for k in grp], g_shards, g_kinds,
                                               all_started))
        all_started = in_flight[-1][3]
    full = {}

    def arrive(gi, after):
        grp = groups[gi]
        g_shards, g_kinds = [info[k][0] for k in grp], [info[k][1] for k in grp]
        send_sem, recv_sem, bufs, _ = in_flight[gi]
        bufs = _all_gather_ici_wait(f"gather_wait_{gi}", bufs, send_sem, recv_sem, after, g_shards, g_kinds)
        full.update(zip(grp, _all_gather_d2d(f"gather_d2d_{gi}", bufs, g_shards, g_kinds)))

    target = jnp.concatenate([jnp.zeros((NM, D), F32), loss_target[0], jnp.zeros((L - L_real, D), F32)], axis=0)
    arrive(0, [all_started, target, moments_m["w_in"][0], moments_v["w_in"][0]])
    wg1, wu1 = full["ffn1_w_gate"], full["ffn1_w_up"]
    c_f = 3 * FW
    QKV_S, GATES, FCOL = 3 * FW, 3 * FW + 3 * SW, 3 * FW + 3 * SW + 2 * D
    W_PROJ = FCOL + LANES

    h0 = jnp.concatenate([full["meta_tokens"], x[0], jnp.zeros((L - L_real, D), F32)], axis=0)

    def late_wd1(s):
        arrive(1, s)
        return full["ffn1_w_down"].reshape(F, D)

    h1, saved1, wd1 = _ffn_fwd("ffn1", h0, ffn1_norm, wg1, wu1, late_wd1, tm)

    arrive(2, h1)
    w_in_full = jnp.transpose(full["w_in"], (1, 0, 2)).reshape(D, W_IN)
    w_proj = jnp.concatenate([w_in_full[:, :c_f], w_in_full[:, c_f + H:],
                              jnp.pad(w_in_full[:, c_f:c_f + H], ((0, 0), (0, LANES - H)))], axis=1)
    n2 = _rmsnorm_fwd("mix_norm", h1, mix_norm, tm)
    tp = _pick(FCOL, (512, 256, 128))
    (proj,) = _matmul("in_proj", [n2], [w_proj], [(0, 0, 0)], 1, lambda accs, _: accs,
                      [((L, FCOL), BF) + _mn(tm, tp)], M=L, N=FCOL, K=D, tm=tm, tn=tp, tk=D, n_outer=True)
    (f_logit,) = _matmul("forget_proj", [n2], [w_proj], [(0, 0, 0)], 1, lambda accs, _: accs,
                         [((L, LANES), F32) + _mn(tm, LANES)], M=L, N=LANES, K=D, tm=tm, tn=LANES, tk=D,
                         b_off=FCOL // LANES)
    fl = jnp.pad(jnp.transpose(f_logit[:, :H]).reshape(H, nblk, LANES), ((0, 0), (0, nbp - nblk), (0, 0)))
    bias = jnp.broadcast_to(b_forget[0][:, None, None], (H, 1, LANES))
    c_row = _cum_fwd("forget_cumsum", fl, bias, nblk)
    c_col = c_row[:, :nblk].reshape(H, L, 1)
    gq, gk = fox_q_norm[0][:, None, :], fox_k_norm[0][:, None, :]
    qn, kn = _qknorm_fwd("fox_qk_norm", proj, gq, gk, H, 0, FW, tm)
    o_fox, lse = _fox_fwd("fox_attention", qn, kn, proj, c_row, c_col, H, ATT_HEADS, 0, 0, 2 * FW)
    o_sb = _sb_fwd("sb_attention", proj, HS, min(HS, ATT_HEADS_SB_FWD), QKV_S, QKV_S + SW, QKV_S + 2 * SW)

    arrive(3, o_sb)
    wg2, wu2 = full["ffn2_w_gate"], full["ffn2_w_up"]
    wd2 = full["ffn2_w_down"].reshape(F, D)
    wbf, wbs = full["w_branch_fox"], full["w_branch_sb"]
    wo = full["w_out"].reshape(D, D)
    td = _pick(D, (512, 256, 128))

    def merge(accs, ex):
        bf_, bs_ = accs
        return _sigmoid(ex[0].astype(F32)) * bf_ + _sigmoid(ex[1].astype(F32)) * bs_, bf_, bs_

    merged, br_f, br_s = _matmul("branch_merge", [o_fox, o_sb], [wbf, wbs], [(0, 0, 0), (1, 1, 1)], 2, merge,
                                 [((L, D), BF) + _mn(tm, td)] * 3, M=L, N=D, K=FW, tm=tm, tn=td, tk=FW,
                                 extras=[(proj,) + _mn(tm, td, GATES), (proj,) + _mn(tm, td, GATES + D)],
                                 n_outer=True)
    (h2,) = _matmul("out_proj", [merged], [wo], [(0, 0, 0)], 1, lambda accs, ex: [ex[0] + accs[0]],
                    [((L, D), F32) + _mn(tm, td)], M=L, N=D, K=D, tm=tm, tn=td, tk=D, extras=[(h1,) + _mn(tm, td)],
                    n_outer=True)

    h3, saved2, _ = _ffn_fwd("ffn2", h2, ffn2_norm, wg2, wu2, wd2, tm)
    dh3, dh3_bf, loss_part = _loss_grad("loss", h3, target, NM, S, tm)

    scatters, totals, grads, updates = {}, {}, {}, {}

    def adamw(k):
        w = weights[k]
        shape2 = (1, w.size) if w.size < LANES * SUBLANES else (w.size // w.shape[-1], w.shape[-1])
        res = _adamw(f"adamw_{k}", w.reshape(shape2), grads[k].reshape(shape2), moments_m[k].reshape(shape2),
                     moments_v[k].reshape(shape2))
        updates[k] = [r.reshape(w.shape) for r in res]

    def swap_begin(tag, keys, local):
        send_sem, recv_sem, grads, lands, token = _sibling_start(f"grads_to_sibling_start_{tag}", list(local))
        scatters[tag] = (keys, send_sem, recv_sem, grads, lands)
        return token

    def exchange_begin(tag, after):
        keys, send_sem, recv_sem, grads, lands = scatters[tag]
        local, from_sibling = _sibling_wait(f"grads_to_sibling_wait_{tag}", grads, lands, send_sem, recv_sem, after)
        partials = [_chip_partial(f"chip_sum_{k}", pos, g, r) for k, g, r in zip(keys, local, from_sibling)]
        send_sem, recv_sem, parts, lands, token = _exchange_start(f"grads_to_owner_start_{tag}", partials)
        scatters[tag] = (keys, local, from_sibling, send_sem, recv_sem, parts, lands)
        return token

    def scatter_end(tag, after):
        keys, local, from_sibling, send_sem, recv_sem, parts, lands = scatters[tag]
        from_chips = _exchange_wait(f"grads_to_owner_wait_{tag}", parts, lands, send_sem, recv_sem, after)
        totals.update({k: _final_half(f"total_{k}", pos, g, r, o)
                       for k, g, r, o in zip(keys, local, from_sibling, from_chips)})

    dh2, dh2_bf, dg_ffn2 = _ffn_bwd(
        "ffn2", dh3, dh3_bf, saved2, ffn2_norm, wg2, wu2, wd2, tm, cs_ff,
        on_weight_grads=lambda *dw: swap_begin("ffn2", ["ffn2_w_gate", "ffn2_w_up", "ffn2_w_down"], dw))
    ffn2_exchanging = exchange_begin("ffn2", dh2)

    def gate_grad(accs, ex):
        dm = accs[0]
        gf, gs, bf_, bs_ = [e.astype(F32) for e in ex]
        sf, ss = _sigmoid(gf), _sigmoid(gs)
        return dm * bf_ * sf * (1.0 - sf), dm * bs_ * ss * (1.0 - ss), dm * sf, dm * ss

    mn_d = _mn(tm, td)
    dgf, dgs, dbr_f, dbr_s = _matmul(
        "d_merged", [dh2_bf], [wo], [(0, 0, 0)], 1, gate_grad, [((L, D), BF) + mn_d] * 4, M=L, N=D, K=D, tm=tm,
        tn=td, tk=D, tb=True, extras=[(proj,) + _mn(tm, td, GATES), (proj,) + _mn(tm, td, GATES + D),
                                      (br_f,) + mn_d, (br_s,) + mn_d], n_outer=True, after=ffn2_exchanging)
    tkl = _pick(L, (1408, 384, 256, 128))
    (dwo,) = _matmul("d_w_out", [merged], [dh2_bf], [(0, 0, 0)], 1, lambda accs, _: accs,
                     [((N_CHIPS, cs_d, D), F32, (None, cs_d, td), lambda i, j: (i, 0, j))], M=D, N=D, K=L, tm=cs_d,
                     tn=td, tk=tkl, ta=True)
    tw = _pick(FW, (512, 256, 128))
    do_fox, do_sb = _matmul("d_branch_in", [dbr_f, dbr_s], [wbf, wbs], [(0, 0, 0), (1, 1, 1)], 2,
                            lambda accs, _: accs, [((L, FW), BF) + _mn(tm, tw)] * 2, M=L, N=FW, K=D, tm=tm, tn=tw,
                            tk=D, tb=True)
    tmb = _pick(FW, (1024, 512, 256, 128))
    dwbf, dwbs = _matmul("d_w_branch", [o_fox, o_sb], [dbr_f, dbr_s], [(0, 0, 0), (1, 1, 1)], 2,
                         lambda accs, _: accs,
                         [((N_CHIPS, FW, cs_d), F32, (None, tmb, cs_d), lambda i, j: (j, i, 0))] * 2, M=FW, N=D, K=L,
                         tm=tmb, tn=cs_d, tk=tkl, ta=True)

    dqn, dkn, dfv, dcs, drs = _fox_bwd("fox_attention_bwd", qn, kn, proj, c_row, c_col, o_fox, do_fox, lse, H,
                                       ATT_HEADS, 0, 0, 2 * FW)
    dsq, dsk_t, dsv_t = _sb_bwd("sb_attention_bwd", proj, do_sb, HS, ATT_HEADS, QKV_S, QKV_S + SW,
                                QKV_S + 2 * SW)
    dsk = jnp.transpose(dsk_t, (0, 2, 1)).reshape(L, SW)
    dsv = jnp.transpose(dsv_t, (0, 2, 1)).reshape(L, SW)
    dfq, dfk, dgq, dgk = _qknorm_bwd("fox_qk_norm_bwd", proj, dqn, dkn, gq, gk, H, 0, FW, tm)
    drs_row = jnp.pad(drs.reshape(H, nblk, LANES), ((0, 0), (0, nbp - nblk), (0, 0)))
    dfl, dbias = _cum_bwd("forget_cumsum_bwd", drs_row, dcs, fl, bias, nblk)
    dfl_cols = jnp.pad(jnp.transpose(dfl[:, :nblk].reshape(H, L)), ((0, 0), (0, LANES - H))).astype(BF)
    dproj = jnp.concatenate([dfq, dfk, dfv.astype(BF), dsq, dsk.astype(BF), dsv.astype(BF), dgf, dgs, dfl_cols],
                            axis=1)

    tdn = _pick(D, (256, 128))
    (dn2,) = _matmul("d_mix_norm_in", [dproj], [w_proj], [(0, 0, 0)], 1, lambda accs, _: accs,
                     [((L, D), F32) + _mn(tm, tdn)], M=L, N=D, K=W_PROJ, tm=tm, tn=tdn, tk=W_PROJ, tb=True)
    tmw = _pick(D, (1024, 512, 256, 128))
    tnp = _pick(W_PROJ, (1152, 640, 512, 384, 256, 128))
    (dw_proj,) = _matmul("d_w_in", [n2], [dproj], [(0, 0, 0)], 1, lambda accs, _: accs,
                         [((D, W_PROJ), F32) + _mn(tmw, tnp)], M=D, N=W_PROJ, K=L, tm=tmw, tn=tnp, tk=tkl,
                         ta=True)
    dh1, dh1_bf, dg_mix = _rmsnorm_bwd("mix_dnorm", dn2, h1, mix_norm, dh2, tm)
    dw_in_ref = jnp.concatenate([dw_proj[:, :c_f], dw_proj[:, FCOL:FCOL + H], dw_proj[:, c_f:FCOL]], axis=1)
    dw_in = jnp.transpose(dw_in_ref.reshape(D, N_CHIPS, cs_in), (1, 0, 2))
    scatter_end("ffn2", dh1)
    ffn2_keys = list(totals)
    grads.update(zip(ffn2_keys, _share_halves("grads_to_core_pair_ffn2", [totals.pop(k) for k in ffn2_keys])))
    mix_swapping = swap_begin("mix", ["w_in", "w_branch_fox", "w_branch_sb", "w_out"], [dw_in, dwbf, dwbs, dwo])

    def ffn1_grads(*dw):
        swap_begin("ffn1", ["ffn1_w_gate", "ffn1_w_up", "ffn1_w_down"], dw)
        for k in ffn2_keys:
            adamw(k)
        return exchange_begin("ffn1", [updates[k][0] for k in ffn2_keys])

    dh0, _, dg_ffn1 = _ffn_bwd("ffn1", dh1, dh1_bf, saved1, ffn1_norm, wg1, wu1, wd1, tm, cs_ff, after=mix_swapping,
                               midway=lambda done: exchange_begin("mix", done), on_weight_grads=ffn1_grads)
    grad_x = dh0[NM:L_real][None]
    scatter_end("mix", dh0)
    scatter_end("ffn1", dh0)
    big = list(totals)
    grads.update(zip(big, _share_halves("grads_to_core_pair", [totals.pop(k) for k in big])))

    small = [loss_part[:, :1].reshape(1), dh0[:NM].reshape(-1), dg_ffn1.reshape(-1), dg_mix.reshape(-1),
             dg_ffn2.reshape(-1), dbias[:, 0, 0], dgq.reshape(-1), dgk.reshape(-1)]
    sizes = [s.shape[0] for s in small]
    flat = jnp.concatenate(small)
    rows = _round_up(-(-flat.shape[0] // LANES), SUBLANES)
    packed = jnp.pad(flat, (0, rows * LANES - flat.shape[0])).reshape(rows, LANES)
    total = _sum_devices("sum_small", _gather_small("gather_small", packed)).reshape(-1)
    pieces, off = [], 0
    for n_el in sizes:
        pieces.append(total[off:off + n_el])
        off += n_el
    loss = pieces[0][0]
    d_meta = lax.dynamic_slice_in_dim(pieces[1].reshape(NM, D), pos[1] * cs_d, cs_d, axis=1)
    grads.update(meta_tokens=d_meta, ffn1_norm=pieces[2].reshape(1, D), mix_norm=pieces[3].reshape(1, D),
                 ffn2_norm=pieces[4].reshape(1, D), b_forget=pieces[5].reshape(1, H),
                 fox_q_norm=pieces[6].reshape(1, H, HEAD_DIM), fox_k_norm=pieces[7].reshape(1, H, HEAD_DIM))

    for k in names:
        if k not in updates:
            adamw(k)
    return (loss, grad_x, *[grads[k].reshape(weights[k].shape) for k in names],
            *[updates[k][i] for i in range(3) for k in names])
```

```python
import functools

import jax
import jax.numpy as jnp
from jax import lax
from jax.experimental import pallas as pl
from jax.experimental.pallas import tpu as pltpu

F32 = jnp.float32
BF = jnp.bfloat16
I32 = jnp.int32

HEAD_DIM = 128
RMS_EPS = 1e-6
FFN_RESIDUAL_WEIGHT = 0.5
ADAM_LR = 0.001
ADAM_B1 = 0.9
ADAM_B2 = 0.999
ADAM_EPS = 1e-08
ADAM_WD = 0.01
ADAM_STEP = 10

LANES = 128
SUBLANES = 8
ATT_TILE = 128
ATT_HEADS = 4
ATT_HEADS_SB_FWD = 8
VMEM_CAP = 56 * 1024 * 1024
MESH_IDS = pl.DeviceIdType.MESH
N_CHIPS = 4
N_DEV = 8


def _pick(n, cands):
    for c in cands:
        if c <= n and n % c == 0:
            return c
    raise ValueError(f"no tile for {n} among {cands}")


def _round_up(n, m):
    return (n + m - 1) // m * m


def _tile_bytes(shape, dtype):
    item = jnp.dtype(dtype).itemsize
    dims = [d for d in shape if d is not None]
    if not dims:
        return 4 * LANES * SUBLANES
    last = _round_up(dims[-1], LANES)
    sub = _round_up(dims[-2], SUBLANES * (4 // item)) if len(dims) > 1 else 1
    lead = 1
    for d in dims[:-2]:
        lead *= d
    return lead * sub * last * item


def _vmem_limit(blocks, scratch=(), temps=0):
    need = 2 * sum(_tile_bytes(s, d) for s, d in blocks) + sum(_tile_bytes(s, d) for s, d in scratch) + temps
    return int(min(VMEM_CAP, max(need + (4 << 20), 16 << 20)))


def _call(body, *, name, out_shape, grid=(), in_specs=None, out_specs=None, scratch=(), sem=None, vmem=None,
          aliases=None, prefetch=0):
    params = pltpu.CompilerParams(dimension_semantics=sem, vmem_limit_bytes=vmem)
    if prefetch:
        grid_spec = pltpu.PrefetchScalarGridSpec(num_scalar_prefetch=prefetch, grid=grid, in_specs=in_specs,
                                                 out_specs=out_specs, scratch_shapes=scratch)
        return pl.pallas_call(body, out_shape=out_shape, grid_spec=grid_spec, name=name, compiler_params=params,
                              input_output_aliases=aliases or {})
    return pl.pallas_call(body, out_shape=out_shape, grid=grid, in_specs=in_specs, out_specs=out_specs,
                          scratch_shapes=scratch, name=name, compiler_params=params,
                          input_output_aliases=aliases or {})


def _dot(a, b, ca, cb):
    return lax.dot_general(a, b, (((ca,), (cb,)), ((), ())), preferred_element_type=F32)


def _sigmoid(x):
    return 1.0 / (1.0 + jnp.exp(-x))


def _log_sigmoid(x):
    return jnp.minimum(x, 0.0) - jnp.log1p(jnp.exp(-jnp.abs(x)))


def _split(x, parts):
    pieces = []
    rem = x
    for p in range(parts):
        piece = rem.astype(BF)
        pieces.append(piece)
        if p + 1 < parts:
            rem = rem - piece.astype(F32)
    return pieces


def _pieces_dot(pieces, ones_bf):
    out = None
    for piece in pieces:
        d = _dot(piece, ones_bf, 1, 0)
        out = d if out is None else out + d
    return out


def _split_dot(x, ones_bf, parts):
    return _pieces_dot(_split(x, parts), ones_bf)


def _matmul(name, a_list, b_list, pairs, n_acc, epi, outs, *, M, N, K, tm, tn, tk, ta=False, tb=False,
            extras=(), n_outer=False, b_off=0, after=None):
    if after is not None:
        user_epi = epi
        extras = list(extras) + [(after, (SUBLANES, LANES), lambda i, j: (0, 0))]
        epi = lambda accs, ex: user_epi(accs, ex[:-1])
    gi, gj, nk = M // tm, N // tn, K // tk
    assert gi * tm == M and gj * tn == N and nk * tk == K, (name, M, N, K, tm, tn, tk)
    n_a, n_b, n_e, n_o = len(a_list), len(b_list), len(extras), len(outs)

    def ij(g0, g1):
        return (g1, g0) if n_outer else (g0, g1)

    def a_map(g0, g1, k):
        i, _ = ij(g0, g1)
        return (k, i) if ta else (i, k)

    def b_map(g0, g1, k):
        _, j = ij(g0, g1)
        return (j + b_off, k) if tb else (k, j + b_off)

    def tile_map(fn):
        return lambda g0, g1, k: fn(*ij(g0, g1))

    a_block = (tk, tm) if ta else (tm, tk)
    b_block = (tn, tk) if tb else (tk, tn)
    in_specs = ([pl.BlockSpec(a_block, a_map)] * n_a + [pl.BlockSpec(b_block, b_map)] * n_b
                + [pl.BlockSpec(bs, tile_map(fn)) for _, bs, fn in extras])
    out_specs = [pl.BlockSpec(bs, tile_map(fn)) for _, _, bs, fn in outs]
    out_shape = [jax.ShapeDtypeStruct(s, d) for s, d, _, _ in outs]
    scratch = [pltpu.VMEM((tm, tn), F32) for _ in range(n_acc)] if nk > 1 else []

    def body(*refs):
        a_refs = refs[:n_a]
        b_refs = refs[n_a:n_a + n_b]
        e_refs = refs[n_a + n_b:n_a + n_b + n_e]
        o_refs = refs[n_a + n_b + n_e:n_a + n_b + n_e + n_o]
        acc_refs = refs[n_a + n_b + n_e + n_o:]

        def products():
            accs = [None] * n_acc
            for ai, bi, ci in pairs:
                a = a_refs[ai][...]
                b = b_refs[bi][...]
                d = _dot(a.astype(BF), b.astype(BF), 0 if ta else 1, 1 if tb else 0)
                accs[ci] = d if accs[ci] is None else accs[ci] + d
            return accs

        def finish(accs):
            res = epi(accs, [e[...] for e in e_refs])
            for o_ref, r in zip(o_refs, res):
                o_ref[...] = r.reshape(o_ref.shape).astype(o_ref.dtype)

        if nk == 1:
            finish(products())
        else:
            k = pl.program_id(2)

            @pl.when(k == 0)
            def _():
                for acc in acc_refs:
                    acc[...] = jnp.zeros_like(acc)

            for acc, d in zip(acc_refs, products()):
                acc[...] += d

            @pl.when(k == nk - 1)
            def _():
                finish([acc[...] for acc in acc_refs])

    blocks = ([(a_block, a.dtype) for a in a_list] + [(b_block, b.dtype) for b in b_list]
              + [(bs, e.dtype) for e, bs, _ in extras] + [(bs, d) for _, d, bs, _ in outs])
    vmem = _vmem_limit(blocks, [((tm, tn), F32)] * (n_acc if nk > 1 else 0), temps=6 * tm * tn * 4)
    grid = (gj, gi, nk) if n_outer else (gi, gj, nk)
    fn = _call(body, name=name, out_shape=out_shape, grid=grid, in_specs=in_specs, out_specs=out_specs,
               scratch=scratch, sem=("parallel", "parallel", "arbitrary"), vmem=vmem)
    return fn(*a_list, *b_list, *[e for e, _, _ in extras])


def _mn(tm, tn, col0=0):
    assert col0 % tn == 0
    off = col0 // tn
    return (tm, tn), (lambda i, j: (i, j + off))


def _rmsnorm_fwd(name, x, gain, tr):
    L, D = x.shape

    def body(x_ref, g_ref, o_ref):
        xv = x_ref[...]
        r = lax.rsqrt(jnp.mean(xv * xv, axis=-1, keepdims=True) + RMS_EPS)
        o_ref[...] = (xv * r * g_ref[...]).astype(BF)

    row = pl.BlockSpec((tr, D), lambda i: (i, 0))
    vec = pl.BlockSpec((1, D), lambda i: (0, 0))
    vmem = _vmem_limit([((tr, D), F32), ((tr, D), BF)], temps=3 * tr * D * 4)
    return _call(body, name=name, out_shape=jax.ShapeDtypeStruct((L, D), BF), grid=(L // tr,), in_specs=[row, vec],
                 out_specs=row, sem=("parallel",), vmem=vmem)(x, gain)


def _rmsnorm_bwd(name, dn, x, gain, dres, tr):
    L, D = x.shape
    steps = L // tr

    def body(dn_ref, x_ref, g_ref, dres_ref, dx_ref, dxb_ref, dg_ref):
        i = pl.program_id(0)
        xv = x_ref[...]
        r = lax.rsqrt(jnp.mean(xv * xv, axis=-1, keepdims=True) + RMS_EPS)
        xhat = xv * r
        dy = dn_ref[...]
        dxhat = dy * g_ref[...]
        dx = dres_ref[...] + r * (dxhat - xhat * jnp.mean(dxhat * xhat, axis=-1, keepdims=True))
        dx_ref[...] = dx
        dxb_ref[...] = dx.astype(BF)

        @pl.when(i == 0)
        def _():
            dg_ref[...] = jnp.zeros_like(dg_ref)

        dg_ref[...] += jnp.sum(dy * xhat, axis=0, keepdims=True)

    row = pl.BlockSpec((tr, D), lambda i: (i, 0))
    vec = pl.BlockSpec((1, D), lambda i: (0, 0))
    vmem = _vmem_limit([((tr, D), F32)] * 4 + [((tr, D), BF)], temps=4 * tr * D * 4)
    out_shape = [jax.ShapeDtypeStruct((L, D), F32), jax.ShapeDtypeStruct((L, D), BF),
                 jax.ShapeDtypeStruct((1, D), F32)]
    return _call(body, name=name, out_shape=out_shape, grid=(steps,), in_specs=[row, row, vec, row],
                 out_specs=[row, row, vec], sem=("arbitrary",), vmem=vmem)(dn, x, gain, dres)


def _loss_grad(name, h, target, n_meta, n_seq, tr):
    L, D = h.shape

    def body(h_ref, t_ref, dh_ref, dhb_ref, loss_ref):
        i = pl.program_id(0)
        rows = i * tr + lax.broadcasted_iota(I32, (tr, 1), 0)
        valid = (rows >= n_meta) & (rows < n_meta + n_seq)
        diff = jnp.where(valid, h_ref[...] - t_ref[...], 0.0)
        dh = diff * (1.0 / D)
        dh_ref[...] = dh
        dhb_ref[...] = dh.astype(BF)

        @pl.when(i == 0)
        def _():
            loss_ref[...] = jnp.zeros_like(loss_ref)

        loss_ref[...] += jnp.sum(diff * diff) * (0.5 / D)

    row = pl.BlockSpec((tr, D), lambda i: (i, 0))
    acc = pl.BlockSpec((1, LANES), lambda i: (0, 0))
    vmem = _vmem_limit([((tr, D), F32)] * 3 + [((tr, D), BF)], temps=3 * tr * D * 4)
    out_shape = [jax.ShapeDtypeStruct((L, D), F32), jax.ShapeDtypeStruct((L, D), BF),
                 jax.ShapeDtypeStruct((1, LANES), F32)]
    return _call(body, name=name, out_shape=out_shape, grid=(L // tr,), in_specs=[row, row],
                 out_specs=[row, row, acc], sem=("arbitrary",), vmem=vmem)(h, target)


def _qknorm_fwd(name, proj, gq, gk, heads, q_col, k_col, tr):
    L = proj.shape[0]

    def body(q_ref, k_ref, gq_ref, gk_ref, qn_ref, kn_ref):
        for x_ref, g_ref, o_ref in ((q_ref, gq_ref, qn_ref), (k_ref, gk_ref, kn_ref)):
            xv = x_ref[...].astype(F32)
            r = lax.rsqrt(jnp.mean(xv * xv, axis=-1, keepdims=True) + RMS_EPS)
            o_ref[...] = (xv * r * g_ref[...]).astype(BF)

    qb, kb = q_col // HEAD_DIM, k_col // HEAD_DIM
    in_specs = [pl.BlockSpec((tr, HEAD_DIM), lambda h, i: (i, qb + h)),
                pl.BlockSpec((tr, HEAD_DIM), lambda h, i: (i, kb + h)),
                pl.BlockSpec((None, 1, HEAD_DIM), lambda h, i: (h, 0, 0)),
                pl.BlockSpec((None, 1, HEAD_DIM), lambda h, i: (h, 0, 0))]
    out = pl.BlockSpec((tr, HEAD_DIM), lambda h, i: (i, h))
    out_shape = [jax.ShapeDtypeStruct((L, heads * HEAD_DIM), BF)] * 2
    return _call(body, name=name, out_shape=out_shape, grid=(heads, L // tr), in_specs=in_specs,
                 out_specs=[out, out], sem=("parallel", "parallel"), vmem=16 << 20)(proj, proj, gq, gk)


def _qknorm_bwd(name, proj, dqn, dkn, gq, gk, heads, q_col, k_col, tr):
    L = proj.shape[0]

    def body(q_ref, k_ref, dqn_ref, dkn_ref, gq_ref, gk_ref, dq_ref, dk_ref, dgq_ref, dgk_ref):
        i = pl.program_id(1)
        for x_ref, dy_ref, g_ref, dx_ref, dg_ref in ((q_ref, dqn_ref, gq_ref, dq_ref, dgq_ref),
                                                     (k_ref, dkn_ref, gk_ref, dk_ref, dgk_ref)):
            xv = x_ref[...].astype(F32)
            r = lax.rsqrt(jnp.mean(xv * xv, axis=-1, keepdims=True) + RMS_EPS)
            xhat = xv * r
            dy = dy_ref[...].astype(F32)
            dxhat = dy * g_ref[...]
            dx_ref[...] = (r * (dxhat - xhat * jnp.mean(dxhat * xhat, axis=-1, keepdims=True))).astype(BF)

            @pl.when(i == 0)
            def _():
                dg_ref[...] = jnp.zeros_like(dg_ref)

            dg_ref[...] += jnp.sum(dy * xhat, axis=0, keepdims=True)

    qb, kb = q_col // HEAD_DIM, k_col // HEAD_DIM
    tile = pl.BlockSpec((tr, HEAD_DIM), lambda h, i: (i, h))
    gain = pl.BlockSpec((None, 1, HEAD_DIM), lambda h, i: (h, 0, 0))
    in_specs = [pl.BlockSpec((tr, HEAD_DIM), lambda h, i: (i, qb + h)),
                pl.BlockSpec((tr, HEAD_DIM), lambda h, i: (i, kb + h)), tile, tile, gain, gain]
    out_shape = [jax.ShapeDtypeStruct((L, heads * HEAD_DIM), BF)] * 2 + [
        jax.ShapeDtypeStruct((heads, 1, HEAD_DIM), F32)] * 2
    return _call(body, name=name, out_shape=out_shape, grid=(heads, L // tr), in_specs=in_specs,
                 out_specs=[tile, tile, gain, gain], sem=("parallel", "arbitrary"),
                 vmem=16 << 20)(proj, proj, dqn, dkn, gq, gk)


def _tri(cmp):
    r = lax.broadcasted_iota(I32, (LANES, LANES), 0)
    c = lax.broadcasted_iota(I32, (LANES, LANES), 1)
    return jnp.where(cmp(r, c), 1.0, 0.0).astype(BF)


def _cum_fwd(name, fl, bias, n_rows):
    H, nbp, _ = fl.shape

    def body(fl_ref, b_ref, c_ref, tot_ref):
        lf = _log_sigmoid(fl_ref[...] + b_ref[...])
        c_ref[...] = _split_dot(lf, _tri(lambda r, c: r <= c), 3)
        tot_ref[...] = _split_dot(lf, jnp.ones((LANES, LANES), BF), 3)

        def step(r, carry):
            c_ref[pl.ds(r, 1), :] = c_ref[pl.ds(r, 1), :] + carry
            return carry + tot_ref[pl.ds(r, 1), :]

        lax.fori_loop(0, n_rows, step, jnp.zeros((1, LANES), F32))

    blk = pl.BlockSpec((None, nbp, LANES), lambda h: (h, 0, 0))
    vec = pl.BlockSpec((None, 1, LANES), lambda h: (h, 0, 0))
    return _call(body, name=name, out_shape=jax.ShapeDtypeStruct((H, nbp, LANES), F32), grid=(H,),
                 in_specs=[blk, vec], out_specs=blk, scratch=[pltpu.VMEM((nbp, LANES), F32)], sem=("parallel",),
                 vmem=16 << 20)(fl, bias)


def _cum_bwd(name, drs, dcs, fl, bias, n_rows):
    H, nbp, _ = fl.shape

    def body(drs_ref, dcs_ref, fl_ref, b_ref, dfl_ref, db_ref, rin_ref, tot_ref):
        dc = drs_ref[...] - dcs_ref[...]
        rin_ref[...] = _split_dot(dc, _tri(lambda r, c: r >= c), 3)
        tot_ref[...] = _split_dot(dc, jnp.ones((LANES, LANES), BF), 3)
        dfl_ref[...] = jnp.zeros_like(dfl_ref)

        def step(t, carry):
            r = n_rows - 1 - t
            x = fl_ref[pl.ds(r, 1), :] + b_ref[...]
            dfl_ref[pl.ds(r, 1), :] = (rin_ref[pl.ds(r, 1), :] + carry) * _sigmoid(-x)
            return carry + tot_ref[pl.ds(r, 1), :]

        lax.fori_loop(0, n_rows, step, jnp.zeros((1, LANES), F32))
        db_ref[...] = jnp.zeros_like(db_ref) + jnp.sum(dfl_ref[...])

    blk = pl.BlockSpec((None, nbp, LANES), lambda h: (h, 0, 0))
    vec = pl.BlockSpec((None, 1, LANES), lambda h: (h, 0, 0))
    out_shape = [jax.ShapeDtypeStruct((H, nbp, LANES), F32), jax.ShapeDtypeStruct((H, 1, LANES), F32)]
    return _call(body, name=name, out_shape=out_shape, grid=(H,), in_specs=[blk, blk, blk, vec],
                 out_specs=[blk, vec], scratch=[pltpu.VMEM((nbp, LANES), F32)] * 2, sem=("parallel",),
                 vmem=16 << 20)(drs, dcs, fl, bias)


def _att_specs(L, G, q_col, k_col, v_col):
    T = ATT_TILE
    W = G * HEAD_DIM
    assert q_col % W == 0 and k_col % W == 0 and v_col % W == 0
    qb, kb, vb = q_col // W, k_col // W, v_col // W
    q_spec = pl.BlockSpec((T, W), lambda h, i: (i, qb + h))
    k_spec = pl.BlockSpec((L, W), lambda h, i: (0, kb + h), pipeline_mode=pl.Buffered(1))
    v_spec = pl.BlockSpec((L, W), lambda h, i: (0, vb + h), pipeline_mode=pl.Buffered(1))
    return q_spec, k_spec, v_spec


def _head_lanes(G):
    return [slice(g * HEAD_DIM, (g + 1) * HEAD_DIM) for g in range(G)]


def _tile_iotas():
    T = ATT_TILE
    return lax.broadcasted_iota(I32, (T, T), 0), lax.broadcasted_iota(I32, (T, T), 1)


def _rows(j):
    return pl.ds(pl.multiple_of(j * ATT_TILE, ATT_TILE), ATT_TILE)


def _fox_fwd(name, q_arr, k_arr, v_arr, c_row, c_col, heads, G, q_col, k_col, v_col):
    L = q_arr.shape[0]
    T = ATT_TILE
    scale = HEAD_DIM ** -0.5
    lanes = _head_lanes(G)

    def body(q_ref, k_ref, v_ref, crow_ref, ccol_ref, o_ref, lse_ref):
        i = pl.program_id(1)
        qs = [q_ref[:, hl] for hl in lanes]
        cts = [jnp.broadcast_to(ccol_ref[g], (T, T)) for g in range(G)]
        row, col = _tile_iotas()

        def tile(j, carry, masked):
            qk = [_dot(qs[g], k_ref[_rows(j), hl], 1, 1) for g, hl in enumerate(lanes)]
            stats = []
            for g in range(G):
                m, l, _ = carry[g]
                s = qk[g] * scale + (cts[g] - crow_ref[g, pl.ds(j, 1), :])
                if masked:
                    s = jnp.where(col <= row, s, -jnp.inf)
                m_new = jnp.maximum(m, jnp.max(s, axis=1, keepdims=True))
                alpha = jnp.exp(m - m_new)
                p = jnp.exp(s - m_new)
                stats.append((m_new, alpha, alpha * l + jnp.sum(p, axis=1, keepdims=True), p.astype(BF)))
            pv = [_dot(stats[g][3], v_ref[_rows(j), hl], 1, 0) for g, hl in enumerate(lanes)]
            return tuple((stats[g][0], stats[g][2], stats[g][1] * carry[g][2] + pv[g]) for g in range(G))

        init = tuple((jnp.full((T, 1), -1e30, F32), jnp.zeros((T, 1), F32), jnp.zeros((T, HEAD_DIM), F32))
                     for _ in range(G))
        carry = lax.fori_loop(0, i, lambda j, c: tile(j, c, False), init)
        for g, (m, l, acc) in enumerate(tile(i, carry, True)):
            o_ref[:, lanes[g]] = (acc / l).astype(o_ref.dtype)
            lse_ref[g] = m + jnp.log(l)

    nbp = c_row.shape[1]
    W = G * HEAD_DIM
    q_spec, k_spec, v_spec = _att_specs(L, G, q_col, k_col, v_col)
    crow_spec = pl.BlockSpec((G, nbp, LANES), lambda h, i: (h, 0, 0))
    col_spec = pl.BlockSpec((G, T, 1), lambda h, i: (h, i, 0))
    o_spec = pl.BlockSpec((T, W), lambda h, i: (i, h))
    out_shape = [jax.ShapeDtypeStruct((L, heads * HEAD_DIM), BF), jax.ShapeDtypeStruct((heads, L, 1), F32)]
    vmem = _vmem_limit([((L, W), BF)] * 2, temps=8 << 20)
    return _call(body, name=name, out_shape=out_shape, grid=(heads // G, L // T),
                 in_specs=[q_spec, k_spec, v_spec, crow_spec, col_spec], out_specs=[o_spec, col_spec],
                 sem=("parallel", "parallel"), vmem=vmem)(q_arr, k_arr, v_arr, c_row, c_col)


def _fox_bwd(name, q_arr, k_arr, v_arr, c_row, c_col, o, do, lse, heads, G, q_col, k_col, v_col):
    L = q_arr.shape[0]
    T = ATT_TILE
    nq = L // T
    scale = HEAD_DIM ** -0.5
    lanes = _head_lanes(G)

    def body(q_ref, k_ref, v_ref, crow_ref, ccol_ref, o_ref, do_ref, lse_ref, dq_ref, dk_ref, dv_ref, dcs_ref,
             drs_ref, dk_acc, dv_acc):
        i = pl.program_id(1)

        @pl.when(i == 0)
        def _():
            dk_acc[...] = jnp.zeros_like(dk_acc)
            dv_acc[...] = jnp.zeros_like(dv_acc)
            dcs_ref[...] = jnp.zeros_like(dcs_ref)

        qs = [q_ref[:, hl] for hl in lanes]
        dos = [do_ref[:, hl] for hl in lanes]
        q_ts = [qs[g].T for g in range(G)]
        do_ts = [dos[g].T for g in range(G)]
        deltas = [jnp.broadcast_to(jnp.sum(dos[g].astype(F32) * o_ref[:, hl].astype(F32), axis=1, keepdims=True),
                                   (T, T)) for g, hl in enumerate(lanes)]
        lses = [jnp.broadcast_to(lse_ref[g], (T, T)) for g in range(G)]
        cts = [jnp.broadcast_to(ccol_ref[g], (T, T)) for g in range(G)]
        row, col = _tile_iotas()

        def tile(j, carry, masked):
            ks = [k_ref[_rows(j), hl] for hl in lanes]
            qk = [_dot(qs[g], ks[g], 1, 1) for g in range(G)]
            dp = [_dot(dos[g], v_ref[_rows(j), hl], 1, 1) for g, hl in enumerate(lanes)]
            pbs, dsbs, row_sums = [], [], []
            for g in range(G):
                s = qk[g] * scale + (cts[g] - crow_ref[g, pl.ds(j, 1), :])
                if masked:
                    s = jnp.where(col <= row, s, -jnp.inf)
                p = jnp.exp(s - lses[g])
                ds = p * (dp[g] - deltas[g])
                dcs_ref[g, pl.ds(j, 1), :] += jnp.sum(ds, axis=0, keepdims=True)
                row_sums.append(carry[g][1] + jnp.sum(ds, axis=1, keepdims=True))
                pbs.append(p.astype(BF))
                dsbs.append((ds * scale).astype(BF))
            for g, hl in enumerate(lanes):
                dk_acc[j, hl, :] += _dot(q_ts[g], dsbs[g], 1, 0)
            for g, hl in enumerate(lanes):
                dv_acc[j, hl, :] += _dot(do_ts[g], pbs[g], 1, 0)
            return tuple((carry[g][0] + _dot(dsbs[g], ks[g], 1, 0), row_sums[g]) for g in range(G))

        init = tuple((jnp.zeros((T, HEAD_DIM), F32), jnp.zeros((T, 1), F32)) for _ in range(G))
        carry = lax.fori_loop(0, i, lambda j, c: tile(j, c, False), init)
        for g, (dq, row_sum) in enumerate(tile(i, carry, True)):
            dq_ref[:, lanes[g]] = dq.astype(dq_ref.dtype)
            drs_ref[g] = row_sum

        @pl.when(i == nq - 1)
        def _():
            for r in range(nq):
                for hl in lanes:
                    dk_ref[r * T:(r + 1) * T, hl] = dk_acc[r, hl, :].T.astype(dk_ref.dtype)
                    dv_ref[r * T:(r + 1) * T, hl] = dv_acc[r, hl, :].T.astype(dv_ref.dtype)

    nbp = c_row.shape[1]
    WG = G * HEAD_DIM
    q_spec, k_spec, v_spec = _att_specs(L, G, q_col, k_col, v_col)
    crow_spec = pl.BlockSpec((G, nbp, LANES), lambda h, i: (h, 0, 0))
    col_spec = pl.BlockSpec((G, T, 1), lambda h, i: (h, i, 0))
    t_spec = pl.BlockSpec((T, WG), lambda h, i: (i, h))
    head_spec = pl.BlockSpec((L, WG), lambda h, i: (0, h), pipeline_mode=pl.Buffered(1))
    W = heads * HEAD_DIM
    out_shape = [jax.ShapeDtypeStruct((L, W), F32)] * 3 + [jax.ShapeDtypeStruct((heads, nbp, LANES), F32),
                                                           jax.ShapeDtypeStruct((heads, L, 1), F32)]
    scratch = [pltpu.VMEM((nq, WG, T), F32)] * 2
    vmem = _vmem_limit([], [((L, WG), BF)] * 2 + [((L, WG), F32)] * 4, temps=8 << 20)
    return _call(body, name=name, out_shape=out_shape, grid=(heads // G, nq),
                 in_specs=[q_spec, k_spec, v_spec, crow_spec, col_spec, t_spec, t_spec, col_spec],
                 out_specs=[t_spec, head_spec, head_spec, crow_spec, col_spec], scratch=scratch,
                 sem=("parallel", "arbitrary"), vmem=vmem)(q_arr, k_arr, v_arr, c_row, c_col, o, do, lse)


def _sb_logits(qk, scale, valid):
    z = qk * scale
    lb = jnp.minimum(z, 0.0) - jnp.log1p(jnp.exp(-jnp.abs(z)))
    lom = lb - z
    if valid is not None:
        lom = jnp.where(valid, lom, 0.0)
    return lb, lom


def _sb_fwd(name, proj, heads, G, q_col, k_col, v_col):
    L = proj.shape[0]
    T = ATT_TILE
    scale = HEAD_DIM ** -0.5
    lanes = _head_lanes(G)

    def body(q_ref, k_ref, v_ref, o_ref):
        i = pl.program_id(1)
        qs = [q_ref[:, hl] for hl in lanes]
        row, col = _tile_iotas()
        later_mat = jnp.where(row > col, 1.0, 0.0).astype(BF)

        def tile(j, carry, masked):
            valid = (col < row) if masked else None
            qk = [_dot(qs[g], k_ref[_rows(j), hl], 1, 1) for g, hl in enumerate(lanes)]
            logits = [_sb_logits(qk[g], scale, valid) for g in range(G)]
            pieces = [_split(lom, 2) for _, lom in logits]
            later = [_pieces_dot(pieces[g], later_mat) for g in range(G)]
            ws = []
            for g in range(G):
                w = jnp.exp(logits[g][0] + later[g] + carry[g][0])
                if masked:
                    w = jnp.where(valid, w, 0.0)
                ws.append(w.astype(BF))
            wv = [_dot(ws[g], v_ref[_rows(j), hl], 1, 0) for g, hl in enumerate(lanes)]
            return tuple((carry[g][0] + jnp.sum(logits[g][1], axis=1, keepdims=True), carry[g][1] + wv[g])
                         for g in range(G))

        init = tuple((jnp.zeros((T, 1), F32), jnp.zeros((T, HEAD_DIM), F32)) for _ in range(G))
        carry = tile(i, init, True)
        carry = lax.fori_loop(0, i, lambda t, c: tile(i - 1 - t, c, False), carry)
        for g, (_, acc) in enumerate(carry):
            o_ref[:, lanes[g]] = acc.astype(o_ref.dtype)

    W = G * HEAD_DIM
    q_spec, k_spec, v_spec = _att_specs(L, G, q_col, k_col, v_col)
    o_spec = pl.BlockSpec((T, W), lambda h, i: (i, h))
    vmem = _vmem_limit([((L, W), BF)] * 2, temps=8 << 20)
    return _call(body, name=name, out_shape=jax.ShapeDtypeStruct((L, heads * HEAD_DIM), BF),
                 grid=(heads // G, L // T), in_specs=[q_spec, k_spec, v_spec], out_specs=o_spec,
                 sem=("parallel", "parallel"), vmem=vmem)(proj, proj, proj)


def _sb_bwd(name, proj, do, heads, G, q_col, k_col, v_col):
    L = proj.shape[0]
    T = ATT_TILE
    nq = L // T
    scale = HEAD_DIM ** -0.5
    lanes = _head_lanes(G)

    def body(q_ref, k_ref, v_ref, do_ref, dq_ref, dk_acc, dv_acc, da_buf, beta_buf):
        i = pl.program_id(1)

        @pl.when(i == 0)
        def _():
            dk_acc[...] = jnp.zeros_like(dk_acc)
            dv_acc[...] = jnp.zeros_like(dv_acc)

        qs = [q_ref[:, hl] for hl in lanes]
        dos = [do_ref[:, hl] for hl in lanes]
        q_ts = [qs[g].T for g in range(G)]
        do_ts = [dos[g].T for g in range(G)]
        row, col = _tile_iotas()
        later_mat = jnp.where(row > col, 1.0, 0.0).astype(BF)
        before_mat = jnp.where(row < col, 1.0, 0.0).astype(BF)

        def pass1(j, runs, masked):
            valid = (col < row) if masked else None
            qk = [_dot(qs[g], k_ref[_rows(j), hl], 1, 1) for g, hl in enumerate(lanes)]
            dw = [_dot(dos[g], v_ref[_rows(j), hl], 1, 1) for g, hl in enumerate(lanes)]
            logits = [_sb_logits(qk[g], scale, valid) for g in range(G)]
            pieces = [_split(lom, 2) for _, lom in logits]
            later = [_pieces_dot(pieces[g], later_mat) for g in range(G)]
            ws = []
            for g in range(G):
                w = jnp.exp(logits[g][0] + later[g] + runs[g])
                if masked:
                    w = jnp.where(valid, w, 0.0)
                da_buf[g * nq + j] = dw[g] * w
                beta_buf[g * nq + j] = jnp.exp(logits[g][0])
                ws.append(w.astype(BF))
            for g, hl in enumerate(lanes):
                dv_acc[j, hl, :] += _dot(do_ts[g], ws[g], 1, 0)
            return tuple(runs[g] + jnp.sum(logits[g][1], axis=1, keepdims=True) for g in range(G))

        runs = pass1(i, tuple(jnp.zeros((T, 1), F32) for _ in range(G)), True)
        lax.fori_loop(0, i, lambda t, c: pass1(i - 1 - t, c, False), runs)

        def pass2(j, carry, masked):
            das = [da_buf[g * nq + j] for g in range(G)]
            pieces = [_split(da, 2) for da in das]
            before = [_pieces_dot(pieces[g], before_mat) for g in range(G)]
            dzbs = []
            for g in range(G):
                beta = beta_buf[g * nq + j]
                dz = das[g] * (1.0 - beta) - (carry[g][0] + before[g]) * beta
                if masked:
                    dz = jnp.where(col < row, dz, 0.0)
                dzbs.append((dz * scale).astype(BF))
            for g, hl in enumerate(lanes):
                dk_acc[j, hl, :] += _dot(q_ts[g], dzbs[g], 1, 0)
            dq = [_dot(dzbs[g], k_ref[_rows(j), hl], 1, 0) for g, hl in enumerate(lanes)]
            return tuple((carry[g][0] + jnp.sum(das[g], axis=1, keepdims=True), carry[g][1] + dq[g])
                         for g in range(G))

        init = tuple((jnp.zeros((T, 1), F32), jnp.zeros((T, HEAD_DIM), F32)) for _ in range(G))
        carry = lax.fori_loop(0, i, lambda j, c: pass2(j, c, False), init)
        for g, (_, dq) in enumerate(pass2(i, carry, True)):
            dq_ref[:, lanes[g]] = dq.astype(dq_ref.dtype)

    WG = G * HEAD_DIM
    q_spec, k_spec, v_spec = _att_specs(L, G, q_col, k_col, v_col)
    t_spec = pl.BlockSpec((T, WG), lambda h, i: (i, h))
    head_spec = pl.BlockSpec((nq, WG, T), lambda h, i: (0, h, 0), pipeline_mode=pl.Buffered(1))
    W = heads * HEAD_DIM
    out_shape = [jax.ShapeDtypeStruct((L, W), BF)] + [jax.ShapeDtypeStruct((nq, W, T), F32)] * 2
    scratch = [pltpu.VMEM((G * nq, T, T), F32)] * 2
    vmem = _vmem_limit([], [((L, WG), BF)] * 2 + [((L, WG), F32)] * 2 + [((G * nq, T, T), F32)] * 2,
                       temps=6 << 20)
    return _call(body, name=name, out_shape=out_shape, grid=(heads // G, nq),
                 in_specs=[q_spec, k_spec, v_spec, t_spec], out_specs=[t_spec, head_spec, head_spec],
                 scratch=scratch, sem=("parallel", "arbitrary"), vmem=vmem)(proj, proj, proj, do)


_ANY = pl.BlockSpec(memory_space=pl.ANY)


def _mesh_pos():
    return lax.axis_index("x"), lax.axis_index("y"), lax.axis_index("c")


def _other_chips(x, y):
    return [(1 - x, y), (x, 1 - y), (1 - x, 1 - y)]


def _shard_window(ref, kind, sidx, r0, nr, cs):
    if kind == "col":
        assert cs % LANES == 0
        return ref.at[pl.ds(r0, nr), pl.ds(pl.multiple_of(sidx * cs, LANES), cs)]
    return ref.at[sidx, pl.ds(r0, nr), :]


def _place_shard(name, pos, shard, kind, dtype):
    R, C = shard.shape
    tr = _row_tile(R, C, 2, 16)

    def body(pos_ref, s_ref, o_ref):
        o_ref[...] = s_ref[...].astype(o_ref.dtype)

    if kind == "col":
        assert C % LANES == 0
        out_shape = jax.ShapeDtypeStruct((R, N_CHIPS * C), dtype)
        out_spec = pl.BlockSpec((tr, C), lambda r, pos_ref: (r, pos_ref[1]))
    else:
        out_shape = jax.ShapeDtypeStruct((N_CHIPS, R, C), dtype)
        out_spec = pl.BlockSpec((None, tr, C), lambda r, pos_ref: (pos_ref[1], r, 0))
    return _call(body, name=name, out_shape=out_shape, grid=(R // tr,),
                 in_specs=[pl.BlockSpec((tr, C), lambda r, pos_ref: (r, 0))], out_specs=out_spec, sem=("parallel",),
                 vmem=32 << 20, prefetch=1)(pos, shard)


_HBM = pl.BlockSpec(memory_space=pltpu.HBM)
_SEM = pl.BlockSpec(memory_space=pltpu.SEMAPHORE)
_KEEP_ORDER = pltpu.SideEffectType.DATAFLOW_SIDE_EFFECTING


def _in_hbm(x):
    return pltpu.with_memory_space_constraint(x, pltpu.HBM)


def _gather_copies(bufs, meta, send_sem, recv_sem):
    x, y, c = _mesh_pos()
    out = []
    for t, (kind, R, cs) in enumerate(meta):
        half = R // 2
        r0 = pl.multiple_of(c * half, SUBLANES)
        mine = _shard_window(bufs[t], kind, 2 * x + y, r0, half, cs)
        for p, (px, py) in enumerate(_other_chips(x, y)):
            k = 3 * t + p
            args = dict(send_sem=send_sem.at[k], recv_sem=recv_sem.at[k], device_id=(px, py, c),
                        device_id_type=MESH_IDS)
            out.append((pltpu.make_async_remote_copy(src_ref=mine, dst_ref=mine, **args),
                        pltpu.make_async_remote_copy(
                            src_ref=mine, dst_ref=_shard_window(bufs[t], kind, 2 * px + py, r0, half, cs), **args)))
    return out


def _all_gather_ici_start(name, gathered, shards, kinds, after):
    n = len(gathered)
    meta = [(kind, s.shape[0], s.shape[1]) for s, kind in zip(shards, kinds)]

    def body(*refs):
        send_sem, recv_sem = refs[n + 1], refs[n + 2]
        bufs = refs[n + 3:2 * n + 3]
        token = refs[2 * n + 3]
        for send, _ in _gather_copies(bufs, meta, send_sem, recv_sem):
            send.start()
        token[...] = jnp.zeros_like(token)

    out_shape = (pltpu.SemaphoreType.DMA((3 * n,)), pltpu.SemaphoreType.DMA((3 * n,)),
                 *[pltpu.HBM(g.shape, g.dtype) for g in gathered], jax.ShapeDtypeStruct((SUBLANES, LANES), F32))
    out_specs = (_SEM, _SEM, *[_HBM] * n, pl.BlockSpec(memory_space=pltpu.VMEM))
    res = pl.pallas_call(body, out_shape=out_shape, in_specs=[_HBM] * n + [_ANY], out_specs=out_specs,
                         input_output_aliases={t: 2 + t for t in range(n)}, name=name,
                         compiler_params=pltpu.CompilerParams(has_side_effects=_KEEP_ORDER))(
        *[_in_hbm(g) for g in gathered], after)
    return res[0], res[1], list(res[2:2 + n]), res[2 + n]


def _all_gather_ici_wait(name, bufs, send_sem, recv_sem, after, shards, kinds):
    n = len(bufs)
    meta = [(kind, s.shape[0], s.shape[1]) for s, kind in zip(shards, kinds)]
    after = list(after) if isinstance(after, (list, tuple)) else [after]

    def body(*refs):
        for send, recv in _gather_copies(refs[:n], meta, refs[n], refs[n + 1]):
            send.wait_send()
            recv.wait_recv()

    out_shape = tuple(pltpu.HBM(b.shape, b.dtype) for b in bufs)
    res = pl.pallas_call(body, out_shape=out_shape, in_specs=[_HBM] * n + [_SEM, _SEM] + [_ANY] * len(after),
                         out_specs=tuple([_HBM] * n), input_output_aliases={t: t for t in range(n)}, name=name,
                         compiler_params=pltpu.CompilerParams(has_side_effects=_KEEP_ORDER))(
        *bufs, send_sem, recv_sem, *after)
    return list(res)


def _all_gather_d2d(name, gathered, shards, kinds):
    n = len(gathered)
    meta = [(kind, s.shape[0], s.shape[1]) for s, kind in zip(shards, kinds)]

    def body(*refs):
        bufs = refs[n:2 * n]
        send_sem, recv_sem = refs[2 * n:]
        x, y, c = _mesh_pos()
        sends, recvs = [], []
        for t, (kind, R, cs) in enumerate(meta):
            half = R // 2
            mine = pl.multiple_of(c * half, SUBLANES)
            theirs = pl.multiple_of((1 - c) * half, SUBLANES)
            for p, (px, py) in enumerate(_other_chips(x, y)):
                k = 3 * t + p
                win = _shard_window(bufs[t], kind, 2 * px + py, mine, half, cs)
                cp = pltpu.make_async_remote_copy(src_ref=win, dst_ref=win, send_sem=send_sem.at[k],
                                                  recv_sem=recv_sem.at[k], device_id=(x, y, 1 - c),
                                                  device_id_type=MESH_IDS)
                cp.start()
                sends.append(cp)
                got = _shard_window(bufs[t], kind, 2 * px + py, theirs, half, cs)
                recvs.append(pltpu.make_async_remote_copy(src_ref=win, dst_ref=got, send_sem=send_sem.at[k],
                                                          recv_sem=recv_sem.at[k], device_id=(x, y, 1 - c),
                                                          device_id_type=MESH_IDS))
        for cp in recvs:
            cp.wait_recv()
        for cp in sends:
            cp.wait_send()

    scratch = [pltpu.SemaphoreType.DMA((3 * n,)), pltpu.SemaphoreType.DMA((3 * n,))]
    out_shape = [jax.ShapeDtypeStruct(g.shape, g.dtype) for g in gathered]
    return _call(body, name=name, out_shape=out_shape, in_specs=[_ANY] * n, out_specs=[_ANY] * n, scratch=scratch,
                 aliases={t: t for t in range(n)})(*gathered)


def _d2d_copies(bufs, meta, send_sem, recv_sem):
    x, y, c = _mesh_pos()
    out = []
    for t, (kind, R, cs) in enumerate(meta):
        half = R // 2
        mine = pl.multiple_of(c * half, SUBLANES)
        theirs = pl.multiple_of((1 - c) * half, SUBLANES)
        for p, (px, py) in enumerate(_other_chips(x, y)):
            k = 3 * t + p
            args = dict(send_sem=send_sem.at[k], recv_sem=recv_sem.at[k], device_id=(x, y, 1 - c),
                        device_id_type=MESH_IDS)
            win = _shard_window(bufs[t], kind, 2 * px + py, mine, half, cs)
            got = _shard_window(bufs[t], kind, 2 * px + py, theirs, half, cs)
            out.append((pltpu.make_async_remote_copy(src_ref=win, dst_ref=win, **args),
                        pltpu.make_async_remote_copy(src_ref=win, dst_ref=got, **args)))
    return out


def _all_gather_d2d_start(name, gathered, shards, kinds):
    n = len(gathered)
    meta = [(kind, s.shape[0], s.shape[1]) for s, kind in zip(shards, kinds)]

    def body(*refs):
        send_sem, recv_sem = refs[n], refs[n + 1]
        for send, _ in _d2d_copies(refs[n + 2:2 * n + 2], meta, send_sem, recv_sem):
            send.start()
        refs[2 * n + 2][...] = jnp.zeros((SUBLANES, LANES), F32)

    out_shape = (pltpu.SemaphoreType.DMA((3 * n,)), pltpu.SemaphoreType.DMA((3 * n,)),
                 *[pltpu.HBM(g.shape, g.dtype) for g in gathered], jax.ShapeDtypeStruct((SUBLANES, LANES), F32))
    out_specs = (_SEM, _SEM, *[_HBM] * n, pl.BlockSpec(memory_space=pltpu.VMEM))
    res = pl.pallas_call(body, out_shape=out_shape, in_specs=[_HBM] * n, out_specs=out_specs,
                         input_output_aliases={t: 2 + t for t in range(n)}, name=name,
                         compiler_params=pltpu.CompilerParams(has_side_effects=_KEEP_ORDER))(
        *[_in_hbm(g) for g in gathered])
    return res[0], res[1], list(res[2:2 + n]), res[2 + n]


def _all_gather_d2d_wait(name, bufs, send_sem, recv_sem, after, shards, kinds):
    n = len(bufs)
    meta = [(kind, s.shape[0], s.shape[1]) for s, kind in zip(shards, kinds)]

    def body(*refs):
        for send, recv in _d2d_copies(refs[:n], meta, refs[n], refs[n + 1]):
            send.wait_send()
            recv.wait_recv()

    out_shape = tuple(pltpu.HBM(b.shape, b.dtype) for b in bufs)
    res = pl.pallas_call(body, out_shape=out_shape, in_specs=[_HBM] * n + [_SEM, _SEM, _ANY],
                         out_specs=tuple([_HBM] * n), input_output_aliases={t: t for t in range(n)}, name=name,
                         compiler_params=pltpu.CompilerParams(has_side_effects=_KEEP_ORDER))(
        *bufs, send_sem, recv_sem, after)
    return list(res)


def _sibling_copies(grads, lands, send_sem, recv_sem):
    x, y, c = _mesh_pos()
    out = []
    for t in range(len(grads)):
        half = lands[t].shape[1]
        theirs = pl.multiple_of((1 - c) * half, SUBLANES)
        out.append(pltpu.make_async_remote_copy(src_ref=grads[t].at[:, pl.ds(theirs, half), :], dst_ref=lands[t],
                                                send_sem=send_sem.at[t], recv_sem=recv_sem.at[t],
                                                device_id=(x, y, 1 - c), device_id_type=MESH_IDS))
    return out


def _sibling_start(name, grads):
    n = len(grads)
    lands = [lax.empty((N_CHIPS, g.shape[1] // 2, g.shape[2]), g.dtype) for g in grads]

    def body(*refs):
        send_sem, recv_sem = refs[2 * n], refs[2 * n + 1]
        srcs, zones = refs[2 * n + 2:3 * n + 2], refs[3 * n + 2:4 * n + 2]
        token = refs[4 * n + 2]
        for cp in _sibling_copies(srcs, zones, send_sem, recv_sem):
            cp.start()
        token[...] = jnp.zeros_like(token)

    out_shape = (pltpu.SemaphoreType.DMA((n,)), pltpu.SemaphoreType.DMA((n,)),
                 *[pltpu.HBM(a.shape, a.dtype) for a in list(grads) + lands],
                 jax.ShapeDtypeStruct((SUBLANES, LANES), F32))
    out_specs = (_SEM, _SEM, *[_HBM] * (2 * n), pl.BlockSpec(memory_space=pltpu.VMEM))
    res = pl.pallas_call(body, out_shape=out_shape, in_specs=[_HBM] * (2 * n), out_specs=out_specs,
                         input_output_aliases={t: 2 + t for t in range(2 * n)}, name=name,
                         compiler_params=pltpu.CompilerParams(has_side_effects=_KEEP_ORDER))(
        *[_in_hbm(a) for a in list(grads) + lands])
    return res[0], res[1], list(res[2:2 + n]), list(res[2 + n:2 + 2 * n]), res[2 + 2 * n]


def _sibling_wait(name, grads, lands, send_sem, recv_sem, after):
    n = len(grads)
    after = list(after) if isinstance(after, (list, tuple)) else [after]

    def body(*refs):
        for cp in _sibling_copies(refs[:n], refs[n:2 * n], refs[2 * n], refs[2 * n + 1]):
            cp.wait_send()
            cp.wait_recv()

    out_shape = tuple(pltpu.HBM(a.shape, a.dtype) for a in list(grads) + list(lands))
    res = pl.pallas_call(body, out_shape=out_shape, in_specs=[_HBM] * (2 * n) + [_SEM, _SEM] + [_ANY] * len(after),
                         out_specs=tuple([_HBM] * (2 * n)), input_output_aliases={t: t for t in range(2 * n)},
                         name=name, compiler_params=pltpu.CompilerParams(has_side_effects=_KEEP_ORDER))(
        *grads, *lands, send_sem, recv_sem, *after)
    return list(res[:n]), list(res[n:])


def _exchange_copies(parts, lands, send_sem, recv_sem):
    x, y, c = _mesh_pos()
    out = []
    for t in range(len(parts)):
        for p, (px, py) in enumerate(_other_chips(x, y)):
            k = 3 * t + p
            out.append(pltpu.make_async_remote_copy(src_ref=parts[t].at[2 * px + py], dst_ref=lands[t].at[p],
                                                    send_sem=send_sem.at[k], recv_sem=recv_sem.at[k],
                                                    device_id=(px, py, c), device_id_type=MESH_IDS))
    return out


def _exchange_start(name, partials):
    n = len(partials)
    lands = [lax.empty((3,) + p.shape[1:], p.dtype) for p in partials]

    def body(*refs):
        send_sem, recv_sem = refs[2 * n], refs[2 * n + 1]
        parts, zones = refs[2 * n + 2:3 * n + 2], refs[3 * n + 2:4 * n + 2]
        token = refs[4 * n + 2]
        for cp in _exchange_copies(parts, zones, send_sem, recv_sem):
            cp.start()
        token[...] = jnp.zeros_like(token)

    out_shape = (pltpu.SemaphoreType.DMA((3 * n,)), pltpu.SemaphoreType.DMA((3 * n,)),
                 *[pltpu.HBM(a.shape, a.dtype) for a in partials + lands],
                 jax.ShapeDtypeStruct((SUBLANES, LANES), F32))
    out_specs = (_SEM, _SEM, *[_HBM] * (2 * n), pl.BlockSpec(memory_space=pltpu.VMEM))
    res = pl.pallas_call(body, out_shape=out_shape, in_specs=[_HBM] * (2 * n), out_specs=out_specs,
                         input_output_aliases={t: 2 + t for t in range(2 * n)}, name=name,
                         compiler_params=pltpu.CompilerParams(has_side_effects=_KEEP_ORDER))(
        *[_in_hbm(a) for a in partials + lands])
    return res[0], res[1], list(res[2:2 + n]), list(res[2 + n:2 + 2 * n]), res[2 + 2 * n]


def _exchange_wait(name, parts, lands, send_sem, recv_sem, after):
    n = len(parts)

    def body(*refs):
        for cp in _exchange_copies(refs[:n], refs[n:2 * n], refs[2 * n], refs[2 * n + 1]):
            cp.wait_send()
            cp.wait_recv()

    out_shape = tuple(pltpu.HBM(a.shape, a.dtype) for a in parts + lands)
    res = pl.pallas_call(body, out_shape=out_shape, in_specs=[_HBM] * (2 * n) + [_SEM, _SEM, _ANY],
                         out_specs=tuple([_HBM] * (2 * n)), input_output_aliases={t: t for t in range(2 * n)},
                         name=name, compiler_params=pltpu.CompilerParams(has_side_effects=_KEEP_ORDER))(
        *parts, *lands, send_sem, recv_sem, after)
    return list(res[n:])


def _share_halves(name, totals):
    n = len(totals)

    def body(*refs):
        bufs = refs[n:2 * n]
        send_sem, recv_sem = refs[2 * n:]
        x, y, c = _mesh_pos()
        sends, recvs = [], []
        for t, g in enumerate(totals):
            half = g.shape[0] // 2
            mine = bufs[t].at[pl.ds(pl.multiple_of(c * half, SUBLANES), half), :]
            theirs = bufs[t].at[pl.ds(pl.multiple_of((1 - c) * half, SUBLANES), half), :]
            cp = pltpu.make_async_remote_copy(src_ref=mine, dst_ref=mine, send_sem=send_sem.at[t],
                                              recv_sem=recv_sem.at[t], device_id=(x, y, 1 - c),
                                              device_id_type=MESH_IDS)
            cp.start()
            sends.append(cp)
            recvs.append(pltpu.make_async_remote_copy(src_ref=mine, dst_ref=theirs, send_sem=send_sem.at[t],
                                                      recv_sem=recv_sem.at[t], device_id=(x, y, 1 - c),
                                                      device_id_type=MESH_IDS))
        for cp in recvs:
            cp.wait_recv()
        for cp in sends:
            cp.wait_send()

    out_shape = [jax.ShapeDtypeStruct(g.shape, g.dtype) for g in totals]
    scratch = [pltpu.SemaphoreType.DMA((n,)), pltpu.SemaphoreType.DMA((n,))]
    return _call(body, name=name, out_shape=out_shape, in_specs=[_ANY] * n, out_specs=[_ANY] * n, scratch=scratch,
                 aliases={t: t for t in range(n)})(*totals)


def _gather_small(name, v):
    def body(v_ref, out_ref, send_sem, recv_sem, local_sem):
        x, y, c = _mesh_pos()
        me = 4 * x + 2 * y + c
        local = pltpu.make_async_copy(v_ref, out_ref.at[me], local_sem)
        local.start()
        sends, recvs = [], []
        for k in range(1, N_DEV):
            px = 1 - x if k & 4 else x
            py = 1 - y if k & 2 else y
            pc = 1 - c if k & 1 else c
            cp = pltpu.make_async_remote_copy(src_ref=v_ref, dst_ref=out_ref.at[me], send_sem=send_sem.at[k],
                                              recv_sem=recv_sem.at[k], device_id=(px, py, pc),
                                              device_id_type=MESH_IDS)
            cp.start()
            sends.append(cp)
            recvs.append(pltpu.make_async_remote_copy(
                src_ref=v_ref, dst_ref=out_ref.at[4 * px + 2 * py + pc], send_sem=send_sem.at[k],
                recv_sem=recv_sem.at[k], device_id=(px, py, pc), device_id_type=MESH_IDS))
        for cp in recvs:
            cp.wait_recv()
        for cp in sends:
            cp.wait_send()
        local.wait()

    scratch = [pltpu.SemaphoreType.DMA((N_DEV,)), pltpu.SemaphoreType.DMA((N_DEV,)), pltpu.SemaphoreType.DMA(())]
    return _call(body, name=name, out_shape=jax.ShapeDtypeStruct((N_DEV,) + v.shape, v.dtype), in_specs=[_ANY],
                 out_specs=_ANY, scratch=scratch)(v)


def _row_tile(rows, cols, n_arrays, mult=SUBLANES):
    budget = (24 << 20) // (2 * n_arrays * _round_up(cols, LANES) * 4)
    for t in (512, 256, 128, 64, 32, 16, 8):
        if t <= max(budget, mult) and rows % t == 0 and t % mult == 0:
            return t
    raise ValueError(f"no row tile for {rows} x {cols}")


def _chip_partial(name, pos, own, recv):
    _, half, C = recv.shape
    tr = _row_tile(half, C, 3, 16)
    nh = half // tr

    def body(pos_ref, own_ref, recv_ref, out_ref):
        out_ref[...] = (own_ref[...] + recv_ref[...]).astype(BF)

    blk = pl.BlockSpec((None, tr, C), lambda s, r, pos_ref: (s, r, 0))
    own_blk = pl.BlockSpec((None, tr, C), lambda s, r, pos_ref: (s, pos_ref[0] * nh + r, 0))
    return _call(body, name=name, out_shape=jax.ShapeDtypeStruct(recv.shape, BF), grid=(N_CHIPS, nh),
                 in_specs=[own_blk, blk], out_specs=blk, sem=("parallel", "parallel"), vmem=40 << 20,
                 prefetch=1)(pos, own, recv)


def _final_half(name, pos, own, recv, others):
    _, half, C = recv.shape
    tr = _row_tile(half, C, 4, 16)
    nh = half // tr

    def body(pos_ref, own_ref, recv_ref, oth_ref, out_ref):
        acc = own_ref[...] + recv_ref[...]
        for p in range(3):
            acc = acc + oth_ref[p].astype(F32)
        out_ref[...] = acc

    own_blk = pl.BlockSpec((None, tr, C), lambda r, pos_ref: (pos_ref[1], pos_ref[0] * nh + r, 0))
    recv_blk = pl.BlockSpec((None, tr, C), lambda r, pos_ref: (pos_ref[1], r, 0))
    oth_blk = pl.BlockSpec((3, tr, C), lambda r, pos_ref: (0, r, 0))
    out_blk = pl.BlockSpec((tr, C), lambda r, pos_ref: (pos_ref[0] * nh + r, 0))
    return _call(body, name=name, out_shape=jax.ShapeDtypeStruct((2 * half, C), F32), grid=(nh,),
                 in_specs=[own_blk, recv_blk, oth_blk], out_specs=out_blk, sem=("parallel",), vmem=40 << 20,
                 prefetch=1)(pos, own, recv, others)


def _adamw(name, w, g, m, v):
    R, C = w.shape
    tr = R if R < SUBLANES or R % SUBLANES else _row_tile(R, C, 7)
    c1 = 1.0 - ADAM_B1 ** ADAM_STEP
    c2 = 1.0 - ADAM_B2 ** ADAM_STEP

    def body(w_ref, g_ref, m_ref, v_ref, d_ref, nm_ref, nv_ref):
        gv = g_ref[...]
        nm = ADAM_B1 * m_ref[...] + (1.0 - ADAM_B1) * gv
        nv = ADAM_B2 * v_ref[...] + (1.0 - ADAM_B2) * (gv * gv)
        d_ref[...] = -ADAM_LR * ((nm / c1) / (jnp.sqrt(nv / c2) + ADAM_EPS) + ADAM_WD * w_ref[...])
        nm_ref[...] = nm
        nv_ref[...] = nv

    blk = pl.BlockSpec((tr, C), lambda r: (r, 0))
    out_shape = [jax.ShapeDtypeStruct((R, C), F32)] * 3
    return _call(body, name=name, out_shape=out_shape, grid=(R // tr,), in_specs=[blk] * 4, out_specs=[blk] * 3,
                 sem=("parallel",), vmem=40 << 20)(w, g, m, v)


def _sum_devices(name, gathered):
    _, R, _ = gathered.shape

    def body(g_ref, o_ref):
        acc = g_ref[0]
        for d in range(1, N_DEV):
            acc = acc + g_ref[d]
        o_ref[...] = acc

    return _call(body, name=name, out_shape=jax.ShapeDtypeStruct((R, LANES), F32), grid=(1,),
                 in_specs=[pl.BlockSpec((N_DEV, R, LANES), lambda i: (0, 0, 0))],
                 out_specs=pl.BlockSpec((R, LANES), lambda i: (0, 0)), sem=("arbitrary",), vmem=16 << 20)(gathered)


def _ffn_fwd(tag, h, gain, wg, wu, wd, tm):
    L, D = h.shape
    F = wg.shape[1]
    tn = _pick(F, (512, 256, 128))
    n = _rmsnorm_fwd(f"{tag}_norm", h, gain, tm)

    def gate_up(accs, _):
        a, u = accs
        return a, u, a * _sigmoid(a) * u

    mn = _mn(tm, tn)
    a, u, s = _matmul(f"{tag}_gate_up", [n], [wg, wu], [(0, 0, 0), (0, 1, 1)], 2, gate_up,
                      [((L, F), BF) + mn] * 3, M=L, N=F, K=D, tm=tm, tn=tn, tk=D, n_outer=True)
    started = None
    if callable(wd):
        wd, started = wd(s)
    td = _pick(D, (512, 256, 128))
    (h_out,) = _matmul(f"{tag}_down", [s], [wd], [(0, 0, 0)], 1,
                       lambda accs, ex: [ex[0] + FFN_RESIDUAL_WEIGHT * accs[0]], [((L, D), F32) + _mn(tm, td)],
                       M=L, N=D, K=F, tm=tm, tn=td, tk=F, extras=[(h,) + _mn(tm, td)], n_outer=True, after=started)
    return h_out, (h, n, a, u, s), wd


def _ffn_bwd(tag, dh, dh_bf, saved, gain, wg, wu, wd, tm, cs_ff, after=None, midway=None, on_weight_grads=None):
    h, n, a, u, s = saved
    L, D = h.shape
    F = wg.shape[1]
    tn = _pick(F, (512, 256, 128))
    tkl = _pick(L, (1408, 384, 256, 128))

    def act_grad(accs, ex):
        ds = FFN_RESIDUAL_WEIGHT * accs[0]
        av, uv = ex[0].astype(F32), ex[1].astype(F32)
        sg = _sigmoid(av)
        return ds * uv * sg * (1.0 + av * (1.0 - sg)), ds * av * sg

    mn = _mn(tm, tn)
    da, du = _matmul(f"{tag}_dact", [dh_bf], [wd], [(0, 0, 0)], 1, act_grad, [((L, F), BF) + mn] * 2, M=L, N=F,
                     K=D, tm=tm, tn=tn, tk=D, tb=True, extras=[(a,) + mn, (u,) + mn], n_outer=True, after=after)
    td = _pick(D, (512, 256, 128))
    (dwd,) = _matmul(f"{tag}_dwd", [s], [dh_bf], [(0, 0, 0)], 1, lambda accs, _: [FFN_RESIDUAL_WEIGHT * accs[0]],
                     [((N_CHIPS, cs_ff, D), F32, (None, cs_ff, td), lambda i, j: (i, 0, j))], M=F, N=D, K=L,
                     tm=cs_ff, tn=td, tk=tkl, ta=True, after=after)
    tmw = _pick(D, (512, 256, 128))
    shard_out = ((N_CHIPS, D, cs_ff), F32, (None, tmw, cs_ff), lambda i, j: (j, i, 0))
    dwg, dwu = _matmul(f"{tag}_dwgu", [n], [da, du], [(0, 0, 0), (0, 1, 1)], 2, lambda accs, _: accs,
                       [shard_out] * 2, M=D, N=F, K=L, tm=tmw, tn=cs_ff, tk=tkl, ta=True,
                       after=midway(dwd) if midway else None)
    started = on_weight_grads(dwg, dwu, dwd) if on_weight_grads else None
    tdn = _pick(D, (256, 128))
    (dn,) = _matmul(f"{tag}_dn", [da, du], [wg, wu], [(0, 0, 0), (1, 1, 0)], 1, lambda accs, _: accs,
                    [((L, D), F32) + _mn(tm, tdn)], M=L, N=D, K=F, tm=tm, tn=tdn, tk=F, tb=True, after=started)
    dh_in, dh_in_bf, dgain = _rmsnorm_bwd(f"{tag}_dnorm", dn, h, gain, dh, tm)
    return dh_in, dh_in_bf, dgain


def kernel(x, meta_tokens, ffn1_norm, ffn1_w_gate, ffn1_w_up, ffn1_w_down, mix_norm, w_in, b_forget, fox_q_norm, fox_k_norm, w_branch_fox, w_branch_sb, w_out, ffn2_norm, ffn2_w_gate, ffn2_w_up, ffn2_w_down, loss_target, m_meta_tokens, m_ffn1_norm, m_ffn1_w_gate, m_ffn1_w_up, m_ffn1_w_down, m_mix_norm, m_w_in, m_b_forget, m_fox_q_norm, m_fox_k_norm, m_w_branch_fox, m_w_branch_sb, m_w_out, m_ffn2_norm, m_ffn2_w_gate, m_ffn2_w_up, m_ffn2_w_down, v_meta_tokens, v_ffn1_norm, v_ffn1_w_gate, v_ffn1_w_up, v_ffn1_w_down, v_mix_norm, v_w_in, v_b_forget, v_fox_q_norm, v_fox_k_norm, v_w_branch_fox, v_w_branch_sb, v_w_out, v_ffn2_norm, v_ffn2_w_gate, v_ffn2_w_up, v_ffn2_w_down):
    weights = dict(meta_tokens=meta_tokens, ffn1_norm=ffn1_norm, ffn1_w_gate=ffn1_w_gate, ffn1_w_up=ffn1_w_up,
                   ffn1_w_down=ffn1_w_down, mix_norm=mix_norm, w_in=w_in, b_forget=b_forget, fox_q_norm=fox_q_norm,
                   fox_k_norm=fox_k_norm, w_branch_fox=w_branch_fox, w_branch_sb=w_branch_sb, w_out=w_out,
                   ffn2_norm=ffn2_norm, ffn2_w_gate=ffn2_w_gate, ffn2_w_up=ffn2_w_up, ffn2_w_down=ffn2_w_down)
    moments_m = dict(meta_tokens=m_meta_tokens, ffn1_norm=m_ffn1_norm, ffn1_w_gate=m_ffn1_w_gate,
                     ffn1_w_up=m_ffn1_w_up, ffn1_w_down=m_ffn1_w_down, mix_norm=m_mix_norm, w_in=m_w_in,
                     b_forget=m_b_forget, fox_q_norm=m_fox_q_norm, fox_k_norm=m_fox_k_norm,
                     w_branch_fox=m_w_branch_fox, w_branch_sb=m_w_branch_sb, w_out=m_w_out, ffn2_norm=m_ffn2_norm,
                     ffn2_w_gate=m_ffn2_w_gate, ffn2_w_up=m_ffn2_w_up, ffn2_w_down=m_ffn2_w_down)
    moments_v = dict(meta_tokens=v_meta_tokens, ffn1_norm=v_ffn1_norm, ffn1_w_gate=v_ffn1_w_gate,
                     ffn1_w_up=v_ffn1_w_up, ffn1_w_down=v_ffn1_w_down, mix_norm=v_mix_norm, w_in=v_w_in,
                     b_forget=v_b_forget, fox_q_norm=v_fox_q_norm, fox_k_norm=v_fox_k_norm,
                     w_branch_fox=v_w_branch_fox, w_branch_sb=v_w_branch_sb, w_out=v_w_out, ffn2_norm=v_ffn2_norm,
                     ffn2_w_gate=v_ffn2_w_gate, ffn2_w_up=v_ffn2_w_up, ffn2_w_down=v_ffn2_w_down)
    names = list(weights)

    _, S, D = x.shape
    NM = meta_tokens.shape[0]
    L_real = NM + S
    L = _round_up(L_real, ATT_TILE)
    nblk = L // ATT_TILE
    nbp = _round_up(nblk, SUBLANES)
    cs_ff = ffn1_w_gate.shape[2]
    F = N_CHIPS * cs_ff
    H = b_forget.shape[1]
    FW = w_branch_fox.shape[1]
    SW = w_branch_sb.shape[1]
    HS = SW // HEAD_DIM
    cs_in = w_in.shape[2]
    W_IN = N_CHIPS * cs_in
    assert FW == H * HEAD_DIM and W_IN == 3 * FW + H + 3 * SW + 2 * D
    cs_d = D // N_CHIPS
    tm = _pick(L, (384, 256, 128))

    x_pos, y_pos, c_pos = _mesh_pos()
    pos = jnp.stack([c_pos, 2 * x_pos + y_pos]).astype(I32)

    shard_of = {
        "ffn1_w_gate": (ffn1_w_gate[0], "col"), "ffn1_w_up": (ffn1_w_up[0], "col"),
        "ffn1_w_down": (ffn1_w_down[0], "maj"), "w_in": (w_in[0], "maj"),
        "w_branch_fox": (w_branch_fox[0], "col"), "w_branch_sb": (w_branch_sb[0], "col"),
        "w_out": (w_out[0], "maj"), "ffn2_w_gate": (ffn2_w_gate[0], "col"), "ffn2_w_up": (ffn2_w_up[0], "col"),
        "ffn2_w_down": (ffn2_w_down[0], "maj"),
    }
    g_names = list(shard_of) + ["meta_tokens"]
    shards = [shard_of[k][0] for k in shard_of] + [meta_tokens]
    kinds = [shard_of[k][1] for k in shard_of] + ["col"]
    dtypes = [BF] * len(shard_of) + [F32]
    info = {k: (s, kind) for k, s, kind in zip(g_names, shards, kinds)}
    placed = {k: _place_shard(f"place_{k}", pos, s, kind, dt)
              for k, s, kind, dt in zip(g_names, shards, kinds, dtypes)}
    groups = [["meta_tokens", "ffn1_w_gate", "ffn1_w_up"], ["ffn1_w_down"], ["w_in"],
              ["w_branch_fox", "w_branch_sb", "w_out", "ffn2_w_gate", "ffn2_w_up", "ffn2_w_down"]]
    in_flight = []
    all_started = jnp.zeros((SUBLANES, LANES), F32)
    for gi, grp in enumerate(groups):
        g_shards, g_kinds = [info[k][0] for k in grp], [info[k][1] for k in grp]
        in_flight.append(_all_gather_ici_start(f"gather_start_{gi}", [placed[k] for k in grp], g_shards, g_kinds,
                                               all_started))
        all_started = in_flight[-1][3]
    full = {}

    def arrive(gi, after):
        grp = groups[gi]
        g_shards, g_kinds = [info[k][0] for k in grp], [info[k][1] for k in grp]
        send_sem, recv_sem, bufs, _ = in_flight[gi]
        bufs = _all_gather_ici_wait(f"gather_wait_{gi}", bufs, send_sem, recv_sem, after, g_shards, g_kinds)
        full.update(zip(grp, _all_gather_d2d(f"gather_d2d_{gi}", bufs, g_shards, g_kinds)))

    def arrive_over_ici(gi, after):
        grp = groups[gi]
        g_shards, g_kinds = [info[k][0] for k in grp], [info[k][1] for k in grp]
        send_sem, recv_sem, bufs, _ = in_flight[gi]
        bufs = _all_gather_ici_wait(f"gather_wait_{gi}", bufs, send_sem, recv_sem, after, g_shards, g_kinds)
        in_flight[gi] = _all_gather_d2d_start(f"gather_d2d_start_{gi}", bufs, g_shards, g_kinds)
        return in_flight[gi][3]

    def arrive_over_d2d(gi, after):
        grp = groups[gi]
        g_shards, g_kinds = [info[k][0] for k in grp], [info[k][1] for k in grp]
        send_sem, recv_sem, bufs, _ = in_flight[gi]
        full.update(zip(grp, _all_gather_d2d_wait(f"gather_d2d_wait_{gi}", bufs, send_sem, recv_sem, after, g_shards,
                                                  g_kinds)))

    target = jnp.concatenate([jnp.zeros((NM, D), F32), loss_target[0], jnp.zeros((L - L_real, D), F32)], axis=0)
    arrive(0, [all_started, target, moments_m["w_in"][0], moments_v["w_in"][0]])
    wg1, wu1 = full["ffn1_w_gate"], full["ffn1_w_up"]
    c_f = 3 * FW
    QKV_S, GATES, FCOL = 3 * FW, 3 * FW + 3 * SW, 3 * FW + 3 * SW + 2 * D
    W_PROJ = FCOL + LANES

    h0 = jnp.concatenate([full["meta_tokens"], x[0], jnp.zeros((L - L_real, D), F32)], axis=0)

    def late_wd1(s):
        arrive(1, s)
        return full["ffn1_w_down"].reshape(F, D), arrive_over_ici(2, s)

    h1, saved1, wd1 = _ffn_fwd("ffn1", h0, ffn1_norm, wg1, wu1, late_wd1, tm)

    arrive_over_d2d(2, h1)
    w_in_full = jnp.transpose(full["w_in"], (1, 0, 2)).reshape(D, W_IN)
    w_proj = jnp.concatenate([w_in_full[:, :c_f], w_in_full[:, c_f + H:],
                              jnp.pad(w_in_full[:, c_f:c_f + H], ((0, 0), (0, LANES - H)))], axis=1)
    n2 = _rmsnorm_fwd("mix_norm", h1, mix_norm, tm)
    tp = _pick(FCOL, (512, 256, 128))
    (proj,) = _matmul("in_proj", [n2], [w_proj], [(0, 0, 0)], 1, lambda accs, _: accs,
                      [((L, FCOL), BF) + _mn(tm, tp)], M=L, N=FCOL, K=D, tm=tm, tn=tp, tk=D, n_outer=True)
    (f_logit,) = _matmul("forget_proj", [n2], [w_proj], [(0, 0, 0)], 1, lambda accs, _: accs,
                         [((L, LANES), F32) + _mn(tm, LANES)], M=L, N=LANES, K=D, tm=tm, tn=LANES, tk=D,
                         b_off=FCOL // LANES)
    fl = jnp.pad(jnp.transpose(f_logit[:, :H]).reshape(H, nblk, LANES), ((0, 0), (0, nbp - nblk), (0, 0)))
    bias = jnp.broadcast_to(b_forget[0][:, None, None], (H, 1, LANES))
    c_row = _cum_fwd("forget_cumsum", fl, bias, nblk)
    c_row = c_row + arrive_over_ici(3, proj)[0, 0]
    c_col = c_row[:, :nblk].reshape(H, L, 1)
    gq, gk = fox_q_norm[0][:, None, :], fox_k_norm[0][:, None, :]
    qn, kn = _qknorm_fwd("fox_qk_norm", proj, gq, gk, H, 0, FW, tm)
    o_fox, lse = _fox_fwd("fox_attention", qn, kn, proj, c_row, c_col, H, ATT_HEADS, 0, 0, 2 * FW)
    o_sb = _sb_fwd("sb_attention", proj, HS, min(HS, ATT_HEADS_SB_FWD), QKV_S, QKV_S + SW, QKV_S + 2 * SW)

    arrive_over_d2d(3, o_fox)
    wg2, wu2 = full["ffn2_w_gate"], full["ffn2_w_up"]
    wd2 = full["ffn2_w_down"].reshape(F, D)
    wbf, wbs = full["w_branch_fox"], full["w_branch_sb"]
    wo = full["w_out"].reshape(D, D)
    td = _pick(D, (512, 256, 128))

    def merge(accs, ex):
        bf_, bs_ = accs
        return _sigmoid(ex[0].astype(F32)) * bf_ + _sigmoid(ex[1].astype(F32)) * bs_, bf_, bs_

    merged, br_f, br_s = _matmul("branch_merge", [o_fox, o_sb], [wbf, wbs], [(0, 0, 0), (1, 1, 1)], 2, merge,
                                 [((L, D), BF) + _mn(tm, td)] * 3, M=L, N=D, K=FW, tm=tm, tn=td, tk=FW,
                                 extras=[(proj,) + _mn(tm, td, GATES), (proj,) + _mn(tm, td, GATES + D)],
                                 n_outer=True)
    (h2,) = _matmul("out_proj", [merged], [wo], [(0, 0, 0)], 1, lambda accs, ex: [ex[0] + accs[0]],
                    [((L, D), F32) + _mn(tm, td)], M=L, N=D, K=D, tm=tm, tn=td, tk=D, extras=[(h1,) + _mn(tm, td)],
                    n_outer=True)

    h3, saved2, _ = _ffn_fwd("ffn2", h2, ffn2_norm, wg2, wu2, wd2, tm)
    dh3, dh3_bf, loss_part = _loss_grad("loss", h3, target, NM, S, tm)

    scatters, totals, grads, updates = {}, {}, {}, {}

    def adamw(k):
        w = weights[k]
        shape2 = (1, w.size) if w.size < LANES * SUBLANES else (w.size // w.shape[-1], w.shape[-1])
        res = _adamw(f"adamw_{k}", w.reshape(shape2), grads[k].reshape(shape2), moments_m[k].reshape(shape2),
                     moments_v[k].reshape(shape2))
        updates[k] = [r.reshape(w.shape) for r in res]

    def swap_begin(tag, keys, local):
        send_sem, recv_sem, grads, lands, token = _sibling_start(f"grads_to_sibling_start_{tag}", list(local))
        scatters[tag] = (keys, send_sem, recv_sem, grads, lands)
        return token

    def exchange_begin(tag, after):
        keys, send_sem, recv_sem, grads, lands = scatters[tag]
        local, from_sibling = _sibling_wait(f"grads_to_sibling_wait_{tag}", grads, lands, send_sem, recv_sem, after)
        partials = [_chip_partial(f"chip_sum_{k}", pos, g, r) for k, g, r in zip(keys, local, from_sibling)]
        send_sem, recv_sem, parts, lands, token = _exchange_start(f"grads_to_owner_start_{tag}", partials)
        scatters[tag] = (keys, local, from_sibling, send_sem, recv_sem, parts, lands)
        return token

    def scatter_end(tag, after):
        keys, local, from_sibling, send_sem, recv_sem, parts, lands = scatters[tag]
        from_chips = _exchange_wait(f"grads_to_owner_wait_{tag}", parts, lands, send_sem, recv_sem, after)
        totals.update({k: _final_half(f"total_{k}", pos, g, r, o)
                       for k, g, r, o in zip(keys, local, from_sibling, from_chips)})

    dh2, dh2_bf, dg_ffn2 = _ffn_bwd(
        "ffn2", dh3, dh3_bf, saved2, ffn2_norm, wg2, wu2, wd2, tm, cs_ff,
        on_weight_grads=lambda *dw: swap_begin("ffn2", ["ffn2_w_gate", "ffn2_w_up", "ffn2_w_down"], dw))
    ffn2_exchanging = exchange_begin("ffn2", dh2)

    def gate_grad(accs, ex):
        dm = accs[0]
        gf, gs, bf_, bs_ = [e.astype(F32) for e in ex]
        sf, ss = _sigmoid(gf), _sigmoid(gs)
        return dm * bf_ * sf * (1.0 - sf), dm * bs_ * ss * (1.0 - ss), dm * sf, dm * ss

    mn_d = _mn(tm, td)
    dgf, dgs, dbr_f, dbr_s = _matmul(
        "d_merged", [dh2_bf], [wo], [(0, 0, 0)], 1, gate_grad, [((L, D), BF) + mn_d] * 4, M=L, N=D, K=D, tm=tm,
        tn=td, tk=D, tb=True, extras=[(proj,) + _mn(tm, td, GATES), (proj,) + _mn(tm, td, GATES + D),
                                      (br_f,) + mn_d, (br_s,) + mn_d], n_outer=True, after=ffn2_exchanging)
    tkl = _pick(L, (1408, 384, 256, 128))
    (dwo,) = _matmul("d_w_out", [merged], [dh2_bf], [(0, 0, 0)], 1, lambda accs, _: accs,
                     [((N_CHIPS, cs_d, D), F32, (None, cs_d, td), lambda i, j: (i, 0, j))], M=D, N=D, K=L, tm=cs_d,
                     tn=td, tk=tkl, ta=True)
    tw = _pick(FW, (512, 256, 128))
    do_fox, do_sb = _matmul("d_branch_in", [dbr_f, dbr_s], [wbf, wbs], [(0, 0, 0), (1, 1, 1)], 2,
                            lambda accs, _: accs, [((L, FW), BF) + _mn(tm, tw)] * 2, M=L, N=FW, K=D, tm=tm, tn=tw,
                            tk=D, tb=True)
    tmb = _pick(FW, (1024, 512, 256, 128))
    dwbf, dwbs = _matmul("d_w_branch", [o_fox, o_sb], [dbr_f, dbr_s], [(0, 0, 0), (1, 1, 1)], 2,
                         lambda accs, _: accs,
                         [((N_CHIPS, FW, cs_d), F32, (None, tmb, cs_d), lambda i, j: (j, i, 0))] * 2, M=FW, N=D, K=L,
                         tm=tmb, tn=cs_d, tk=tkl, ta=True)

    dqn, dkn, dfv, dcs, drs = _fox_bwd("fox_attention_bwd", qn, kn, proj, c_row, c_col, o_fox, do_fox, lse, H,
                                       ATT_HEADS, 0, 0, 2 * FW)
    dsq, dsk_t, dsv_t = _sb_bwd("sb_attention_bwd", proj, do_sb, HS, ATT_HEADS, QKV_S, QKV_S + SW,
                                QKV_S + 2 * SW)
    dsk = jnp.transpose(dsk_t, (0, 2, 1)).reshape(L, SW)
    dsv = jnp.transpose(dsv_t, (0, 2, 1)).reshape(L, SW)
    dfq, dfk, dgq, dgk = _qknorm_bwd("fox_qk_norm_bwd", proj, dqn, dkn, gq, gk, H, 0, FW, tm)
    drs_row = jnp.pad(drs.reshape(H, nblk, LANES), ((0, 0), (0, nbp - nblk), (0, 0)))
    dfl, dbias = _cum_bwd("forget_cumsum_bwd", drs_row, dcs, fl, bias, nblk)
    dfl_cols = jnp.pad(jnp.transpose(dfl[:, :nblk].reshape(H, L)), ((0, 0), (0, LANES - H))).astype(BF)
    dproj = jnp.concatenate([dfq, dfk, dfv.astype(BF), dsq, dsk.astype(BF), dsv.astype(BF), dgf, dgs, dfl_cols],
                            axis=1)

    tdn = _pick(D, (256, 128))
    (dn2,) = _matmul("d_mix_norm_in", [dproj], [w_proj], [(0, 0, 0)], 1, lambda accs, _: accs,
                     [((L, D), F32) + _mn(tm, tdn)], M=L, N=D, K=W_PROJ, tm=tm, tn=tdn, tk=W_PROJ, tb=True)
    tmw = _pick(D, (1024, 512, 256, 128))
    tnp = _pick(W_PROJ, (1152, 640, 512, 384, 256, 128))
    (dw_proj,) = _matmul("d_w_in", [n2], [dproj], [(0, 0, 0)], 1, lambda accs, _: accs,
                         [((D, W_PROJ), F32) + _mn(tmw, tnp)], M=D, N=W_PROJ, K=L, tm=tmw, tn=tnp, tk=tkl,
                         ta=True)
    dh1, dh1_bf, dg_mix = _rmsnorm_bwd("mix_dnorm", dn2, h1, mix_norm, dh2, tm)
    dw_in_ref = jnp.concatenate([dw_proj[:, :c_f], dw_proj[:, FCOL:FCOL + H], dw_proj[:, c_f:FCOL]], axis=1)
    dw_in = jnp.transpose(dw_in_ref.reshape(D, N_CHIPS, cs_in), (1, 0, 2))
    scatter_end("ffn2", dh1)
    ffn2_keys = list(totals)
    grads.update(zip(ffn2_keys, _share_halves("grads_to_core_pair_ffn2", [totals.pop(k) for k in ffn2_keys])))
    mix_swapping = swap_begin("mix", ["w_in", "w_branch_fox", "w_branch_sb", "w_out"], [dw_in, dwbf, dwbs, dwo])

    def ffn1_grads(*dw):
        swap_begin("ffn1", ["ffn1_w_gate", "ffn1_w_up", "ffn1_w_down"], dw)
        for k in ffn2_keys:
            adamw(k)
        return exchange_begin("ffn1", [updates[k][0] for k in ffn2_keys])

    dh0, _, dg_ffn1 = _ffn_bwd("ffn1", dh1, dh1_bf, saved1, ffn1_norm, wg1, wu1, wd1, tm, cs_ff, after=mix_swapping,
                               midway=lambda done: exchange_begin("mix", done), on_weight_grads=ffn1_grads)
    grad_x = dh0[NM:L_real][None]
    scatter_end("mix", dh0)
    scatter_end("ffn1", dh0)
    big = list(totals)
    grads.update(zip(big, _share_halves("grads_to_core_pair", [totals.pop(k) for k in big])))

    small = [loss_part[:, :1].reshape(1), dh0[:NM].reshape(-1), dg_ffn1.reshape(-1), dg_mix.reshape(-1),
             dg_ffn2.reshape(-1), dbias[:, 0, 0], dgq.reshape(-1), dgk.reshape(-1)]
    sizes = [s.shape[0] for s in small]
    flat = jnp.concatenate(small)
    rows = _round_up(-(-flat.shape[0] // LANES), SUBLANES)
    packed = jnp.pad(flat, (0, rows * LANES - flat.shape[0])).reshape(rows, LANES)
    total = _sum_devices("sum_small", _gather_small("gather_small", packed)).reshape(-1)
    pieces, off = [], 0
    for n_el in sizes:
        pieces.append(total[off:off + n_el])
        off += n_el
    loss = pieces[0][0]
    d_meta = lax.dynamic_slice_in_dim(pieces[1].reshape(NM, D), pos[1] * cs_d, cs_d, axis=1)
    grads.update(meta_tokens=d_meta, ffn1_norm=pieces[2].reshape(1, D), mix_norm=pieces[3].reshape(1, D),
                 ffn2_norm=pieces[4].reshape(1, D), b_forget=pieces[5].reshape(1, H),
                 fox_q_norm=pieces[6].reshape(1, H, HEAD_DIM), fox_k_norm=pieces[7].reshape(1, H, HEAD_DIM))

    for k in names:
        if k not in updates:
            adamw(k)
    return (loss, grad_x, *[grads[k].reshape(weights[k].shape) for k in names],
            *[updates[k][i] for i in range(3) for k in names])
```

```python
import functools

import jax
import jax.numpy as jnp
from jax import lax
from jax.experimental import pallas as pl
from jax.experimental.pallas import tpu as pltpu

F32 = jnp.float32
BF = jnp.bfloat16
I32 = jnp.int32

HEAD_DIM = 128
RMS_EPS = 1e-6
FFN_RESIDUAL_WEIGHT = 0.5
ADAM_LR = 0.001
ADAM_B1 = 0.9
ADAM_B2 = 0.999
ADAM_EPS = 1e-08
ADAM_WD = 0.01
ADAM_STEP = 10

LANES = 128
SUBLANES = 8
ATT_TILE = 128
ATT_HEADS = 4
ATT_HEADS_SB_FWD = 8
VMEM_CAP = 56 * 1024 * 1024
MESH_IDS = pl.DeviceIdType.MESH
N_CHIPS = 4
N_DEV = 8


def _pick(n, cands):
    for c in cands:
        if c <= n and n % c == 0:
            return c
    raise ValueError(f"no tile for {n} among {cands}")


def _round_up(n, m):
    return (n + m - 1) // m * m


def _tile_bytes(shape, dtype):
    item = jnp.dtype(dtype).itemsize
    dims = [d for d in shape if d is not None]
    if not dims:
        return 4 * LANES * SUBLANES
    last = _round_up(dims[-1], LANES)
    sub = _round_up(dims[-2], SUBLANES * (4 // item)) if len(dims) > 1 else 1
    lead = 1
    for d in dims[:-2]:
        lead *= d
    return lead * sub * last * item


def _vmem_limit(blocks, scratch=(), temps=0):
    need = 2 * sum(_tile_bytes(s, d) for s, d in blocks) + sum(_tile_bytes(s, d) for s, d in scratch) + temps
    return int(min(VMEM_CAP, max(need + (4 << 20), 16 << 20)))


def _call(body, *, name, out_shape, grid=(), in_specs=None, out_specs=None, scratch=(), sem=None, vmem=None,
          aliases=None, prefetch=0):
    params = pltpu.CompilerParams(dimension_semantics=sem, vmem_limit_bytes=vmem)
    if prefetch:
        grid_spec = pltpu.PrefetchScalarGridSpec(num_scalar_prefetch=prefetch, grid=grid, in_specs=in_specs,
                                                 out_specs=out_specs, scratch_shapes=scratch)
        return pl.pallas_call(body, out_shape=out_shape, grid_spec=grid_spec, name=name, compiler_params=params,
                              input_output_aliases=aliases or {})
    return pl.pallas_call(body, out_shape=out_shape, grid=grid, in_specs=in_specs, out_specs=out_specs,
                          scratch_shapes=scratch, name=name, compiler_params=params,
                          input_output_aliases=aliases or {})


def _dot(a, b, ca, cb):
    return lax.dot_general(a, b, (((ca,), (cb,)), ((), ())), preferred_element_type=F32)


def _sigmoid(x):
    return 1.0 / (1.0 + jnp.exp(-x))


def _sigmoid_fast(x):
    return pl.reciprocal(1.0 + jnp.exp(-x), approx=True)


def _log_sigmoid(x):
    return jnp.minimum(x, 0.0) - jnp.log1p(jnp.exp(-jnp.abs(x)))


def _split(x, parts):
    pieces = []
    rem = x
    for p in range(parts):
        piece = rem.astype(BF)
        pieces.append(piece)
        if p + 1 < parts:
            rem = rem - piece.astype(F32)
    return pieces


def _pieces_dot(pieces, ones_bf):
    out = None
    for piece in pieces:
        d = _dot(piece, ones_bf, 1, 0)
        out = d if out is None else out + d
    return out


def _split_dot(x, ones_bf, parts):
    return _pieces_dot(_split(x, parts), ones_bf)


def _matmul(name, a_list, b_list, pairs, n_acc, epi, outs, *, M, N, K, tm, tn, tk, ta=False, tb=False,
            extras=(), n_outer=False, b_off=0, after=None):
    if after is not None:
        user_epi = epi
        extras = list(extras) + [(after, (SUBLANES, LANES), lambda i, j: (0, 0))]
        epi = lambda accs, ex: user_epi(accs, ex[:-1])
    gi, gj, nk = M // tm, N // tn, K // tk
    assert gi * tm == M and gj * tn == N and nk * tk == K, (name, M, N, K, tm, tn, tk)
    n_a, n_b, n_e, n_o = len(a_list), len(b_list), len(extras), len(outs)

    def ij(g0, g1):
        return (g1, g0) if n_outer else (g0, g1)

    def a_map(g0, g1, k):
        i, _ = ij(g0, g1)
        return (k, i) if ta else (i, k)

    def b_map(g0, g1, k):
        _, j = ij(g0, g1)
        return (j + b_off, k) if tb else (k, j + b_off)

    def tile_map(fn):
        return lambda g0, g1, k: fn(*ij(g0, g1))

    a_block = (tk, tm) if ta else (tm, tk)
    b_block = (tn, tk) if tb else (tk, tn)
    in_specs = ([pl.BlockSpec(a_block, a_map)] * n_a + [pl.BlockSpec(b_block, b_map)] * n_b
                + [pl.BlockSpec(bs, tile_map(fn)) for _, bs, fn in extras])
    out_specs = [pl.BlockSpec(bs, tile_map(fn)) for _, _, bs, fn in outs]
    out_shape = [jax.ShapeDtypeStruct(s, d) for s, d, _, _ in outs]
    scratch = [pltpu.VMEM((tm, tn), F32) for _ in range(n_acc)] if nk > 1 else []

    def body(*refs):
        a_refs = refs[:n_a]
        b_refs = refs[n_a:n_a + n_b]
        e_refs = refs[n_a + n_b:n_a + n_b + n_e]
        o_refs = refs[n_a + n_b + n_e:n_a + n_b + n_e + n_o]
        acc_refs = refs[n_a + n_b + n_e + n_o:]

        def products():
            accs = [None] * n_acc
            for ai, bi, ci in pairs:
                a = a_refs[ai][...]
                b = b_refs[bi][...]
                d = _dot(a.astype(BF), b.astype(BF), 0 if ta else 1, 1 if tb else 0)
                accs[ci] = d if accs[ci] is None else accs[ci] + d
            return accs

        def finish(accs):
            res = epi(accs, [e[...] for e in e_refs])
            for o_ref, r in zip(o_refs, res):
                o_ref[...] = r.reshape(o_ref.shape).astype(o_ref.dtype)

        if nk == 1:
            finish(products())
        else:
            k = pl.program_id(2)

            @pl.when(k == 0)
            def _():
                for acc in acc_refs:
                    acc[...] = jnp.zeros_like(acc)

            for acc, d in zip(acc_refs, products()):
                acc[...] += d

            @pl.when(k == nk - 1)
            def _():
                finish([acc[...] for acc in acc_refs])

    blocks = ([(a_block, a.dtype) for a in a_list] + [(b_block, b.dtype) for b in b_list]
              + [(bs, e.dtype) for e, bs, _ in extras] + [(bs, d) for _, d, bs, _ in outs])
    vmem = _vmem_limit(blocks, [((tm, tn), F32)] * (n_acc if nk > 1 else 0), temps=6 * tm * tn * 4)
    grid = (gj, gi, nk) if n_outer else (gi, gj, nk)
    fn = _call(body, name=name, out_shape=out_shape, grid=grid, in_specs=in_specs, out_specs=out_specs,
               scratch=scratch, sem=("parallel", "parallel", "arbitrary"), vmem=vmem)
    return fn(*a_list, *b_list, *[e for e, _, _ in extras])


def _mn(tm, tn, col0=0):
    assert col0 % tn == 0
    off = col0 // tn
    return (tm, tn), (lambda i, j: (i, j + off))


def _rmsnorm_fwd(name, x, gain, tr):
    L, D = x.shape

    def body(x_ref, g_ref, o_ref):
        xv = x_ref[...]
        r = lax.rsqrt(jnp.mean(xv * xv, axis=-1, keepdims=True) + RMS_EPS)
        o_ref[...] = (xv * r * g_ref[...]).astype(BF)

    row = pl.BlockSpec((tr, D), lambda i: (i, 0))
    vec = pl.BlockSpec((1, D), lambda i: (0, 0))
    vmem = _vmem_limit([((tr, D), F32), ((tr, D), BF)], temps=3 * tr * D * 4)
    return _call(body, name=name, out_shape=jax.ShapeDtypeStruct((L, D), BF), grid=(L // tr,), in_specs=[row, vec],
                 out_specs=row, sem=("parallel",), vmem=vmem)(x, gain)


def _rmsnorm_bwd(name, dn, x, gain, dres, tr):
    L, D = x.shape
    steps = L // tr

    def body(dn_ref, x_ref, g_ref, dres_ref, dx_ref, dxb_ref, dg_ref):
        i = pl.program_id(0)
        xv = x_ref[...]
        r = lax.rsqrt(jnp.mean(xv * xv, axis=-1, keepdims=True) + RMS_EPS)
        xhat = xv * r
        dy = dn_ref[...]
        dxhat = dy * g_ref[...]
        dx = dres_ref[...] + r * (dxhat - xhat * jnp.mean(dxhat * xhat, axis=-1, keepdims=True))
        dx_ref[...] = dx
        dxb_ref[...] = dx.astype(BF)

        @pl.when(i == 0)
        def _():
            dg_ref[...] = jnp.zeros_like(dg_ref)

        dg_ref[...] += jnp.sum(dy * xhat, axis=0, keepdims=True)

    row = pl.BlockSpec((tr, D), lambda i: (i, 0))
    vec = pl.BlockSpec((1, D), lambda i: (0, 0))
    vmem = _vmem_limit([((tr, D), F32)] * 4 + [((tr, D), BF)], temps=4 * tr * D * 4)
    out_shape = [jax.ShapeDtypeStruct((L, D), F32), jax.ShapeDtypeStruct((L, D), BF),
                 jax.ShapeDtypeStruct((1, D), F32)]
    return _call(body, name=name, out_shape=out_shape, grid=(steps,), in_specs=[row, row, vec, row],
                 out_specs=[row, row, vec], sem=("arbitrary",), vmem=vmem)(dn, x, gain, dres)


def _loss_grad(name, h, target, n_meta, n_seq, tr):
    L, D = h.shape

    def body(h_ref, t_ref, dh_ref, dhb_ref, loss_ref):
        i = pl.program_id(0)
        rows = i * tr + lax.broadcasted_iota(I32, (tr, 1), 0)
        valid = (rows >= n_meta) & (rows < n_meta + n_seq)
        diff = jnp.where(valid, h_ref[...] - t_ref[...], 0.0)
        dh = diff * (1.0 / D)
        dh_ref[...] = dh
        dhb_ref[...] = dh.astype(BF)

        @pl.when(i == 0)
        def _():
            loss_ref[...] = jnp.zeros_like(loss_ref)

        loss_ref[...] += jnp.sum(diff * diff) * (0.5 / D)

    row = pl.BlockSpec((tr, D), lambda i: (i, 0))
    acc = pl.BlockSpec((1, LANES), lambda i: (0, 0))
    vmem = _vmem_limit([((tr, D), F32)] * 3 + [((tr, D), BF)], temps=3 * tr * D * 4)
    out_shape = [jax.ShapeDtypeStruct((L, D), F32), jax.ShapeDtypeStruct((L, D), BF),
                 jax.ShapeDtypeStruct((1, LANES), F32)]
    return _call(body, name=name, out_shape=out_shape, grid=(L // tr,), in_specs=[row, row],
                 out_specs=[row, row, acc], sem=("arbitrary",), vmem=vmem)(h, target)


def _qknorm_fwd(name, proj, gq, gk, heads, q_col, k_col, tr):
    L = proj.shape[0]

    def body(q_ref, k_ref, gq_ref, gk_ref, qn_ref, kn_ref):
        for x_ref, g_ref, o_ref in ((q_ref, gq_ref, qn_ref), (k_ref, gk_ref, kn_ref)):
            xv = x_ref[...].astype(F32)
            r = lax.rsqrt(jnp.mean(xv * xv, axis=-1, keepdims=True) + RMS_EPS)
            o_ref[...] = (xv * r * g_ref[...]).astype(BF)

    qb, kb = q_col // HEAD_DIM, k_col // HEAD_DIM
    in_specs = [pl.BlockSpec((tr, HEAD_DIM), lambda h, i: (i, qb + h)),
                pl.BlockSpec((tr, HEAD_DIM), lambda h, i: (i, kb + h)),
                pl.BlockSpec((None, 1, HEAD_DIM), lambda h, i: (h, 0, 0)),
                pl.BlockSpec((None, 1, HEAD_DIM), lambda h, i: (h, 0, 0))]
    out = pl.BlockSpec((tr, HEAD_DIM), lambda h, i: (i, h))
    out_shape = [jax.ShapeDtypeStruct((L, heads * HEAD_DIM), BF)] * 2
    return _call(body, name=name, out_shape=out_shape, grid=(heads, L // tr), in_specs=in_specs,
                 out_specs=[out, out], sem=("parallel", "parallel"), vmem=16 << 20)(proj, proj, gq, gk)


def _qknorm_bwd(name, proj, dqn, dkn, gq, gk, heads, q_col, k_col, tr):
    L = proj.shape[0]

    def body(q_ref, k_ref, dqn_ref, dkn_ref, gq_ref, gk_ref, dq_ref, dk_ref, dgq_ref, dgk_ref):
        i = pl.program_id(1)
        for x_ref, dy_ref, g_ref, dx_ref, dg_ref in ((q_ref, dqn_ref, gq_ref, dq_ref, dgq_ref),
                                                     (k_ref, dkn_ref, gk_ref, dk_ref, dgk_ref)):
            xv = x_ref[...].astype(F32)
            r = lax.rsqrt(jnp.mean(xv * xv, axis=-1, keepdims=True) + RMS_EPS)
            xhat = xv * r
            dy = dy_ref[...].astype(F32)
            dxhat = dy * g_ref[...]
            dx_ref[...] = (r * (dxhat - xhat * jnp.mean(dxhat * xhat, axis=-1, keepdims=True))).astype(BF)

            @pl.when(i == 0)
            def _():
                dg_ref[...] = jnp.zeros_like(dg_ref)

            dg_ref[...] += jnp.sum(dy * xhat, axis=0, keepdims=True)

    qb, kb = q_col // HEAD_DIM, k_col // HEAD_DIM
    tile = pl.BlockSpec((tr, HEAD_DIM), lambda h, i: (i, h))
    gain = pl.BlockSpec((None, 1, HEAD_DIM), lambda h, i: (h, 0, 0))
    in_specs = [pl.BlockSpec((tr, HEAD_DIM), lambda h, i: (i, qb + h)),
                pl.BlockSpec((tr, HEAD_DIM), lambda h, i: (i, kb + h)), tile, tile, gain, gain]
    out_shape = [jax.ShapeDtypeStruct((L, heads * HEAD_DIM), BF)] * 2 + [
        jax.ShapeDtypeStruct((heads, 1, HEAD_DIM), F32)] * 2
    return _call(body, name=name, out_shape=out_shape, grid=(heads, L // tr), in_specs=in_specs,
                 out_specs=[tile, tile, gain, gain], sem=("parallel", "arbitrary"),
                 vmem=16 << 20)(proj, proj, dqn, dkn, gq, gk)


def _tri(cmp):
    r = lax.broadcasted_iota(I32, (LANES, LANES), 0)
    c = lax.broadcasted_iota(I32, (LANES, LANES), 1)
    return jnp.where(cmp(r, c), 1.0, 0.0).astype(BF)


def _cum_fwd(name, fl, bias, n_rows):
    H, nbp, _ = fl.shape

    def body(fl_ref, b_ref, c_ref, tot_ref):
        lf = _log_sigmoid(fl_ref[...] + b_ref[...])
        c_ref[...] = _split_dot(lf, _tri(lambda r, c: r <= c), 3)
        tot_ref[...] = _split_dot(lf, jnp.ones((LANES, LANES), BF), 3)

        def step(r, carry):
            c_ref[pl.ds(r, 1), :] = c_ref[pl.ds(r, 1), :] + carry
            return carry + tot_ref[pl.ds(r, 1), :]

        lax.fori_loop(0, n_rows, step, jnp.zeros((1, LANES), F32))

    blk = pl.BlockSpec((None, nbp, LANES), lambda h: (h, 0, 0))
    vec = pl.BlockSpec((None, 1, LANES), lambda h: (h, 0, 0))
    return _call(body, name=name, out_shape=jax.ShapeDtypeStruct((H, nbp, LANES), F32), grid=(H,),
                 in_specs=[blk, vec], out_specs=blk, scratch=[pltpu.VMEM((nbp, LANES), F32)], sem=("parallel",),
                 vmem=16 << 20)(fl, bias)


def _cum_bwd(name, drs, dcs, fl, bias, n_rows):
    H, nbp, _ = fl.shape

    def body(drs_ref, dcs_ref, fl_ref, b_ref, dfl_ref, db_ref, rin_ref, tot_ref):
        dc = drs_ref[...] - dcs_ref[...]
        rin_ref[...] = _split_dot(dc, _tri(lambda r, c: r >= c), 3)
        tot_ref[...] = _split_dot(dc, jnp.ones((LANES, LANES), BF), 3)
        dfl_ref[...] = jnp.zeros_like(dfl_ref)

        def step(t, carry):
            r = n_rows - 1 - t
            x = fl_ref[pl.ds(r, 1), :] + b_ref[...]
            dfl_ref[pl.ds(r, 1), :] = (rin_ref[pl.ds(r, 1), :] + carry) * _sigmoid(-x)
            return carry + tot_ref[pl.ds(r, 1), :]

        lax.fori_loop(0, n_rows, step, jnp.zeros((1, LANES), F32))
        db_ref[...] = jnp.zeros_like(db_ref) + jnp.sum(dfl_ref[...])

    blk = pl.BlockSpec((None, nbp, LANES), lambda h: (h, 0, 0))
    vec = pl.BlockSpec((None, 1, LANES), lambda h: (h, 0, 0))
    out_shape = [jax.ShapeDtypeStruct((H, nbp, LANES), F32), jax.ShapeDtypeStruct((H, 1, LANES), F32)]
    return _call(body, name=name, out_shape=out_shape, grid=(H,), in_specs=[blk, blk, blk, vec],
                 out_specs=[blk, vec], scratch=[pltpu.VMEM((nbp, LANES), F32)] * 2, sem=("parallel",),
                 vmem=16 << 20)(drs, dcs, fl, bias)


def _att_specs(L, G, q_col, k_col, v_col):
    T = ATT_TILE
    W = G * HEAD_DIM
    assert q_col % W == 0 and k_col % W == 0 and v_col % W == 0
    qb, kb, vb = q_col // W, k_col // W, v_col // W
    q_spec = pl.BlockSpec((T, W), lambda h, i: (i, qb + h))
    k_spec = pl.BlockSpec((L, W), lambda h, i: (0, kb + h), pipeline_mode=pl.Buffered(1))
    v_spec = pl.BlockSpec((L, W), lambda h, i: (0, vb + h), pipeline_mode=pl.Buffered(1))
    return q_spec, k_spec, v_spec


def _head_lanes(G):
    return [slice(g * HEAD_DIM, (g + 1) * HEAD_DIM) for g in range(G)]


def _tile_iotas():
    T = ATT_TILE
    return lax.broadcasted_iota(I32, (T, T), 0), lax.broadcasted_iota(I32, (T, T), 1)


def _rows(j):
    return pl.ds(pl.multiple_of(j * ATT_TILE, ATT_TILE), ATT_TILE)


def _fox_fwd(name, q_arr, k_arr, v_arr, c_row, c_col, heads, G, q_col, k_col, v_col):
    L = q_arr.shape[0]
    T = ATT_TILE
    scale = HEAD_DIM ** -0.5
    lanes = _head_lanes(G)

    def body(q_ref, k_ref, v_ref, crow_ref, ccol_ref, o_ref, lse_ref):
        i = pl.program_id(1)
        qs = [q_ref[:, hl] for hl in lanes]
        cts = [jnp.broadcast_to(ccol_ref[g], (T, T)) for g in range(G)]
        row, col = _tile_iotas()

        def tile(j, carry, masked):
            qk = [_dot(qs[g], k_ref[_rows(j), hl], 1, 1) for g, hl in enumerate(lanes)]
            stats = []
            for g in range(G):
                m, l, _ = carry[g]
                s = qk[g] * scale + (cts[g] - crow_ref[g, pl.ds(j, 1), :])
                if masked:
                    s = jnp.where(col <= row, s, -jnp.inf)
                m_new = jnp.maximum(m, jnp.max(s, axis=1, keepdims=True))
                alpha = jnp.exp(m - m_new)
                p = jnp.exp(s - m_new)
                stats.append((m_new, alpha, alpha * l + jnp.sum(p, axis=1, keepdims=True), p.astype(BF)))
            pv = [_dot(stats[g][3], v_ref[_rows(j), hl], 1, 0) for g, hl in enumerate(lanes)]
            return tuple((stats[g][0], stats[g][2], stats[g][1] * carry[g][2] + pv[g]) for g in range(G))

        init = tuple((jnp.full((T, 1), -1e30, F32), jnp.zeros((T, 1), F32), jnp.zeros((T, HEAD_DIM), F32))
                     for _ in range(G))
        carry = lax.fori_loop(0, i, lambda j, c: tile(j, c, False), init)
        for g, (m, l, acc) in enumerate(tile(i, carry, True)):
            o_ref[:, lanes[g]] = (acc / l).astype(o_ref.dtype)
            lse_ref[g] = m + jnp.log(l)

    nbp = c_row.shape[1]
    W = G * HEAD_DIM
    q_spec, k_spec, v_spec = _att_specs(L, G, q_col, k_col, v_col)
    crow_spec = pl.BlockSpec((G, nbp, LANES), lambda h, i: (h, 0, 0))
    col_spec = pl.BlockSpec((G, T, 1), lambda h, i: (h, i, 0))
    o_spec = pl.BlockSpec((T, W), lambda h, i: (i, h))
    out_shape = [jax.ShapeDtypeStruct((L, heads * HEAD_DIM), BF), jax.ShapeDtypeStruct((heads, L, 1), F32)]
    vmem = _vmem_limit([((L, W), BF)] * 2, temps=8 << 20)
    return _call(body, name=name, out_shape=out_shape, grid=(heads // G, L // T),
                 in_specs=[q_spec, k_spec, v_spec, crow_spec, col_spec], out_specs=[o_spec, col_spec],
                 sem=("parallel", "parallel"), vmem=vmem)(q_arr, k_arr, v_arr, c_row, c_col)


def _fox_bwd(name, q_arr, k_arr, v_arr, c_row, c_col, o, do, lse, heads, G, q_col, k_col, v_col):
    L = q_arr.shape[0]
    T = ATT_TILE
    nq = L // T
    scale = HEAD_DIM ** -0.5
    lanes = _head_lanes(G)

    def body(q_ref, k_ref, v_ref, crow_ref, ccol_ref, o_ref, do_ref, lse_ref, dq_ref, dk_ref, dv_ref, dcs_ref,
             drs_ref, dk_acc, dv_acc):
        i = pl.program_id(1)

        @pl.when(i == 0)
        def _():
            dk_acc[...] = jnp.zeros_like(dk_acc)
            dv_acc[...] = jnp.zeros_like(dv_acc)
            dcs_ref[...] = jnp.zeros_like(dcs_ref)

        qs = [q_ref[:, hl] for hl in lanes]
        dos = [do_ref[:, hl] for hl in lanes]
        q_ts = [qs[g].T for g in range(G)]
        do_ts = [dos[g].T for g in range(G)]
        deltas = [jnp.broadcast_to(jnp.sum(dos[g].astype(F32) * o_ref[:, hl].astype(F32), axis=1, keepdims=True),
                                   (T, T)) for g, hl in enumerate(lanes)]
        lses = [jnp.broadcast_to(lse_ref[g], (T, T)) for g in range(G)]
        cts = [jnp.broadcast_to(ccol_ref[g], (T, T)) for g in range(G)]
        row, col = _tile_iotas()

        def tile(j, carry, masked):
            ks = [k_ref[_rows(j), hl] for hl in lanes]
            qk = [_dot(qs[g], ks[g], 1, 1) for g in range(G)]
            dp = [_dot(dos[g], v_ref[_rows(j), hl], 1, 1) for g, hl in enumerate(lanes)]
            pbs, dsbs, row_sums = [], [], []
            for g in range(G):
                s = qk[g] * scale + (cts[g] - crow_ref[g, pl.ds(j, 1), :])
                if masked:
                    s = jnp.where(col <= row, s, -jnp.inf)
                p = jnp.exp(s - lses[g])
                ds = p * (dp[g] - deltas[g])
                dcs_ref[g, pl.ds(j, 1), :] += jnp.sum(ds, axis=0, keepdims=True)
                row_sums.append(carry[g][1] + jnp.sum(ds, axis=1, keepdims=True))
                pbs.append(p.astype(BF))
                dsbs.append((ds * scale).astype(BF))
            for g, hl in enumerate(lanes):
                dk_acc[j, hl, :] += _dot(q_ts[g], dsbs[g], 1, 0)
            for g, hl in enumerate(lanes):
                dv_acc[j, hl, :] += _dot(do_ts[g], pbs[g], 1, 0)
            return tuple((carry[g][0] + _dot(dsbs[g], ks[g], 1, 0), row_sums[g]) for g in range(G))

        init = tuple((jnp.zeros((T, HEAD_DIM), F32), jnp.zeros((T, 1), F32)) for _ in range(G))
        carry = lax.fori_loop(0, i, lambda j, c: tile(j, c, False), init)
        for g, (dq, row_sum) in enumerate(tile(i, carry, True)):
            dq_ref[:, lanes[g]] = dq.astype(dq_ref.dtype)
            drs_ref[g] = row_sum

        @pl.when(i == nq - 1)
        def _():
            for r in range(nq):
                for hl in lanes:
                    dk_ref[r * T:(r + 1) * T, hl] = dk_acc[r, hl, :].T.astype(dk_ref.dtype)
                    dv_ref[r * T:(r + 1) * T, hl] = dv_acc[r, hl, :].T.astype(dv_ref.dtype)

    nbp = c_row.shape[1]
    WG = G * HEAD_DIM
    q_spec, k_spec, v_spec = _att_specs(L, G, q_col, k_col, v_col)
    crow_spec = pl.BlockSpec((G, nbp, LANES), lambda h, i: (h, 0, 0))
    col_spec = pl.BlockSpec((G, T, 1), lambda h, i: (h, i, 0))
    t_spec = pl.BlockSpec((T, WG), lambda h, i: (i, h))
    head_spec = pl.BlockSpec((L, WG), lambda h, i: (0, h), pipeline_mode=pl.Buffered(1))
    W = heads * HEAD_DIM
    out_shape = [jax.ShapeDtypeStruct((L, W), F32)] * 3 + [jax.ShapeDtypeStruct((heads, nbp, LANES), F32),
                                                           jax.ShapeDtypeStruct((heads, L, 1), F32)]
    scratch = [pltpu.VMEM((nq, WG, T), F32)] * 2
    vmem = _vmem_limit([], [((L, WG), BF)] * 2 + [((L, WG), F32)] * 4, temps=8 << 20)
    return _call(body, name=name, out_shape=out_shape, grid=(heads // G, nq),
                 in_specs=[q_spec, k_spec, v_spec, crow_spec, col_spec, t_spec, t_spec, col_spec],
                 out_specs=[t_spec, head_spec, head_spec, crow_spec, col_spec], scratch=scratch,
                 sem=("parallel", "arbitrary"), vmem=vmem)(q_arr, k_arr, v_arr, c_row, c_col, o, do, lse)


def _sb_logits(qk, scale, valid):
    z = qk * scale
    lb = jnp.minimum(z, 0.0) - jnp.log1p(jnp.exp(-jnp.abs(z)))
    lom = lb - z
    if valid is not None:
        lom = jnp.where(valid, lom, 0.0)
    return lb, lom


def _sb_fwd(name, proj, heads, G, q_col, k_col, v_col):
    L = proj.shape[0]
    T = ATT_TILE
    scale = HEAD_DIM ** -0.5
    lanes = _head_lanes(G)

    def body(q_ref, k_ref, v_ref, o_ref):
        i = pl.program_id(1)
        qs = [q_ref[:, hl] for hl in lanes]
        row, col = _tile_iotas()
        later_mat = jnp.where(row > col, 1.0, 0.0).astype(BF)

        def tile(j, carry, masked):
            valid = (col < row) if masked else None
            qk = [_dot(qs[g], k_ref[_rows(j), hl], 1, 1) for g, hl in enumerate(lanes)]
            logits = [_sb_logits(qk[g], scale, valid) for g in range(G)]
            pieces = [_split(lom, 2) for _, lom in logits]
            later = [_pieces_dot(pieces[g], later_mat) for g in range(G)]
            ws = []
            for g in range(G):
                w = jnp.exp(logits[g][0] + later[g] + carry[g][0])
                if masked:
                    w = jnp.where(valid, w, 0.0)
                ws.append(w.astype(BF))
            wv = [_dot(ws[g], v_ref[_rows(j), hl], 1, 0) for g, hl in enumerate(lanes)]
            return tuple((carry[g][0] + jnp.sum(logits[g][1], axis=1, keepdims=True), carry[g][1] + wv[g])
                         for g in range(G))

        init = tuple((jnp.zeros((T, 1), F32), jnp.zeros((T, HEAD_DIM), F32)) for _ in range(G))
        carry = tile(i, init, True)
        carry = lax.fori_loop(0, i, lambda t, c: tile(i - 1 - t, c, False), carry)
        for g, (_, acc) in enumerate(carry):
            o_ref[:, lanes[g]] = acc.astype(o_ref.dtype)

    W = G * HEAD_DIM
    q_spec, k_spec, v_spec = _att_specs(L, G, q_col, k_col, v_col)
    o_spec = pl.BlockSpec((T, W), lambda h, i: (i, h))
    vmem = _vmem_limit([((L, W), BF)] * 2, temps=8 << 20)
    return _call(body, name=name, out_shape=jax.ShapeDtypeStruct((L, heads * HEAD_DIM), BF),
                 grid=(heads // G, L // T), in_specs=[q_spec, k_spec, v_spec], out_specs=o_spec,
                 sem=("parallel", "parallel"), vmem=vmem)(proj, proj, proj)


def _sb_bwd(name, proj, do, heads, G, q_col, k_col, v_col):
    L = proj.shape[0]
    T = ATT_TILE
    nq = L // T
    scale = HEAD_DIM ** -0.5
    lanes = _head_lanes(G)

    def body(q_ref, k_ref, v_ref, do_ref, dq_ref, dk_acc, dv_acc, da_buf, beta_buf):
        i = pl.program_id(1)

        @pl.when(i == 0)
        def _():
            dk_acc[...] = jnp.zeros_like(dk_acc)
            dv_acc[...] = jnp.zeros_like(dv_acc)

        qs = [q_ref[:, hl] for hl in lanes]
        dos = [do_ref[:, hl] for hl in lanes]
        q_ts = [qs[g].T for g in range(G)]
        do_ts = [dos[g].T for g in range(G)]
        row, col = _tile_iotas()
        later_mat = jnp.where(row > col, 1.0, 0.0).astype(BF)
        before_mat = jnp.where(row < col, 1.0, 0.0).astype(BF)

        def pass1(j, runs, masked):
            valid = (col < row) if masked else None
            qk = [_dot(qs[g], k_ref[_rows(j), hl], 1, 1) for g, hl in enumerate(lanes)]
            dw = [_dot(dos[g], v_ref[_rows(j), hl], 1, 1) for g, hl in enumerate(lanes)]
            logits = [_sb_logits(qk[g], scale, valid) for g in range(G)]
            pieces = [_split(lom, 2) for _, lom in logits]
            later = [_pieces_dot(pieces[g], later_mat) for g in range(G)]
            ws = []
            for g in range(G):
                w = jnp.exp(logits[g][0] + later[g] + runs[g])
                if masked:
                    w = jnp.where(valid, w, 0.0)
                da_buf[g * nq + j] = dw[g] * w
                beta_buf[g * nq + j] = jnp.exp(logits[g][0])
                ws.append(w.astype(BF))
            for g, hl in enumerate(lanes):
                dv_acc[j, hl, :] += _dot(do_ts[g], ws[g], 1, 0)
            return tuple(runs[g] + jnp.sum(logits[g][1], axis=1, keepdims=True) for g in range(G))

        runs = pass1(i, tuple(jnp.zeros((T, 1), F32) for _ in range(G)), True)
        lax.fori_loop(0, i, lambda t, c: pass1(i - 1 - t, c, False), runs)

        def pass2(j, carry, masked):
            das = [da_buf[g * nq + j] for g in range(G)]
            pieces = [_split(da, 2) for da in das]
            before = [_pieces_dot(pieces[g], before_mat) for g in range(G)]
            dzbs = []
            for g in range(G):
                beta = beta_buf[g * nq + j]
                dz = das[g] * (1.0 - beta) - (carry[g][0] + before[g]) * beta
                if masked:
                    dz = jnp.where(col < row, dz, 0.0)
                dzbs.append((dz * scale).astype(BF))
            for g, hl in enumerate(lanes):
                dk_acc[j, hl, :] += _dot(q_ts[g], dzbs[g], 1, 0)
            dq = [_dot(dzbs[g], k_ref[_rows(j), hl], 1, 0) for g, hl in enumerate(lanes)]
            return tuple((carry[g][0] + jnp.sum(das[g], axis=1, keepdims=True), carry[g][1] + dq[g])
                         for g in range(G))

        init = tuple((jnp.zeros((T, 1), F32), jnp.zeros((T, HEAD_DIM), F32)) for _ in range(G))
        carry = lax.fori_loop(0, i, lambda j, c: pass2(j, c, False), init)
        for g, (_, dq) in enumerate(pass2(i, carry, True)):
            dq_ref[:, lanes[g]] = dq.astype(dq_ref.dtype)

    WG = G * HEAD_DIM
    q_spec, k_spec, v_spec = _att_specs(L, G, q_col, k_col, v_col)
    t_spec = pl.BlockSpec((T, WG), lambda h, i: (i, h))
    head_spec = pl.BlockSpec((nq, WG, T), lambda h, i: (0, h, 0), pipeline_mode=pl.Buffered(1))
    W = heads * HEAD_DIM
    out_shape = [jax.ShapeDtypeStruct((L, W), BF)] + [jax.ShapeDtypeStruct((nq, W, T), F32)] * 2
    scratch = [pltpu.VMEM((G * nq, T, T), F32)] * 2
    vmem = _vmem_limit([], [((L, WG), BF)] * 2 + [((L, WG), F32)] * 2 + [((G * nq, T, T), F32)] * 2,
                       temps=6 << 20)
    return _call(body, name=name, out_shape=out_shape, grid=(heads // G, nq),
                 in_specs=[q_spec, k_spec, v_spec, t_spec], out_specs=[t_spec, head_spec, head_spec],
                 scratch=scratch, sem=("parallel", "arbitrary"), vmem=vmem)(proj, proj, proj, do)


_ANY = pl.BlockSpec(memory_space=pl.ANY)


def _mesh_pos():
    return lax.axis_index("x"), lax.axis_index("y"), lax.axis_index("c")


def _other_chips(x, y):
    return [(1 - x, y), (x, 1 - y), (1 - x, 1 - y)]


def _shard_window(ref, kind, sidx, r0, nr, cs):
    if kind == "col":
        assert cs % LANES == 0
        return ref.at[pl.ds(r0, nr), pl.ds(pl.multiple_of(sidx * cs, LANES), cs)]
    return ref.at[sidx, pl.ds(r0, nr), :]


def _place_shard(name, pos, shard, kind, dtype):
    R, C = shard.shape
    tr = _row_tile(R, C, 2, 16)

    def body(pos_ref, s_ref, o_ref):
        o_ref[...] = s_ref[...].astype(o_ref.dtype)

    if kind == "col":
        assert C % LANES == 0
        out_shape = jax.ShapeDtypeStruct((R, N_CHIPS * C), dtype)
        out_spec = pl.BlockSpec((tr, C), lambda r, pos_ref: (r, pos_ref[1]))
    else:
        out_shape = jax.ShapeDtypeStruct((N_CHIPS, R, C), dtype)
        out_spec = pl.BlockSpec((None, tr, C), lambda r, pos_ref: (pos_ref[1], r, 0))
    return _call(body, name=name, out_shape=out_shape, grid=(R // tr,),
                 in_specs=[pl.BlockSpec((tr, C), lambda r, pos_ref: (r, 0))], out_specs=out_spec, sem=("parallel",),
                 vmem=32 << 20, prefetch=1)(pos, shard)


_HBM = pl.BlockSpec(memory_space=pltpu.HBM)
_SEM = pl.BlockSpec(memory_space=pltpu.SEMAPHORE)
_KEEP_ORDER = pltpu.SideEffectType.DATAFLOW_SIDE_EFFECTING


def _in_hbm(x):
    return pltpu.with_memory_space_constraint(x, pltpu.HBM)


def _gather_copies(bufs, meta, send_sem, recv_sem):
    x, y, c = _mesh_pos()
    out = []
    for t, (kind, R, cs) in enumerate(meta):
        half = R // 2
        r0 = pl.multiple_of(c * half, SUBLANES)
        mine = _shard_window(bufs[t], kind, 2 * x + y, r0, half, cs)
        for p, (px, py) in enumerate(_other_chips(x, y)):
            k = 3 * t + p
            args = dict(send_sem=send_sem.at[k], recv_sem=recv_sem.at[k], device_id=(px, py, c),
                        device_id_type=MESH_IDS)
            out.append((pltpu.make_async_remote_copy(src_ref=mine, dst_ref=mine, **args),
                        pltpu.make_async_remote_copy(
                            src_ref=mine, dst_ref=_shard_window(bufs[t], kind, 2 * px + py, r0, half, cs), **args)))
    return out


def _all_gather_ici_start(name, gathered, shards, kinds, after):
    n = len(gathered)
    meta = [(kind, s.shape[0], s.shape[1]) for s, kind in zip(shards, kinds)]

    def body(*refs):
        send_sem, recv_sem = refs[n + 1], refs[n + 2]
        bufs = refs[n + 3:2 * n + 3]
        token = refs[2 * n + 3]
        for send, _ in _gather_copies(bufs, meta, send_sem, recv_sem):
            send.start()
        token[...] = jnp.zeros_like(token)

    out_shape = (pltpu.SemaphoreType.DMA((3 * n,)), pltpu.SemaphoreType.DMA((3 * n,)),
                 *[pltpu.HBM(g.shape, g.dtype) for g in gathered], jax.ShapeDtypeStruct((SUBLANES, LANES), F32))
    out_specs = (_SEM, _SEM, *[_HBM] * n, pl.BlockSpec(memory_space=pltpu.VMEM))
    res = pl.pallas_call(body, out_shape=out_shape, in_specs=[_HBM] * n + [_ANY], out_specs=out_specs,
                         input_output_aliases={t: 2 + t for t in range(n)}, name=name,
                         compiler_params=pltpu.CompilerParams(has_side_effects=_KEEP_ORDER))(
        *[_in_hbm(g) for g in gathered], after)
    return res[0], res[1], list(res[2:2 + n]), res[2 + n]


def _all_gather_ici_wait(name, bufs, send_sem, recv_sem, after, shards, kinds):
    n = len(bufs)
    meta = [(kind, s.shape[0], s.shape[1]) for s, kind in zip(shards, kinds)]
    after = list(after) if isinstance(after, (list, tuple)) else [after]

    def body(*refs):
        for send, recv in _gather_copies(refs[:n], meta, refs[n], refs[n + 1]):
            send.wait_send()
            recv.wait_recv()

    out_shape = tuple(pltpu.HBM(b.shape, b.dtype) for b in bufs)
    res = pl.pallas_call(body, out_shape=out_shape, in_specs=[_HBM] * n + [_SEM, _SEM] + [_ANY] * len(after),
                         out_specs=tuple([_HBM] * n), input_output_aliases={t: t for t in range(n)}, name=name,
                         compiler_params=pltpu.CompilerParams(has_side_effects=_KEEP_ORDER))(
        *bufs, send_sem, recv_sem, *after)
    return list(res)


def _all_gather_d2d(name, gathered, shards, kinds):
    n = len(gathered)
    meta = [(kind, s.shape[0], s.shape[1]) for s, kind in zip(shards, kinds)]

    def body(*refs):
        bufs = refs[n:2 * n]
        send_sem, recv_sem = refs[2 * n:]
        x, y, c = _mesh_pos()
        sends, recvs = [], []
        for t, (kind, R, cs) in enumerate(meta):
            half = R // 2
            mine = pl.multiple_of(c * half, SUBLANES)
            theirs = pl.multiple_of((1 - c) * half, SUBLANES)
            for p, (px, py) in enumerate(_other_chips(x, y)):
                k = 3 * t + p
                win = _shard_window(bufs[t], kind, 2 * px + py, mine, half, cs)
                cp = pltpu.make_async_remote_copy(src_ref=win, dst_ref=win, send_sem=send_sem.at[k],
                                                  recv_sem=recv_sem.at[k], device_id=(x, y, 1 - c),
                                                  device_id_type=MESH_IDS)
                cp.start()
                sends.append(cp)
                got = _shard_window(bufs[t], kind, 2 * px + py, theirs, half, cs)
                recvs.append(pltpu.make_async_remote_copy(src_ref=win, dst_ref=got, send_sem=send_sem.at[k],
                                                          recv_sem=recv_sem.at[k], device_id=(x, y, 1 - c),
                                                          device_id_type=MESH_IDS))
        for cp in recvs:
            cp.wait_recv()
        for cp in sends:
            cp.wait_send()

    scratch = [pltpu.SemaphoreType.DMA((3 * n,)), pltpu.SemaphoreType.DMA((3 * n,))]
    out_shape = [jax.ShapeDtypeStruct(g.shape, g.dtype) for g in gathered]
    return _call(body, name=name, out_shape=out_shape, in_specs=[_ANY] * n, out_specs=[_ANY] * n, scratch=scratch,
                 aliases={t: t for t in range(n)})(*gathered)


def _sibling_copies(grads, lands, send_sem, recv_sem):
    x, y, c = _mesh_pos()
    out = []
    for t in range(len(grads)):
        half = lands[t].shape[1]
        theirs = pl.multiple_of((1 - c) * half, SUBLANES)
        out.append(pltpu.make_async_remote_copy(src_ref=grads[t].at[:, pl.ds(theirs, half), :], dst_ref=lands[t],
                                                send_sem=send_sem.at[t], recv_sem=recv_sem.at[t],
                                                device_id=(x, y, 1 - c), device_id_type=MESH_IDS))
    return out


def _sibling_start(name, grads):
    n = len(grads)
    lands = [lax.empty((N_CHIPS, g.shape[1] // 2, g.shape[2]), g.dtype) for g in grads]

    def body(*refs):
        send_sem, recv_sem = refs[2 * n], refs[2 * n + 1]
        srcs, zones = refs[2 * n + 2:3 * n + 2], refs[3 * n + 2:4 * n + 2]
        token = refs[4 * n + 2]
        for cp in _sibling_copies(srcs, zones, send_sem, recv_sem):
            cp.start()
        token[...] = jnp.zeros_like(token)

    out_shape = (pltpu.SemaphoreType.DMA((n,)), pltpu.SemaphoreType.DMA((n,)),
                 *[pltpu.HBM(a.shape, a.dtype) for a in list(grads) + lands],
                 jax.ShapeDtypeStruct((SUBLANES, LANES), F32))
    out_specs = (_SEM, _SEM, *[_HBM] * (2 * n), pl.BlockSpec(memory_space=pltpu.VMEM))
    res = pl.pallas_call(body, out_shape=out_shape, in_specs=[_HBM] * (2 * n), out_specs=out_specs,
                         input_output_aliases={t: 2 + t for t in range(2 * n)}, name=name,
                         compiler_params=pltpu.CompilerParams(has_side_effects=_KEEP_ORDER))(
        *[_in_hbm(a) for a in list(grads) + lands])
    return res[0], res[1], list(res[2:2 + n]), list(res[2 + n:2 + 2 * n]), res[2 + 2 * n]


def _sibling_wait(name, grads, lands, send_sem, recv_sem, after):
    n = len(grads)
    after = list(after) if isinstance(after, (list, tuple)) else [after]

    def body(*refs):
        for cp in _sibling_copies(refs[:n], refs[n:2 * n], refs[2 * n], refs[2 * n + 1]):
            cp.wait_send()
            cp.wait_recv()

    out_shape = tuple(pltpu.HBM(a.shape, a.dtype) for a in list(grads) + list(lands))
    res = pl.pallas_call(body, out_shape=out_shape, in_specs=[_HBM] * (2 * n) + [_SEM, _SEM] + [_ANY] * len(after),
                         out_specs=tuple([_HBM] * (2 * n)), input_output_aliases={t: t for t in range(2 * n)},
                         name=name, compiler_params=pltpu.CompilerParams(has_side_effects=_KEEP_ORDER))(
        *grads, *lands, send_sem, recv_sem, *after)
    return list(res[:n]), list(res[n:])


def _exchange_copies(parts, lands, send_sem, recv_sem):
    x, y, c = _mesh_pos()
    out = []
    for t in range(len(parts)):
        for p, (px, py) in enumerate(_other_chips(x, y)):
            k = 3 * t + p
            out.append(pltpu.make_async_remote_copy(src_ref=parts[t].at[2 * px + py], dst_ref=lands[t].at[p],
                                                    send_sem=send_sem.at[k], recv_sem=recv_sem.at[k],
                                                    device_id=(px, py, c), device_id_type=MESH_IDS))
    return out


def _exchange_start(name, partials):
    n = len(partials)
    lands = [lax.empty((3,) + p.shape[1:], p.dtype) for p in partials]

    def body(*refs):
        send_sem, recv_sem = refs[2 * n], refs[2 * n + 1]
        parts, zones = refs[2 * n + 2:3 * n + 2], refs[3 * n + 2:4 * n + 2]
        token = refs[4 * n + 2]
        for cp in _exchange_copies(parts, zones, send_sem, recv_sem):
            cp.start()
        token[...] = jnp.zeros_like(token)

    out_shape = (pltpu.SemaphoreType.DMA((3 * n,)), pltpu.SemaphoreType.DMA((3 * n,)),
                 *[pltpu.HBM(a.shape, a.dtype) for a in partials + lands],
                 jax.ShapeDtypeStruct((SUBLANES, LANES), F32))
    out_specs = (_SEM, _SEM, *[_HBM] * (2 * n), pl.BlockSpec(memory_space=pltpu.VMEM))
    res = pl.pallas_call(body, out_shape=out_shape, in_specs=[_HBM] * (2 * n), out_specs=out_specs,
                         input_output_aliases={t: 2 + t for t in range(2 * n)}, name=name,
                         compiler_params=pltpu.CompilerParams(has_side_effects=_KEEP_ORDER))(
        *[_in_hbm(a) for a in partials + lands])
    return res[0], res[1], list(res[2:2 + n]), list(res[2 + n:2 + 2 * n]), res[2 + 2 * n]


def _exchange_wait(name, parts, lands, send_sem, recv_sem, after):
    n = len(parts)

    def body(*refs):
        for cp in _exchange_copies(refs[:n], refs[n:2 * n], refs[2 * n], refs[2 * n + 1]):
            cp.wait_send()
            cp.wait_recv()

    out_shape = tuple(pltpu.HBM(a.shape, a.dtype) for a in parts + lands)
    res = pl.pallas_call(body, out_shape=out_shape, in_specs=[_HBM] * (2 * n) + [_SEM, _SEM, _ANY],
                         out_specs=tuple([_HBM] * (2 * n)), input_output_aliases={t: t for t in range(2 * n)},
                         name=name, compiler_params=pltpu.CompilerParams(has_side_effects=_KEEP_ORDER))(
        *parts, *lands, send_sem, recv_sem, after)
    return list(res[n:])


def _share_halves(name, totals):
    n = len(totals)

    def body(*refs):
        bufs = refs[n:2 * n]
        send_sem, recv_sem = refs[2 * n:]
        x, y, c = _mesh_pos()
        sends, recvs = [], []
        for t, g in enumerate(totals):
            half = g.shape[0] // 2
            mine = bufs[t].at[pl.ds(pl.multiple_of(c * half, SUBLANES), half), :]
            theirs = bufs[t].at[pl.ds(pl.multiple_of((1 - c) * half, SUBLANES), half), :]
            cp = pltpu.make_async_remote_copy(src_ref=mine, dst_ref=mine, send_sem=send_sem.at[t],
                                              recv_sem=recv_sem.at[t], device_id=(x, y, 1 - c),
                                              device_id_type=MESH_IDS)
            cp.start()
            sends.append(cp)
            recvs.append(pltpu.make_async_remote_copy(src_ref=mine, dst_ref=theirs, send_sem=send_sem.at[t],
                                                      recv_sem=recv_sem.at[t], device_id=(x, y, 1 - c),
                                                      device_id_type=MESH_IDS))
        for cp in recvs:
            cp.wait_recv()
        for cp in sends:
            cp.wait_send()

    out_shape = [jax.ShapeDtypeStruct(g.shape, g.dtype) for g in totals]
    scratch = [pltpu.SemaphoreType.DMA((n,)), pltpu.SemaphoreType.DMA((n,))]
    return _call(body, name=name, out_shape=out_shape, in_specs=[_ANY] * n, out_specs=[_ANY] * n, scratch=scratch,
                 aliases={t: t for t in range(n)})(*totals)


def _gather_small(name, v):
    def body(v_ref, out_ref, send_sem, recv_sem, local_sem):
        x, y, c = _mesh_pos()
        me = 4 * x + 2 * y + c
        local = pltpu.make_async_copy(v_ref, out_ref.at[me], local_sem)
        local.start()
        sends, recvs = [], []
        for k in range(1, N_DEV):
            px = 1 - x if k & 4 else x
            py = 1 - y if k & 2 else y
            pc = 1 - c if k & 1 else c
            cp = pltpu.make_async_remote_copy(src_ref=v_ref, dst_ref=out_ref.at[me], send_sem=send_sem.at[k],
                                              recv_sem=recv_sem.at[k], device_id=(px, py, pc),
                                              device_id_type=MESH_IDS)
            cp.start()
            sends.append(cp)
            recvs.append(pltpu.make_async_remote_copy(
                src_ref=v_ref, dst_ref=out_ref.at[4 * px + 2 * py + pc], send_sem=send_sem.at[k],
                recv_sem=recv_sem.at[k], device_id=(px, py, pc), device_id_type=MESH_IDS))
        for cp in recvs:
            cp.wait_recv()
        for cp in sends:
            cp.wait_send()
        local.wait()

    scratch = [pltpu.SemaphoreType.DMA((N_DEV,)), pltpu.SemaphoreType.DMA((N_DEV,)), pltpu.SemaphoreType.DMA(())]
    return _call(body, name=name, out_shape=jax.ShapeDtypeStruct((N_DEV,) + v.shape, v.dtype), in_specs=[_ANY],
                 out_specs=_ANY, scratch=scratch)(v)


def _row_tile(rows, cols, n_arrays, mult=SUBLANES):
    budget = (24 << 20) // (2 * n_arrays * _round_up(cols, LANES) * 4)
    for t in (512, 256, 128, 64, 32, 16, 8):
        if t <= max(budget, mult) and rows % t == 0 and t % mult == 0:
            return t
    raise ValueError(f"no row tile for {rows} x {cols}")


def _chip_partial(name, pos, own, recv):
    _, half, C = recv.shape
    tr = _row_tile(half, C, 3, 16)
    nh = half // tr

    def body(pos_ref, own_ref, recv_ref, out_ref):
        out_ref[...] = (own_ref[...] + recv_ref[...]).astype(BF)

    def shard(s, pos_ref):
        return (pos_ref[1] + 1 + s) % N_CHIPS

    blk = pl.BlockSpec((None, tr, C), lambda s, r, pos_ref: (shard(s, pos_ref), r, 0))
    own_blk = pl.BlockSpec((None, tr, C), lambda s, r, pos_ref: (shard(s, pos_ref), pos_ref[0] * nh + r, 0))
    return _call(body, name=name, out_shape=jax.ShapeDtypeStruct(recv.shape, BF), grid=(N_CHIPS - 1, nh),
                 in_specs=[own_blk, blk], out_specs=blk, sem=("parallel", "parallel"), vmem=40 << 20,
                 prefetch=1)(pos, own, recv)


def _final_half(name, pos, own, recv, others):
    _, half, C = recv.shape
    tr = _row_tile(half, C, 4, 16)
    nh = half // tr

    def body(pos_ref, own_ref, recv_ref, oth_ref, out_ref):
        acc = own_ref[...] + recv_ref[...]
        for p in range(3):
            acc = acc + oth_ref[p].astype(F32)
        out_ref[...] = acc

    own_blk = pl.BlockSpec((None, tr, C), lambda r, pos_ref: (pos_ref[1], pos_ref[0] * nh + r, 0))
    recv_blk = pl.BlockSpec((None, tr, C), lambda r, pos_ref: (pos_ref[1], r, 0))
    oth_blk = pl.BlockSpec((3, tr, C), lambda r, pos_ref: (0, r, 0))
    out_blk = pl.BlockSpec((tr, C), lambda r, pos_ref: (pos_ref[0] * nh + r, 0))
    return _call(body, name=name, out_shape=jax.ShapeDtypeStruct((2 * half, C), F32), grid=(nh,),
                 in_specs=[own_blk, recv_blk, oth_blk], out_specs=out_blk, sem=("parallel",), vmem=40 << 20,
                 prefetch=1)(pos, own, recv, others)


def _adamw(name, w, g, m, v):
    R, C = w.shape
    tr = R if R < SUBLANES or R % SUBLANES else _row_tile(R, C, 7)
    c1 = 1.0 - ADAM_B1 ** ADAM_STEP
    c2 = 1.0 - ADAM_B2 ** ADAM_STEP

    def body(w_ref, g_ref, m_ref, v_ref, d_ref, nm_ref, nv_ref):
        gv = g_ref[...]
        nm = ADAM_B1 * m_ref[...] + (1.0 - ADAM_B1) * gv
        nv = ADAM_B2 * v_ref[...] + (1.0 - ADAM_B2) * (gv * gv)
        d_ref[...] = -ADAM_LR * ((nm * (1.0 / c1)) / (jnp.sqrt(nv * (1.0 / c2)) + ADAM_EPS) + ADAM_WD * w_ref[...])
        nm_ref[...] = nm
        nv_ref[...] = nv

    blk = pl.BlockSpec((tr, C), lambda r: (r, 0))
    out_shape = [jax.ShapeDtypeStruct((R, C), F32)] * 3
    return _call(body, name=name, out_shape=out_shape, grid=(R // tr,), in_specs=[blk] * 4, out_specs=[blk] * 3,
                 sem=("parallel",), vmem=40 << 20)(w, g, m, v)


def _sum_devices(name, gathered):
    _, R, _ = gathered.shape

    def body(g_ref, o_ref):
        acc = g_ref[0]
        for d in range(1, N_DEV):
            acc = acc + g_ref[d]
        o_ref[...] = acc

    return _call(body, name=name, out_shape=jax.ShapeDtypeStruct((R, LANES), F32), grid=(1,),
                 in_specs=[pl.BlockSpec((N_DEV, R, LANES), lambda i: (0, 0, 0))],
                 out_specs=pl.BlockSpec((R, LANES), lambda i: (0, 0)), sem=("arbitrary",), vmem=16 << 20)(gathered)


def _ffn_fwd(tag, h, gain, wg, wu, wd, tm):
    L, D = h.shape
    F = wg.shape[1]
    tn = _pick(F, (512, 256, 128))
    n = _rmsnorm_fwd(f"{tag}_norm", h, gain, tm)

    def gate_up(accs, _):
        a, u = accs
        return a, u, a * _sigmoid_fast(a) * u

    mn = _mn(tm, tn)
    a, u, s = _matmul(f"{tag}_gate_up", [n], [wg, wu], [(0, 0, 0), (0, 1, 1)], 2, gate_up,
                      [((L, F), BF) + mn] * 3, M=L, N=F, K=D, tm=tm, tn=tn, tk=D, n_outer=True)
    if callable(wd):
        wd = wd(s)
    td = _pick(D, (512, 256, 128))
    (h_out,) = _matmul(f"{tag}_down", [s], [wd], [(0, 0, 0)], 1,
                       lambda accs, ex: [ex[0] + FFN_RESIDUAL_WEIGHT * accs[0]], [((L, D), F32) + _mn(tm, td)],
                       M=L, N=D, K=F, tm=tm, tn=td, tk=F, extras=[(h,) + _mn(tm, td)], n_outer=True)
    return h_out, (h, n, a, u, s), wd


def _ffn_bwd(tag, dh, dh_bf, saved, gain, wg, wu, wd, tm, cs_ff, after=None, midway=None, on_weight_grads=None):
    h, n, a, u, s = saved
    L, D = h.shape
    F = wg.shape[1]
    tn = _pick(F, (512, 256, 128))
    tkl = _pick(L, (1408, 384, 256, 128))

    def act_grad(accs, ex):
        ds = FFN_RESIDUAL_WEIGHT * accs[0]
        av, uv = ex[0].astype(F32), ex[1].astype(F32)
        sg = _sigmoid_fast(av)
        return ds * uv * sg * (1.0 + av * (1.0 - sg)), ds * av * sg

    mn = _mn(tm, tn)
    da, du = _matmul(f"{tag}_dact", [dh_bf], [wd], [(0, 0, 0)], 1, act_grad, [((L, F), BF) + mn] * 2, M=L, N=F,
                     K=D, tm=tm, tn=tn, tk=D, tb=True, extras=[(a,) + mn, (u,) + mn], n_outer=True, after=after)
    td = _pick(D, (512, 256, 128))
    (dwd,) = _matmul(f"{tag}_dwd", [s], [dh_bf], [(0, 0, 0)], 1, lambda accs, _: [FFN_RESIDUAL_WEIGHT * accs[0]],
                     [((N_CHIPS, cs_ff, D), F32, (None, cs_ff, td), lambda i, j: (i, 0, j))], M=F, N=D, K=L,
                     tm=cs_ff, tn=td, tk=tkl, ta=True, after=after)
    tmw = _pick(D, (512, 256, 128))
    shard_out = ((N_CHIPS, D, cs_ff), F32, (None, tmw, cs_ff), lambda i, j: (j, i, 0))
    dwg, dwu = _matmul(f"{tag}_dwgu", [n], [da, du], [(0, 0, 0), (0, 1, 1)], 2, lambda accs, _: accs,
                       [shard_out] * 2, M=D, N=F, K=L, tm=tmw, tn=cs_ff, tk=tkl, ta=True,
                       after=midway(dwd) if midway else None)
    started = on_weight_grads(dwg, dwu, dwd) if on_weight_grads else None
    tdn = _pick(D, (256, 128))
    (dn,) = _matmul(f"{tag}_dn", [da, du], [wg, wu], [(0, 0, 0), (1, 1, 0)], 1, lambda accs, _: accs,
                    [((L, D), F32) + _mn(tm, tdn)], M=L, N=D, K=F, tm=tm, tn=tdn, tk=F, tb=True, after=started)
    dh_in, dh_in_bf, dgain = _rmsnorm_bwd(f"{tag}_dnorm", dn, h, gain, dh, tm)
    return dh_in, dh_in_bf, dgain


def kernel(x, meta_tokens, ffn1_norm, ffn1_w_gate, ffn1_w_up, ffn1_w_down, mix_norm, w_in, b_forget, fox_q_norm, fox_k_norm, w_branch_fox, w_branch_sb, w_out, ffn2_norm, ffn2_w_gate, ffn2_w_up, ffn2_w_down, loss_target, m_meta_tokens, m_ffn1_norm, m_ffn1_w_gate, m_ffn1_w_up, m_ffn1_w_down, m_mix_norm, m_w_in, m_b_forget, m_fox_q_norm, m_fox_k_norm, m_w_branch_fox, m_w_branch_sb, m_w_out, m_ffn2_norm, m_ffn2_w_gate, m_ffn2_w_up, m_ffn2_w_down, v_meta_tokens, v_ffn1_norm, v_ffn1_w_gate, v_ffn1_w_up, v_ffn1_w_down, v_mix_norm, v_w_in, v_b_forget, v_fox_q_norm, v_fox_k_norm, v_w_branch_fox, v_w_branch_sb, v_w_out, v_ffn2_norm, v_ffn2_w_gate, v_ffn2_w_up, v_ffn2_w_down):
    weights = dict(meta_tokens=meta_tokens, ffn1_norm=ffn1_norm, ffn1_w_gate=ffn1_w_gate, ffn1_w_up=ffn1_w_up,
                   ffn1_w_down=ffn1_w_down, mix_norm=mix_norm, w_in=w_in, b_forget=b_forget, fox_q_norm=fox_q_norm,
                   fox_k_norm=fox_k_norm, w_branch_fox=w_branch_fox, w_branch_sb=w_branch_sb, w_out=w_out,
                   ffn2_norm=ffn2_norm, ffn2_w_gate=ffn2_w_gate, ffn2_w_up=ffn2_w_up, ffn2_w_down=ffn2_w_down)
    moments_m = dict(meta_tokens=m_meta_tokens, ffn1_norm=m_ffn1_norm, ffn1_w_gate=m_ffn1_w_gate,
                     ffn1_w_up=m_ffn1_w_up, ffn1_w_down=m_ffn1_w_down, mix_norm=m_mix_norm, w_in=m_w_in,
                     b_forget=m_b_forget, fox_q_norm=m_fox_q_norm, fox_k_norm=m_fox_k_norm,
                     w_branch_fox=m_w_branch_fox, w_branch_sb=m_w_branch_sb, w_out=m_w_out, ffn2_norm=m_ffn2_norm,
                     ffn2_w_gate=m_ffn2_w_gate, ffn2_w_up=m_ffn2_w_up, ffn2_w_down=m_ffn2_w_down)
    moments_v = dict(meta_tokens=v_meta_tokens, ffn1_norm=v_ffn1_norm, ffn1_w_gate=v_ffn1_w_gate,
                     ffn1_w_up=v_ffn1_w_up, ffn1_w_down=v_ffn1_w_down, mix_norm=v_mix_norm, w_in=v_w_in,
                     b_forget=v_b_forget, fox_q_norm=v_fox_q_norm, fox_k_norm=v_fox_k_norm,
                     w_branch_fox=v_w_branch_fox, w_branch_sb=v_w_branch_sb, w_out=v_w_out, ffn2_norm=v_ffn2_norm,
                     ffn2_w_gate=v_ffn2_w_gate, ffn2_w_up=v_ffn2_w_up, ffn2_w_down=v_ffn2_w_down)
    names = list(weights)

    _, S, D = x.shape
    NM = meta_tokens.shape[0]
    L_real = NM + S
    L = _round_up(L_real, ATT_TILE)
    nblk = L // ATT_TILE
    nbp = _round_up(nblk, SUBLANES)
    cs_ff = ffn1_w_gate.shape[2]
    F = N_CHIPS * cs_ff
    H = b_forget.shape[1]
    FW = w_branch_fox.shape[1]
    SW = w_branch_sb.shape[1]
    HS = SW // HEAD_DIM
    cs_in = w_in.shape[2]
    W_IN = N_CHIPS * cs_in
    assert FW == H * HEAD_DIM and W_IN == 3 * FW + H + 3 * SW + 2 * D
    cs_d = D // N_CHIPS
    tm = _pick(L, (384, 256, 128))

    x_pos, y_pos, c_pos = _mesh_pos()
    pos = jnp.stack([c_pos, 2 * x_pos + y_pos]).astype(I32)

    shard_of = {
        "ffn1_w_gate": (ffn1_w_gate[0], "col"), "ffn1_w_up": (ffn1_w_up[0], "col"),
        "ffn1_w_down": (ffn1_w_down[0], "maj"), "w_in": (w_in[0], "maj"),
        "w_branch_fox": (w_branch_fox[0], "col"), "w_branch_sb": (w_branch_sb[0], "col"),
        "w_out": (w_out[0], "maj"), "ffn2_w_gate": (ffn2_w_gate[0], "col"), "ffn2_w_up": (ffn2_w_up[0], "col"),
        "ffn2_w_down": (ffn2_w_down[0], "maj"),
    }
    g_names = list(shard_of) + ["meta_tokens"]
    shards = [shard_of[k][0] for k in shard_of] + [meta_tokens]
    kinds = [shard_of[k][1] for k in shard_of] + ["col"]
    dtypes = [BF] * len(shard_of) + [F32]
    info = {k: (s, kind) for k, s, kind in zip(g_names, shards, kinds)}
    placed = {k: _place_shard(f"place_{k}", pos, s, kind, dt)
              for k, s, kind, dt in zip(g_names, shards, kinds, dtypes)}
    groups = [["meta_tokens", "ffn1_w_gate", "ffn1_w_up"], ["ffn1_w_down"], ["w_in"],
              ["w_branch_fox", "w_branch_sb", "w_out", "ffn2_w_gate", "ffn2_w_up", "ffn2_w_down"]]
    in_flight = []
    all_started = jnp.zeros((SUBLANES, LANES), F32)
    for gi, grp in enumerate(groups):
        g_shards, g_kinds = [info[k][0] for k in grp], [info[k][1] for k in grp]
        in_flight.append(_all_gather_ici_start(f"gather_start_{gi}", [placed[k] for k in grp], g_shards, g_kinds,
                                               all_started))
        all_started = in_flight[-1][3]
    full = {}

    def arrive(gi, after):
        grp = groups[gi]
        g_shards, g_kinds = [info[k][0] for k in grp], [info[k][1] for k in grp]
        send_sem, recv_sem, bufs, _ = in_flight[gi]
        bufs = _all_gather_ici_wait(f"gather_wait_{gi}", bufs, send_sem, recv_sem, after, g_shards, g_kinds)
        full.update(zip(grp, _all_gather_d2d(f"gather_d2d_{gi}", bufs, g_shards, g_kinds)))

    target = jnp.concatenate([jnp.zeros((NM, D), F32), loss_target[0], jnp.zeros((L - L_real, D), F32)], axis=0)
    arrive(0, [all_started, target, moments_m["w_in"][0], moments_v["w_in"][0]])
    wg1, wu1 = full["ffn1_w_gate"], full["ffn1_w_up"]
    c_f = 3 * FW
    QKV_S, GATES, FCOL = 3 * FW, 3 * FW + 3 * SW, 3 * FW + 3 * SW + 2 * D
    W_PROJ = FCOL + LANES

    h0 = jnp.concatenate([full["meta_tokens"], x[0], jnp.zeros((L - L_real, D), F32)], axis=0)

    def late_wd1(s):
        arrive(1, s)
        return full["ffn1_w_down"].reshape(F, D)

    h1, saved1, wd1 = _ffn_fwd("ffn1", h0, ffn1_norm, wg1, wu1, late_wd1, tm)

    arrive(2, h1)
    w_in_full = jnp.transpose(full["w_in"], (1, 0, 2)).reshape(D, W_IN)
    w_proj = jnp.concatenate([w_in_full[:, :c_f], w_in_full[:, c_f + H:],
                              jnp.pad(w_in_full[:, c_f:c_f + H], ((0, 0), (0, LANES - H)))], axis=1)
    n2 = _rmsnorm_fwd("mix_norm", h1, mix_norm, tm)
    tp = _pick(FCOL, (512, 256, 128))
    (proj,) = _matmul("in_proj", [n2], [w_proj], [(0, 0, 0)], 1, lambda accs, _: accs,
                      [((L, FCOL), BF) + _mn(tm, tp)], M=L, N=FCOL, K=D, tm=tm, tn=tp, tk=D, n_outer=True)
    (f_logit,) = _matmul("forget_proj", [n2], [w_proj], [(0, 0, 0)], 1, lambda accs, _: accs,
                         [((L, LANES), F32) + _mn(tm, LANES)], M=L, N=LANES, K=D, tm=tm, tn=LANES, tk=D,
                         b_off=FCOL // LANES)
    fl = jnp.pad(jnp.transpose(f_logit[:, :H]).reshape(H, nblk, LANES), ((0, 0), (0, nbp - nblk), (0, 0)))
    bias = jnp.broadcast_to(b_forget[0][:, None, None], (H, 1, LANES))
    c_row = _cum_fwd("forget_cumsum", fl, bias, nblk)
    c_col = c_row[:, :nblk].reshape(H, L, 1)
    gq, gk = fox_q_norm[0][:, None, :], fox_k_norm[0][:, None, :]
    qn, kn = _qknorm_fwd("fox_qk_norm", proj, gq, gk, H, 0, FW, tm)
    o_fox, lse = _fox_fwd("fox_attention", qn, kn, proj, c_row, c_col, H, ATT_HEADS, 0, 0, 2 * FW)
    o_sb = _sb_fwd("sb_attention", proj, HS, min(HS, ATT_HEADS_SB_FWD), QKV_S, QKV_S + SW, QKV_S + 2 * SW)

    arrive(3, o_sb)
    wg2, wu2 = full["ffn2_w_gate"], full["ffn2_w_up"]
    wd2 = full["ffn2_w_down"].reshape(F, D)
    wbf, wbs = full["w_branch_fox"], full["w_branch_sb"]
    wo = full["w_out"].reshape(D, D)
    td = _pick(D, (512, 256, 128))

    def merge(accs, ex):
        bf_, bs_ = accs
        return _sigmoid_fast(ex[0].astype(F32)) * bf_ + _sigmoid_fast(ex[1].astype(F32)) * bs_, bf_, bs_

    merged, br_f, br_s = _matmul("branch_merge", [o_fox, o_sb], [wbf, wbs], [(0, 0, 0), (1, 1, 1)], 2, merge,
                                 [((L, D), BF) + _mn(tm, td)] * 3, M=L, N=D, K=FW, tm=tm, tn=td, tk=FW,
                                 extras=[(proj,) + _mn(tm, td, GATES), (proj,) + _mn(tm, td, GATES + D)],
                                 n_outer=True)
    (h2,) = _matmul("out_proj", [merged], [wo], [(0, 0, 0)], 1, lambda accs, ex: [ex[0] + accs[0]],
                    [((L, D), F32) + _mn(tm, td)], M=L, N=D, K=D, tm=tm, tn=td, tk=D, extras=[(h1,) + _mn(tm, td)],
                    n_outer=True)

    h3, saved2, _ = _ffn_fwd("ffn2", h2, ffn2_norm, wg2, wu2, wd2, tm)
    dh3, dh3_bf, loss_part = _loss_grad("loss", h3, target, NM, S, tm)

    scatters, totals, grads, updates = {}, {}, {}, {}

    def adamw(k):
        w = weights[k]
        shape2 = (1, w.size) if w.size < LANES * SUBLANES else (w.size // w.shape[-1], w.shape[-1])
        res = _adamw(f"adamw_{k}", w.reshape(shape2), grads[k].reshape(shape2), moments_m[k].reshape(shape2),
                     moments_v[k].reshape(shape2))
        updates[k] = [r.reshape(w.shape) for r in res]

    def swap_begin(tag, keys, local):
        send_sem, recv_sem, grads, lands, token = _sibling_start(f"grads_to_sibling_start_{tag}", list(local))
        scatters[tag] = (keys, send_sem, recv_sem, grads, lands)
        return token

    def exchange_begin(tag, after):
        keys, send_sem, recv_sem, grads, lands = scatters[tag]
        local, from_sibling = _sibling_wait(f"grads_to_sibling_wait_{tag}", grads, lands, send_sem, recv_sem, after)
        partials = [_chip_partial(f"chip_sum_{k}", pos, g, r) for k, g, r in zip(keys, local, from_sibling)]
        send_sem, recv_sem, parts, lands, token = _exchange_start(f"grads_to_owner_start_{tag}", partials)
        scatters[tag] = (keys, local, from_sibling, send_sem, recv_sem, parts, lands)
        return token

    def scatter_end(tag, after):
        keys, local, from_sibling, send_sem, recv_sem, parts, lands = scatters[tag]
        from_chips = _exchange_wait(f"grads_to_owner_wait_{tag}", parts, lands, send_sem, recv_sem, after)
        totals.update({k: _final_half(f"total_{k}", pos, g, r, o)
                       for k, g, r, o in zip(keys, local, from_sibling, from_chips)})

    dh2, dh2_bf, dg_ffn2 = _ffn_bwd(
        "ffn2", dh3, dh3_bf, saved2, ffn2_norm, wg2, wu2, wd2, tm, cs_ff,
        on_weight_grads=lambda *dw: swap_begin("ffn2", ["ffn2_w_gate", "ffn2_w_up", "ffn2_w_down"], dw))
    ffn2_exchanging = exchange_begin("ffn2", dh2)

    def gate_grad(accs, ex):
        dm = accs[0]
        gf, gs, bf_, bs_ = [e.astype(F32) for e in ex]
        sf, ss = _sigmoid_fast(gf), _sigmoid_fast(gs)
        return dm * bf_ * sf * (1.0 - sf), dm * bs_ * ss * (1.0 - ss), dm * sf, dm * ss

    mn_d = _mn(tm, td)
    dgf, dgs, dbr_f, dbr_s = _matmul(
        "d_merged", [dh2_bf], [wo], [(0, 0, 0)], 1, gate_grad, [((L, D), BF) + mn_d] * 4, M=L, N=D, K=D, tm=tm,
        tn=td, tk=D, tb=True, extras=[(proj,) + _mn(tm, td, GATES), (proj,) + _mn(tm, td, GATES + D),
                                      (br_f,) + mn_d, (br_s,) + mn_d], n_outer=True, after=ffn2_exchanging)
    tkl = _pick(L, (1408, 384, 256, 128))
    (dwo,) = _matmul("d_w_out", [merged], [dh2_bf], [(0, 0, 0)], 1, lambda accs, _: accs,
                     [((N_CHIPS, cs_d, D), F32, (None, cs_d, td), lambda i, j: (i, 0, j))], M=D, N=D, K=L, tm=cs_d,
                     tn=td, tk=tkl, ta=True)
    tw = _pick(FW, (512, 256, 128))
    do_fox, do_sb = _matmul("d_branch_in", [dbr_f, dbr_s], [wbf, wbs], [(0, 0, 0), (1, 1, 1)], 2,
                            lambda accs, _: accs, [((L, FW), BF) + _mn(tm, tw)] * 2, M=L, N=FW, K=D, tm=tm, tn=tw,
                            tk=D, tb=True)
    tmb = _pick(FW, (1024, 512, 256, 128))
    dwbf, dwbs = _matmul("d_w_branch", [o_fox, o_sb], [dbr_f, dbr_s], [(0, 0, 0), (1, 1, 1)], 2,
                         lambda accs, _: accs,
                         [((N_CHIPS, FW, cs_d), F32, (None, tmb, cs_d), lambda i, j: (j, i, 0))] * 2, M=FW, N=D, K=L,
                         tm=tmb, tn=cs_d, tk=tkl, ta=True)

    dqn, dkn, dfv, dcs, drs = _fox_bwd("fox_attention_bwd", qn, kn, proj, c_row, c_col, o_fox, do_fox, lse, H,
                                       ATT_HEADS, 0, 0, 2 * FW)
    dsq, dsk_t, dsv_t = _sb_bwd("sb_attention_bwd", proj, do_sb, HS, ATT_HEADS, QKV_S, QKV_S + SW,
                                QKV_S + 2 * SW)
    dsk = jnp.transpose(dsk_t, (0, 2, 1)).reshape(L, SW)
    dsv = jnp.transpose(dsv_t, (0, 2, 1)).reshape(L, SW)
    dfq, dfk, dgq, dgk = _qknorm_bwd("fox_qk_norm_bwd", proj, dqn, dkn, gq, gk, H, 0, FW, tm)
    drs_row = jnp.pad(drs.reshape(H, nblk, LANES), ((0, 0), (0, nbp - nblk), (0, 0)))
    dfl, dbias = _cum_bwd("forget_cumsum_bwd", drs_row, dcs, fl, bias, nblk)
    dfl_cols = jnp.pad(jnp.transpose(dfl[:, :nblk].reshape(H, L)), ((0, 0), (0, LANES - H))).astype(BF)
    dproj = jnp.concatenate([dfq, dfk, dfv.astype(BF), dsq, dsk.astype(BF), dsv.astype(BF), dgf, dgs, dfl_cols],
                            axis=1)

    tdn = _pick(D, (256, 128))
    (dn2,) = _matmul("d_mix_norm_in", [dproj], [w_proj], [(0, 0, 0)], 1, lambda accs, _: accs,
                     [((L, D), F32) + _mn(tm, tdn)], M=L, N=D, K=W_PROJ, tm=tm, tn=tdn, tk=W_PROJ, tb=True)
    tmw = _pick(D, (1024, 512, 256, 128))
    tnp = _pick(W_PROJ, (1152, 640, 512, 384, 256, 128))
    (dw_proj,) = _matmul("d_w_in", [n2], [dproj], [(0, 0, 0)], 1, lambda accs, _: accs,
                         [((D, W_PROJ), F32) + _mn(tmw, tnp)], M=D, N=W_PROJ, K=L, tm=tmw, tn=tnp, tk=tkl,
                         ta=True)
    dh1, dh1_bf, dg_mix = _rmsnorm_bwd("mix_dnorm", dn2, h1, mix_norm, dh2, tm)
    dw_in_ref = jnp.concatenate([dw_proj[:, :c_f], dw_proj[:, FCOL:FCOL + H], dw_proj[:, c_f:FCOL]], axis=1)
    dw_in = jnp.transpose(dw_in_ref.reshape(D, N_CHIPS, cs_in), (1, 0, 2))
    scatter_end("ffn2", dh1)
    ffn2_keys = list(totals)
    grads.update(zip(ffn2_keys, _share_halves("grads_to_core_pair_ffn2", [totals.pop(k) for k in ffn2_keys])))
    mix_swapping = swap_begin("mix", ["w_in", "w_branch_fox", "w_branch_sb", "w_out"], [dw_in, dwbf, dwbs, dwo])

    def ffn1_grads(*dw):
        swap_begin("ffn1", ["ffn1_w_gate", "ffn1_w_up", "ffn1_w_down"], dw)
        for k in ffn2_keys:
            adamw(k)
        return exchange_begin("ffn1", [updates[k][0] for k in ffn2_keys])

    dh0, _, dg_ffn1 = _ffn_bwd("ffn1", dh1, dh1_bf, saved1, ffn1_norm, wg1, wu1, wd1, tm, cs_ff, after=mix_swapping,
                               midway=lambda done: exchange_begin("mix", done), on_weight_grads=ffn1_grads)
    grad_x = dh0[NM:L_real][None]
    scatter_end("mix", dh0)
    scatter_end("ffn1", dh0)
    big = list(totals)
    grads.update(zip(big, _share_halves("grads_to_core_pair", [totals.pop(k) for k in big])))

    small = [loss_part[:, :1].reshape(1), dh0[:NM].reshape(-1), dg_ffn1.reshape(-1), dg_mix.reshape(-1),
             dg_ffn2.reshape(-1), dbias[:, 0, 0], dgq.reshape(-1), dgk.reshape(-1)]
    sizes = [s.shape[0] for s in small]
    flat = jnp.concatenate(small)
    rows = _round_up(-(-flat.shape[0] // LANES), SUBLANES)
    packed = jnp.pad(flat, (0, rows * LANES - flat.shape[0])).reshape(rows, LANES)
    total = _sum_devices("sum_small", _gather_small("gather_small", packed)).reshape(-1)
    pieces, off = [], 0
    for n_el in sizes:
        pieces.append(total[off:off + n_el])
        off += n_el
    loss = pieces[0][0]
    d_meta = lax.dynamic_slice_in_dim(pieces[1].reshape(NM, D), pos[1] * cs_d, cs_d, axis=1)
    grads.update(meta_tokens=d_meta, ffn1_norm=pieces[2].reshape(1, D), mix_norm=pieces[3].reshape(1, D),
                 ffn2_norm=pieces[4].reshape(1, D), b_forget=pieces[5].reshape(1, H),
                 fox_q_norm=pieces[6].reshape(1, H, HEAD_DIM), fox_k_norm=pieces[7].reshape(1, H, HEAD_DIM))

    for k in names:
        if k not in updates:
            adamw(k)
    return (loss, grad_x, *[grads[k].reshape(weights[k].shape) for k in names],
            *[updates[k][i] for i in range(3) for k in names])
```

```python
import functools

import jax
import jax.numpy as jnp
from jax import lax
from jax.experimental import pallas as pl
from jax.experimental.pallas import tpu as pltpu

F32 = jnp.float32
BF = jnp.bfloat16
I32 = jnp.int32

HEAD_DIM = 128
RMS_EPS = 1e-6
FFN_RESIDUAL_WEIGHT = 0.5
ADAM_LR = 0.001
ADAM_B1 = 0.9
ADAM_B2 = 0.999
ADAM_EPS = 1e-08
ADAM_WD = 0.01
ADAM_STEP = 10

LANES = 128
SUBLANES = 8
ATT_TILE = 128
ATT_HEADS = 4
ATT_HEADS_SB_FWD = 8
VMEM_CAP = 56 * 1024 * 1024
MESH_IDS = pl.DeviceIdType.MESH
N_CHIPS = 4
N_DEV = 8


def _pick(n, cands):
    for c in cands:
        if c <= n and n % c == 0:
            return c
    raise ValueError(f"no tile for {n} among {cands}")


def _round_up(n, m):
    return (n + m - 1) // m * m


def _tile_bytes(shape, dtype):
    item = jnp.dtype(dtype).itemsize
    dims = [d for d in shape if d is not None]
    if not dims:
        return 4 * LANES * SUBLANES
    last = _round_up(dims[-1], LANES)
    sub = _round_up(dims[-2], SUBLANES * (4 // item)) if len(dims) > 1 else 1
    lead = 1
    for d in dims[:-2]:
        lead *= d
    return lead * sub * last * item


def _vmem_limit(blocks, scratch=(), temps=0):
    need = 2 * sum(_tile_bytes(s, d) for s, d in blocks) + sum(_tile_bytes(s, d) for s, d in scratch) + temps
    return int(min(VMEM_CAP, max(need + (4 << 20), 16 << 20)))


def _call(body, *, name, out_shape, grid=(), in_specs=None, out_specs=None, scratch=(), sem=None, vmem=None,
          aliases=None, prefetch=0):
    params = pltpu.CompilerParams(dimension_semantics=sem, vmem_limit_bytes=vmem)
    if prefetch:
        grid_spec = pltpu.PrefetchScalarGridSpec(num_scalar_prefetch=prefetch, grid=grid, in_specs=in_specs,
                                                 out_specs=out_specs, scratch_shapes=scratch)
        return pl.pallas_call(body, out_shape=out_shape, grid_spec=grid_spec, name=name, compiler_params=params,
                              input_output_aliases=aliases or {})
    return pl.pallas_call(body, out_shape=out_shape, grid=grid, in_specs=in_specs, out_specs=out_specs,
                          scratch_shapes=scratch, name=name, compiler_params=params,
                          input_output_aliases=aliases or {})


def _dot(a, b, ca, cb):
    return lax.dot_general(a, b, (((ca,), (cb,)), ((), ())), preferred_element_type=F32)


def _sigmoid(x):
    return 1.0 / (1.0 + jnp.exp(-x))


def _sigmoid_fast(x):
    return pl.reciprocal(1.0 + jnp.exp(-x), approx=True)


def _log_sigmoid(x):
    return jnp.minimum(x, 0.0) - jnp.log1p(jnp.exp(-jnp.abs(x)))


def _split(x, parts):
    pieces = []
    rem = x
    for p in range(parts):
        piece = rem.astype(BF)
        pieces.append(piece)
        if p + 1 < parts:
            rem = rem - piece.astype(F32)
    return pieces


def _pieces_dot(pieces, ones_bf):
    out = None
    for piece in pieces:
        d = _dot(piece, ones_bf, 1, 0)
        out = d if out is None else out + d
    return out


def _split_dot(x, ones_bf, parts):
    return _pieces_dot(_split(x, parts), ones_bf)


def _matmul(name, a_list, b_list, pairs, n_acc, epi, outs, *, M, N, K, tm, tn, tk, ta=False, tb=False,
            extras=(), n_outer=False, b_off=0, after=None):
    if after is not None:
        user_epi = epi
        extras = list(extras) + [(after, (SUBLANES, LANES), lambda i, j: (0, 0))]
        epi = lambda accs, ex: user_epi(accs, ex[:-1])
    gi, gj, nk = M // tm, N // tn, K // tk
    assert gi * tm == M and gj * tn == N and nk * tk == K, (name, M, N, K, tm, tn, tk)
    n_a, n_b, n_e, n_o = len(a_list), len(b_list), len(extras), len(outs)

    def ij(g0, g1):
        return (g1, g0) if n_outer else (g0, g1)

    def a_map(g0, g1, k):
        i, _ = ij(g0, g1)
        return (k, i) if ta else (i, k)

    def b_map(g0, g1, k):
        _, j = ij(g0, g1)
        return (j + b_off, k) if tb else (k, j + b_off)

    def tile_map(fn):
        return lambda g0, g1, k: fn(*ij(g0, g1))

    a_block = (tk, tm) if ta else (tm, tk)
    b_block = (tn, tk) if tb else (tk, tn)
    in_specs = ([pl.BlockSpec(a_block, a_map)] * n_a + [pl.BlockSpec(b_block, b_map)] * n_b
                + [pl.BlockSpec(bs, tile_map(fn)) for _, bs, fn in extras])
    out_specs = [pl.BlockSpec(bs, tile_map(fn)) for _, _, bs, fn in outs]
    out_shape = [jax.ShapeDtypeStruct(s, d) for s, d, _, _ in outs]
    scratch = [pltpu.VMEM((tm, tn), F32) for _ in range(n_acc)] if nk > 1 else []

    def body(*refs):
        a_refs = refs[:n_a]
        b_refs = refs[n_a:n_a + n_b]
        e_refs = refs[n_a + n_b:n_a + n_b + n_e]
        o_refs = refs[n_a + n_b + n_e:n_a + n_b + n_e + n_o]
        acc_refs = refs[n_a + n_b + n_e + n_o:]

        def products():
            accs = [None] * n_acc
            for ai, bi, ci in pairs:
                a = a_refs[ai][...]
                b = b_refs[bi][...]
                d = _dot(a.astype(BF), b.astype(BF), 0 if ta else 1, 1 if tb else 0)
                accs[ci] = d if accs[ci] is None else accs[ci] + d
            return accs

        def finish(accs):
            res = epi(accs, [e[...] for e in e_refs])
            for o_ref, r in zip(o_refs, res):
                o_ref[...] = r.reshape(o_ref.shape).astype(o_ref.dtype)

        if nk == 1:
            finish(products())
        else:
            k = pl.program_id(2)

            @pl.when(k == 0)
            def _():
                for acc in acc_refs:
                    acc[...] = jnp.zeros_like(acc)

            for acc, d in zip(acc_refs, products()):
                acc[...] += d

            @pl.when(k == nk - 1)
            def _():
                finish([acc[...] for acc in acc_refs])

    blocks = ([(a_block, a.dtype) for a in a_list] + [(b_block, b.dtype) for b in b_list]
              + [(bs, e.dtype) for e, bs, _ in extras] + [(bs, d) for _, d, bs, _ in outs])
    vmem = _vmem_limit(blocks, [((tm, tn), F32)] * (n_acc if nk > 1 else 0), temps=6 * tm * tn * 4)
    grid = (gj, gi, nk) if n_outer else (gi, gj, nk)
    fn = _call(body, name=name, out_shape=out_shape, grid=grid, in_specs=in_specs, out_specs=out_specs,
               scratch=scratch, sem=("parallel", "parallel", "arbitrary"), vmem=vmem)
    return fn(*a_list, *b_list, *[e for e, _, _ in extras])


def _mn(tm, tn, col0=0):
    assert col0 % tn == 0
    off = col0 // tn
    return (tm, tn), (lambda i, j: (i, j + off))


def _rmsnorm_fwd(name, x, gain, tr):
    L, D = x.shape

    def body(x_ref, g_ref, o_ref):
        xv = x_ref[...]
        r = lax.rsqrt(jnp.mean(xv * xv, axis=-1, keepdims=True) + RMS_EPS)
        o_ref[...] = (xv * r * g_ref[...]).astype(BF)

    row = pl.BlockSpec((tr, D), lambda i: (i, 0))
    vec = pl.BlockSpec((1, D), lambda i: (0, 0))
    vmem = _vmem_limit([((tr, D), F32), ((tr, D), BF)], temps=3 * tr * D * 4)
    return _call(body, name=name, out_shape=jax.ShapeDtypeStruct((L, D), BF), grid=(L // tr,), in_specs=[row, vec],
                 out_specs=row, sem=("parallel",), vmem=vmem)(x, gain)


def _rmsnorm_bwd(name, dn, x, gain, dres, tr):
    L, D = x.shape
    steps = L // tr

    def body(dn_ref, x_ref, g_ref, dres_ref, dx_ref, dxb_ref, dg_ref):
        i = pl.program_id(0)
        xv = x_ref[...]
        r = lax.rsqrt(jnp.mean(xv * xv, axis=-1, keepdims=True) + RMS_EPS)
        xhat = xv * r
        dy = dn_ref[...]
        dxhat = dy * g_ref[...]
        dx = dres_ref[...] + r * (dxhat - xhat * jnp.mean(dxhat * xhat, axis=-1, keepdims=True))
        dx_ref[...] = dx
        dxb_ref[...] = dx.astype(BF)

        @pl.when(i == 0)
        def _():
            dg_ref[...] = jnp.zeros_like(dg_ref)

        dg_ref[...] += jnp.sum(dy * xhat, axis=0, keepdims=True)

    row = pl.BlockSpec((tr, D), lambda i: (i, 0))
    vec = pl.BlockSpec((1, D), lambda i: (0, 0))
    vmem = _vmem_limit([((tr, D), F32)] * 4 + [((tr, D), BF)], temps=4 * tr * D * 4)
    out_shape = [jax.ShapeDtypeStruct((L, D), F32), jax.ShapeDtypeStruct((L, D), BF),
                 jax.ShapeDtypeStruct((1, D), F32)]
    return _call(body, name=name, out_shape=out_shape, grid=(steps,), in_specs=[row, row, vec, row],
                 out_specs=[row, row, vec], sem=("arbitrary",), vmem=vmem)(dn, x, gain, dres)


def _loss_grad(name, h, target, n_meta, n_seq, tr):
    L, D = h.shape

    def body(h_ref, t_ref, dh_ref, dhb_ref, loss_ref):
        i = pl.program_id(0)
        rows = i * tr + lax.broadcasted_iota(I32, (tr, 1), 0)
        valid = (rows >= n_meta) & (rows < n_meta + n_seq)
        diff = jnp.where(valid, h_ref[...] - t_ref[...], 0.0)
        dh = diff * (1.0 / D)
        dh_ref[...] = dh
        dhb_ref[...] = dh.astype(BF)

        @pl.when(i == 0)
        def _():
            loss_ref[...] = jnp.zeros_like(loss_ref)

        loss_ref[...] += jnp.sum(diff * diff) * (0.5 / D)

    row = pl.BlockSpec((tr, D), lambda i: (i, 0))
    acc = pl.BlockSpec((1, LANES), lambda i: (0, 0))
    vmem = _vmem_limit([((tr, D), F32)] * 3 + [((tr, D), BF)], temps=3 * tr * D * 4)
    out_shape = [jax.ShapeDtypeStruct((L, D), F32), jax.ShapeDtypeStruct((L, D), BF),
                 jax.ShapeDtypeStruct((1, LANES), F32)]
    return _call(body, name=name, out_shape=out_shape, grid=(L // tr,), in_specs=[row, row],
                 out_specs=[row, row, acc], sem=("arbitrary",), vmem=vmem)(h, target)


def _qknorm_fwd(name, proj, gq, gk, heads, q_col, k_col, tr):
    L = proj.shape[0]

    def body(q_ref, k_ref, gq_ref, gk_ref, qn_ref, kn_ref):
        for x_ref, g_ref, o_ref in ((q_ref, gq_ref, qn_ref), (k_ref, gk_ref, kn_ref)):
            xv = x_ref[...].astype(F32)
            r = lax.rsqrt(jnp.mean(xv * xv, axis=-1, keepdims=True) + RMS_EPS)
            o_ref[...] = (xv * r * g_ref[...]).astype(BF)

    qb, kb = q_col // HEAD_DIM, k_col // HEAD_DIM
    in_specs = [pl.BlockSpec((tr, HEAD_DIM), lambda h, i: (i, qb + h)),
                pl.BlockSpec((tr, HEAD_DIM), lambda h, i: (i, kb + h)),
                pl.BlockSpec((None, 1, HEAD_DIM), lambda h, i: (h, 0, 0)),
                pl.BlockSpec((None, 1, HEAD_DIM), lambda h, i: (h, 0, 0))]
    out = pl.BlockSpec((tr, HEAD_DIM), lambda h, i: (i, h))
    out_shape = [jax.ShapeDtypeStruct((L, heads * HEAD_DIM), BF)] * 2
    return _call(body, name=name, out_shape=out_shape, grid=(heads, L // tr), in_specs=in_specs,
                 out_specs=[out, out], sem=("parallel", "parallel"), vmem=16 << 20)(proj, proj, gq, gk)


def _qknorm_bwd(name, proj, dqn, dkn, gq, gk, heads, q_col, k_col, tr):
    L = proj.shape[0]

    def body(q_ref, k_ref, dqn_ref, dkn_ref, gq_ref, gk_ref, dq_ref, dk_ref, dgq_ref, dgk_ref):
        i = pl.program_id(1)
        for x_ref, dy_ref, g_ref, dx_ref, dg_ref in ((q_ref, dqn_ref, gq_ref, dq_ref, dgq_ref),
                                                     (k_ref, dkn_ref, gk_ref, dk_ref, dgk_ref)):
            xv = x_ref[...].astype(F32)
            r = lax.rsqrt(jnp.mean(xv * xv, axis=-1, keepdims=True) + RMS_EPS)
            xhat = xv * r
            dy = dy_ref[...].astype(F32)
            dxhat = dy * g_ref[...]
            dx_ref[...] = (r * (dxhat - xhat * jnp.mean(dxhat * xhat, axis=-1, keepdims=True))).astype(BF)

            @pl.when(i == 0)
            def _():
                dg_ref[...] = jnp.zeros_like(dg_ref)

            dg_ref[...] += jnp.sum(dy * xhat, axis=0, keepdims=True)

    qb, kb = q_col // HEAD_DIM, k_col // HEAD_DIM
    tile = pl.BlockSpec((tr, HEAD_DIM), lambda h, i: (i, h))
    gain = pl.BlockSpec((None, 1, HEAD_DIM), lambda h, i: (h, 0, 0))
    in_specs = [pl.BlockSpec((tr, HEAD_DIM), lambda h, i: (i, qb + h)),
                pl.BlockSpec((tr, HEAD_DIM), lambda h, i: (i, kb + h)), tile, tile, gain, gain]
    out_shape = [jax.ShapeDtypeStruct((L, heads * HEAD_DIM), BF)] * 2 + [
        jax.ShapeDtypeStruct((heads, 1, HEAD_DIM), F32)] * 2
    return _call(body, name=name, out_shape=out_shape, grid=(heads, L // tr), in_specs=in_specs,
                 out_specs=[tile, tile, gain, gain], sem=("parallel", "arbitrary"),
                 vmem=16 << 20)(proj, proj, dqn, dkn, gq, gk)


def _tri(cmp):
    r = lax.broadcasted_iota(I32, (LANES, LANES), 0)
    c = lax.broadcasted_iota(I32, (LANES, LANES), 1)
    return jnp.where(cmp(r, c), 1.0, 0.0).astype(BF)


def _cum_fwd(name, fl, bias, n_rows):
    H, nbp, _ = fl.shape

    def body(fl_ref, b_ref, c_ref, tot_ref):
        lf = _log_sigmoid(fl_ref[...] + b_ref[...])
        c_ref[...] = _split_dot(lf, _tri(lambda r, c: r <= c), 3)
        tot_ref[...] = _split_dot(lf, jnp.ones((LANES, LANES), BF), 3)

        def step(r, carry):
            c_ref[pl.ds(r, 1), :] = c_ref[pl.ds(r, 1), :] + carry
            return carry + tot_ref[pl.ds(r, 1), :]

        lax.fori_loop(0, n_rows, step, jnp.zeros((1, LANES), F32))

    blk = pl.BlockSpec((None, nbp, LANES), lambda h: (h, 0, 0))
    vec = pl.BlockSpec((None, 1, LANES), lambda h: (h, 0, 0))
    return _call(body, name=name, out_shape=jax.ShapeDtypeStruct((H, nbp, LANES), F32), grid=(H,),
                 in_specs=[blk, vec], out_specs=blk, scratch=[pltpu.VMEM((nbp, LANES), F32)], sem=("parallel",),
                 vmem=16 << 20)(fl, bias)


def _cum_bwd(name, drs, dcs, fl, bias, n_rows):
    H, nbp, _ = fl.shape

    def body(drs_ref, dcs_ref, fl_ref, b_ref, dfl_ref, db_ref, rin_ref, tot_ref):
        dc = drs_ref[...] - dcs_ref[...]
        rin_ref[...] = _split_dot(dc, _tri(lambda r, c: r >= c), 3)
        tot_ref[...] = _split_dot(dc, jnp.ones((LANES, LANES), BF), 3)
        dfl_ref[...] = jnp.zeros_like(dfl_ref)

        def step(t, carry):
            r = n_rows - 1 - t
            x = fl_ref[pl.ds(r, 1), :] + b_ref[...]
            dfl_ref[pl.ds(r, 1), :] = (rin_ref[pl.ds(r, 1), :] + carry) * _sigmoid(-x)
            return carry + tot_ref[pl.ds(r, 1), :]

        lax.fori_loop(0, n_rows, step, jnp.zeros((1, LANES), F32))
        db_ref[...] = jnp.zeros_like(db_ref) + jnp.sum(dfl_ref[...])

    blk = pl.BlockSpec((None, nbp, LANES), lambda h: (h, 0, 0))
    vec = pl.BlockSpec((None, 1, LANES), lambda h: (h, 0, 0))
    out_shape = [jax.ShapeDtypeStruct((H, nbp, LANES), F32), jax.ShapeDtypeStruct((H, 1, LANES), F32)]
    return _call(body, name=name, out_shape=out_shape, grid=(H,), in_specs=[blk, blk, blk, vec],
                 out_specs=[blk, vec], scratch=[pltpu.VMEM((nbp, LANES), F32)] * 2, sem=("parallel",),
                 vmem=16 << 20)(drs, dcs, fl, bias)


def _att_specs(L, G, q_col, k_col, v_col):
    T = ATT_TILE
    W = G * HEAD_DIM
    assert q_col % W == 0 and k_col % W == 0 and v_col % W == 0
    qb, kb, vb = q_col // W, k_col // W, v_col // W
    q_spec = pl.BlockSpec((T, W), lambda h, i: (i, qb + h))
    k_spec = pl.BlockSpec((L, W), lambda h, i: (0, kb + h), pipeline_mode=pl.Buffered(1))
    v_spec = pl.BlockSpec((L, W), lambda h, i: (0, vb + h), pipeline_mode=pl.Buffered(1))
    return q_spec, k_spec, v_spec


def _head_lanes(G):
    return [slice(g * HEAD_DIM, (g + 1) * HEAD_DIM) for g in range(G)]


def _tile_iotas():
    T = ATT_TILE
    return lax.broadcasted_iota(I32, (T, T), 0), lax.broadcasted_iota(I32, (T, T), 1)


def _rows(j):
    return pl.ds(pl.multiple_of(j * ATT_TILE, ATT_TILE), ATT_TILE)


def _fox_fwd(name, q_arr, k_arr, v_arr, c_row, c_col, heads, G, q_col, k_col, v_col):
    L = q_arr.shape[0]
    T = ATT_TILE
    scale = HEAD_DIM ** -0.5
    lanes = _head_lanes(G)

    def body(q_ref, k_ref, v_ref, crow_ref, ccol_ref, o_ref, lse_ref):
        i = pl.program_id(1)
        qs = [q_ref[:, hl] for hl in lanes]
        cts = [jnp.broadcast_to(ccol_ref[g], (T, T)) for g in range(G)]
        row, col = _tile_iotas()

        def tile(j, carry, masked):
            qk = [_dot(qs[g], k_ref[_rows(j), hl], 1, 1) for g, hl in enumerate(lanes)]
            stats = []
            for g in range(G):
                m, l, _ = carry[g]
                s = qk[g] * scale + (cts[g] - crow_ref[g, pl.ds(j, 1), :])
                if masked:
                    s = jnp.where(col <= row, s, -jnp.inf)
                m_new = jnp.maximum(m, jnp.max(s, axis=1, keepdims=True))
                alpha = jnp.exp(m - m_new)
                p = jnp.exp(s - m_new)
                stats.append((m_new, alpha, alpha * l + jnp.sum(p, axis=1, keepdims=True), p.astype(BF)))
            pv = [_dot(stats[g][3], v_ref[_rows(j), hl], 1, 0) for g, hl in enumerate(lanes)]
            return tuple((stats[g][0], stats[g][2], stats[g][1] * carry[g][2] + pv[g]) for g in range(G))

        init = tuple((jnp.full((T, 1), -1e30, F32), jnp.zeros((T, 1), F32), jnp.zeros((T, HEAD_DIM), F32))
                     for _ in range(G))
        carry = lax.fori_loop(0, i, lambda j, c: tile(j, c, False), init)
        for g, (m, l, acc) in enumerate(tile(i, carry, True)):
            o_ref[:, lanes[g]] = (acc / l).astype(o_ref.dtype)
            lse_ref[g] = m + jnp.log(l)

    nbp = c_row.shape[1]
    W = G * HEAD_DIM
    q_spec, k_spec, v_spec = _att_specs(L, G, q_col, k_col, v_col)
    crow_spec = pl.BlockSpec((G, nbp, LANES), lambda h, i: (h, 0, 0))
    col_spec = pl.BlockSpec((G, T, 1), lambda h, i: (h, i, 0))
    o_spec = pl.BlockSpec((T, W), lambda h, i: (i, h))
    out_shape = [jax.ShapeDtypeStruct((L, heads * HEAD_DIM), BF), jax.ShapeDtypeStruct((heads, L, 1), F32)]
    vmem = _vmem_limit([((L, W), BF)] * 2, temps=8 << 20)
    return _call(body, name=name, out_shape=out_shape, grid=(heads // G, L // T),
                 in_specs=[q_spec, k_spec, v_spec, crow_spec, col_spec], out_specs=[o_spec, col_spec],
                 sem=("parallel", "parallel"), vmem=vmem)(q_arr, k_arr, v_arr, c_row, c_col)


def _fox_bwd(name, q_arr, k_arr, v_arr, c_row, c_col, o, do, lse, heads, G, q_col, k_col, v_col):
    L = q_arr.shape[0]
    T = ATT_TILE
    nq = L // T
    scale = HEAD_DIM ** -0.5
    lanes = _head_lanes(G)

    def body(q_ref, k_ref, v_ref, crow_ref, ccol_ref, o_ref, do_ref, lse_ref, dq_ref, dk_ref, dv_ref, dcs_ref,
             drs_ref, dk_acc, dv_acc):
        i = pl.program_id(1)

        @pl.when(i == 0)
        def _():
            dk_acc[...] = jnp.zeros_like(dk_acc)
            dv_acc[...] = jnp.zeros_like(dv_acc)
            dcs_ref[...] = jnp.zeros_like(dcs_ref)

        qs = [q_ref[:, hl] for hl in lanes]
        dos = [do_ref[:, hl] for hl in lanes]
        q_ts = [qs[g].T for g in range(G)]
        do_ts = [dos[g].T for g in range(G)]
        deltas = [jnp.broadcast_to(jnp.sum(dos[g].astype(F32) * o_ref[:, hl].astype(F32), axis=1, keepdims=True),
                                   (T, T)) for g, hl in enumerate(lanes)]
        lses = [jnp.broadcast_to(lse_ref[g], (T, T)) for g in range(G)]
        cts = [jnp.broadcast_to(ccol_ref[g], (T, T)) for g in range(G)]
        row, col = _tile_iotas()

        def tile(j, carry, masked):
            ks = [k_ref[_rows(j), hl] for hl in lanes]
            qk = [_dot(qs[g], ks[g], 1, 1) for g in range(G)]
            dp = [_dot(dos[g], v_ref[_rows(j), hl], 1, 1) for g, hl in enumerate(lanes)]
            pbs, dsbs, row_sums = [], [], []
            for g in range(G):
                s = qk[g] * scale + (cts[g] - crow_ref[g, pl.ds(j, 1), :])
                if masked:
                    s = jnp.where(col <= row, s, -jnp.inf)
                p = jnp.exp(s - lses[g])
                ds = p * (dp[g] - deltas[g])
                dcs_ref[g, pl.ds(j, 1), :] += jnp.sum(ds, axis=0, keepdims=True)
                row_sums.append(carry[g][1] + jnp.sum(ds, axis=1, keepdims=True))
                pbs.append(p.astype(BF))
                dsbs.append((ds * scale).astype(BF))
            for g, hl in enumerate(lanes):
                dk_acc[j, hl, :] += _dot(q_ts[g], dsbs[g], 1, 0)
            for g, hl in enumerate(lanes):
                dv_acc[j, hl, :] += _dot(do_ts[g], pbs[g], 1, 0)
            return tuple((carry[g][0] + _dot(dsbs[g], ks[g], 1, 0), row_sums[g]) for g in range(G))

        init = tuple((jnp.zeros((T, HEAD_DIM), F32), jnp.zeros((T, 1), F32)) for _ in range(G))
        carry = lax.fori_loop(0, i, lambda j, c: tile(j, c, False), init)
        for g, (dq, row_sum) in enumerate(tile(i, carry, True)):
            dq_ref[:, lanes[g]] = dq.astype(dq_ref.dtype)
            drs_ref[g] = row_sum

        @pl.when(i == nq - 1)
        def _():
            for r in range(nq):
                for hl in lanes:
                    dk_ref[r * T:(r + 1) * T, hl] = dk_acc[r, hl, :].T.astype(dk_ref.dtype)
                    dv_ref[r * T:(r + 1) * T, hl] = dv_acc[r, hl, :].T.astype(dv_ref.dtype)

    nbp = c_row.shape[1]
    WG = G * HEAD_DIM
    q_spec, k_spec, v_spec = _att_specs(L, G, q_col, k_col, v_col)
    crow_spec = pl.BlockSpec((G, nbp, LANES), lambda h, i: (h, 0, 0))
    col_spec = pl.BlockSpec((G, T, 1), lambda h, i: (h, i, 0))
    t_spec = pl.BlockSpec((T, WG), lambda h, i: (i, h))
    head_spec = pl.BlockSpec((L, WG), lambda h, i: (0, h), pipeline_mode=pl.Buffered(1))
    W = heads * HEAD_DIM
    out_shape = [jax.ShapeDtypeStruct((L, W), F32)] * 3 + [jax.ShapeDtypeStruct((heads, nbp, LANES), F32),
                                                           jax.ShapeDtypeStruct((heads, L, 1), F32)]
    scratch = [pltpu.VMEM((nq, WG, T), F32)] * 2
    vmem = _vmem_limit([], [((L, WG), BF)] * 2 + [((L, WG), F32)] * 4, temps=8 << 20)
    return _call(body, name=name, out_shape=out_shape, grid=(heads // G, nq),
                 in_specs=[q_spec, k_spec, v_spec, crow_spec, col_spec, t_spec, t_spec, col_spec],
                 out_specs=[t_spec, head_spec, head_spec, crow_spec, col_spec], scratch=scratch,
                 sem=("parallel", "arbitrary"), vmem=vmem)(q_arr, k_arr, v_arr, c_row, c_col, o, do, lse)


def _sb_logits(qk, scale, valid):
    z = qk * scale
    lb = jnp.minimum(z, 0.0) - jnp.log1p(jnp.exp(-jnp.abs(z)))
    lom = lb - z
    if valid is not None:
        lom = jnp.where(valid, lom, 0.0)
    return lb, lom


def _sb_fwd(name, proj, heads, G, q_col, k_col, v_col):
    L = proj.shape[0]
    T = ATT_TILE
    scale = HEAD_DIM ** -0.5
    lanes = _head_lanes(G)

    def body(q_ref, k_ref, v_ref, o_ref):
        i = pl.program_id(1)
        qs = [q_ref[:, hl] for hl in lanes]
        row, col = _tile_iotas()
        later_mat = jnp.where(row > col, 1.0, 0.0).astype(BF)

        def tile(j, carry, masked):
            valid = (col < row) if masked else None
            qk = [_dot(qs[g], k_ref[_rows(j), hl], 1, 1) for g, hl in enumerate(lanes)]
            logits = [_sb_logits(qk[g], scale, valid) for g in range(G)]
            pieces = [_split(lom, 2) for _, lom in logits]
            later = [_pieces_dot(pieces[g], later_mat) for g in range(G)]
            ws = []
            for g in range(G):
                w = jnp.exp(logits[g][0] + later[g] + carry[g][0])
                if masked:
                    w = jnp.where(valid, w, 0.0)
                ws.append(w.astype(BF))
            wv = [_dot(ws[g], v_ref[_rows(j), hl], 1, 0) for g, hl in enumerate(lanes)]
            return tuple((carry[g][0] + jnp.sum(logits[g][1], axis=1, keepdims=True), carry[g][1] + wv[g])
                         for g in range(G))

        init = tuple((jnp.zeros((T, 1), F32), jnp.zeros((T, HEAD_DIM), F32)) for _ in range(G))
        carry = tile(i, init, True)
        carry = lax.fori_loop(0, i, lambda t, c: tile(i - 1 - t, c, False), carry)
        for g, (_, acc) in enumerate(carry):
            o_ref[:, lanes[g]] = acc.astype(o_ref.dtype)

    W = G * HEAD_DIM
    q_spec, k_spec, v_spec = _att_specs(L, G, q_col, k_col, v_col)
    o_spec = pl.BlockSpec((T, W), lambda h, i: (i, h))
    vmem = _vmem_limit([((L, W), BF)] * 2, temps=8 << 20)
    return _call(body, name=name, out_shape=jax.ShapeDtypeStruct((L, heads * HEAD_DIM), BF),
                 grid=(heads // G, L // T), in_specs=[q_spec, k_spec, v_spec], out_specs=o_spec,
                 sem=("parallel", "parallel"), vmem=vmem)(proj, proj, proj)


def _sb_bwd(name, proj, do, heads, G, q_col, k_col, v_col):
    L = proj.shape[0]
    T = ATT_TILE
    nq = L // T
    scale = HEAD_DIM ** -0.5
    lanes = _head_lanes(G)

    def body(q_ref, k_ref, v_ref, do_ref, dq_ref, dk_acc, dv_acc, da_buf, beta_buf):
        i = pl.program_id(1)

        @pl.when(i == 0)
        def _():
            dk_acc[...] = jnp.zeros_like(dk_acc)
            dv_acc[...] = jnp.zeros_like(dv_acc)

        qs = [q_ref[:, hl] for hl in lanes]
        dos = [do_ref[:, hl] for hl in lanes]
        q_ts = [qs[g].T for g in range(G)]
        do_ts = [dos[g].T for g in range(G)]
        row, col = _tile_iotas()
        later_mat = jnp.where(row > col, 1.0, 0.0).astype(BF)
        before_mat = jnp.where(row < col, 1.0, 0.0).astype(BF)

        def pass1(j, runs, masked):
            valid = (col < row) if masked else None
            qk = [_dot(qs[g], k_ref[_rows(j), hl], 1, 1) for g, hl in enumerate(lanes)]
            dw = [_dot(dos[g], v_ref[_rows(j), hl], 1, 1) for g, hl in enumerate(lanes)]
            logits = [_sb_logits(qk[g], scale, valid) for g in range(G)]
            pieces = [_split(lom, 2) for _, lom in logits]
            later = [_pieces_dot(pieces[g], later_mat) for g in range(G)]
            ws = []
            for g in range(G):
                w = jnp.exp(logits[g][0] + later[g] + runs[g])
                if masked:
                    w = jnp.where(valid, w, 0.0)
                da_buf[g * nq + j] = dw[g] * w
                beta_buf[g * nq + j] = jnp.exp(logits[g][0])
                ws.append(w.astype(BF))
            for g, hl in enumerate(lanes):
                dv_acc[j, hl, :] += _dot(do_ts[g], ws[g], 1, 0)
            return tuple(runs[g] + jnp.sum(logits[g][1], axis=1, keepdims=True) for g in range(G))

        runs = pass1(i, tuple(jnp.zeros((T, 1), F32) for _ in range(G)), True)
        lax.fori_loop(0, i, lambda t, c: pass1(i - 1 - t, c, False), runs)

        def pass2(j, carry, masked):
            das = [da_buf[g * nq + j] for g in range(G)]
            pieces = [_split(da, 2) for da in das]
            before = [_pieces_dot(pieces[g], before_mat) for g in range(G)]
            dzbs = []
            for g in range(G):
                beta = beta_buf[g * nq + j]
                dz = das[g] * (1.0 - beta) - (carry[g][0] + before[g]) * beta
                if masked:
                    dz = jnp.where(col < row, dz, 0.0)
                dzbs.append((dz * scale).astype(BF))
            for g, hl in enumerate(lanes):
                dk_acc[j, hl, :] += _dot(q_ts[g], dzbs[g], 1, 0)
            dq = [_dot(dzbs[g], k_ref[_rows(j), hl], 1, 0) for g, hl in enumerate(lanes)]
            return tuple((carry[g][0] + jnp.sum(das[g], axis=1, keepdims=True), carry[g][1] + dq[g])
                         for g in range(G))

        init = tuple((jnp.zeros((T, 1), F32), jnp.zeros((T, HEAD_DIM), F32)) for _ in range(G))
        carry = lax.fori_loop(0, i, lambda j, c: pass2(j, c, False), init)
        for g, (_, dq) in enumerate(pass2(i, carry, True)):
            dq_ref[:, lanes[g]] = dq.astype(dq_ref.dtype)

    WG = G * HEAD_DIM
    q_spec, k_spec, v_spec = _att_specs(L, G, q_col, k_col, v_col)
    t_spec = pl.BlockSpec((T, WG), lambda h, i: (i, h))
    head_spec = pl.BlockSpec((nq, WG, T), lambda h, i: (0, h, 0), pipeline_mode=pl.Buffered(1))
    W = heads * HEAD_DIM
    out_shape = [jax.ShapeDtypeStruct((L, W), BF)] + [jax.ShapeDtypeStruct((nq, W, T), F32)] * 2
    scratch = [pltpu.VMEM((G * nq, T, T), F32)] * 2
    vmem = _vmem_limit([], [((L, WG), BF)] * 2 + [((L, WG), F32)] * 2 + [((G * nq, T, T), F32)] * 2,
                       temps=6 << 20)
    return _call(body, name=name, out_shape=out_shape, grid=(heads // G, nq),
                 in_specs=[q_spec, k_spec, v_spec, t_spec], out_specs=[t_spec, head_spec, head_spec],
                 scratch=scratch, sem=("parallel", "arbitrary"), vmem=vmem)(proj, proj, proj, do)


_ANY = pl.BlockSpec(memory_space=pl.ANY)


def _mesh_pos():
    return lax.axis_index("x"), lax.axis_index("y"), lax.axis_index("c")


def _other_chips(x, y):
    return [(1 - x, y), (x, 1 - y), (1 - x, 1 - y)]


def _shard_window(ref, kind, sidx, r0, nr, cs):
    if kind == "col":
        assert cs % LANES == 0
        return ref.at[pl.ds(r0, nr), pl.ds(pl.multiple_of(sidx * cs, LANES), cs)]
    return ref.at[sidx, pl.ds(r0, nr), :]


def _place_shard(name, pos, shard, kind, dtype):
    R, C = shard.shape
    tr = _row_tile(R, C, 2, 16)

    def body(pos_ref, s_ref, o_ref):
        o_ref[...] = s_ref[...].astype(o_ref.dtype)

    if kind == "col":
        assert C % LANES == 0
        out_shape = jax.ShapeDtypeStruct((R, N_CHIPS * C), dtype)
        out_spec = pl.BlockSpec((tr, C), lambda r, pos_ref: (r, pos_ref[1]))
    else:
        out_shape = jax.ShapeDtypeStruct((N_CHIPS, R, C), dtype)
        out_spec = pl.BlockSpec((None, tr, C), lambda r, pos_ref: (pos_ref[1], r, 0))
    return _call(body, name=name, out_shape=out_shape, grid=(R // tr,),
                 in_specs=[pl.BlockSpec((tr, C), lambda r, pos_ref: (r, 0))], out_specs=out_spec, sem=("parallel",),
                 vmem=32 << 20, prefetch=1)(pos, shard)


_HBM = pl.BlockSpec(memory_space=pltpu.HBM)
_SEM = pl.BlockSpec(memory_space=pltpu.SEMAPHORE)
_KEEP_ORDER = pltpu.SideEffectType.DATAFLOW_SIDE_EFFECTING


def _in_hbm(x):
    return pltpu.with_memory_space_constraint(x, pltpu.HBM)


def _gather_copies(bufs, meta, send_sem, recv_sem):
    x, y, c = _mesh_pos()
    out = []
    for t, (kind, R, cs) in enumerate(meta):
        half = R // 2
        r0 = pl.multiple_of(c * half, SUBLANES)
        mine = _shard_window(bufs[t], kind, 2 * x + y, r0, half, cs)
        for p, (px, py) in enumerate(_other_chips(x, y)):
            k = 3 * t + p
            args = dict(send_sem=send_sem.at[k], recv_sem=recv_sem.at[k], device_id=(px, py, c),
                        device_id_type=MESH_IDS)
            out.append((pltpu.make_async_remote_copy(src_ref=mine, dst_ref=mine, **args),
                        pltpu.make_async_remote_copy(
                            src_ref=mine, dst_ref=_shard_window(bufs[t], kind, 2 * px + py, r0, half, cs), **args)))
    return out


def _all_gather_ici_start(name, gathered, shards, kinds, after):
    n = len(gathered)
    meta = [(kind, s.shape[0], s.shape[1]) for s, kind in zip(shards, kinds)]

    def body(*refs):
        send_sem, recv_sem = refs[n + 1], refs[n + 2]
        bufs = refs[n + 3:2 * n + 3]
        token = refs[2 * n + 3]
        for send, _ in _gather_copies(bufs, meta, send_sem, recv_sem):
            send.start()
        token[...] = jnp.zeros_like(token)

    out_shape = (pltpu.SemaphoreType.DMA((3 * n,)), pltpu.SemaphoreType.DMA((3 * n,)),
                 *[pltpu.HBM(g.shape, g.dtype) for g in gathered], jax.ShapeDtypeStruct((SUBLANES, LANES), F32))
    out_specs = (_SEM, _SEM, *[_HBM] * n, pl.BlockSpec(memory_space=pltpu.VMEM))
    res = pl.pallas_call(body, out_shape=out_shape, in_specs=[_HBM] * n + [_ANY], out_specs=out_specs,
                         input_output_aliases={t: 2 + t for t in range(n)}, name=name,
                         compiler_params=pltpu.CompilerParams(has_side_effects=_KEEP_ORDER))(
        *[_in_hbm(g) for g in gathered], after)
    return res[0], res[1], list(res[2:2 + n]), res[2 + n]


def _all_gather_ici_wait(name, bufs, send_sem, recv_sem, after, shards, kinds):
    n = len(bufs)
    meta = [(kind, s.shape[0], s.shape[1]) for s, kind in zip(shards, kinds)]
    after = list(after) if isinstance(after, (list, tuple)) else [after]

    def body(*refs):
        for send, recv in _gather_copies(refs[:n], meta, refs[n], refs[n + 1]):
            send.wait_send()
            recv.wait_recv()

    out_shape = tuple(pltpu.HBM(b.shape, b.dtype) for b in bufs)
    res = pl.pallas_call(body, out_shape=out_shape, in_specs=[_HBM] * n + [_SEM, _SEM] + [_ANY] * len(after),
                         out_specs=tuple([_HBM] * n), input_output_aliases={t: t for t in range(n)}, name=name,
                         compiler_params=pltpu.CompilerParams(has_side_effects=_KEEP_ORDER))(
        *bufs, send_sem, recv_sem, *after)
    return list(res)


def _all_gather_d2d(name, gathered, shards, kinds):
    n = len(gathered)
    meta = [(kind, s.shape[0], s.shape[1]) for s, kind in zip(shards, kinds)]

    def body(*refs):
        bufs = refs[n:2 * n]
        send_sem, recv_sem = refs[2 * n:]
        x, y, c = _mesh_pos()
        sends, recvs = [], []
        for t, (kind, R, cs) in enumerate(meta):
            half = R // 2
            mine = pl.multiple_of(c * half, SUBLANES)
            theirs = pl.multiple_of((1 - c) * half, SUBLANES)
            for p, (px, py) in enumerate(_other_chips(x, y)):
                k = 3 * t + p
                win = _shard_window(bufs[t], kind, 2 * px + py, mine, half, cs)
                cp = pltpu.make_async_remote_copy(src_ref=win, dst_ref=win, send_sem=send_sem.at[k],
                                                  recv_sem=recv_sem.at[k], device_id=(x, y, 1 - c),
                                                  device_id_type=MESH_IDS)
                cp.start()
                sends.append(cp)
                got = _shard_window(bufs[t], kind, 2 * px + py, theirs, half, cs)
                recvs.append(pltpu.make_async_remote_copy(src_ref=win, dst_ref=got, send_sem=send_sem.at[k],
                                                          recv_sem=recv_sem.at[k], device_id=(x, y, 1 - c),
                                                          device_id_type=MESH_IDS))
        for cp in recvs:
            cp.wait_recv()
        for cp in sends:
            cp.wait_send()

    scratch = [pltpu.SemaphoreType.DMA((3 * n,)), pltpu.SemaphoreType.DMA((3 * n,))]
    out_shape = [jax.ShapeDtypeStruct(g.shape, g.dtype) for g in gathered]
    return _call(body, name=name, out_shape=out_shape, in_specs=[_ANY] * n, out_specs=[_ANY] * n, scratch=scratch,
                 aliases={t: t for t in range(n)})(*gathered)


def _sibling_copies(grads, lands, send_sem, recv_sem):
    x, y, c = _mesh_pos()
    out = []
    for t in range(len(grads)):
        half = lands[t].shape[1]
        theirs = pl.multiple_of((1 - c) * half, SUBLANES)
        out.append(pltpu.make_async_remote_copy(src_ref=grads[t].at[:, pl.ds(theirs, half), :], dst_ref=lands[t],
                                                send_sem=send_sem.at[t], recv_sem=recv_sem.at[t],
                                                device_id=(x, y, 1 - c), device_id_type=MESH_IDS))
    return out


def _sibling_start(name, grads):
    n = len(grads)
    lands = [lax.empty((N_CHIPS, g.shape[1] // 2, g.shape[2]), g.dtype) for g in grads]

    def body(*refs):
        send_sem, recv_sem = refs[2 * n], refs[2 * n + 1]
        srcs, zones = refs[2 * n + 2:3 * n + 2], refs[3 * n + 2:4 * n + 2]
        token = refs[4 * n + 2]
        for cp in _sibling_copies(srcs, zones, send_sem, recv_sem):
            cp.start()
        token[...] = jnp.zeros_like(token)

    out_shape = (pltpu.SemaphoreType.DMA((n,)), pltpu.SemaphoreType.DMA((n,)),
                 *[pltpu.HBM(a.shape, a.dtype) for a in list(grads) + lands],
                 jax.ShapeDtypeStruct((SUBLANES, LANES), F32))
    out_specs = (_SEM, _SEM, *[_HBM] * (2 * n), pl.BlockSpec(memory_space=pltpu.VMEM))
    res = pl.pallas_call(body, out_shape=out_shape, in_specs=[_HBM] * (2 * n), out_specs=out_specs,
                         input_output_aliases={t: 2 + t for t in range(2 * n)}, name=name,
                         compiler_params=pltpu.CompilerParams(has_side_effects=_KEEP_ORDER))(
        *[_in_hbm(a) for a in list(grads) + lands])
    return res[0], res[1], list(res[2:2 + n]), list(res[2 + n:2 + 2 * n]), res[2 + 2 * n]


def _sibling_wait(name, grads, lands, send_sem, recv_sem, after):
    n = len(grads)
    after = list(after) if isinstance(after, (list, tuple)) else [after]

    def body(*refs):
        for cp in _sibling_copies(refs[:n], refs[n:2 * n], refs[2 * n], refs[2 * n + 1]):
            cp.wait_send()
            cp.wait_recv()

    out_shape = tuple(pltpu.HBM(a.shape, a.dtype) for a in list(grads) + list(lands))
    res = pl.pallas_call(body, out_shape=out_shape, in_specs=[_HBM] * (2 * n) + [_SEM, _SEM] + [_ANY] * len(after),
                         out_specs=tuple([_HBM] * (2 * n)), input_output_aliases={t: t for t in range(2 * n)},
                         name=name, compiler_params=pltpu.CompilerParams(has_side_effects=_KEEP_ORDER))(
        *grads, *lands, send_sem, recv_sem, *after)
    return list(res[:n]), list(res[n:])


def _exchange_copies(parts, lands, send_sem, recv_sem):
    x, y, c = _mesh_pos()
    out = []
    for t in range(len(parts)):
        for p, (px, py) in enumerate(_other_chips(x, y)):
            k = 3 * t + p
            out.append(pltpu.make_async_remote_copy(src_ref=parts[t].at[2 * px + py], dst_ref=lands[t].at[p],
                                                    send_sem=send_sem.at[k], recv_sem=recv_sem.at[k],
                                                    device_id=(px, py, c), device_id_type=MESH_IDS))
    return out


def _exchange_start(name, partials):
    n = len(partials)
    lands = [lax.empty((3,) + p.shape[1:], p.dtype) for p in partials]

    def body(*refs):
        send_sem, recv_sem = refs[2 * n], refs[2 * n + 1]
        parts, zones = refs[2 * n + 2:3 * n + 2], refs[3 * n + 2:4 * n + 2]
        token = refs[4 * n + 2]
        for cp in _exchange_copies(parts, zones, send_sem, recv_sem):
            cp.start()
        token[...] = jnp.zeros_like(token)

    out_shape = (pltpu.SemaphoreType.DMA((3 * n,)), pltpu.SemaphoreType.DMA((3 * n,)),
                 *[pltpu.HBM(a.shape, a.dtype) for a in partials + lands],
                 jax.ShapeDtypeStruct((SUBLANES, LANES), F32))
    out_specs = (_SEM, _SEM, *[_HBM] * (2 * n), pl.BlockSpec(memory_space=pltpu.VMEM))
    res = pl.pallas_call(body, out_shape=out_shape, in_specs=[_HBM] * (2 * n), out_specs=out_specs,
                         input_output_aliases={t: 2 + t for t in range(2 * n)}, name=name,
                         compiler_params=pltpu.CompilerParams(has_side_effects=_KEEP_ORDER))(
        *[_in_hbm(a) for a in partials + lands])
    return res[0], res[1], list(res[2:2 + n]), list(res[2 + n:2 + 2 * n]), res[2 + 2 * n]


def _exchange_wait(name, parts, lands, send_sem, recv_sem, after):
    n = len(parts)

    def body(*refs):
        for cp in _exchange_copies(refs[:n], refs[n:2 * n], refs[2 * n], refs[2 * n + 1]):
            cp.wait_send()
            cp.wait_recv()

    out_shape = tuple(pltpu.HBM(a.shape, a.dtype) for a in parts + lands)
    res = pl.pallas_call(body, out_shape=out_shape, in_specs=[_HBM] * (2 * n) + [_SEM, _SEM, _ANY],
                         out_specs=tuple([_HBM] * (2 * n)), input_output_aliases={t: t for t in range(2 * n)},
                         name=name, compiler_params=pltpu.CompilerParams(has_side_effects=_KEEP_ORDER))(
        *parts, *lands, send_sem, recv_sem, after)
    return list(res[n:])


def _share_halves(name, totals):
    n = len(totals)

    def body(*refs):
        bufs = refs[n:2 * n]
        send_sem, recv_sem = refs[2 * n:]
        x, y, c = _mesh_pos()
        sends, recvs = [], []
        for t, g in enumerate(totals):
            half = g.shape[0] // 2
            mine = bufs[t].at[pl.ds(pl.multiple_of(c * half, SUBLANES), half), :]
            theirs = bufs[t].at[pl.ds(pl.multiple_of((1 - c) * half, SUBLANES), half), :]
            cp = pltpu.make_async_remote_copy(src_ref=mine, dst_ref=mine, send_sem=send_sem.at[t],
                                              recv_sem=recv_sem.at[t], device_id=(x, y, 1 - c),
                                              device_id_type=MESH_IDS)
            cp.start()
            sends.append(cp)
            recvs.append(pltpu.make_async_remote_copy(src_ref=mine, dst_ref=theirs, send_sem=send_sem.at[t],
                                                      recv_sem=recv_sem.at[t], device_id=(x, y, 1 - c),
                                                      device_id_type=MESH_IDS))
        for cp in recvs:
            cp.wait_recv()
        for cp in sends:
            cp.wait_send()

    out_shape = [jax.ShapeDtypeStruct(g.shape, g.dtype) for g in totals]
    scratch = [pltpu.SemaphoreType.DMA((n,)), pltpu.SemaphoreType.DMA((n,))]
    return _call(body, name=name, out_shape=out_shape, in_specs=[_ANY] * n, out_specs=[_ANY] * n, scratch=scratch,
                 aliases={t: t for t in range(n)})(*totals)


def _gather_small(name, v):
    def body(v_ref, out_ref, send_sem, recv_sem, local_sem):
        x, y, c = _mesh_pos()
        me = 4 * x + 2 * y + c
        local = pltpu.make_async_copy(v_ref, out_ref.at[me], local_sem)
        local.start()
        sends, recvs = [], []
        for k in range(1, N_DEV):
            px = 1 - x if k & 4 else x
            py = 1 - y if k & 2 else y
            pc = 1 - c if k & 1 else c
            cp = pltpu.make_async_remote_copy(src_ref=v_ref, dst_ref=out_ref.at[me], send_sem=send_sem.at[k],
                                              recv_sem=recv_sem.at[k], device_id=(px, py, pc),
                                              device_id_type=MESH_IDS)
            cp.start()
            sends.append(cp)
            recvs.append(pltpu.make_async_remote_copy(
                src_ref=v_ref, dst_ref=out_ref.at[4 * px + 2 * py + pc], send_sem=send_sem.at[k],
                recv_sem=recv_sem.at[k], device_id=(px, py, pc), device_id_type=MESH_IDS))
        for cp in recvs:
            cp.wait_recv()
        for cp in sends:
            cp.wait_send()
        local.wait()

    scratch = [pltpu.SemaphoreType.DMA((N_DEV,)), pltpu.SemaphoreType.DMA((N_DEV,)), pltpu.SemaphoreType.DMA(())]
    return _call(body, name=name, out_shape=jax.ShapeDtypeStruct((N_DEV,) + v.shape, v.dtype), in_specs=[_ANY],
                 out_specs=_ANY, scratch=scratch)(v)


def _row_tile(rows, cols, n_arrays, mult=SUBLANES):
    budget = (24 << 20) // (2 * n_arrays * _round_up(cols, LANES) * 4)
    for t in (512, 256, 128, 64, 32, 16, 8):
        if t <= max(budget, mult) and rows % t == 0 and t % mult == 0:
            return t
    raise ValueError(f"no row tile for {rows} x {cols}")


def _chip_partial(name, pos, own, recv):
    _, half, C = recv.shape
    tr = _row_tile(half, C, 3, 16)
    nh = half // tr

    def body(pos_ref, own_ref, recv_ref, out_ref):
        out_ref[...] = (own_ref[...] + recv_ref[...]).astype(BF)

    def shard(s, pos_ref):
        return (pos_ref[1] + 1 + s) % N_CHIPS

    blk = pl.BlockSpec((None, tr, C), lambda s, r, pos_ref: (shard(s, pos_ref), r, 0))
    own_blk = pl.BlockSpec((None, tr, C), lambda s, r, pos_ref: (shard(s, pos_ref), pos_ref[0] * nh + r, 0))
    return _call(body, name=name, out_shape=jax.ShapeDtypeStruct(recv.shape, BF), grid=(N_CHIPS - 1, nh),
                 in_specs=[own_blk, blk], out_specs=blk, sem=("parallel", "parallel"), vmem=40 << 20,
                 prefetch=1)(pos, own, recv)


def _final_half(name, pos, own, recv, others):
    _, half, C = recv.shape
    tr = _row_tile(half, C, 4, 16)
    nh = half // tr

    def body(pos_ref, own_ref, recv_ref, oth_ref, out_ref):
        acc = own_ref[...] + recv_ref[...]
        for p in range(3):
            acc = acc + oth_ref[p].astype(F32)
        out_ref[...] = acc

    own_blk = pl.BlockSpec((None, tr, C), lambda r, pos_ref: (pos_ref[1], pos_ref[0] * nh + r, 0))
    recv_blk = pl.BlockSpec((None, tr, C), lambda r, pos_ref: (pos_ref[1], r, 0))
    oth_blk = pl.BlockSpec((3, tr, C), lambda r, pos_ref: (0, r, 0))
    out_blk = pl.BlockSpec((tr, C), lambda r, pos_ref: (pos_ref[0] * nh + r, 0))
    return _call(body, name=name, out_shape=jax.ShapeDtypeStruct((2 * half, C), F32), grid=(nh,),
                 in_specs=[own_blk, recv_blk, oth_blk], out_specs=out_blk, sem=("parallel",), vmem=40 << 20,
                 prefetch=1)(pos, own, recv, others)


def _adamw(name, w, g, m, v):
    R, C = w.shape
    tr = R if R < SUBLANES or R % SUBLANES else _row_tile(R, C, 8)
    c1 = 1.0 - ADAM_B1 ** ADAM_STEP
    c2 = 1.0 - ADAM_B2 ** ADAM_STEP

    def body(w_ref, g_ref, m_ref, v_ref, d_ref, nm_ref, nv_ref, g_out_ref):
        gv = g_ref[...]
        nm = ADAM_B1 * m_ref[...] + (1.0 - ADAM_B1) * gv
        nv = ADAM_B2 * v_ref[...] + (1.0 - ADAM_B2) * (gv * gv)
        d_ref[...] = -ADAM_LR * ((nm * (1.0 / c1)) / (jnp.sqrt(nv * (1.0 / c2)) + ADAM_EPS) + ADAM_WD * w_ref[...])
        nm_ref[...] = nm
        nv_ref[...] = nv
        g_out_ref[...] = gv

    blk = pl.BlockSpec((tr, C), lambda r: (r, 0))
    out_shape = [jax.ShapeDtypeStruct((R, C), F32)] * 4
    return _call(body, name=name, out_shape=out_shape, grid=(R // tr,), in_specs=[blk] * 4, out_specs=[blk] * 4,
                 sem=("parallel",), vmem=40 << 20)(w, g, m, v)


def _sum_devices(name, gathered):
    _, R, _ = gathered.shape

    def body(g_ref, o_ref):
        acc = g_ref[0]
        for d in range(1, N_DEV):
            acc = acc + g_ref[d]
        o_ref[...] = acc

    return _call(body, name=name, out_shape=jax.ShapeDtypeStruct((R, LANES), F32), grid=(1,),
                 in_specs=[pl.BlockSpec((N_DEV, R, LANES), lambda i: (0, 0, 0))],
                 out_specs=pl.BlockSpec((R, LANES), lambda i: (0, 0)), sem=("arbitrary",), vmem=16 << 20)(gathered)


def _ffn_fwd(tag, h, gain, wg, wu, wd, tm):
    L, D = h.shape
    F = wg.shape[1]
    tn = _pick(F, (512, 256, 128))
    n = _rmsnorm_fwd(f"{tag}_norm", h, gain, tm)

    def gate_up(accs, _):
        a, u = accs
        return a, u, a * _sigmoid_fast(a) * u

    mn = _mn(tm, tn)
    a, u, s = _matmul(f"{tag}_gate_up", [n], [wg, wu], [(0, 0, 0), (0, 1, 1)], 2, gate_up,
                      [((L, F), BF) + mn] * 3, M=L, N=F, K=D, tm=tm, tn=tn, tk=D, n_outer=True)
    if callable(wd):
        wd = wd(s)
    td = _pick(D, (512, 256, 128))
    (h_out,) = _matmul(f"{tag}_down", [s], [wd], [(0, 0, 0)], 1,
                       lambda accs, ex: [ex[0] + FFN_RESIDUAL_WEIGHT * accs[0]], [((L, D), F32) + _mn(tm, td)],
                       M=L, N=D, K=F, tm=tm, tn=td, tk=F, extras=[(h,) + _mn(tm, td)], n_outer=True)
    return h_out, (h, n, a, u, s), wd


def _ffn_bwd(tag, dh, dh_bf, saved, gain, wg, wu, wd, tm, cs_ff, after=None, midway=None, on_weight_grads=None):
    h, n, a, u, s = saved
    L, D = h.shape
    F = wg.shape[1]
    tn = _pick(F, (512, 256, 128))
    tkl = _pick(L, (1408, 384, 256, 128))

    def act_grad(accs, ex):
        ds = FFN_RESIDUAL_WEIGHT * accs[0]
        av, uv = ex[0].astype(F32), ex[1].astype(F32)
        sg = _sigmoid_fast(av)
        return ds * uv * sg * (1.0 + av * (1.0 - sg)), ds * av * sg

    mn = _mn(tm, tn)
    da, du = _matmul(f"{tag}_dact", [dh_bf], [wd], [(0, 0, 0)], 1, act_grad, [((L, F), BF) + mn] * 2, M=L, N=F,
                     K=D, tm=tm, tn=tn, tk=D, tb=True, extras=[(a,) + mn, (u,) + mn], n_outer=True, after=after)
    td = _pick(D, (512, 256, 128))
    (dwd,) = _matmul(f"{tag}_dwd", [s], [dh_bf], [(0, 0, 0)], 1, lambda accs, _: [FFN_RESIDUAL_WEIGHT * accs[0]],
                     [((N_CHIPS, cs_ff, D), F32, (None, cs_ff, td), lambda i, j: (i, 0, j))], M=F, N=D, K=L,
                     tm=cs_ff, tn=td, tk=tkl, ta=True, after=after)
    tmw = _pick(D, (512, 256, 128))
    shard_out = ((N_CHIPS, D, cs_ff), F32, (None, tmw, cs_ff), lambda i, j: (j, i, 0))
    dwg, dwu = _matmul(f"{tag}_dwgu", [n], [da, du], [(0, 0, 0), (0, 1, 1)], 2, lambda accs, _: accs,
                       [shard_out] * 2, M=D, N=F, K=L, tm=tmw, tn=cs_ff, tk=tkl, ta=True,
                       after=midway(dwd) if midway else None)
    started = on_weight_grads(dwg, dwu, dwd) if on_weight_grads else None
    tdn = _pick(D, (256, 128))
    (dn,) = _matmul(f"{tag}_dn", [da, du], [wg, wu], [(0, 0, 0), (1, 1, 0)], 1, lambda accs, _: accs,
                    [((L, D), F32) + _mn(tm, tdn)], M=L, N=D, K=F, tm=tm, tn=tdn, tk=F, tb=True, after=started)
    dh_in, dh_in_bf, dgain = _rmsnorm_bwd(f"{tag}_dnorm", dn, h, gain, dh, tm)
    return dh_in, dh_in_bf, dgain


def kernel(x, meta_tokens, ffn1_norm, ffn1_w_gate, ffn1_w_up, ffn1_w_down, mix_norm, w_in, b_forget, fox_q_norm, fox_k_norm, w_branch_fox, w_branch_sb, w_out, ffn2_norm, ffn2_w_gate, ffn2_w_up, ffn2_w_down, loss_target, m_meta_tokens, m_ffn1_norm, m_ffn1_w_gate, m_ffn1_w_up, m_ffn1_w_down, m_mix_norm, m_w_in, m_b_forget, m_fox_q_norm, m_fox_k_norm, m_w_branch_fox, m_w_branch_sb, m_w_out, m_ffn2_norm, m_ffn2_w_gate, m_ffn2_w_up, m_ffn2_w_down, v_meta_tokens, v_ffn1_norm, v_ffn1_w_gate, v_ffn1_w_up, v_ffn1_w_down, v_mix_norm, v_w_in, v_b_forget, v_fox_q_norm, v_fox_k_norm, v_w_branch_fox, v_w_branch_sb, v_w_out, v_ffn2_norm, v_ffn2_w_gate, v_ffn2_w_up, v_ffn2_w_down):
    weights = dict(meta_tokens=meta_tokens, ffn1_norm=ffn1_norm, ffn1_w_gate=ffn1_w_gate, ffn1_w_up=ffn1_w_up,
                   ffn1_w_down=ffn1_w_down, mix_norm=mix_norm, w_in=w_in, b_forget=b_forget, fox_q_norm=fox_q_norm,
                   fox_k_norm=fox_k_norm, w_branch_fox=w_branch_fox, w_branch_sb=w_branch_sb, w_out=w_out,
                   ffn2_norm=ffn2_norm, ffn2_w_gate=ffn2_w_gate, ffn2_w_up=ffn2_w_up, ffn2_w_down=ffn2_w_down)
    moments_m = dict(meta_tokens=m_meta_tokens, ffn1_norm=m_ffn1_norm, ffn1_w_gate=m_ffn1_w_gate,
                     ffn1_w_up=m_ffn1_w_up, ffn1_w_down=m_ffn1_w_down, mix_norm=m_mix_norm, w_in=m_w_in,
                     b_forget=m_b_forget, fox_q_norm=m_fox_q_norm, fox_k_norm=m_fox_k_norm,
                     w_branch_fox=m_w_branch_fox, w_branch_sb=m_w_branch_sb, w_out=m_w_out, ffn2_norm=m_ffn2_norm,
                     ffn2_w_gate=m_ffn2_w_gate, ffn2_w_up=m_ffn2_w_up, ffn2_w_down=m_ffn2_w_down)
    moments_v = dict(meta_tokens=v_meta_tokens, ffn1_norm=v_ffn1_norm, ffn1_w_gate=v_ffn1_w_gate,
                     ffn1_w_up=v_ffn1_w_up, ffn1_w_down=v_ffn1_w_down, mix_norm=v_mix_norm, w_in=v_w_in,
                     b_forget=v_b_forget, fox_q_norm=v_fox_q_norm, fox_k_norm=v_fox_k_norm,
                     w_branch_fox=v_w_branch_fox, w_branch_sb=v_w_branch_sb, w_out=v_w_out, ffn2_norm=v_ffn2_norm,
                     ffn2_w_gate=v_ffn2_w_gate, ffn2_w_up=v_ffn2_w_up, ffn2_w_down=v_ffn2_w_down)
    names = list(weights)

    _, S, D = x.shape
    NM = meta_tokens.shape[0]
    L_real = NM + S
    L = _round_up(L_real, ATT_TILE)
    nblk = L // ATT_TILE
    nbp = _round_up(nblk, SUBLANES)
    cs_ff = ffn1_w_gate.shape[2]
    F = N_CHIPS * cs_ff
    H = b_forget.shape[1]
    FW = w_branch_fox.shape[1]
    SW = w_branch_sb.shape[1]
    HS = SW // HEAD_DIM
    cs_in = w_in.shape[2]
    W_IN = N_CHIPS * cs_in
    assert FW == H * HEAD_DIM and W_IN == 3 * FW + H + 3 * SW + 2 * D
    cs_d = D // N_CHIPS
    tm = _pick(L, (384, 256, 128))

    x_pos, y_pos, c_pos = _mesh_pos()
    pos = jnp.stack([c_pos, 2 * x_pos + y_pos]).astype(I32)

    shard_of = {
        "ffn1_w_gate": (ffn1_w_gate[0], "col"), "ffn1_w_up": (ffn1_w_up[0], "col"),
        "ffn1_w_down": (ffn1_w_down[0], "maj"), "w_in": (w_in[0], "maj"),
        "w_branch_fox": (w_branch_fox[0], "col"), "w_branch_sb": (w_branch_sb[0], "col"),
        "w_out": (w_out[0], "maj"), "ffn2_w_gate": (ffn2_w_gate[0], "col"), "ffn2_w_up": (ffn2_w_up[0], "col"),
        "ffn2_w_down": (ffn2_w_down[0], "maj"),
    }
    g_names = list(shard_of) + ["meta_tokens"]
    shards = [shard_of[k][0] for k in shard_of] + [meta_tokens]
    kinds = [shard_of[k][1] for k in shard_of] + ["col"]
    dtypes = [BF] * len(shard_of) + [F32]
    info = {k: (s, kind) for k, s, kind in zip(g_names, shards, kinds)}
    placed = {k: _place_shard(f"place_{k}", pos, s, kind, dt)
              for k, s, kind, dt in zip(g_names, shards, kinds, dtypes)}
    groups = [["meta_tokens", "ffn1_w_gate", "ffn1_w_up"], ["ffn1_w_down"], ["w_in"],
              ["w_branch_fox", "w_branch_sb", "w_out", "ffn2_w_gate", "ffn2_w_up", "ffn2_w_down"]]
    in_flight = []
    all_started = jnp.zeros((SUBLANES, LANES), F32)
    for gi, grp in enumerate(groups):
        g_shards, g_kinds = [info[k][0] for k in grp], [info[k][1] for k in grp]
        in_flight.append(_all_gather_ici_start(f"gather_start_{gi}", [placed[k] for k in grp], g_shards, g_kinds,
                                               all_started))
        all_started = in_flight[-1][3]
    full = {}

    def arrive(gi, after):
        grp = groups[gi]
        g_shards, g_kinds = [info[k][0] for k in grp], [info[k][1] for k in grp]
        send_sem, recv_sem, bufs, _ = in_flight[gi]
        bufs = _all_gather_ici_wait(f"gather_wait_{gi}", bufs, send_sem, recv_sem, after, g_shards, g_kinds)
        full.update(zip(grp, _all_gather_d2d(f"gather_d2d_{gi}", bufs, g_shards, g_kinds)))

    target = jnp.concatenate([jnp.zeros((NM, D), F32), loss_target[0], jnp.zeros((L - L_real, D), F32)], axis=0)
    arrive(0, [all_started, target, moments_m["w_in"][0], moments_v["w_in"][0]])
    wg1, wu1 = full["ffn1_w_gate"], full["ffn1_w_up"]
    c_f = 3 * FW
    QKV_S, GATES, FCOL = 3 * FW, 3 * FW + 3 * SW, 3 * FW + 3 * SW + 2 * D
    W_PROJ = FCOL + LANES

    h0 = jnp.concatenate([full["meta_tokens"], x[0], jnp.zeros((L - L_real, D), F32)], axis=0)

    def late_wd1(s):
        arrive(1, s)
        return full["ffn1_w_down"].reshape(F, D)

    h1, saved1, wd1 = _ffn_fwd("ffn1", h0, ffn1_norm, wg1, wu1, late_wd1, tm)

    arrive(2, h1)
    w_in_full = jnp.transpose(full["w_in"], (1, 0, 2)).reshape(D, W_IN)
    w_proj = jnp.concatenate([w_in_full[:, :c_f], w_in_full[:, c_f + H:],
                              jnp.pad(w_in_full[:, c_f:c_f + H], ((0, 0), (0, LANES - H)))], axis=1)
    n2 = _rmsnorm_fwd("mix_norm", h1, mix_norm, tm)
    tp = _pick(FCOL, (512, 256, 128))
    (proj,) = _matmul("in_proj", [n2], [w_proj], [(0, 0, 0)], 1, lambda accs, _: accs,
                      [((L, FCOL), BF) + _mn(tm, tp)], M=L, N=FCOL, K=D, tm=tm, tn=tp, tk=D, n_outer=True)
    (f_logit,) = _matmul("forget_proj", [n2], [w_proj], [(0, 0, 0)], 1, lambda accs, _: accs,
                         [((L, LANES), F32) + _mn(tm, LANES)], M=L, N=LANES, K=D, tm=tm, tn=LANES, tk=D,
                         b_off=FCOL // LANES)
    fl = jnp.pad(jnp.transpose(f_logit[:, :H]).reshape(H, nblk, LANES), ((0, 0), (0, nbp - nblk), (0, 0)))
    bias = jnp.broadcast_to(b_forget[0][:, None, None], (H, 1, LANES))
    c_row = _cum_fwd("forget_cumsum", fl, bias, nblk)
    c_col = c_row[:, :nblk].reshape(H, L, 1)
    gq, gk = fox_q_norm[0][:, None, :], fox_k_norm[0][:, None, :]
    qn, kn = _qknorm_fwd("fox_qk_norm", proj, gq, gk, H, 0, FW, tm)
    o_fox, lse = _fox_fwd("fox_attention", qn, kn, proj, c_row, c_col, H, ATT_HEADS, 0, 0, 2 * FW)
    o_sb = _sb_fwd("sb_attention", proj, HS, min(HS, ATT_HEADS_SB_FWD), QKV_S, QKV_S + SW, QKV_S + 2 * SW)

    arrive(3, o_sb)
    wg2, wu2 = full["ffn2_w_gate"], full["ffn2_w_up"]
    wd2 = full["ffn2_w_down"].reshape(F, D)
    wbf, wbs = full["w_branch_fox"], full["w_branch_sb"]
    wo = full["w_out"].reshape(D, D)
    td = _pick(D, (512, 256, 128))

    def merge(accs, ex):
        bf_, bs_ = accs
        return _sigmoid_fast(ex[0].astype(F32)) * bf_ + _sigmoid_fast(ex[1].astype(F32)) * bs_, bf_, bs_

    merged, br_f, br_s = _matmul("branch_merge", [o_fox, o_sb], [wbf, wbs], [(0, 0, 0), (1, 1, 1)], 2, merge,
                                 [((L, D), BF) + _mn(tm, td)] * 3, M=L, N=D, K=FW, tm=tm, tn=td, tk=FW,
                                 extras=[(proj,) + _mn(tm, td, GATES), (proj,) + _mn(tm, td, GATES + D)],
                                 n_outer=True)
    (h2,) = _matmul("out_proj", [merged], [wo], [(0, 0, 0)], 1, lambda accs, ex: [ex[0] + accs[0]],
                    [((L, D), F32) + _mn(tm, td)], M=L, N=D, K=D, tm=tm, tn=td, tk=D, extras=[(h1,) + _mn(tm, td)],
                    n_outer=True)

    h3, saved2, _ = _ffn_fwd("ffn2", h2, ffn2_norm, wg2, wu2, wd2, tm)
    dh3, dh3_bf, loss_part = _loss_grad("loss", h3, target, NM, S, tm)

    scatters, totals, grads, updates = {}, {}, {}, {}

    def adamw(k):
        w = weights[k]
        shape2 = (1, w.size) if w.size < LANES * SUBLANES else (w.size // w.shape[-1], w.shape[-1])
        res = _adamw(f"adamw_{k}", w.reshape(shape2), grads[k].reshape(shape2), moments_m[k].reshape(shape2),
                     moments_v[k].reshape(shape2))
        updates[k] = [r.reshape(w.shape) for r in res]

    def swap_begin(tag, keys, local):
        send_sem, recv_sem, grads, lands, token = _sibling_start(f"grads_to_sibling_start_{tag}", list(local))
        scatters[tag] = (keys, send_sem, recv_sem, grads, lands)
        return token

    def exchange_begin(tag, after):
        keys, send_sem, recv_sem, grads, lands = scatters[tag]
        local, from_sibling = _sibling_wait(f"grads_to_sibling_wait_{tag}", grads, lands, send_sem, recv_sem, after)
        partials = [_chip_partial(f"chip_sum_{k}", pos, g, r) for k, g, r in zip(keys, local, from_sibling)]
        send_sem, recv_sem, parts, lands, token = _exchange_start(f"grads_to_owner_start_{tag}", partials)
        scatters[tag] = (keys, local, from_sibling, send_sem, recv_sem, parts, lands)
        return token

    def scatter_end(tag, after):
        keys, local, from_sibling, send_sem, recv_sem, parts, lands = scatters[tag]
        from_chips = _exchange_wait(f"grads_to_owner_wait_{tag}", parts, lands, send_sem, recv_sem, after)
        totals.update({k: _final_half(f"total_{k}", pos, g, r, o)
                       for k, g, r, o in zip(keys, local, from_sibling, from_chips)})

    dh2, dh2_bf, dg_ffn2 = _ffn_bwd(
        "ffn2", dh3, dh3_bf, saved2, ffn2_norm, wg2, wu2, wd2, tm, cs_ff,
        on_weight_grads=lambda *dw: swap_begin("ffn2", ["ffn2_w_gate", "ffn2_w_up", "ffn2_w_down"], dw))
    ffn2_exchanging = exchange_begin("ffn2", dh2)

    def gate_grad(accs, ex):
        dm = accs[0]
        gf, gs, bf_, bs_ = [e.astype(F32) for e in ex]
        sf, ss = _sigmoid_fast(gf), _sigmoid_fast(gs)
        return dm * bf_ * sf * (1.0 - sf), dm * bs_ * ss * (1.0 - ss), dm * sf, dm * ss

    mn_d = _mn(tm, td)
    dgf, dgs, dbr_f, dbr_s = _matmul(
        "d_merged", [dh2_bf], [wo], [(0, 0, 0)], 1, gate_grad, [((L, D), BF) + mn_d] * 4, M=L, N=D, K=D, tm=tm,
        tn=td, tk=D, tb=True, extras=[(proj,) + _mn(tm, td, GATES), (proj,) + _mn(tm, td, GATES + D),
                                      (br_f,) + mn_d, (br_s,) + mn_d], n_outer=True, after=ffn2_exchanging)
    tkl = _pick(L, (1408, 384, 256, 128))
    (dwo,) = _matmul("d_w_out", [merged], [dh2_bf], [(0, 0, 0)], 1, lambda accs, _: accs,
                     [((N_CHIPS, cs_d, D), F32, (None, cs_d, td), lambda i, j: (i, 0, j))], M=D, N=D, K=L, tm=cs_d,
                     tn=td, tk=tkl, ta=True)
    tw = _pick(FW, (512, 256, 128))
    do_fox, do_sb = _matmul("d_branch_in", [dbr_f, dbr_s], [wbf, wbs], [(0, 0, 0), (1, 1, 1)], 2,
                            lambda accs, _: accs, [((L, FW), BF) + _mn(tm, tw)] * 2, M=L, N=FW, K=D, tm=tm, tn=tw,
                            tk=D, tb=True)
    tmb = _pick(FW, (1024, 512, 256, 128))
    dwbf, dwbs = _matmul("d_w_branch", [o_fox, o_sb], [dbr_f, dbr_s], [(0, 0, 0), (1, 1, 1)], 2,
                         lambda accs, _: accs,
                         [((N_CHIPS, FW, cs_d), F32, (None, tmb, cs_d), lambda i, j: (j, i, 0))] * 2, M=FW, N=D, K=L,
                         tm=tmb, tn=cs_d, tk=tkl, ta=True)

    dqn, dkn, dfv, dcs, drs = _fox_bwd("fox_attention_bwd", qn, kn, proj, c_row, c_col, o_fox, do_fox, lse, H,
                                       ATT_HEADS, 0, 0, 2 * FW)
    dsq, dsk_t, dsv_t = _sb_bwd("sb_attention_bwd", proj, do_sb, HS, ATT_HEADS, QKV_S, QKV_S + SW,
                                QKV_S + 2 * SW)
    dsk = jnp.transpose(dsk_t, (0, 2, 1)).reshape(L, SW)
    dsv = jnp.transpose(dsv_t, (0, 2, 1)).reshape(L, SW)
    dfq, dfk, dgq, dgk = _qknorm_bwd("fox_qk_norm_bwd", proj, dqn, dkn, gq, gk, H, 0, FW, tm)
    drs_row = jnp.pad(drs.reshape(H, nblk, LANES), ((0, 0), (0, nbp - nblk), (0, 0)))
    dfl, dbias = _cum_bwd("forget_cumsum_bwd", drs_row, dcs, fl, bias, nblk)
    dfl_cols = jnp.pad(jnp.transpose(dfl[:, :nblk].reshape(H, L)), ((0, 0), (0, LANES - H))).astype(BF)
    dproj = jnp.concatenate([dfq, dfk, dfv.astype(BF), dsq, dsk.astype(BF), dsv.astype(BF), dgf, dgs, dfl_cols],
                            axis=1)

    tdn = _pick(D, (256, 128))
    (dn2,) = _matmul("d_mix_norm_in", [dproj], [w_proj], [(0, 0, 0)], 1, lambda accs, _: accs,
                     [((L, D), F32) + _mn(tm, tdn)], M=L, N=D, K=W_PROJ, tm=tm, tn=tdn, tk=W_PROJ, tb=True)
    tmw = _pick(D, (1024, 512, 256, 128))
    tnp = _pick(W_PROJ, (1152, 640, 512, 384, 256, 128))
    (dw_proj,) = _matmul("d_w_in", [n2], [dproj], [(0, 0, 0)], 1, lambda accs, _: accs,
                         [((D, W_PROJ), F32) + _mn(tmw, tnp)], M=D, N=W_PROJ, K=L, tm=tmw, tn=tnp, tk=tkl,
                         ta=True)
    dh1, dh1_bf, dg_mix = _rmsnorm_bwd("mix_dnorm", dn2, h1, mix_norm, dh2, tm)
    dw_in_ref = jnp.concatenate([dw_proj[:, :c_f], dw_proj[:, FCOL:FCOL + H], dw_proj[:, c_f:FCOL]], axis=1)
    dw_in = jnp.transpose(dw_in_ref.reshape(D, N_CHIPS, cs_in), (1, 0, 2))
    scatter_end("ffn2", dh1)
    ffn2_keys = list(totals)
    grads.update(zip(ffn2_keys, _share_halves("grads_to_core_pair_ffn2", [totals.pop(k) for k in ffn2_keys])))
    mix_swapping = swap_begin("mix", ["w_in", "w_branch_fox", "w_branch_sb", "w_out"], [dw_in, dwbf, dwbs, dwo])

    def ffn1_grads(*dw):
        swap_begin("ffn1", ["ffn1_w_gate", "ffn1_w_up", "ffn1_w_down"], dw)
        for k in ffn2_keys:
            adamw(k)
        return exchange_begin("ffn1", [updates[k][0] for k in ffn2_keys])

    dh0, _, dg_ffn1 = _ffn_bwd("ffn1", dh1, dh1_bf, saved1, ffn1_norm, wg1, wu1, wd1, tm, cs_ff, after=mix_swapping,
                               midway=lambda done: exchange_begin("mix", done), on_weight_grads=ffn1_grads)
    grad_x = dh0[NM:L_real][None]
    scatter_end("mix", dh0)
    scatter_end("ffn1", dh0)
    big = list(totals)
    grads.update(zip(big, _share_halves("grads_to_core_pair", [totals.pop(k) for k in big])))

    small = [loss_part[:, :1].reshape(1), dh0[:NM].reshape(-1), dg_ffn1.reshape(-1), dg_mix.reshape(-1),
             dg_ffn2.reshape(-1), dbias[:, 0, 0], dgq.reshape(-1), dgk.reshape(-1)]
    sizes = [s.shape[0] for s in small]
    flat = jnp.concatenate(small)
    rows = _round_up(-(-flat.shape[0] // LANES), SUBLANES)
    packed = jnp.pad(flat, (0, rows * LANES - flat.shape[0])).reshape(rows, LANES)
    total = _sum_devices("sum_small", _gather_small("gather_small", packed)).reshape(-1)
    pieces, off = [], 0
    for n_el in sizes:
        pieces.append(total[off:off + n_el])
        off += n_el
    loss = pieces[0][0]
    d_meta = lax.dynamic_slice_in_dim(pieces[1].reshape(NM, D), pos[1] * cs_d, cs_d, axis=1)
    grads.update(meta_tokens=d_meta, ffn1_norm=pieces[2].reshape(1, D), mix_norm=pieces[3].reshape(1, D),
                 ffn2_norm=pieces[4].reshape(1, D), b_forget=pieces[5].reshape(1, H),
                 fox_q_norm=pieces[6].reshape(1, H, HEAD_DIM), fox_k_norm=pieces[7].reshape(1, H, HEAD_DIM))

    for k in names:
        if k not in updates:
            adamw(k)
    return (loss, grad_x, *[updates[k][i] for i in (3, 0, 1, 2) for k in names])
```

```python
import functools

import jax
import jax.numpy as jnp
from jax import lax
from jax.experimental import pallas as pl
from jax.experimental.pallas import tpu as pltpu

F32 = jnp.float32
BF = jnp.bfloat16
I32 = jnp.int32

HEAD_DIM = 128
RMS_EPS = 1e-6
FFN_RESIDUAL_WEIGHT = 0.5
ADAM_LR = 0.001
ADAM_B1 = 0.9
ADAM_B2 = 0.999
ADAM_EPS = 1e-08
ADAM_WD = 0.01
ADAM_STEP = 10

LANES = 128
SUBLANES = 8
ATT_TILE = 128
ATT_HEADS = 4
ATT_HEADS_SB_FWD = 8
VMEM_CAP = 56 * 1024 * 1024
MESH_IDS = pl.DeviceIdType.MESH
N_CHIPS = 4
N_DEV = 8


def _pick(n, cands):
    for c in cands:
        if c <= n and n % c == 0:
            return c
    raise ValueError(f"no tile for {n} among {cands}")


def _round_up(n, m):
    return (n + m - 1) // m * m


def _tile_bytes(shape, dtype):
    item = jnp.dtype(dtype).itemsize
    dims = [d for d in shape if d is not None]
    if not dims:
        return 4 * LANES * SUBLANES
    last = _round_up(dims[-1], LANES)
    sub = _round_up(dims[-2], SUBLANES * (4 // item)) if len(dims) > 1 else 1
    lead = 1
    for d in dims[:-2]:
        lead *= d
    return lead * sub * last * item


def _vmem_limit(blocks, scratch=(), temps=0):
    need = 2 * sum(_tile_bytes(s, d) for s, d in blocks) + sum(_tile_bytes(s, d) for s, d in scratch) + temps
    return int(min(VMEM_CAP, max(need + (4 << 20), 16 << 20)))


def _call(body, *, name, out_shape, grid=(), in_specs=None, out_specs=None, scratch=(), sem=None, vmem=None,
          aliases=None, prefetch=0):
    params = pltpu.CompilerParams(dimension_semantics=sem, vmem_limit_bytes=vmem)
    if prefetch:
        grid_spec = pltpu.PrefetchScalarGridSpec(num_scalar_prefetch=prefetch, grid=grid, in_specs=in_specs,
                                                 out_specs=out_specs, scratch_shapes=scratch)
        return pl.pallas_call(body, out_shape=out_shape, grid_spec=grid_spec, name=name, compiler_params=params,
                              input_output_aliases=aliases or {})
    return pl.pallas_call(body, out_shape=out_shape, grid=grid, in_specs=in_specs, out_specs=out_specs,
                          scratch_shapes=scratch, name=name, compiler_params=params,
                          input_output_aliases=aliases or {})


def _dot(a, b, ca, cb):
    return lax.dot_general(a, b, (((ca,), (cb,)), ((), ())), preferred_element_type=F32)


def _sigmoid(x):
    return 1.0 / (1.0 + jnp.exp(-x))


def _sigmoid_fast(x):
    return pl.reciprocal(1.0 + jnp.exp(-x), approx=True)


def _log_sigmoid(x):
    return jnp.minimum(x, 0.0) - jnp.log1p(jnp.exp(-jnp.abs(x)))


def _split(x, parts):
    pieces = []
    rem = x
    for p in range(parts):
        piece = rem.astype(BF)
        pieces.append(piece)
        if p + 1 < parts:
            rem = rem - piece.astype(F32)
    return pieces


def _pieces_dot(pieces, ones_bf):
    out = None
    for piece in pieces:
        d = _dot(piece, ones_bf, 1, 0)
        out = d if out is None else out + d
    return out


def _split_dot(x, ones_bf, parts):
    return _pieces_dot(_split(x, parts), ones_bf)


def _matmul(name, a_list, b_list, pairs, n_acc, epi, outs, *, M, N, K, tm, tn, tk, ta=False, tb=False,
            extras=(), n_outer=False, b_off=0, after=None):
    if after is not None:
        user_epi = epi
        extras = list(extras) + [(after, (SUBLANES, LANES), lambda i, j: (0, 0))]
        epi = lambda accs, ex: user_epi(accs, ex[:-1])
    gi, gj, nk = M // tm, N // tn, K // tk
    assert gi * tm == M and gj * tn == N and nk * tk == K, (name, M, N, K, tm, tn, tk)
    n_a, n_b, n_e, n_o = len(a_list), len(b_list), len(extras), len(outs)

    def ij(g0, g1):
        return (g1, g0) if n_outer else (g0, g1)

    def a_map(g0, g1, k):
        i, _ = ij(g0, g1)
        return (k, i) if ta else (i, k)

    def b_map(g0, g1, k):
        _, j = ij(g0, g1)
        return (j + b_off, k) if tb else (k, j + b_off)

    def tile_map(fn):
        return lambda g0, g1, k: fn(*ij(g0, g1))

    a_block = (tk, tm) if ta else (tm, tk)
    b_block = (tn, tk) if tb else (tk, tn)
    in_specs = ([pl.BlockSpec(a_block, a_map)] * n_a + [pl.BlockSpec(b_block, b_map)] * n_b
                + [pl.BlockSpec(bs, tile_map(fn)) for _, bs, fn in extras])
    out_specs = [pl.BlockSpec(bs, tile_map(fn)) for _, _, bs, fn in outs]
    out_shape = [jax.ShapeDtypeStruct(s, d) for s, d, _, _ in outs]
    scratch = [pltpu.VMEM((tm, tn), F32) for _ in range(n_acc)] if nk > 1 else []

    def body(*refs):
        a_refs = refs[:n_a]
        b_refs = refs[n_a:n_a + n_b]
        e_refs = refs[n_a + n_b:n_a + n_b + n_e]
        o_refs = refs[n_a + n_b + n_e:n_a + n_b + n_e + n_o]
        acc_refs = refs[n_a + n_b + n_e + n_o:]

        def products():
            accs = [None] * n_acc
            for ai, bi, ci in pairs:
                a = a_refs[ai][...]
                b = b_refs[bi][...]
                d = _dot(a.astype(BF), b.astype(BF), 0 if ta else 1, 1 if tb else 0)
                accs[ci] = d if accs[ci] is None else accs[ci] + d
            return accs

        def finish(accs):
            res = epi(accs, [e[...] for e in e_refs])
            for o_ref, r in zip(o_refs, res):
                o_ref[...] = r.reshape(o_ref.shape).astype(o_ref.dtype)

        if nk == 1:
            finish(products())
        else:
            k = pl.program_id(2)

            @pl.when(k == 0)
            def _():
                for acc in acc_refs:
                    acc[...] = jnp.zeros_like(acc)

            for acc, d in zip(acc_refs, products()):
                acc[...] += d

            @pl.when(k == nk - 1)
            def _():
                finish([acc[...] for acc in acc_refs])

    blocks = ([(a_block, a.dtype) for a in a_list] + [(b_block, b.dtype) for b in b_list]
              + [(bs, e.dtype) for e, bs, _ in extras] + [(bs, d) for _, d, bs, _ in outs])
    vmem = _vmem_limit(blocks, [((tm, tn), F32)] * (n_acc if nk > 1 else 0), temps=6 * tm * tn * 4)
    grid = (gj, gi, nk) if n_outer else (gi, gj, nk)
    fn = _call(body, name=name, out_shape=out_shape, grid=grid, in_specs=in_specs, out_specs=out_specs,
               scratch=scratch, sem=("parallel", "parallel", "arbitrary"), vmem=vmem)
    return fn(*a_list, *b_list, *[e for e, _, _ in extras])


def _mn(tm, tn, col0=0):
    assert col0 % tn == 0
    off = col0 // tn
    return (tm, tn), (lambda i, j: (i, j + off))


def _rmsnorm_fwd(name, x, gain, tr):
    L, D = x.shape

    def body(x_ref, g_ref, o_ref):
        xv = x_ref[...]
        r = lax.rsqrt(jnp.mean(xv * xv, axis=-1, keepdims=True) + RMS_EPS)
        o_ref[...] = (xv * r * g_ref[...]).astype(BF)

    row = pl.BlockSpec((tr, D), lambda i: (i, 0))
    vec = pl.BlockSpec((1, D), lambda i: (0, 0))
    vmem = _vmem_limit([((tr, D), F32), ((tr, D), BF)], temps=3 * tr * D * 4)
    return _call(body, name=name, out_shape=jax.ShapeDtypeStruct((L, D), BF), grid=(L // tr,), in_specs=[row, vec],
                 out_specs=row, sem=("parallel",), vmem=vmem)(x, gain)


def _rmsnorm_bwd(name, dn, x, gain, dres, tr):
    L, D = x.shape
    steps = L // tr

    def body(dn_ref, x_ref, g_ref, dres_ref, dx_ref, dxb_ref, dg_ref):
        i = pl.program_id(0)
        xv = x_ref[...]
        r = lax.rsqrt(jnp.mean(xv * xv, axis=-1, keepdims=True) + RMS_EPS)
        xhat = xv * r
        dy = dn_ref[...]
        dxhat = dy * g_ref[...]
        dx = dres_ref[...] + r * (dxhat - xhat * jnp.mean(dxhat * xhat, axis=-1, keepdims=True))
        dx_ref[...] = dx
        dxb_ref[...] = dx.astype(BF)

        @pl.when(i == 0)
        def _():
            dg_ref[...] = jnp.zeros_like(dg_ref)

        dg_ref[...] += jnp.sum(dy * xhat, axis=0, keepdims=True)

    row = pl.BlockSpec((tr, D), lambda i: (i, 0))
    vec = pl.BlockSpec((1, D), lambda i: (0, 0))
    vmem = _vmem_limit([((tr, D), F32)] * 4 + [((tr, D), BF)], temps=4 * tr * D * 4)
    out_shape = [jax.ShapeDtypeStruct((L, D), F32), jax.ShapeDtypeStruct((L, D), BF),
                 jax.ShapeDtypeStruct((1, D), F32)]
    return _call(body, name=name, out_shape=out_shape, grid=(steps,), in_specs=[row, row, vec, row],
                 out_specs=[row, row, vec], sem=("arbitrary",), vmem=vmem)(dn, x, gain, dres)


def _loss_grad(name, h, target, n_meta, n_seq, tr):
    L, D = h.shape

    def body(h_ref, t_ref, dh_ref, dhb_ref, loss_ref):
        i = pl.program_id(0)
        rows = i * tr + lax.broadcasted_iota(I32, (tr, 1), 0)
        valid = (rows >= n_meta) & (rows < n_meta + n_seq)
        diff = jnp.where(valid, h_ref[...] - t_ref[...], 0.0)
        dh = diff * (1.0 / D)
        dh_ref[...] = dh
        dhb_ref[...] = dh.astype(BF)

        @pl.when(i == 0)
        def _():
            loss_ref[...] = jnp.zeros_like(loss_ref)

        loss_ref[...] += jnp.sum(diff * diff) * (0.5 / D)

    row = pl.BlockSpec((tr, D), lambda i: (i, 0))
    acc = pl.BlockSpec((1, LANES), lambda i: (0, 0))
    vmem = _vmem_limit([((tr, D), F32)] * 3 + [((tr, D), BF)], temps=3 * tr * D * 4)
    out_shape = [jax.ShapeDtypeStruct((L, D), F32), jax.ShapeDtypeStruct((L, D), BF),
                 jax.ShapeDtypeStruct((1, LANES), F32)]
    return _call(body, name=name, out_shape=out_shape, grid=(L // tr,), in_specs=[row, row],
                 out_specs=[row, row, acc], sem=("arbitrary",), vmem=vmem)(h, target)


def _qknorm_fwd(name, proj, gq, gk, heads, q_col, k_col, tr):
    L = proj.shape[0]

    def body(q_ref, k_ref, gq_ref, gk_ref, qn_ref, kn_ref):
        for x_ref, g_ref, o_ref in ((q_ref, gq_ref, qn_ref), (k_ref, gk_ref, kn_ref)):
            xv = x_ref[...].astype(F32)
            r = lax.rsqrt(jnp.mean(xv * xv, axis=-1, keepdims=True) + RMS_EPS)
            o_ref[...] = (xv * r * g_ref[...]).astype(BF)

    qb, kb = q_col // HEAD_DIM, k_col // HEAD_DIM
    in_specs = [pl.BlockSpec((tr, HEAD_DIM), lambda h, i: (i, qb + h)),
                pl.BlockSpec((tr, HEAD_DIM), lambda h, i: (i, kb + h)),
                pl.BlockSpec((None, 1, HEAD_DIM), lambda h, i: (h, 0, 0)),
                pl.BlockSpec((None, 1, HEAD_DIM), lambda h, i: (h, 0, 0))]
    out = pl.BlockSpec((tr, HEAD_DIM), lambda h, i: (i, h))
    out_shape = [jax.ShapeDtypeStruct((L, heads * HEAD_DIM), BF)] * 2
    return _call(body, name=name, out_shape=out_shape, grid=(heads, L // tr), in_specs=in_specs,
                 out_specs=[out, out], sem=("parallel", "parallel"), vmem=16 << 20)(proj, proj, gq, gk)


def _qknorm_bwd(name, proj, dqn, dkn, gq, gk, heads, q_col, k_col, tr):
    L = proj.shape[0]

    def body(q_ref, k_ref, dqn_ref, dkn_ref, gq_ref, gk_ref, dq_ref, dk_ref, dgq_ref, dgk_ref):
        i = pl.program_id(1)
        for x_ref, dy_ref, g_ref, dx_ref, dg_ref in ((q_ref, dqn_ref, gq_ref, dq_ref, dgq_ref),
                                                     (k_ref, dkn_ref, gk_ref, dk_ref, dgk_ref)):
            xv = x_ref[...].astype(F32)
            r = lax.rsqrt(jnp.mean(xv * xv, axis=-1, keepdims=True) + RMS_EPS)
            xhat = xv * r
            dy = dy_ref[...].astype(F32)
            dxhat = dy * g_ref[...]
            dx_ref[...] = (r * (dxhat - xhat * jnp.mean(dxhat * xhat, axis=-1, keepdims=True))).astype(BF)

            @pl.when(i == 0)
            def _():
                dg_ref[...] = jnp.zeros_like(dg_ref)

            dg_ref[...] += jnp.sum(dy * xhat, axis=0, keepdims=True)

    qb, kb = q_col // HEAD_DIM, k_col // HEAD_DIM
    tile = pl.BlockSpec((tr, HEAD_DIM), lambda h, i: (i, h))
    gain = pl.BlockSpec((None, 1, HEAD_DIM), lambda h, i: (h, 0, 0))
    in_specs = [pl.BlockSpec((tr, HEAD_DIM), lambda h, i: (i, qb + h)),
                pl.BlockSpec((tr, HEAD_DIM), lambda h, i: (i, kb + h)), tile, tile, gain, gain]
    out_shape = [jax.ShapeDtypeStruct((L, heads * HEAD_DIM), BF)] * 2 + [
        jax.ShapeDtypeStruct((heads, 1, HEAD_DIM), F32)] * 2
    return _call(body, name=name, out_shape=out_shape, grid=(heads, L // tr), in_specs=in_specs,
                 out_specs=[tile, tile, gain, gain], sem=("parallel", "arbitrary"),
                 vmem=16 << 20)(proj, proj, dqn, dkn, gq, gk)


def _tri(cmp):
    r = lax.broadcasted_iota(I32, (LANES, LANES), 0)
    c = lax.broadcasted_iota(I32, (LANES, LANES), 1)
    return jnp.where(cmp(r, c), 1.0, 0.0).astype(BF)


def _cum_fwd(name, fl, bias, n_rows):
    H, nbp, _ = fl.shape

    def body(fl_ref, b_ref, c_ref, tot_ref):
        lf = _log_sigmoid(fl_ref[...] + b_ref[...])
        c_ref[...] = _split_dot(lf, _tri(lambda r, c: r <= c), 3)
        tot_ref[...] = _split_dot(lf, jnp.ones((LANES, LANES), BF), 3)

        def step(r, carry):
            c_ref[pl.ds(r, 1), :] = c_ref[pl.ds(r, 1), :] + carry
            return carry + tot_ref[pl.ds(r, 1), :]

        lax.fori_loop(0, n_rows, step, jnp.zeros((1, LANES), F32))

    blk = pl.BlockSpec((None, nbp, LANES), lambda h: (h, 0, 0))
    vec = pl.BlockSpec((None, 1, LANES), lambda h: (h, 0, 0))
    return _call(body, name=name, out_shape=jax.ShapeDtypeStruct((H, nbp, LANES), F32), grid=(H,),
                 in_specs=[blk, vec], out_specs=blk, scratch=[pltpu.VMEM((nbp, LANES), F32)], sem=("parallel",),
                 vmem=16 << 20)(fl, bias)


def _cum_bwd(name, drs, dcs, fl, bias, n_rows):
    H, nbp, _ = fl.shape

    def body(drs_ref, dcs_ref, fl_ref, b_ref, dfl_ref, db_ref, rin_ref, tot_ref):
        dc = drs_ref[...] - dcs_ref[...]
        rin_ref[...] = _split_dot(dc, _tri(lambda r, c: r >= c), 3)
        tot_ref[...] = _split_dot(dc, jnp.ones((LANES, LANES), BF), 3)
        dfl_ref[...] = jnp.zeros_like(dfl_ref)

        def step(t, carry):
            r = n_rows - 1 - t
            x = fl_ref[pl.ds(r, 1), :] + b_ref[...]
            dfl_ref[pl.ds(r, 1), :] = (rin_ref[pl.ds(r, 1), :] + carry) * _sigmoid(-x)
            return carry + tot_ref[pl.ds(r, 1), :]

        lax.fori_loop(0, n_rows, step, jnp.zeros((1, LANES), F32))
        db_ref[...] = jnp.zeros_like(db_ref) + jnp.sum(dfl_ref[...])

    blk = pl.BlockSpec((None, nbp, LANES), lambda h: (h, 0, 0))
    vec = pl.BlockSpec((None, 1, LANES), lambda h: (h, 0, 0))
    out_shape = [jax.ShapeDtypeStruct((H, nbp, LANES), F32), jax.ShapeDtypeStruct((H, 1, LANES), F32)]
    return _call(body, name=name, out_shape=out_shape, grid=(H,), in_specs=[blk, blk, blk, vec],
                 out_specs=[blk, vec], scratch=[pltpu.VMEM((nbp, LANES), F32)] * 2, sem=("parallel",),
                 vmem=16 << 20)(drs, dcs, fl, bias)


def _att_specs(L, G, q_col, k_col, v_col):
    T = ATT_TILE
    W = G * HEAD_DIM
    assert q_col % W == 0 and k_col % W == 0 and v_col % W == 0
    qb, kb, vb = q_col // W, k_col // W, v_col // W
    q_spec = pl.BlockSpec((T, W), lambda h, i: (i, qb + h))
    k_spec = pl.BlockSpec((L, W), lambda h, i: (0, kb + h), pipeline_mode=pl.Buffered(1))
    v_spec = pl.BlockSpec((L, W), lambda h, i: (0, vb + h), pipeline_mode=pl.Buffered(1))
    return q_spec, k_spec, v_spec


def _head_lanes(G):
    return [slice(g * HEAD_DIM, (g + 1) * HEAD_DIM) for g in range(G)]


def _tile_iotas():
    T = ATT_TILE
    return lax.broadcasted_iota(I32, (T, T), 0), lax.broadcasted_iota(I32, (T, T), 1)


def _rows(j):
    return pl.ds(pl.multiple_of(j * ATT_TILE, ATT_TILE), ATT_TILE)


def _fox_fwd(name, q_arr, k_arr, v_arr, c_row, c_col, heads, G, q_col, k_col, v_col):
    L = q_arr.shape[0]
    T = ATT_TILE
    scale = HEAD_DIM ** -0.5
    lanes = _head_lanes(G)

    def body(q_ref, k_ref, v_ref, crow_ref, ccol_ref, o_ref, lse_ref):
        i = pl.program_id(1)
        qs = [q_ref[:, hl] for hl in lanes]
        cts = [jnp.broadcast_to(ccol_ref[g], (T, T)) for g in range(G)]
        row, col = _tile_iotas()

        def tile(j, carry, masked):
            qk = [_dot(qs[g], k_ref[_rows(j), hl], 1, 1) for g, hl in enumerate(lanes)]
            stats = []
            for g in range(G):
                m = carry[g][0]
                s = qk[g] * scale + (cts[g] - crow_ref[g, pl.ds(j, 1), :])
                if masked:
                    s = jnp.where(col <= row, s, -jnp.inf)
                m_new = jnp.maximum(m, jnp.max(s, axis=1, keepdims=True))
                alpha = jnp.exp(m - m_new)
                stats.append((m_new, alpha, jnp.exp(s - m_new).astype(BF)))
            pv = [_dot(stats[g][2], v_ref[_rows(j), hl], 1, 0) for g, hl in enumerate(lanes)]
            sums = [_dot(stats[g][2], ones_bf, 1, 0) for g in range(G)]
            return tuple((stats[g][0], stats[g][1] * carry[g][1] + sums[g], stats[g][1] * carry[g][2] + pv[g])
                         for g in range(G))

        ones_bf = jnp.ones((T, HEAD_DIM), BF)
        init = tuple((jnp.full((T, 1), -1e30, F32), jnp.zeros((T, HEAD_DIM), F32), jnp.zeros((T, HEAD_DIM), F32))
                     for _ in range(G))
        carry = lax.fori_loop(0, i, lambda j, c: tile(j, c, False), init)
        for g, (m, l, acc) in enumerate(tile(i, carry, True)):
            o_ref[:, lanes[g]] = (acc / l).astype(o_ref.dtype)
            lse_ref[g] = m + jnp.log(l[:, :1])

    nbp = c_row.shape[1]
    W = G * HEAD_DIM
    q_spec, k_spec, v_spec = _att_specs(L, G, q_col, k_col, v_col)
    crow_spec = pl.BlockSpec((G, nbp, LANES), lambda h, i: (h, 0, 0))
    col_spec = pl.BlockSpec((G, T, 1), lambda h, i: (h, i, 0))
    o_spec = pl.BlockSpec((T, W), lambda h, i: (i, h))
    out_shape = [jax.ShapeDtypeStruct((L, heads * HEAD_DIM), BF), jax.ShapeDtypeStruct((heads, L, 1), F32)]
    vmem = _vmem_limit([((L, W), BF)] * 2, temps=8 << 20)
    return _call(body, name=name, out_shape=out_shape, grid=(heads // G, L // T),
                 in_specs=[q_spec, k_spec, v_spec, crow_spec, col_spec], out_specs=[o_spec, col_spec],
                 sem=("parallel", "parallel"), vmem=vmem)(q_arr, k_arr, v_arr, c_row, c_col)


def _fox_bwd(name, q_arr, k_arr, v_arr, c_row, c_col, o, do, lse, heads, G, q_col, k_col, v_col):
    L = q_arr.shape[0]
    T = ATT_TILE
    nq = L // T
    scale = HEAD_DIM ** -0.5
    lanes = _head_lanes(G)

    def body(q_ref, k_ref, v_ref, crow_ref, ccol_ref, o_ref, do_ref, lse_ref, dq_ref, dk_ref, dv_ref, dcs_ref,
             drs_ref, dk_acc, dv_acc):
        i = pl.program_id(1)

        @pl.when(i == 0)
        def _():
            dk_acc[...] = jnp.zeros_like(dk_acc)
            dv_acc[...] = jnp.zeros_like(dv_acc)
            dcs_ref[...] = jnp.zeros_like(dcs_ref)

        qs = [q_ref[:, hl] for hl in lanes]
        dos = [do_ref[:, hl] for hl in lanes]
        q_ts = [qs[g].T for g in range(G)]
        do_ts = [dos[g].T for g in range(G)]
        deltas = [jnp.broadcast_to(jnp.sum(dos[g].astype(F32) * o_ref[:, hl].astype(F32), axis=1, keepdims=True),
                                   (T, T)) for g, hl in enumerate(lanes)]
        lses = [jnp.broadcast_to(lse_ref[g], (T, T)) for g in range(G)]
        cts = [jnp.broadcast_to(ccol_ref[g], (T, T)) for g in range(G)]
        row, col = _tile_iotas()

        def tile(j, carry, masked):
            ks = [k_ref[_rows(j), hl] for hl in lanes]
            qk = [_dot(qs[g], ks[g], 1, 1) for g in range(G)]
            dp = [_dot(dos[g], v_ref[_rows(j), hl], 1, 1) for g, hl in enumerate(lanes)]
            pbs, dsbs, row_sums = [], [], []
            for g in range(G):
                s = qk[g] * scale + (cts[g] - crow_ref[g, pl.ds(j, 1), :])
                if masked:
                    s = jnp.where(col <= row, s, -jnp.inf)
                p = jnp.exp(s - lses[g])
                ds = p * (dp[g] - deltas[g])
                dcs_ref[g, pl.ds(j, 1), :] += jnp.sum(ds, axis=0, keepdims=True)
                row_sums.append(carry[g][1] + jnp.sum(ds, axis=1, keepdims=True))
                pbs.append(p.astype(BF))
                dsbs.append((ds * scale).astype(BF))
            for g, hl in enumerate(lanes):
                dk_acc[j, hl, :] += _dot(q_ts[g], dsbs[g], 1, 0)
            for g, hl in enumerate(lanes):
                dv_acc[j, hl, :] += _dot(do_ts[g], pbs[g], 1, 0)
            return tuple((carry[g][0] + _dot(dsbs[g], ks[g], 1, 0), row_sums[g]) for g in range(G))

        init = tuple((jnp.zeros((T, HEAD_DIM), F32), jnp.zeros((T, 1), F32)) for _ in range(G))
        carry = lax.fori_loop(0, i, lambda j, c: tile(j, c, False), init)
        for g, (dq, row_sum) in enumerate(tile(i, carry, True)):
            dq_ref[:, lanes[g]] = dq.astype(dq_ref.dtype)
            drs_ref[g] = row_sum

        @pl.when(i == nq - 1)
        def _():
            for r in range(nq):
                for hl in lanes:
                    dk_ref[r * T:(r + 1) * T, hl] = dk_acc[r, hl, :].T.astype(dk_ref.dtype)
                    dv_ref[r * T:(r + 1) * T, hl] = dv_acc[r, hl, :].T.astype(dv_ref.dtype)

    nbp = c_row.shape[1]
    WG = G * HEAD_DIM
    q_spec, k_spec, v_spec = _att_specs(L, G, q_col, k_col, v_col)
    crow_spec = pl.BlockSpec((G, nbp, LANES), lambda h, i: (h, 0, 0))
    col_spec = pl.BlockSpec((G, T, 1), lambda h, i: (h, i, 0))
    t_spec = pl.BlockSpec((T, WG), lambda h, i: (i, h))
    head_spec = pl.BlockSpec((L, WG), lambda h, i: (0, h), pipeline_mode=pl.Buffered(1))
    W = heads * HEAD_DIM
    out_shape = [jax.ShapeDtypeStruct((L, W), F32)] * 3 + [jax.ShapeDtypeStruct((heads, nbp, LANES), F32),
                                                           jax.ShapeDtypeStruct((heads, L, 1), F32)]
    scratch = [pltpu.VMEM((nq, WG, T), F32)] * 2
    vmem = _vmem_limit([], [((L, WG), BF)] * 2 + [((L, WG), F32)] * 4, temps=8 << 20)
    return _call(body, name=name, out_shape=out_shape, grid=(heads // G, nq),
                 in_specs=[q_spec, k_spec, v_spec, crow_spec, col_spec, t_spec, t_spec, col_spec],
                 out_specs=[t_spec, head_spec, head_spec, crow_spec, col_spec], scratch=scratch,
                 sem=("parallel", "arbitrary"), vmem=vmem)(q_arr, k_arr, v_arr, c_row, c_col, o, do, lse)


def _sb_logits(qk, scale, valid):
    z = qk * scale
    lb = jnp.minimum(z, 0.0) - jnp.log1p(jnp.exp(-jnp.abs(z)))
    lom = lb - z
    if valid is not None:
        lom = jnp.where(valid, lom, 0.0)
    return lb, lom


def _sb_fwd(name, proj, heads, G, q_col, k_col, v_col):
    L = proj.shape[0]
    T = ATT_TILE
    scale = HEAD_DIM ** -0.5
    lanes = _head_lanes(G)

    def body(q_ref, k_ref, v_ref, o_ref):
        i = pl.program_id(1)
        qs = [q_ref[:, hl] for hl in lanes]
        row, col = _tile_iotas()
        later_mat = jnp.where(row > col, 1.0, 0.0).astype(BF)

        def tile(j, carry, masked):
            valid = (col < row) if masked else None
            qk = [_dot(qs[g], k_ref[_rows(j), hl], 1, 1) for g, hl in enumerate(lanes)]
            logits = [_sb_logits(qk[g], scale, valid) for g in range(G)]
            pieces = [_split(lom, 2) for _, lom in logits]
            later = [_pieces_dot(pieces[g], later_mat) for g in range(G)]
            ws = []
            for g in range(G):
                w = jnp.exp(logits[g][0] + later[g] + carry[g][0])
                if masked:
                    w = jnp.where(valid, w, 0.0)
                ws.append(w.astype(BF))
            wv = [_dot(ws[g], v_ref[_rows(j), hl], 1, 0) for g, hl in enumerate(lanes)]
            return tuple((carry[g][0] + jnp.sum(logits[g][1], axis=1, keepdims=True), carry[g][1] + wv[g])
                         for g in range(G))

        init = tuple((jnp.zeros((T, 1), F32), jnp.zeros((T, HEAD_DIM), F32)) for _ in range(G))
        carry = tile(i, init, True)
        carry = lax.fori_loop(0, i, lambda t, c: tile(i - 1 - t, c, False), carry)
        for g, (_, acc) in enumerate(carry):
            o_ref[:, lanes[g]] = acc.astype(o_ref.dtype)

    W = G * HEAD_DIM
    q_spec, k_spec, v_spec = _att_specs(L, G, q_col, k_col, v_col)
    o_spec = pl.BlockSpec((T, W), lambda h, i: (i, h))
    vmem = _vmem_limit([((L, W), BF)] * 2, temps=8 << 20)
    return _call(body, name=name, out_shape=jax.ShapeDtypeStruct((L, heads * HEAD_DIM), BF),
                 grid=(heads // G, L // T), in_specs=[q_spec, k_spec, v_spec], out_specs=o_spec,
                 sem=("parallel", "parallel"), vmem=vmem)(proj, proj, proj)


def _sb_bwd(name, proj, do, heads, G, q_col, k_col, v_col):
    L = proj.shape[0]
    T = ATT_TILE
    nq = L // T
    scale = HEAD_DIM ** -0.5
    lanes = _head_lanes(G)

    def body(q_ref, k_ref, v_ref, do_ref, dq_ref, dk_acc, dv_acc, da_buf, beta_buf):
        i = pl.program_id(1)

        @pl.when(i == 0)
        def _():
            dk_acc[...] = jnp.zeros_like(dk_acc)
            dv_acc[...] = jnp.zeros_like(dv_acc)

        qs = [q_ref[:, hl] for hl in lanes]
        dos = [do_ref[:, hl] for hl in lanes]
        q_ts = [qs[g].T for g in range(G)]
        do_ts = [dos[g].T for g in range(G)]
        row, col = _tile_iotas()
        later_mat = jnp.where(row > col, 1.0, 0.0).astype(BF)
        before_mat = jnp.where(row < col, 1.0, 0.0).astype(BF)

        def pass1(j, runs, masked):
            valid = (col < row) if masked else None
            qk = [_dot(qs[g], k_ref[_rows(j), hl], 1, 1) for g, hl in enumerate(lanes)]
            dw = [_dot(dos[g], v_ref[_rows(j), hl], 1, 1) for g, hl in enumerate(lanes)]
            logits = [_sb_logits(qk[g], scale, valid) for g in range(G)]
            pieces = [_split(lom, 2) for _, lom in logits]
            later = [_pieces_dot(pieces[g], later_mat) for g in range(G)]
            ws = []
            for g in range(G):
                w = jnp.exp(logits[g][0] + later[g] + runs[g])
                if masked:
                    w = jnp.where(valid, w, 0.0)
                da_buf[g * nq + j] = dw[g] * w
                beta_buf[g * nq + j] = jnp.exp(logits[g][0])
                ws.append(w.astype(BF))
            for g, hl in enumerate(lanes):
                dv_acc[j, hl, :] += _dot(do_ts[g], ws[g], 1, 0)
            return tuple(runs[g] + jnp.sum(logits[g][1], axis=1, keepdims=True) for g in range(G))

        runs = pass1(i, tuple(jnp.zeros((T, 1), F32) for _ in range(G)), True)
        lax.fori_loop(0, i, lambda t, c: pass1(i - 1 - t, c, False), runs)

        def pass2(j, carry, masked):
            das = [da_buf[g * nq + j] for g in range(G)]
            pieces = [_split(da, 2) for da in das]
            before = [_pieces_dot(pieces[g], before_mat) for g in range(G)]
            dzbs = []
            for g in range(G):
                beta = beta_buf[g * nq + j]
                dz = das[g] * (1.0 - beta) - (carry[g][0] + before[g]) * beta
                if masked:
                    dz = jnp.where(col < row, dz, 0.0)
                dzbs.append((dz * scale).astype(BF))
            for g, hl in enumerate(lanes):
                dk_acc[j, hl, :] += _dot(q_ts[g], dzbs[g], 1, 0)
            dq = [_dot(dzbs[g], k_ref[_rows(j), hl], 1, 0) for g, hl in enumerate(lanes)]
            return tuple((carry[g][0] + jnp.sum(das[g], axis=1, keepdims=True), carry[g][1] + dq[g])
                         for g in range(G))

        init = tuple((jnp.zeros((T, 1), F32), jnp.zeros((T, HEAD_DIM), F32)) for _ in range(G))
        carry = lax.fori_loop(0, i, lambda j, c: pass2(j, c, False), init)
        for g, (_, dq) in enumerate(pass2(i, carry, True)):
            dq_ref[:, lanes[g]] = dq.astype(dq_ref.dtype)

    WG = G * HEAD_DIM
    q_spec, k_spec, v_spec = _att_specs(L, G, q_col, k_col, v_col)
    t_spec = pl.BlockSpec((T, WG), lambda h, i: (i, h))
    head_spec = pl.BlockSpec((nq, WG, T), lambda h, i: (0, h, 0), pipeline_mode=pl.Buffered(1))
    W = heads * HEAD_DIM
    out_shape = [jax.ShapeDtypeStruct((L, W), BF)] + [jax.ShapeDtypeStruct((nq, W, T), F32)] * 2
    scratch = [pltpu.VMEM((G * nq, T, T), F32)] * 2
    vmem = _vmem_limit([], [((L, WG), BF)] * 2 + [((L, WG), F32)] * 2 + [((G * nq, T, T), F32)] * 2,
                       temps=6 << 20)
    return _call(body, name=name, out_shape=out_shape, grid=(heads // G, nq),
                 in_specs=[q_spec, k_spec, v_spec, t_spec], out_specs=[t_spec, head_spec, head_spec],
                 scratch=scratch, sem=("parallel", "arbitrary"), vmem=vmem)(proj, proj, proj, do)


_ANY = pl.BlockSpec(memory_space=pl.ANY)


def _mesh_pos():
    return lax.axis_index("x"), lax.axis_index("y"), lax.axis_index("c")


def _other_chips(x, y):
    return [(1 - x, y), (x, 1 - y), (1 - x, 1 - y)]


def _shard_window(ref, kind, sidx, r0, nr, cs):
    if kind == "col":
        assert cs % LANES == 0
        return ref.at[pl.ds(r0, nr), pl.ds(pl.multiple_of(sidx * cs, LANES), cs)]
    return ref.at[sidx, pl.ds(r0, nr), :]


def _place_shard(name, pos, shard, kind, dtype):
    R, C = shard.shape
    tr = _row_tile(R, C, 2, 16)

    def body(pos_ref, s_ref, o_ref):
        o_ref[...] = s_ref[...].astype(o_ref.dtype)

    if kind == "col":
        assert C % LANES == 0
        out_shape = jax.ShapeDtypeStruct((R, N_CHIPS * C), dtype)
        out_spec = pl.BlockSpec((tr, C), lambda r, pos_ref: (r, pos_ref[1]))
    else:
        out_shape = jax.ShapeDtypeStruct((N_CHIPS, R, C), dtype)
        out_spec = pl.BlockSpec((None, tr, C), lambda r, pos_ref: (pos_ref[1], r, 0))
    return _call(body, name=name, out_shape=out_shape, grid=(R // tr,),
                 in_specs=[pl.BlockSpec((tr, C), lambda r, pos_ref: (r, 0))], out_specs=out_spec, sem=("parallel",),
                 vmem=32 << 20, prefetch=1)(pos, shard)


_HBM = pl.BlockSpec(memory_space=pltpu.HBM)
_SEM = pl.BlockSpec(memory_space=pltpu.SEMAPHORE)
_KEEP_ORDER = pltpu.SideEffectType.DATAFLOW_SIDE_EFFECTING


def _in_hbm(x):
    return pltpu.with_memory_space_constraint(x, pltpu.HBM)


def _gather_copies(bufs, meta, send_sem, recv_sem):
    x, y, c = _mesh_pos()
    out = []
    for t, (kind, R, cs) in enumerate(meta):
        half = R // 2
        r0 = pl.multiple_of(c * half, SUBLANES)
        mine = _shard_window(bufs[t], kind, 2 * x + y, r0, half, cs)
        for p, (px, py) in enumerate(_other_chips(x, y)):
            k = 3 * t + p
            args = dict(send_sem=send_sem.at[k], recv_sem=recv_sem.at[k], device_id=(px, py, c),
                        device_id_type=MESH_IDS)
            out.append((pltpu.make_async_remote_copy(src_ref=mine, dst_ref=mine, **args),
                        pltpu.make_async_remote_copy(
                            src_ref=mine, dst_ref=_shard_window(bufs[t], kind, 2 * px + py, r0, half, cs), **args)))
    return out


def _all_gather_ici_start(name, gathered, shards, kinds, after):
    n = len(gathered)
    meta = [(kind, s.shape[0], s.shape[1]) for s, kind in zip(shards, kinds)]

    def body(*refs):
        send_sem, recv_sem = refs[n + 1], refs[n + 2]
        bufs = refs[n + 3:2 * n + 3]
        token = refs[2 * n + 3]
        for send, _ in _gather_copies(bufs, meta, send_sem, recv_sem):
            send.start()
        token[...] = jnp.zeros_like(token)

    out_shape = (pltpu.SemaphoreType.DMA((3 * n,)), pltpu.SemaphoreType.DMA((3 * n,)),
                 *[pltpu.HBM(g.shape, g.dtype) for g in gathered], jax.ShapeDtypeStruct((SUBLANES, LANES), F32))
    out_specs = (_SEM, _SEM, *[_HBM] * n, pl.BlockSpec(memory_space=pltpu.VMEM))
    res = pl.pallas_call(body, out_shape=out_shape, in_specs=[_HBM] * n + [_ANY], out_specs=out_specs,
                         input_output_aliases={t: 2 + t for t in range(n)}, name=name,
                         compiler_params=pltpu.CompilerParams(has_side_effects=_KEEP_ORDER))(
        *[_in_hbm(g) for g in gathered], after)
    return res[0], res[1], list(res[2:2 + n]), res[2 + n]


def _all_gather_ici_wait(name, bufs, send_sem, recv_sem, after, shards, kinds):
    n = len(bufs)
    meta = [(kind, s.shape[0], s.shape[1]) for s, kind in zip(shards, kinds)]
    after = list(after) if isinstance(after, (list, tuple)) else [after]

    def body(*refs):
        for send, recv in _gather_copies(refs[:n], meta, refs[n], refs[n + 1]):
            send.wait_send()
            recv.wait_recv()

    out_shape = tuple(pltpu.HBM(b.shape, b.dtype) for b in bufs)
    res = pl.pallas_call(body, out_shape=out_shape, in_specs=[_HBM] * n + [_SEM, _SEM] + [_ANY] * len(after),
                         out_specs=tuple([_HBM] * n), input_output_aliases={t: t for t in range(n)}, name=name,
                         compiler_params=pltpu.CompilerParams(has_side_effects=_KEEP_ORDER))(
        *bufs, send_sem, recv_sem, *after)
    return list(res)


def _all_gather_d2d(name, gathered, shards, kinds):
    n = len(gathered)
    meta = [(kind, s.shape[0], s.shape[1]) for s, kind in zip(shards, kinds)]

    def body(*refs):
        bufs = refs[n:2 * n]
        send_sem, recv_sem = refs[2 * n:]
        x, y, c = _mesh_pos()
        sends, recvs = [], []
        for t, (kind, R, cs) in enumerate(meta):
            half = R // 2
            mine = pl.multiple_of(c * half, SUBLANES)
            theirs = pl.multiple_of((1 - c) * half, SUBLANES)
            for p, (px, py) in enumerate(_other_chips(x, y)):
                k = 3 * t + p
                win = _shard_window(bufs[t], kind, 2 * px + py, mine, half, cs)
                cp = pltpu.make_async_remote_copy(src_ref=win, dst_ref=win, send_sem=send_sem.at[k],
                                                  recv_sem=recv_sem.at[k], device_id=(x, y, 1 - c),
                                                  device_id_type=MESH_IDS)
                cp.start()
                sends.append(cp)
                got = _shard_window(bufs[t], kind, 2 * px + py, theirs, half, cs)
                recvs.append(pltpu.make_async_remote_copy(src_ref=win, dst_ref=got, send_sem=send_sem.at[k],
                                                          recv_sem=recv_sem.at[k], device_id=(x, y, 1 - c),
                                                          device_id_type=MESH_IDS))
        for cp in recvs:
            cp.wait_recv()
        for cp in sends:
            cp.wait_send()

    scratch = [pltpu.SemaphoreType.DMA((3 * n,)), pltpu.SemaphoreType.DMA((3 * n,))]
    out_shape = [jax.ShapeDtypeStruct(g.shape, g.dtype) for g in gathered]
    return _call(body, name=name, out_shape=out_shape, in_specs=[_ANY] * n, out_specs=[_ANY] * n, scratch=scratch,
                 aliases={t: t for t in range(n)})(*gathered)


def _sibling_copies(grads, lands, send_sem, recv_sem):
    x, y, c = _mesh_pos()
    out = []
    for t in range(len(grads)):
        half = lands[t].shape[1]
        theirs = pl.multiple_of((1 - c) * half, SUBLANES)
        out.append(pltpu.make_async_remote_copy(src_ref=grads[t].at[:, pl.ds(theirs, half), :], dst_ref=lands[t],
                                                send_sem=send_sem.at[t], recv_sem=recv_sem.at[t],
                                                device_id=(x, y, 1 - c), device_id_type=MESH_IDS))
    return out


def _sibling_start(name, grads):
    n = len(grads)
    lands = [lax.empty((N_CHIPS, g.shape[1] // 2, g.shape[2]), g.dtype) for g in grads]

    def body(*refs):
        send_sem, recv_sem = refs[2 * n], refs[2 * n + 1]
        srcs, zones = refs[2 * n + 2:3 * n + 2], refs[3 * n + 2:4 * n + 2]
        token = refs[4 * n + 2]
        for cp in _sibling_copies(srcs, zones, send_sem, recv_sem):
            cp.start()
        token[...] = jnp.zeros_like(token)

    out_shape = (pltpu.SemaphoreType.DMA((n,)), pltpu.SemaphoreType.DMA((n,)),
                 *[pltpu.HBM(a.shape, a.dtype) for a in list(grads) + lands],
                 jax.ShapeDtypeStruct((SUBLANES, LANES), F32))
    out_specs = (_SEM, _SEM, *[_HBM] * (2 * n), pl.BlockSpec(memory_space=pltpu.VMEM))
    res = pl.pallas_call(body, out_shape=out_shape, in_specs=[_HBM] * (2 * n), out_specs=out_specs,
                         input_output_aliases={t: 2 + t for t in range(2 * n)}, name=name,
                         compiler_params=pltpu.CompilerParams(has_side_effects=_KEEP_ORDER))(
        *[_in_hbm(a) for a in list(grads) + lands])
    return res[0], res[1], list(res[2:2 + n]), list(res[2 + n:2 + 2 * n]), res[2 + 2 * n]


def _sibling_wait(name, grads, lands, send_sem, recv_sem, after):
    n = len(grads)
    after = list(after) if isinstance(after, (list, tuple)) else [after]

    def body(*refs):
        for cp in _sibling_copies(refs[:n], refs[n:2 * n], refs[2 * n], refs[2 * n + 1]):
            cp.wait_send()
            cp.wait_recv()

    out_shape = tuple(pltpu.HBM(a.shape, a.dtype) for a in list(grads) + list(lands))
    res = pl.pallas_call(body, out_shape=out_shape, in_specs=[_HBM] * (2 * n) + [_SEM, _SEM] + [_ANY] * len(after),
                         out_specs=tuple([_HBM] * (2 * n)), input_output_aliases={t: t for t in range(2 * n)},
                         name=name, compiler_params=pltpu.CompilerParams(has_side_effects=_KEEP_ORDER))(
        *grads, *lands, send_sem, recv_sem, *after)
    return list(res[:n]), list(res[n:])


def _exchange_copies(parts, lands, send_sem, recv_sem):
    x, y, c = _mesh_pos()
    out = []
    for t in range(len(parts)):
        for p, (px, py) in enumerate(_other_chips(x, y)):
            k = 3 * t + p
            out.append(pltpu.make_async_remote_copy(src_ref=parts[t].at[2 * px + py], dst_ref=lands[t].at[p],
                                                    send_sem=send_sem.at[k], recv_sem=recv_sem.at[k],
                                                    device_id=(px, py, c), device_id_type=MESH_IDS))
    return out


def _exchange_start(name, partials):
    n = len(partials)
    lands = [lax.empty((3,) + p.shape[1:], p.dtype) for p in partials]

    def body(*refs):
        send_sem, recv_sem = refs[2 * n], refs[2 * n + 1]
        parts, zones = refs[2 * n + 2:3 * n + 2], refs[3 * n + 2:4 * n + 2]
        token = refs[4 * n + 2]
        for cp in _exchange_copies(parts, zones, send_sem, recv_sem):
            cp.start()
        token[...] = jnp.zeros_like(token)

    out_shape = (pltpu.SemaphoreType.DMA((3 * n,)), pltpu.SemaphoreType.DMA((3 * n,)),
                 *[pltpu.HBM(a.shape, a.dtype) for a in partials + lands],
                 jax.ShapeDtypeStruct((SUBLANES, LANES), F32))
    out_specs = (_SEM, _SEM, *[_HBM] * (2 * n), pl.BlockSpec(memory_space=pltpu.VMEM))
    res = pl.pallas_call(body, out_shape=out_shape, in_specs=[_HBM] * (2 * n), out_specs=out_specs,
                         input_output_aliases={t: 2 + t for t in range(2 * n)}, name=name,
                         compiler_params=pltpu.CompilerParams(has_side_effects=_KEEP_ORDER))(
        *[_in_hbm(a) for a in partials + lands])
    return res[0], res[1], list(res[2:2 + n]), list(res[2 + n:2 + 2 * n]), res[2 + 2 * n]


def _exchange_wait(name, parts, lands, send_sem, recv_sem, after):
    n = len(parts)

    def body(*refs):
        for cp in _exchange_copies(refs[:n], refs[n:2 * n], refs[2 * n], refs[2 * n + 1]):
            cp.wait_send()
            cp.wait_recv()

    out_shape = tuple(pltpu.HBM(a.shape, a.dtype) for a in parts + lands)
    res = pl.pallas_call(body, out_shape=out_shape, in_specs=[_HBM] * (2 * n) + [_SEM, _SEM, _ANY],
                         out_specs=tuple([_HBM] * (2 * n)), input_output_aliases={t: t for t in range(2 * n)},
                         name=name, compiler_params=pltpu.CompilerParams(has_side_effects=_KEEP_ORDER))(
        *parts, *lands, send_sem, recv_sem, after)
    return list(res[n:])


def _share_halves(name, totals):
    n = len(totals)

    def body(*refs):
        bufs = refs[n:2 * n]
        send_sem, recv_sem = refs[2 * n:]
        x, y, c = _mesh_pos()
        sends, recvs = [], []
        for t, g in enumerate(totals):
            half = g.shape[0] // 2
            mine = bufs[t].at[pl.ds(pl.multiple_of(c * half, SUBLANES), half), :]
            theirs = bufs[t].at[pl.ds(pl.multiple_of((1 - c) * half, SUBLANES), half), :]
            cp = pltpu.make_async_remote_copy(src_ref=mine, dst_ref=mine, send_sem=send_sem.at[t],
                                              recv_sem=recv_sem.at[t], device_id=(x, y, 1 - c),
                                              device_id_type=MESH_IDS)
            cp.start()
            sends.append(cp)
            recvs.append(pltpu.make_async_remote_copy(src_ref=mine, dst_ref=theirs, send_sem=send_sem.at[t],
                                                      recv_sem=recv_sem.at[t], device_id=(x, y, 1 - c),
                                                      device_id_type=MESH_IDS))
        for cp in recvs:
            cp.wait_recv()
        for cp in sends:
            cp.wait_send()

    out_shape = [jax.ShapeDtypeStruct(g.shape, g.dtype) for g in totals]
    scratch = [pltpu.SemaphoreType.DMA((n,)), pltpu.SemaphoreType.DMA((n,))]
    return _call(body, name=name, out_shape=out_shape, in_specs=[_ANY] * n, out_specs=[_ANY] * n, scratch=scratch,
                 aliases={t: t for t in range(n)})(*totals)


def _gather_small(name, v):
    def body(v_ref, out_ref, send_sem, recv_sem, local_sem):
        x, y, c = _mesh_pos()
        me = 4 * x + 2 * y + c
        local = pltpu.make_async_copy(v_ref, out_ref.at[me], local_sem)
        local.start()
        sends, recvs = [], []
        for k in range(1, N_DEV):
            px = 1 - x if k & 4 else x
            py = 1 - y if k & 2 else y
            pc = 1 - c if k & 1 else c
            cp = pltpu.make_async_remote_copy(src_ref=v_ref, dst_ref=out_ref.at[me], send_sem=send_sem.at[k],
                                              recv_sem=recv_sem.at[k], device_id=(px, py, pc),
                                              device_id_type=MESH_IDS)
            cp.start()
            sends.append(cp)
            recvs.append(pltpu.make_async_remote_copy(
                src_ref=v_ref, dst_ref=out_ref.at[4 * px + 2 * py + pc], send_sem=send_sem.at[k],
                recv_sem=recv_sem.at[k], device_id=(px, py, pc), device_id_type=MESH_IDS))
        for cp in recvs:
            cp.wait_recv()
        for cp in sends:
            cp.wait_send()
        local.wait()

    scratch = [pltpu.SemaphoreType.DMA((N_DEV,)), pltpu.SemaphoreType.DMA((N_DEV,)), pltpu.SemaphoreType.DMA(())]
    return _call(body, name=name, out_shape=jax.ShapeDtypeStruct((N_DEV,) + v.shape, v.dtype), in_specs=[_ANY],
                 out_specs=_ANY, scratch=scratch)(v)


def _row_tile(rows, cols, n_arrays, mult=SUBLANES):
    budget = (24 << 20) // (2 * n_arrays * _round_up(cols, LANES) * 4)
    for t in (512, 256, 128, 64, 32, 16, 8):
        if t <= max(budget, mult) and rows % t == 0 and t % mult == 0:
            return t
    raise ValueError(f"no row tile for {rows} x {cols}")


def _chip_partial(name, pos, own, recv):
    _, half, C = recv.shape
    tr = _row_tile(half, C, 3, 16)
    nh = half // tr

    def body(pos_ref, own_ref, recv_ref, out_ref):
        out_ref[...] = (own_ref[...] + recv_ref[...]).astype(BF)

    def shard(s, pos_ref):
        return (pos_ref[1] + 1 + s) % N_CHIPS

    blk = pl.BlockSpec((None, tr, C), lambda s, r, pos_ref: (shard(s, pos_ref), r, 0))
    own_blk = pl.BlockSpec((None, tr, C), lambda s, r, pos_ref: (shard(s, pos_ref), pos_ref[0] * nh + r, 0))
    return _call(body, name=name, out_shape=jax.ShapeDtypeStruct(recv.shape, BF), grid=(N_CHIPS - 1, nh),
                 in_specs=[own_blk, blk], out_specs=blk, sem=("parallel", "parallel"), vmem=40 << 20,
                 prefetch=1)(pos, own, recv)


def _final_half(name, pos, own, recv, others):
    _, half, C = recv.shape
    tr = _row_tile(half, C, 4, 16)
    nh = half // tr

    def body(pos_ref, own_ref, recv_ref, oth_ref, out_ref):
        acc = own_ref[...] + recv_ref[...]
        for p in range(3):
            acc = acc + oth_ref[p].astype(F32)
        out_ref[...] = acc

    own_blk = pl.BlockSpec((None, tr, C), lambda r, pos_ref: (pos_ref[1], pos_ref[0] * nh + r, 0))
    recv_blk = pl.BlockSpec((None, tr, C), lambda r, pos_ref: (pos_ref[1], r, 0))
    oth_blk = pl.BlockSpec((3, tr, C), lambda r, pos_ref: (0, r, 0))
    out_blk = pl.BlockSpec((tr, C), lambda r, pos_ref: (pos_ref[0] * nh + r, 0))
    return _call(body, name=name, out_shape=jax.ShapeDtypeStruct((2 * half, C), F32), grid=(nh,),
                 in_specs=[own_blk, recv_blk, oth_blk], out_specs=out_blk, sem=("parallel",), vmem=40 << 20,
                 prefetch=1)(pos, own, recv, others)


def _adamw(name, w, g, m, v):
    R, C = w.shape
    tr = R if R < SUBLANES or R % SUBLANES else _row_tile(R, C, 8)
    c1 = 1.0 - ADAM_B1 ** ADAM_STEP
    c2 = 1.0 - ADAM_B2 ** ADAM_STEP

    def body(w_ref, g_ref, m_ref, v_ref, d_ref, nm_ref, nv_ref, g_out_ref):
        gv = g_ref[...]
        nm = ADAM_B1 * m_ref[...] + (1.0 - ADAM_B1) * gv
        nv = ADAM_B2 * v_ref[...] + (1.0 - ADAM_B2) * (gv * gv)
        d_ref[...] = -ADAM_LR * ((nm * (1.0 / c1)) / (jnp.sqrt(nv * (1.0 / c2)) + ADAM_EPS) + ADAM_WD * w_ref[...])
        nm_ref[...] = nm
        nv_ref[...] = nv
        g_out_ref[...] = gv

    blk = pl.BlockSpec((tr, C), lambda r: (r, 0))
    out_shape = [jax.ShapeDtypeStruct((R, C), F32)] * 4
    return _call(body, name=name, out_shape=out_shape, grid=(R // tr,), in_specs=[blk] * 4, out_specs=[blk] * 4,
                 sem=("parallel",), vmem=40 << 20)(w, g, m, v)


def _sum_devices(name, gathered):
    _, R, _ = gathered.shape

    def body(g_ref, o_ref):
        acc = g_ref[0]
        for d in range(1, N_DEV):
            acc = acc + g_ref[d]
        o_ref[...] = acc

    return _call(body, name=name, out_shape=jax.ShapeDtypeStruct((R, LANES), F32), grid=(1,),
                 in_specs=[pl.BlockSpec((N_DEV, R, LANES), lambda i: (0, 0, 0))],
                 out_specs=pl.BlockSpec((R, LANES), lambda i: (0, 0)), sem=("arbitrary",), vmem=16 << 20)(gathered)


def _ffn_fwd(tag, h, gain, wg, wu, wd, tm):
    L, D = h.shape
    F = wg.shape[1]
    tn = _pick(F, (512, 256, 128))
    n = _rmsnorm_fwd(f"{tag}_norm", h, gain, tm)

    def gate_up(accs, _):
        a, u = accs
        return a, u, a * _sigmoid_fast(a) * u

    mn = _mn(tm, tn)
    a, u, s = _matmul(f"{tag}_gate_up", [n], [wg, wu], [(0, 0, 0), (0, 1, 1)], 2, gate_up,
                      [((L, F), BF) + mn] * 3, M=L, N=F, K=D, tm=tm, tn=tn, tk=D, n_outer=True)
    if callable(wd):
        wd = wd(s)
    td = _pick(D, (512, 256, 128))
    (h_out,) = _matmul(f"{tag}_down", [s], [wd], [(0, 0, 0)], 1,
                       lambda accs, ex: [ex[0] + FFN_RESIDUAL_WEIGHT * accs[0]], [((L, D), F32) + _mn(tm, td)],
                       M=L, N=D, K=F, tm=tm, tn=td, tk=F, extras=[(h,) + _mn(tm, td)], n_outer=True)
    return h_out, (h, n, a, u, s), wd


def _ffn_bwd(tag, dh, dh_bf, saved, gain, wg, wu, wd, tm, cs_ff, after=None, midway=None, on_weight_grads=None):
    h, n, a, u, s = saved
    L, D = h.shape
    F = wg.shape[1]
    tn = _pick(F, (512, 256, 128))
    tkl = _pick(L, (1408, 384, 256, 128))

    def act_grad(accs, ex):
        ds = FFN_RESIDUAL_WEIGHT * accs[0]
        av, uv = ex[0].astype(F32), ex[1].astype(F32)
        sg = _sigmoid_fast(av)
        return ds * uv * sg * (1.0 + av * (1.0 - sg)), ds * av * sg

    mn = _mn(tm, tn)
    da, du = _matmul(f"{tag}_dact", [dh_bf], [wd], [(0, 0, 0)], 1, act_grad, [((L, F), BF) + mn] * 2, M=L, N=F,
                     K=D, tm=tm, tn=tn, tk=D, tb=True, extras=[(a,) + mn, (u,) + mn], n_outer=True, after=after)
    td = _pick(D, (512, 256, 128))
    (dwd,) = _matmul(f"{tag}_dwd", [s], [dh_bf], [(0, 0, 0)], 1, lambda accs, _: [FFN_RESIDUAL_WEIGHT * accs[0]],
                     [((N_CHIPS, cs_ff, D), F32, (None, cs_ff, td), lambda i, j: (i, 0, j))], M=F, N=D, K=L,
                     tm=cs_ff, tn=td, tk=tkl, ta=True, after=after)
    tmw = _pick(D, (512, 256, 128))
    shard_out = ((N_CHIPS, D, cs_ff), F32, (None, tmw, cs_ff), lambda i, j: (j, i, 0))
    dwg, dwu = _matmul(f"{tag}_dwgu", [n], [da, du], [(0, 0, 0), (0, 1, 1)], 2, lambda accs, _: accs,
                       [shard_out] * 2, M=D, N=F, K=L, tm=tmw, tn=cs_ff, tk=tkl, ta=True,
                       after=midway(dwd) if midway else None)
    started = on_weight_grads(dwg, dwu, dwd) if on_weight_grads else None
    tdn = _pick(D, (256, 128))
    (dn,) = _matmul(f"{tag}_dn", [da, du], [wg, wu], [(0, 0, 0), (1, 1, 0)], 1, lambda accs, _: accs,
                    [((L, D), F32) + _mn(tm, tdn)], M=L, N=D, K=F, tm=tm, tn=tdn, tk=F, tb=True, after=started)
    dh_in, dh_in_bf, dgain = _rmsnorm_bwd(f"{tag}_dnorm", dn, h, gain, dh, tm)
    return dh_in, dh_in_bf, dgain


def kernel(x, meta_tokens, ffn1_norm, ffn1_w_gate, ffn1_w_up, ffn1_w_down, mix_norm, w_in, b_forget, fox_q_norm, fox_k_norm, w_branch_fox, w_branch_sb, w_out, ffn2_norm, ffn2_w_gate, ffn2_w_up, ffn2_w_down, loss_target, m_meta_tokens, m_ffn1_norm, m_ffn1_w_gate, m_ffn1_w_up, m_ffn1_w_down, m_mix_norm, m_w_in, m_b_forget, m_fox_q_norm, m_fox_k_norm, m_w_branch_fox, m_w_branch_sb, m_w_out, m_ffn2_norm, m_ffn2_w_gate, m_ffn2_w_up, m_ffn2_w_down, v_meta_tokens, v_ffn1_norm, v_ffn1_w_gate, v_ffn1_w_up, v_ffn1_w_down, v_mix_norm, v_w_in, v_b_forget, v_fox_q_norm, v_fox_k_norm, v_w_branch_fox, v_w_branch_sb, v_w_out, v_ffn2_norm, v_ffn2_w_gate, v_ffn2_w_up, v_ffn2_w_down):
    weights = dict(meta_tokens=meta_tokens, ffn1_norm=ffn1_norm, ffn1_w_gate=ffn1_w_gate, ffn1_w_up=ffn1_w_up,
                   ffn1_w_down=ffn1_w_down, mix_norm=mix_norm, w_in=w_in, b_forget=b_forget, fox_q_norm=fox_q_norm,
                   fox_k_norm=fox_k_norm, w_branch_fox=w_branch_fox, w_branch_sb=w_branch_sb, w_out=w_out,
                   ffn2_norm=ffn2_norm, ffn2_w_gate=ffn2_w_gate, ffn2_w_up=ffn2_w_up, ffn2_w_down=ffn2_w_down)
    moments_m = dict(meta_tokens=m_meta_tokens, ffn1_norm=m_ffn1_norm, ffn1_w_gate=m_ffn1_w_gate,
                     ffn1_w_up=m_ffn1_w_up, ffn1_w_down=m_ffn1_w_down, mix_norm=m_mix_norm, w_in=m_w_in,
                     b_forget=m_b_forget, fox_q_norm=m_fox_q_norm, fox_k_norm=m_fox_k_norm,
                     w_branch_fox=m_w_branch_fox, w_branch_sb=m_w_branch_sb, w_out=m_w_out, ffn2_norm=m_ffn2_norm,
                     ffn2_w_gate=m_ffn2_w_gate, ffn2_w_up=m_ffn2_w_up, ffn2_w_down=m_ffn2_w_down)
    moments_v = dict(meta_tokens=v_meta_tokens, ffn1_norm=v_ffn1_norm, ffn1_w_gate=v_ffn1_w_gate,
                     ffn1_w_up=v_ffn1_w_up, ffn1_w_down=v_ffn1_w_down, mix_norm=v_mix_norm, w_in=v_w_in,
                     b_forget=v_b_forget, fox_q_norm=v_fox_q_norm, fox_k_norm=v_fox_k_norm,
                     w_branch_fox=v_w_branch_fox, w_branch_sb=v_w_branch_sb, w_out=v_w_out, ffn2_norm=v_ffn2_norm,
                     ffn2_w_gate=v_ffn2_w_gate, ffn2_w_up=v_ffn2_w_up, ffn2_w_down=v_ffn2_w_down)
    names = list(weights)

    _, S, D = x.shape
    NM = meta_tokens.shape[0]
    L_real = NM + S
    L = _round_up(L_real, ATT_TILE)
    nblk = L // ATT_TILE
    nbp = _round_up(nblk, SUBLANES)
    cs_ff = ffn1_w_gate.shape[2]
    F = N_CHIPS * cs_ff
    H = b_forget.shape[1]
    FW = w_branch_fox.shape[1]
    SW = w_branch_sb.shape[1]
    HS = SW // HEAD_DIM
    cs_in = w_in.shape[2]
    W_IN = N_CHIPS * cs_in
    assert FW == H * HEAD_DIM and W_IN == 3 * FW + H + 3 * SW + 2 * D
    cs_d = D // N_CHIPS
    tm = _pick(L, (384, 256, 128))

    x_pos, y_pos, c_pos = _mesh_pos()
    pos = jnp.stack([c_pos, 2 * x_pos + y_pos]).astype(I32)

    shard_of = {
        "ffn1_w_gate": (ffn1_w_gate[0], "col"), "ffn1_w_up": (ffn1_w_up[0], "col"),
        "ffn1_w_down": (ffn1_w_down[0], "maj"), "w_in": (w_in[0], "maj"),
        "w_branch_fox": (w_branch_fox[0], "col"), "w_branch_sb": (w_branch_sb[0], "col"),
        "w_out": (w_out[0], "maj"), "ffn2_w_gate": (ffn2_w_gate[0], "col"), "ffn2_w_up": (ffn2_w_up[0], "col"),
        "ffn2_w_down": (ffn2_w_down[0], "maj"),
    }
    g_names = list(shard_of) + ["meta_tokens"]
    shards = [shard_of[k][0] for k in shard_of] + [meta_tokens]
    kinds = [shard_of[k][1] for k in shard_of] + ["col"]
    dtypes = [BF] * len(shard_of) + [F32]
    info = {k: (s, kind) for k, s, kind in zip(g_names, shards, kinds)}
    placed = {k: _place_shard(f"place_{k}", pos, s, kind, dt)
              for k, s, kind, dt in zip(g_names, shards, kinds, dtypes)}
    groups = [["meta_tokens", "ffn1_w_gate", "ffn1_w_up"], ["ffn1_w_down"], ["w_in"],
              ["w_branch_fox", "w_branch_sb", "w_out", "ffn2_w_gate", "ffn2_w_up", "ffn2_w_down"]]
    in_flight = []
    all_started = jnp.zeros((SUBLANES, LANES), F32)
    for gi, grp in enumerate(groups):
        g_shards, g_kinds = [info[k][0] for k in grp], [info[k][1] for k in grp]
        in_flight.append(_all_gather_ici_start(f"gather_start_{gi}", [placed[k] for k in grp], g_shards, g_kinds,
                                               all_started))
        all_started = in_flight[-1][3]
    full = {}

    def arrive(gi, after):
        grp = groups[gi]
        g_shards, g_kinds = [info[k][0] for k in grp], [info[k][1] for k in grp]
        send_sem, recv_sem, bufs, _ = in_flight[gi]
        bufs = _all_gather_ici_wait(f"gather_wait_{gi}", bufs, send_sem, recv_sem, after, g_shards, g_kinds)
        full.update(zip(grp, _all_gather_d2d(f"gather_d2d_{gi}", bufs, g_shards, g_kinds)))

    target = jnp.concatenate([jnp.zeros((NM, D), F32), loss_target[0], jnp.zeros((L - L_real, D), F32)], axis=0)
    arrive(0, [all_started, target, moments_m["w_in"][0], moments_v["w_in"][0]])
    wg1, wu1 = full["ffn1_w_gate"], full["ffn1_w_up"]
    c_f = 3 * FW
    QKV_S, GATES, FCOL = 3 * FW, 3 * FW + 3 * SW, 3 * FW + 3 * SW + 2 * D
    W_PROJ = FCOL + LANES

    h0 = jnp.concatenate([full["meta_tokens"], x[0], jnp.zeros((L - L_real, D), F32)], axis=0)

    def late_wd1(s):
        arrive(1, s)
        return full["ffn1_w_down"].reshape(F, D)

    h1, saved1, wd1 = _ffn_fwd("ffn1", h0, ffn1_norm, wg1, wu1, late_wd1, tm)

    arrive(2, h1)
    w_in_full = jnp.transpose(full["w_in"], (1, 0, 2)).reshape(D, W_IN)
    w_proj = jnp.concatenate([w_in_full[:, :c_f], w_in_full[:, c_f + H:],
                              jnp.pad(w_in_full[:, c_f:c_f + H], ((0, 0), (0, LANES - H)))], axis=1)
    n2 = _rmsnorm_fwd("mix_norm", h1, mix_norm, tm)
    tp = _pick(FCOL, (512, 256, 128))
    (proj,) = _matmul("in_proj", [n2], [w_proj], [(0, 0, 0)], 1, lambda accs, _: accs,
                      [((L, FCOL), BF) + _mn(tm, tp)], M=L, N=FCOL, K=D, tm=tm, tn=tp, tk=D, n_outer=True)
    (f_logit,) = _matmul("forget_proj", [n2], [w_proj], [(0, 0, 0)], 1, lambda accs, _: accs,
                         [((L, LANES), F32) + _mn(tm, LANES)], M=L, N=LANES, K=D, tm=tm, tn=LANES, tk=D,
                         b_off=FCOL // LANES)
    fl = jnp.pad(jnp.transpose(f_logit[:, :H]).reshape(H, nblk, LANES), ((0, 0), (0, nbp - nblk), (0, 0)))
    bias = jnp.broadcast_to(b_forget[0][:, None, None], (H, 1, LANES))
    c_row = _cum_fwd("forget_cumsum", fl, bias, nblk)
    c_col = c_row[:, :nblk].reshape(H, L, 1)
    gq, gk = fox_q_norm[0][:, None, :], fox_k_norm[0][:, None, :]
    qn, kn = _qknorm_fwd("fox_qk_norm", proj, gq, gk, H, 0, FW, tm)
    o_fox, lse = _fox_fwd("fox_attention", qn, kn, proj, c_row, c_col, H, ATT_HEADS, 0, 0, 2 * FW)
    o_sb = _sb_fwd("sb_attention", proj, HS, min(HS, ATT_HEADS_SB_FWD), QKV_S, QKV_S + SW, QKV_S + 2 * SW)

    arrive(3, o_sb)
    wg2, wu2 = full["ffn2_w_gate"], full["ffn2_w_up"]
    wd2 = full["ffn2_w_down"].reshape(F, D)
    wbf, wbs = full["w_branch_fox"], full["w_branch_sb"]
    wo = full["w_out"].reshape(D, D)
    td = _pick(D, (512, 256, 128))

    def merge(accs, ex):
        bf_, bs_ = accs
        return _sigmoid_fast(ex[0].astype(F32)) * bf_ + _sigmoid_fast(ex[1].astype(F32)) * bs_, bf_, bs_

    merged, br_f, br_s = _matmul("branch_merge", [o_fox, o_sb], [wbf, wbs], [(0, 0, 0), (1, 1, 1)], 2, merge,
                                 [((L, D), BF) + _mn(tm, td)] * 3, M=L, N=D, K=FW, tm=tm, tn=td, tk=FW,
                                 extras=[(proj,) + _mn(tm, td, GATES), (proj,) + _mn(tm, td, GATES + D)],
                                 n_outer=True)
    (h2,) = _matmul("out_proj", [merged], [wo], [(0, 0, 0)], 1, lambda accs, ex: [ex[0] + accs[0]],
                    [((L, D), F32) + _mn(tm, td)], M=L, N=D, K=D, tm=tm, tn=td, tk=D, extras=[(h1,) + _mn(tm, td)],
                    n_outer=True)

    h3, saved2, _ = _ffn_fwd("ffn2", h2, ffn2_norm, wg2, wu2, wd2, tm)
    dh3, dh3_bf, loss_part = _loss_grad("loss", h3, target, NM, S, tm)

    scatters, totals, grads, updates = {}, {}, {}, {}

    def adamw(k):
        w = weights[k]
        shape2 = (1, w.size) if w.size < LANES * SUBLANES else (w.size // w.shape[-1], w.shape[-1])
        res = _adamw(f"adamw_{k}", w.reshape(shape2), grads[k].reshape(shape2), moments_m[k].reshape(shape2),
                     moments_v[k].reshape(shape2))
        updates[k] = [r.reshape(w.shape) for r in res]

    def swap_begin(tag, keys, local):
        send_sem, recv_sem, grads, lands, token = _sibling_start(f"grads_to_sibling_start_{tag}", list(local))
        scatters[tag] = (keys, send_sem, recv_sem, grads, lands)
        return token

    def exchange_begin(tag, after):
        keys, send_sem, recv_sem, grads, lands = scatters[tag]
        local, from_sibling = _sibling_wait(f"grads_to_sibling_wait_{tag}", grads, lands, send_sem, recv_sem, after)
        partials = [_chip_partial(f"chip_sum_{k}", pos, g, r) for k, g, r in zip(keys, local, from_sibling)]
        send_sem, recv_sem, parts, lands, token = _exchange_start(f"grads_to_owner_start_{tag}", partials)
        scatters[tag] = (keys, local, from_sibling, send_sem, recv_sem, parts, lands)
        return token

    def scatter_end(tag, after):
        keys, local, from_sibling, send_sem, recv_sem, parts, lands = scatters[tag]
        from_chips = _exchange_wait(f"grads_to_owner_wait_{tag}", parts, lands, send_sem, recv_sem, after)
        totals.update({k: _final_half(f"total_{k}", pos, g, r, o)
                       for k, g, r, o in zip(keys, local, from_sibling, from_chips)})

    dh2, dh2_bf, dg_ffn2 = _ffn_bwd(
        "ffn2", dh3, dh3_bf, saved2, ffn2_norm, wg2, wu2, wd2, tm, cs_ff,
        on_weight_grads=lambda *dw: swap_begin("ffn2", ["ffn2_w_gate", "ffn2_w_up", "ffn2_w_down"], dw))
    ffn2_exchanging = exchange_begin("ffn2", dh2)

    def gate_grad(accs, ex):
        dm = accs[0]
        gf, gs, bf_, bs_ = [e.astype(F32) for e in ex]
        sf, ss = _sigmoid_fast(gf), _sigmoid_fast(gs)
        return dm * bf_ * sf * (1.0 - sf), dm * bs_ * ss * (1.0 - ss), dm * sf, dm * ss

    mn_d = _mn(tm, td)
    dgf, dgs, dbr_f, dbr_s = _matmul(
        "d_merged", [dh2_bf], [wo], [(0, 0, 0)], 1, gate_grad, [((L, D), BF) + mn_d] * 4, M=L, N=D, K=D, tm=tm,
        tn=td, tk=D, tb=True, extras=[(proj,) + _mn(tm, td, GATES), (proj,) + _mn(tm, td, GATES + D),
                                      (br_f,) + mn_d, (br_s,) + mn_d], n_outer=True, after=ffn2_exchanging)
    tkl = _pick(L, (1408, 384, 256, 128))
    (dwo,) = _matmul("d_w_out", [merged], [dh2_bf], [(0, 0, 0)], 1, lambda accs, _: accs,
                     [((N_CHIPS, cs_d, D), F32, (None, cs_d, td), lambda i, j: (i, 0, j))], M=D, N=D, K=L, tm=cs_d,
                     tn=td, tk=tkl, ta=True)
    tw = _pick(FW, (512, 256, 128))
    do_fox, do_sb = _matmul("d_branch_in", [dbr_f, dbr_s], [wbf, wbs], [(0, 0, 0), (1, 1, 1)], 2,
                            lambda accs, _: accs, [((L, FW), BF) + _mn(tm, tw)] * 2, M=L, N=FW, K=D, tm=tm, tn=tw,
                            tk=D, tb=True)
    tmb = _pick(FW, (1024, 512, 256, 128))
    dwbf, dwbs = _matmul("d_w_branch", [o_fox, o_sb], [dbr_f, dbr_s], [(0, 0, 0), (1, 1, 1)], 2,
                         lambda accs, _: accs,
                         [((N_CHIPS, FW, cs_d), F32, (None, tmb, cs_d), lambda i, j: (j, i, 0))] * 2, M=FW, N=D, K=L,
                         tm=tmb, tn=cs_d, tk=tkl, ta=True)

    dqn, dkn, dfv, dcs, drs = _fox_bwd("fox_attention_bwd", qn, kn, proj, c_row, c_col, o_fox, do_fox, lse, H,
                                       ATT_HEADS, 0, 0, 2 * FW)
    dsq, dsk_t, dsv_t = _sb_bwd("sb_attention_bwd", proj, do_sb, HS, ATT_HEADS, QKV_S, QKV_S + SW,
                                QKV_S + 2 * SW)
    dsk = jnp.transpose(dsk_t, (0, 2, 1)).reshape(L, SW)
    dsv = jnp.transpose(dsv_t, (0, 2, 1)).reshape(L, SW)
    dfq, dfk, dgq, dgk = _qknorm_bwd("fox_qk_norm_bwd", proj, dqn, dkn, gq, gk, H, 0, FW, tm)
    drs_row = jnp.pad(drs.reshape(H, nblk, LANES), ((0, 0), (0, nbp - nblk), (0, 0)))
    dfl, dbias = _cum_bwd("forget_cumsum_bwd", drs_row, dcs, fl, bias, nblk)
    dfl_cols = jnp.pad(jnp.transpose(dfl[:, :nblk].reshape(H, L)), ((0, 0), (0, LANES - H))).astype(BF)
    dproj = jnp.concatenate([dfq, dfk, dfv.astype(BF), dsq, dsk.astype(BF), dsv.astype(BF), dgf, dgs, dfl_cols],
                            axis=1)

    tdn = _pick(D, (256, 128))
    (dn2,) = _matmul("d_mix_norm_in", [dproj], [w_proj], [(0, 0, 0)], 1, lambda accs, _: accs,
                     [((L, D), F32) + _mn(tm, tdn)], M=L, N=D, K=W_PROJ, tm=tm, tn=tdn, tk=W_PROJ, tb=True)
    tmw = _pick(D, (1024, 512, 256, 128))
    tnp = _pick(W_PROJ, (1152, 640, 512, 384, 256, 128))
    (dw_proj,) = _matmul("d_w_in", [n2], [dproj], [(0, 0, 0)], 1, lambda accs, _: accs,
                         [((D, W_PROJ), F32) + _mn(tmw, tnp)], M=D, N=W_PROJ, K=L, tm=tmw, tn=tnp, tk=tkl,
                         ta=True)
    dh1, dh1_bf, dg_mix = _rmsnorm_bwd("mix_dnorm", dn2, h1, mix_norm, dh2, tm)
    dw_in_ref = jnp.concatenate([dw_proj[:, :c_f], dw_proj[:, FCOL:FCOL + H], dw_proj[:, c_f:FCOL]], axis=1)
    dw_in = jnp.transpose(dw_in_ref.reshape(D, N_CHIPS, cs_in), (1, 0, 2))
    scatter_end("ffn2", dh1)
    ffn2_keys = list(totals)
    grads.update(zip(ffn2_keys, _share_halves("grads_to_core_pair_ffn2", [totals.pop(k) for k in ffn2_keys])))
    mix_swapping = swap_begin("mix", ["w_in", "w_branch_fox", "w_branch_sb", "w_out"], [dw_in, dwbf, dwbs, dwo])

    def ffn1_grads(*dw):
        swap_begin("ffn1", ["ffn1_w_gate", "ffn1_w_up", "ffn1_w_down"], dw)
        for k in ffn2_keys:
            adamw(k)
        return exchange_begin("ffn1", [updates[k][0] for k in ffn2_keys])

    dh0, _, dg_ffn1 = _ffn_bwd("ffn1", dh1, dh1_bf, saved1, ffn1_norm, wg1, wu1, wd1, tm, cs_ff, after=mix_swapping,
                               midway=lambda done: exchange_begin("mix", done), on_weight_grads=ffn1_grads)
    grad_x = dh0[NM:L_real][None]
    scatter_end("mix", dh0)
    scatter_end("ffn1", dh0)
    big = list(totals)
    grads.update(zip(big, _share_halves("grads_to_core_pair", [totals.pop(k) for k in big])))

    small = [loss_part[:, :1].reshape(1), dh0[:NM].reshape(-1), dg_ffn1.reshape(-1), dg_mix.reshape(-1),
             dg_ffn2.reshape(-1), dbias[:, 0, 0], dgq.reshape(-1), dgk.reshape(-1)]
    sizes = [s.shape[0] for s in small]
    flat = jnp.concatenate(small)
    rows = _round_up(-(-flat.shape[0] // LANES), SUBLANES)
    packed = jnp.pad(flat, (0, rows * LANES - flat.shape[0])).reshape(rows, LANES)
    total = _sum_devices("sum_small", _gather_small("gather_small", packed)).reshape(-1)
    pieces, off = [], 0
    for n_el in sizes:
        pieces.append(total[off:off + n_el])
        off += n_el
    loss = pieces[0][0]
    d_meta = lax.dynamic_slice_in_dim(pieces[1].reshape(NM, D), pos[1] * cs_d, cs_d, axis=1)
    grads.update(meta_tokens=d_meta, ffn1_norm=pieces[2].reshape(1, D), mix_norm=pieces[3].reshape(1, D),
                 ffn2_norm=pieces[4].reshape(1, D), b_forget=pieces[5].reshape(1, H),
                 fox_q_norm=pieces[6].reshape(1, H, HEAD_DIM), fox_k_norm=pieces[7].reshape(1, H, HEAD_DIM))

    for k in names:
        if k not in updates:
            adamw(k)
    return (loss, grad_x, *[updates[k][i] for i in (3, 0, 1, 2) for k in names])
```
